```python
import jax, jax.numpy as jnp
from jax import lax
import numpy as np

D_MODEL = 1024
BATCH = 1
SEQ = 16384
DEPTH = 1
DEC_BATCH = 8
DEC_SEQ = 4096
PAST_LEN = 128

POOL_GROUPS = 4
POOL_GROUP_W = D_MODEL // 8
POOL_W = POOL_GROUPS * POOL_GROUP_W
POOL_WINDOWS = (2, 4, 8, 16)
GLA_HEADS = 4
GLA_K_W = D_MODEL // 2
GLA_V_W = D_MODEL
GLA_DK = GLA_K_W // GLA_HEADS
GLA_DV = GLA_V_W // GLA_HEADS
GATE_RANK = 16
GATE_TAU = 16.0
GLA_CHUNK = 64
IN_SPLITS = (POOL_W, GLA_K_W, GLA_K_W, GLA_V_W, GLA_V_W, GATE_RANK, GATE_RANK, D_MODEL, D_MODEL)
IN_COLS = sum(IN_SPLITS)
N_EXPERTS = 256
TOP_K = 8
N_GROUPS = 8
TOPK_GROUPS = 4
EXPERTS_PER_GROUP = N_EXPERTS // N_GROUPS
EXPERT_HIDDEN = 256
SHARED_HIDDEN = 256
ROUTED_SCALE = 2.5
MOE_BLOCK = 128
DN_ALPHA = (2 * DEPTH) ** 0.25
DN_BETA = (8 * DEPTH) ** -0.25
LN_EPS = 1e-5
RMS_EPS = 1e-6

kernel_name = 'hybrid_pool_gla_moe_encoder'

F32 = jnp.float32


def layer_norm(x, g, b):
    xf = x.astype(F32)
    mu = xf.mean(-1, keepdims=True)
    xc = xf - mu
    var = (xc * xc).mean(-1, keepdims=True)
    return (xc * lax.rsqrt(var + LN_EPS) * g.astype(F32) + b.astype(F32)).astype(x.dtype)


def multiscale_pool(u, w_grp, scale):
    B, L, _ = u.shape
    uf = u.astype(F32)
    c = jnp.concatenate([jnp.zeros((B, 1, POOL_W), F32), jnp.cumsum(uf, axis=1)], axis=1)
    t = jnp.arange(L)
    outs = []
    for gi, w in enumerate(POOL_WINDOWS):
        lo = jnp.clip(t - w // 2, 0, L)
        hi = jnp.clip(t + w // 2, 0, L)
        sl = slice(gi * POOL_GROUP_W, (gi + 1) * POOL_GROUP_W)
        cg = c[:, :, sl]
        mean = (jnp.take(cg, hi, axis=1) - jnp.take(cg, lo, axis=1)) / (hi - lo).astype(F32)[None, :, None]
        outs.append(mean - uf[:, :, sl])
    p = jnp.stack(outs, axis=2)
    p = jnp.einsum('blgc,gcd->blgd', p, w_grp.astype(F32)).reshape(B, L, POOL_W)
    return (p * scale.astype(F32)).astype(u.dtype)


def gla_direction(q, k, v, log_a):
    B, H, L, DK = q.shape
    DV = v.shape[-1]
    C = GLA_CHUNK
    N = L // C
    q = q.reshape(B, H, N, C, DK)
    k = k.reshape(B, H, N, C, DK)
    v = v.reshape(B, H, N, C, DV)
    b = jnp.cumsum(log_a.reshape(B, H, N, C, DK), axis=3)
    q_dec = q * jnp.exp(b)
    k_dec = k * jnp.exp(-b)
    mask = jnp.tril(jnp.ones((C, C), dtype=bool))
    att = jnp.where(mask, jnp.einsum('bhncd,bhnsd->bhncs', q_dec, k_dec), 0.0)
    o = jnp.einsum('bhncs,bhnsv->bhncv', att, v)
    b_last = b[:, :, :, -1:, :]
    kv = jnp.einsum('bhncd,bhncv->nbhdv', k * jnp.exp(b_last - b), v)
    decay = jnp.moveaxis(jnp.exp(b_last[:, :, :, 0, :]), 2, 0)

    def step(S, inp):
        dec, kv_n = inp
        return dec[..., None] * S + kv_n, S

    _, S_prev = lax.scan(step, jnp.zeros((B, H, DK, DV), F32), (decay, kv))
    o = o + jnp.einsum('bhncd,nbhdv->bhncv', q_dec, S_prev)
    return o.reshape(B, H, L, DV)


def gla_branch(q_in, k_in, v_in, og, z_f, z_b, w2_f, b_f, w2_b, b_b, norm_g):
    B, L, _ = q_in.shape

    def heads(t, d):
        return t.astype(F32).reshape(B, L, GLA_HEADS, d).transpose(0, 2, 1, 3)

    def flip(t):
        return jnp.flip(t, axis=2)

    q = heads(q_in, GLA_DK) * (GLA_DK ** -0.5)
    k = heads(k_in, GLA_DK)
    v = heads(v_in, GLA_DV)
    la_f = heads(jax.nn.log_sigmoid(z_f.astype(F32) @ w2_f.astype(F32) + b_f.astype(F32)) / GATE_TAU, GLA_DK)
    la_b = heads(jax.nn.log_sigmoid(z_b.astype(F32) @ w2_b.astype(F32) + b_b.astype(F32)) / GATE_TAU, GLA_DK)
    o = gla_direction(q, k, v, la_f) + flip(gla_direction(flip(q), flip(k), flip(v), flip(la_b)))
    o = o * lax.rsqrt((o * o).mean(-1, keepdims=True) + RMS_EPS)
    o = o.transpose(0, 2, 1, 3).reshape(B, L, GLA_V_W) * norm_g.astype(F32)
    return (o * jax.nn.silu(og.astype(F32))).astype(q_in.dtype)


def token_mixer(h, w_in, pool_w, pool_scale, w_pool_up, gate_w2_fwd, gate_b_fwd, gate_w2_bwd, gate_b_bwd,
                gla_norm_g, w_gla_up, w_out):
    proj = h @ w_in
    cuts = np.cumsum(IN_SPLITS)[:-1].tolist()
    u, q, k, v, og, z_f, z_b, g_pool, g_gla = jnp.split(proj, cuts, axis=-1)
    pool_out = multiscale_pool(u, pool_w, pool_scale) @ w_pool_up
    gla_out = gla_branch(q, k, v, og, z_f, z_b, gate_w2_fwd, gate_b_fwd, gate_w2_bwd, gate_b_bwd,
                         gla_norm_g) @ w_gla_up
    merged = jax.nn.sigmoid(g_pool) * pool_out + jax.nn.sigmoid(g_gla) * gla_out
    return merged @ w_out


def moe_ffn(h, w_router, router_bias, w_exp_gate, w_exp_up, w_exp_down, w_sh_gate, w_sh_up, w_sh_down):
    B, L, D = h.shape
    T = B * L
    x2d = h.reshape(T, D)
    scores = jax.nn.sigmoid(x2d.astype(F32) @ w_router.astype(F32))
    sel = scores + router_bias.astype(F32)
    grp_score = lax.top_k(sel.reshape(T, N_GROUPS, EXPERTS_PER_GROUP), 2)[0].sum(-1)
    _, top_g = lax.top_k(grp_score, TOPK_GROUPS)
    gmask = jax.nn.one_hot(top_g, N_GROUPS, dtype=F32).sum(1) > 0
    sel = jnp.where(jnp.repeat(gmask, EXPERTS_PER_GROUP, axis=1), sel, -jnp.inf)
    _, idx = lax.top_k(sel, TOP_K)
    wts = jnp.take_along_axis(scores, idx, axis=1)
    wts = wts / wts.sum(-1, keepdims=True) * ROUTED_SCALE
    TK = T * TOP_K
    flat_e = idx.reshape(TK).astype(jnp.int32)
    flat_tok = jnp.repeat(jnp.arange(T, dtype=jnp.int32), TOP_K)
    flat_w = wts.reshape(TK)
    order = jnp.argsort(flat_e)
    se, stok, sw = flat_e[order], flat_tok[order], flat_w[order]
    counts = jax.ops.segment_sum(jnp.ones((TK,), jnp.int32), flat_e, num_segments=N_EXPERTS)
    starts = jnp.cumsum(counts) - counts
    padded = (counts + MOE_BLOCK - 1) // MOE_BLOCK * MOE_BLOCK
    pend = jnp.cumsum(padded)
    pstart = pend - padded
    dest = pstart[se] + jnp.arange(TK, dtype=jnp.int32) - starts[se]
    n_rows = TK + N_EXPERTS * MOE_BLOCK
    n_blocks = n_rows // MOE_BLOCK
    row_tok = jnp.zeros((n_rows,), jnp.int32).at[dest].set(stok)
    row_w = jnp.zeros((n_rows,), h.dtype).at[dest].set(sw.astype(h.dtype))
    block_e = jnp.minimum(jnp.searchsorted(pend, jnp.arange(n_blocks, dtype=jnp.int32) * MOE_BLOCK, side='right'),
                          N_EXPERTS - 1)

    def expert_block(y, blk):
        e, tok, wt = blk
        xb = x2d[tok]
        hb = jax.nn.silu(xb @ w_exp_gate[e]) * (xb @ w_exp_up[e])
        return y.at[tok].add(((hb @ w_exp_down[e]) * wt[:, None]).astype(y.dtype)), None

    routed, _ = lax.scan(expert_block, jnp.zeros_like(x2d),
                         (block_e, row_tok.reshape(n_blocks, MOE_BLOCK), row_w.reshape(n_blocks, MOE_BLOCK)))
    shared = (jax.nn.silu(x2d @ w_sh_gate) * (x2d @ w_sh_up)) @ w_sh_down
    return (routed + shared).reshape(B, L, D)


def encoder_trunk(x, ln0_g, ln0_b, w_in, pool_w, pool_scale, w_pool_up, gate_w2_fwd, gate_b_fwd, gate_w2_bwd,
                  gate_b_bwd, gla_norm_g, w_gla_up, w_out, ln1_g, ln1_b, w_router, router_bias, w_exp_gate,
                  w_exp_up, w_exp_down, w_sh_gate, w_sh_up, w_sh_down, ln2_g, ln2_b):
    h = layer_norm(x, ln0_g, ln0_b)
    for l in range(DEPTH):
        mix = token_mixer(h, w_in[l], pool_w[l], pool_scale[l], w_pool_up[l], gate_w2_fwd[l], gate_b_fwd[l],
                          gate_w2_bwd[l], gate_b_bwd[l], gla_norm_g[l], w_gla_up[l], w_out[l])
        h = layer_norm(DN_ALPHA * h + mix, ln1_g[l], ln1_b[l])
        ffn = moe_ffn(h, w_router[l], router_bias[l], w_exp_gate[l], w_exp_up[l], w_exp_down[l],
                      w_sh_gate[l], w_sh_up[l], w_sh_down[l])
        h = layer_norm(DN_ALPHA * h + ffn, ln2_g[l], ln2_b[l])
    return h


def setup_inputs(seed: int = 0) -> dict:
    key = jax.random.key(seed)
    ks = jax.random.split(key, 32)

    def nrm(k, shape, scale):
        return jax.random.normal(k, shape, F32) * scale

    D, L, E, F = D_MODEL, DEPTH, N_EXPERTS, EXPERT_HIDDEN
    return {
        'x_prompt': nrm(ks[0], (BATCH, SEQ, D), 1.0),
        'x_sample': nrm(ks[1], (DEC_BATCH, DEC_SEQ, D), 1.0),
        'ln0_g': 1.0 + nrm(ks[2], (D,), 0.02),
        'ln0_b': nrm(ks[3], (D,), 0.02),
        'w_in': nrm(ks[4], (L, D, IN_COLS), D ** -0.5),
        'pool_w': nrm(ks[5], (L, POOL_GROUPS, POOL_GROUP_W, POOL_GROUP_W), POOL_GROUP_W ** -0.5),
        'pool_scale': 1.0 + nrm(ks[6], (L, POOL_W), 0.1),
        'w_pool_up': nrm(ks[7], (L, POOL_W, D), POOL_W ** -0.5),
        'gate_w2_fwd': nrm(ks[8], (L, GATE_RANK, GLA_K_W), GATE_RANK ** -0.5),
        'gate_b_fwd': nrm(ks[9], (L, GLA_K_W), 0.1),
        'gate_w2_bwd': nrm(ks[10], (L, GATE_RANK, GLA_K_W), GATE_RANK ** -0.5),
        'gate_b_bwd': nrm(ks[11], (L, GLA_K_W), 0.1),
        'gla_norm_g': 1.0 + nrm(ks[12], (L, GLA_V_W), 0.02),
        'w_gla_up': nrm(ks[13], (L, GLA_V_W, D), GLA_V_W ** -0.5),
        'w_out': nrm(ks[14], (L, D, D), D ** -0.5 * DN_BETA),
        'ln1_g': 1.0 + nrm(ks[15], (L, D), 0.02),
        'ln1_b': nrm(ks[16], (L, D), 0.02),
        'w_router': nrm(ks[17], (L, D, E), D ** -0.5),
        'router_bias': nrm(ks[18], (L, E), 0.01),
        'w_exp_gate': nrm(ks[19], (L, E, D, F), D ** -0.5),
        'w_exp_up': nrm(ks[20], (L, E, D, F), D ** -0.5),
        'w_exp_down': nrm(ks[21], (L, E, F, D), F ** -0.5 * DN_BETA),
        'w_sh_gate': nrm(ks[22], (L, D, SHARED_HIDDEN), D ** -0.5),
        'w_sh_up': nrm(ks[23], (L, D, SHARED_HIDDEN), D ** -0.5),
        'w_sh_down': nrm(ks[24], (L, SHARED_HIDDEN, D), SHARED_HIDDEN ** -0.5 * DN_BETA),
        'ln2_g': 1.0 + nrm(ks[25], (L, D), 0.02),
        'ln2_b': nrm(ks[26], (L, D), 0.02),
    }


def reference(x_prompt, x_sample, ln0_g, ln0_b, w_in, pool_w, pool_scale, w_pool_up, gate_w2_fwd, gate_b_fwd,
              gate_w2_bwd, gate_b_bwd, gla_norm_g, w_gla_up, w_out, ln1_g, ln1_b, w_router, router_bias,
              w_exp_gate, w_exp_up, w_exp_down, w_sh_gate, w_sh_up, w_sh_down, ln2_g, ln2_b):
    y_prompt = encoder_trunk(x_prompt, ln0_g, ln0_b, w_in, pool_w, pool_scale, w_pool_up, gate_w2_fwd, gate_b_fwd,
                             gate_w2_bwd, gate_b_bwd, gla_norm_g, w_gla_up, w_out, ln1_g, ln1_b, w_router,
                             router_bias, w_exp_gate, w_exp_up, w_exp_down, w_sh_gate, w_sh_up, w_sh_down,
                             ln2_g, ln2_b)
    y_sample = encoder_trunk(x_sample, ln0_g, ln0_b, w_in, pool_w, pool_scale, w_pool_up, gate_w2_fwd, gate_b_fwd,
                             gate_w2_bwd, gate_b_bwd, gla_norm_g, w_gla_up, w_out, ln1_g, ln1_b, w_router,
                             router_bias, w_exp_gate, w_exp_up, w_exp_down, w_sh_gate, w_sh_up, w_sh_down,
                             ln2_g, ln2_b)
    return (y_prompt, y_sample)
```

```python
import functools

import jax
import jax.numpy as jnp
from jax import lax
from jax.experimental import pallas as pl
from jax.experimental.pallas import tpu as pltpu

F32 = jnp.float32
BF16 = jnp.bfloat16

D_MODEL = 1024
POOL_GROUPS = 4
POOL_GROUP_W = 128
POOL_W = POOL_GROUPS * POOL_GROUP_W
POOL_WINDOWS = (2, 4, 8, 16)
POOL_HALO = 16
GLA_HEADS = 4
GLA_K_W = 512
GLA_V_W = 1024
GLA_DK = GLA_K_W // GLA_HEADS
GLA_DV = GLA_V_W // GLA_HEADS
GATE_RANK = 16
GATE_TAU = 16.0
GLA_CHUNK = 64
Z_PAD = 128
N_EXPERTS = 256
TOP_K = 8
N_GROUPS = 8
TOPK_GROUPS = 4
EXPERTS_PER_GROUP = N_EXPERTS // N_GROUPS
EXPERT_HIDDEN = 256
SHARED_HIDDEN = 256
ROUTED_SCALE = 2.5
DEPTH = 1
DN_ALPHA = (2 * DEPTH) ** 0.25
LN_EPS = 1e-5
RMS_EPS = 1e-6

VMEM_LIMIT_BYTES = 56 * 1024 * 1024

TILE_INPROJ = 512
TILE_POOL = 512
TILE_GLA = 256
TILE_ROUTER = 512
TILE_FINAL = 512
MOE_BLOCK = 256


def _params(semantics):
    return pltpu.CompilerParams(dimension_semantics=semantics, vmem_limit_bytes=VMEM_LIMIT_BYTES)


def _const_spec(shape):
    nd = len(shape)
    return pl.BlockSpec(shape, lambda *_: (0,) * nd)


def _layer_norm(x, g, b):
    mu = jnp.mean(x, axis=-1, keepdims=True)
    xc = x - mu
    var = jnp.mean(xc * xc, axis=-1, keepdims=True)
    return xc * lax.rsqrt(var + LN_EPS) * g + b


def _sigmoid(x):
    return 1.0 / (1.0 + jnp.exp(-x))


def _log_sigmoid(x):
    return jnp.minimum(x, 0.0) - jnp.log(1.0 + jnp.exp(-jnp.abs(x)))


def _silu(x):
    return x * _sigmoid(x)


def _tile_hits(i, tile, offsets):
    hit = i < 0
    for off in offsets:
        if off % tile == 0:
            hit = jnp.logical_or(hit, i == off // tile)
    return hit


_MAIN_SPLITS = (POOL_W, GLA_K_W, GLA_K_W, GLA_V_W, GLA_V_W, D_MODEL, D_MODEL)


def _inproj_kernel(xp_ref, xs_ref, g_ref, b_ref, w_ref, wz_ref,
                   h_ref, u_ref, q_ref, k_ref, v_ref, og_ref, gp_ref, gg_ref, z_ref, *, n_prompt_tiles):
    i = pl.program_id(0)
    x = jnp.where(i < n_prompt_tiles, xp_ref[...], xs_ref[...])
    h = _layer_norm(x, g_ref[...], b_ref[...])
    h_ref[...] = h
    hb = h.astype(BF16)
    off = 0
    for ref, width in zip((u_ref, q_ref, k_ref, v_ref, og_ref, gp_ref, gg_ref), _MAIN_SPLITS):
        ref[...] = jnp.dot(hb, w_ref[:, off:off + width], preferred_element_type=F32).astype(BF16)
        off += width
    z_ref[...] = jnp.dot(hb, wz_ref[...], preferred_element_type=F32).astype(BF16)


def _inproj(xp, xs, ln_g, ln_b, w_main, w_z):
    n_p, n_s = xp.shape[0], xs.shape[0]
    n = n_p + n_s
    t = TILE_INPROJ
    npt, nst = n_p // t, n_s // t
    row = lambda i: (i, 0)
    widths = _MAIN_SPLITS + (Z_PAD,)
    out_shape = [jax.ShapeDtypeStruct((n, D_MODEL), F32)] + [jax.ShapeDtypeStruct((n, w), BF16) for w in widths]
    out_specs = [pl.BlockSpec((t, D_MODEL), row)] + [pl.BlockSpec((t, w), row) for w in widths]
    return pl.pallas_call(
        functools.partial(_inproj_kernel, n_prompt_tiles=npt),
        grid=(npt + nst,),
        in_specs=[
            pl.BlockSpec((t, D_MODEL), lambda i: (jnp.minimum(i, npt - 1), 0)),
            pl.BlockSpec((t, D_MODEL), lambda i: (jnp.maximum(i - npt, 0), 0)),
            _const_spec((1, D_MODEL)), _const_spec((1, D_MODEL)),
            _const_spec(w_main.shape), _const_spec(w_z.shape),
        ],
        out_specs=out_specs,
        out_shape=out_shape,
        compiler_params=_params(("parallel",)),
        name="inproj",
    )(xp, xs, ln_g, ln_b, w_main, w_z)


def _pool_kernel(u_ref, up_ref, un_ref, gp_ref, band_ref, pw_ref, sc_ref, wup_ref, pm_ref, *, starts, ends):
    t = u_ref.shape[0]
    i = pl.program_id(0)
    is_start = _tile_hits(i, t, starts)
    is_end = _tile_hits(i + 1, t, ends)
    cur = u_ref[...]
    prev = jnp.where(is_start, jnp.zeros_like(up_ref[...]), up_ref[...])
    nxt = jnp.where(is_end, jnp.zeros_like(un_ref[...]), un_ref[...])
    ext = jnp.concatenate([prev, cur, nxt], axis=0)
    r = lax.broadcasted_iota(jnp.int32, (t, POOL_GROUP_W), 0).astype(F32)
    fs = is_start.astype(F32)
    fe = is_end.astype(F32)
    parts = []
    for gi, w in enumerate(POOL_WINDOWS):
        sl = slice(gi * POOL_GROUP_W, (gi + 1) * POOL_GROUP_W)
        s = jnp.dot(band_ref[gi], ext[:, sl], preferred_element_type=F32)
        cnt = w - fs * jnp.maximum(w // 2 - r, 0.0) - fe * jnp.maximum(r + (w // 2 - t), 0.0)
        parts.append(s / cnt - cur[:, sl].astype(F32))
    p = jnp.concatenate(parts, axis=1).astype(BF16)
    p = (jnp.dot(p, pw_ref[...], preferred_element_type=F32) * sc_ref[...]).astype(BF16)
    pool_out = jnp.dot(p, wup_ref[...], preferred_element_type=F32)
    pm_ref[...] = (_sigmoid(gp_ref[...].astype(F32)) * pool_out).astype(BF16)


def _pool_bands(t):
    r = jnp.arange(t)[:, None]
    c = jnp.arange(t + 2 * POOL_HALO)[None, :] - POOL_HALO
    return jnp.stack([((c >= r - w // 2) & (c <= r + w // 2 - 1)).astype(BF16) for w in POOL_WINDOWS])


def _pool(u, gp, pool_w_bd, pool_scale, w_pool_up, starts, ends):
    n = u.shape[0]
    t = TILE_POOL
    hb = t // POOL_HALO
    last = n // POOL_HALO - 1
    return pl.pallas_call(
        functools.partial(_pool_kernel, starts=starts, ends=ends),
        grid=(n // t,),
        in_specs=[
            pl.BlockSpec((t, POOL_W), lambda i: (i, 0)),
            pl.BlockSpec((POOL_HALO, POOL_W), lambda i: (jnp.maximum(i * hb - 1, 0), 0)),
            pl.BlockSpec((POOL_HALO, POOL_W), lambda i: (jnp.minimum((i + 1) * hb, last), 0)),
            pl.BlockSpec((t, D_MODEL), lambda i: (i, 0)),
            _const_spec((POOL_GROUPS, t, t + 2 * POOL_HALO)),
            _const_spec((POOL_W, POOL_W)), _const_spec((1, POOL_W)), _const_spec((POOL_W, D_MODEL)),
        ],
        out_specs=pl.BlockSpec((t, D_MODEL), lambda i: (i, 0)),
        out_shape=jax.ShapeDtypeStruct((n, D_MODEL), BF16),
        compiler_params=_params(("parallel",)),
        name="pool",
    )(u, u, u, gp, _pool_bands(t), pool_w_bd, pool_scale, w_pool_up)


def _gla_tile(q, k, v, la, tri, st_ref, *, reverse):
    t = q.shape[0]
    c_sz = GLA_CHUNK
    hi = la.astype(BF16)
    r1 = la - hi.astype(F32)
    mid = r1.astype(BF16)
    lo = (r1 - mid.astype(F32)).astype(BF16)
    b = (jnp.dot(tri, hi, preferred_element_type=F32) + jnp.dot(tri, mid, preferred_element_type=F32)
         + jnp.dot(tri, lo, preferred_element_type=F32))
    kf = k.astype(F32)
    qd = (q.astype(F32) * jnp.exp(b) * (GLA_DK ** -0.5)).astype(BF16)
    kd = (kf * jnp.exp(-b)).astype(BF16)
    row = lax.broadcasted_iota(jnp.int32, (c_sz, c_sz), 0)
    col = lax.broadcasted_iota(jnp.int32, (c_sz, c_sz), 1)
    mask = (col >= row) if reverse else (col <= row)
    n_chunks = t // c_sz
    outs = [None] * n_chunks
    nt = (((1,), (1,)), ((), ()))
    tn = (((0,), (0,)), ((), ()))
    for c in (range(n_chunks - 1, -1, -1) if reverse else range(n_chunks)):
        rows = slice(c * c_sz, (c + 1) * c_sz)
        b_c = b[rows]
        b_edge = b_c[0:1] if reverse else b_c[c_sz - 1:c_sz]
        k2 = (kf[rows] * jnp.exp(b_edge - b_c)).astype(BF16)
        dec = jnp.exp(b_edge)
        heads = []
        for h in range(GLA_HEADS):
            ls = slice(h * GLA_DK, (h + 1) * GLA_DK)
            vs = slice(h * GLA_DV, (h + 1) * GLA_DV)
            qh = qd[rows, ls]
            att = lax.dot_general(qh, kd[rows, ls], nt, preferred_element_type=F32)
            att = jnp.where(mask, att, 0.0).astype(BF16)
            vh = v[rows, vs]
            st = st_ref[h]
            o = jnp.dot(att, vh, preferred_element_type=F32)
            o = o + lax.dot_general(qh, st.astype(BF16), nt, preferred_element_type=F32)
            st_ref[h] = st * dec[:, ls] + lax.dot_general(vh, k2[:, ls], tn, preferred_element_type=F32)
            heads.append(o)
        outs[c] = jnp.concatenate(heads, axis=1)
    return jnp.concatenate(outs, axis=0)


def _log_decay(z, w2_ref, gb_ref):
    return _log_sigmoid(jnp.dot(z, w2_ref[...], preferred_element_type=F32) + gb_ref[...]) * (1.0 / GATE_TAU)


def _gla_bwd_kernel(q_ref, k_ref, v_ref, z_ref, w2_ref, gb_ref, tri_ref, ob_ref, st_ref, *, n_tiles, ends):
    t = q_ref.shape[0]
    j = n_tiles - 1 - pl.program_id(0)

    @pl.when(_tile_hits(j + 1, t, ends))
    def _():
        st_ref[...] = jnp.zeros_like(st_ref)

    la = _log_decay(z_ref[...], w2_ref, gb_ref)
    ob_ref[...] = _gla_tile(q_ref[...], k_ref[...], v_ref[...], la, tri_ref[...], st_ref, reverse=True)


def _gla_fwd_kernel(q_ref, k_ref, v_ref, z_ref, w2_ref, gb_ref, tri_ref, ob_ref, og_ref, gg_ref, pm_ref, h_ref,
                    ng_ref, wgu_ref, wo_ref, l1g_ref, l1b_ref, h1_ref, st_ref, *, starts):
    t = q_ref.shape[0]
    i = pl.program_id(0)

    @pl.when(_tile_hits(i, t, starts))
    def _():
        st_ref[...] = jnp.zeros_like(st_ref)

    la = _log_decay(z_ref[...], w2_ref, gb_ref)
    o = _gla_tile(q_ref[...], k_ref[...], v_ref[...], la, tri_ref[...], st_ref, reverse=False) + ob_ref[...]
    heads = []
    for h in range(GLA_HEADS):
        oh = o[:, h * GLA_DV:(h + 1) * GLA_DV]
        heads.append(oh * lax.rsqrt(jnp.mean(oh * oh, axis=-1, keepdims=True) + RMS_EPS))
    o = jnp.concatenate(heads, axis=1) * ng_ref[...]
    o = (o * _silu(og_ref[...].astype(F32))).astype(BF16)
    gla_out = jnp.dot(o, wgu_ref[...], preferred_element_type=F32)
    merged = _sigmoid(gg_ref[...].astype(F32)) * gla_out + pm_ref[...].astype(F32)
    mix = jnp.dot(merged.astype(BF16), wo_ref[...], preferred_element_type=F32)
    h1_ref[...] = _layer_norm(DN_ALPHA * h_ref[...] + mix, l1g_ref[...], l1b_ref[...])


def _chunk_tri(t, reverse):
    r = jnp.arange(t)[:, None]
    c = jnp.arange(t)[None, :]
    same = (r // GLA_CHUNK) == (c // GLA_CHUNK)
    return (same & ((c >= r) if reverse else (c <= r))).astype(BF16)


def _gla_bwd(q, k, v, z, w2b_pad, gate_b, ends):
    n = q.shape[0]
    t = TILE_GLA
    nt = n // t
    rev = lambda i: (nt - 1 - i, 0)
    return pl.pallas_call(
        functools.partial(_gla_bwd_kernel, n_tiles=nt, ends=ends),
        grid=(nt,),
        in_specs=[
            pl.BlockSpec((t, GLA_K_W), rev), pl.BlockSpec((t, GLA_K_W), rev), pl.BlockSpec((t, GLA_V_W), rev),
            pl.BlockSpec((t, Z_PAD), rev),
            _const_spec((Z_PAD, GLA_K_W)), _const_spec((1, GLA_K_W)), _const_spec((t, t)),
        ],
        out_specs=pl.BlockSpec((t, GLA_V_W), rev),
        out_shape=jax.ShapeDtypeStruct((n, GLA_V_W), F32),
        scratch_shapes=[pltpu.VMEM((GLA_HEADS, GLA_DV, GLA_DK), F32)],
        compiler_params=_params(("arbitrary",)),
        name="gla_bwd",
    )(q, k, v, z, w2b_pad, gate_b, _chunk_tri(t, True))


def _gla_fwd_mix(q, k, v, z, w2f_pad, gate_b, ob, og, gg, pm, h, norm_g, w_gla_up, w_out, ln1_g, ln1_b, starts):
    n = q.shape[0]
    t = TILE_GLA
    row = lambda i: (i, 0)
    return pl.pallas_call(
        functools.partial(_gla_fwd_kernel, starts=starts),
        grid=(n // t,),
        in_specs=[
            pl.BlockSpec((t, GLA_K_W), row), pl.BlockSpec((t, GLA_K_W), row), pl.BlockSpec((t, GLA_V_W), row),
            pl.BlockSpec((t, Z_PAD), row),
            _const_spec((Z_PAD, GLA_K_W)), _const_spec((1, GLA_K_W)), _const_spec((t, t)),
            pl.BlockSpec((t, GLA_V_W), row), pl.BlockSpec((t, GLA_V_W), row), pl.BlockSpec((t, D_MODEL), row),
            pl.BlockSpec((t, D_MODEL), row), pl.BlockSpec((t, D_MODEL), row),
            _const_spec((1, GLA_V_W)), _const_spec((GLA_V_W, D_MODEL)), _const_spec((D_MODEL, D_MODEL)),
            _const_spec((1, D_MODEL)), _const_spec((1, D_MODEL)),
        ],
        out_specs=pl.BlockSpec((t, D_MODEL), row),
        out_shape=jax.ShapeDtypeStruct((n, D_MODEL), F32),
        scratch_shapes=[pltpu.VMEM((GLA_HEADS, GLA_DV, GLA_DK), F32)],
        compiler_params=_params(("arbitrary",)),
        name="gla_fwd_mix",
    )(q, k, v, z, w2f_pad, gate_b, _chunk_tri(t, False), ob, og, gg, pm, h, norm_g, w_gla_up, w_out, ln1_g, ln1_b)


def _router_kernel(h_ref, wrh_ref, wrl_ref, rb_ref, tri_ref, idx_ref, wts_ref, rank_ref, cnt_ref, base_ref):
    t = h_ref.shape[0]

    @pl.when(pl.program_id(0) == 0)
    def _():
        base_ref[...] = jnp.zeros_like(base_ref)

    h = h_ref[...]
    hh = h.astype(BF16)
    hl = (h - hh.astype(F32)).astype(BF16)
    nt = (((1,), (1,)), ((), ()))
    logits = (lax.dot_general(wrh_ref[...], hh, nt, preferred_element_type=F32)
              + lax.dot_general(wrl_ref[...], hh, nt, preferred_element_type=F32)
              + lax.dot_general(wrh_ref[...], hl, nt, preferred_element_type=F32))
    scores = _sigmoid(logits)
    sel = scores + rb_ref[:, 0:1]
    neg = jnp.float32(-jnp.inf)

    sub = lax.broadcasted_iota(jnp.int32, (EXPERTS_PER_GROUP, t), 0)
    gscore = []
    for g in range(N_GROUPS):
        sg = sel[g * EXPERTS_PER_GROUP:(g + 1) * EXPERTS_PER_GROUP]
        m1 = jnp.max(sg, axis=0, keepdims=True)
        first = jnp.min(jnp.where(sg == m1, sub, EXPERTS_PER_GROUP), axis=0, keepdims=True)
        m2 = jnp.max(jnp.where(sub == first, neg, sg), axis=0, keepdims=True)
        gscore.append(m1 + m2)
    masked = []
    for g in range(N_GROUPS):
        beaten = jnp.zeros((1, t), jnp.int32)
        for g2 in range(N_GROUPS):
            if g2 == g:
                continue
            wins = (gscore[g2] >= gscore[g]) if g2 < g else (gscore[g2] > gscore[g])
            beaten = beaten + wins.astype(jnp.int32)
        keep = beaten < TOPK_GROUPS
        masked.append(jnp.where(keep, sel[g * EXPERTS_PER_GROUP:(g + 1) * EXPERTS_PER_GROUP], neg))
    cand = jnp.concatenate(masked, axis=0)

    eid = lax.broadcasted_iota(jnp.int32, (N_EXPERTS, t), 0)
    picked = []
    assign = jnp.zeros((N_EXPERTS, t), F32)
    wsum = jnp.zeros((1, t), F32)
    for kk in range(TOP_K):
        m = jnp.max(cand, axis=0, keepdims=True)
        e_k = jnp.min(jnp.where(cand == m, eid, N_EXPERTS), axis=0, keepdims=True)
        hit = eid == e_k
        w_k = jnp.sum(jnp.where(hit, scores, 0.0), axis=0, keepdims=True)
        cand = jnp.where(hit, neg, cand)
        assign = jnp.where(hit, 1.0, assign)
        wsum = wsum + w_k
        picked.append((e_k, w_k))
        idx_ref[kk:kk + 1, :] = e_k

    a16 = assign.astype(BF16)
    before = jnp.dot(a16, tri_ref[...], preferred_element_type=F32)
    base = base_ref[...]
    before = before + jnp.concatenate([base] * (t // 128), axis=1)
    inv = ROUTED_SCALE / wsum
    for kk, (e_k, w_k) in enumerate(picked):
        wts_ref[kk:kk + 1, :] = w_k * inv
        rank_ref[kk:kk + 1, :] = jnp.sum(jnp.where(eid == e_k, before, 0.0), axis=0, keepdims=True).astype(jnp.int32)
    total = base + jnp.dot(a16, jnp.ones((t, 128), BF16), preferred_element_type=F32)
    base_ref[...] = total
    cnt_ref[...] = total


def _router(h1, wr_hi, wr_lo, rbias):
    n = h1.shape[0]
    t = TILE_ROUTER
    r = jnp.arange(t)
    tri = (r[:, None] < r[None, :]).astype(BF16)
    col = lambda i: (0, i)
    idx, wts, rank, cnt = pl.pallas_call(
        _router_kernel,
        grid=(n // t,),
        in_specs=[
            pl.BlockSpec((t, D_MODEL), lambda i: (i, 0)),
            _const_spec((N_EXPERTS, D_MODEL)), _const_spec((N_EXPERTS, D_MODEL)),
            _const_spec((N_EXPERTS, 128)), _const_spec((t, t)),
        ],
        out_specs=[pl.BlockSpec((TOP_K, t), col), pl.BlockSpec((TOP_K, t), col), pl.BlockSpec((TOP_K, t), col),
                   _const_spec((N_EXPERTS, 128))],
        out_shape=[jax.ShapeDtypeStruct((TOP_K, n), jnp.int32), jax.ShapeDtypeStruct((TOP_K, n), F32),
                   jax.ShapeDtypeStruct((TOP_K, n), jnp.int32), jax.ShapeDtypeStruct((N_EXPERTS, 128), F32)],
        scratch_shapes=[pltpu.VMEM((N_EXPERTS, 128), F32)],
        compiler_params=_params(("arbitrary",)),
        name="router",
    )(h1, wr_hi, wr_lo, rbias, tri)
    return idx, wts, rank, cnt[:, 0].astype(jnp.int32)


def _expert_kernel(be_ref, nu_ref, x_ref, wg_ref, wu_ref, wd_ref, y_ref, wgu_s, wd_s):
    b = pl.program_id(0)
    prev = be_ref[jnp.maximum(b - 1, 0)]
    fresh = jnp.logical_or(b == 0, be_ref[b] != prev)

    @pl.when(jnp.logical_and(fresh, b < nu_ref[0]))
    def _():
        wgu_s[:, :EXPERT_HIDDEN] = wg_ref[0].astype(BF16)
        wgu_s[:, EXPERT_HIDDEN:] = wu_ref[0].astype(BF16)
        wd_s[...] = wd_ref[0].astype(BF16)

    @pl.when(b < nu_ref[0])
    def _():
        gu = jnp.dot(x_ref[...], wgu_s[...], preferred_element_type=F32)
        hid = (_silu(gu[:, :EXPERT_HIDDEN]) * gu[:, EXPERT_HIDDEN:]).astype(BF16)
        y_ref[...] = jnp.dot(hid, wd_s[...], preferred_element_type=F32)


def _experts(block_e, n_used, xs, w_gate, w_up, w_down):
    rows = xs.shape[0]
    nb = rows // MOE_BLOCK
    return pl.pallas_call(
        _expert_kernel,
        grid_spec=pltpu.PrefetchScalarGridSpec(
            num_scalar_prefetch=2,
            grid=(nb,),
            in_specs=[
                pl.BlockSpec((MOE_BLOCK, D_MODEL), lambda b, be, nu: (jnp.minimum(b, nu[0] - 1), 0)),
                pl.BlockSpec((1, D_MODEL, EXPERT_HIDDEN), lambda b, be, nu: (be[b], 0, 0)),
                pl.BlockSpec((1, D_MODEL, EXPERT_HIDDEN), lambda b, be, nu: (be[b], 0, 0)),
                pl.BlockSpec((1, EXPERT_HIDDEN, D_MODEL), lambda b, be, nu: (be[b], 0, 0)),
            ],
            out_specs=pl.BlockSpec((MOE_BLOCK, D_MODEL), lambda b, be, nu: (b, 0)),
            scratch_shapes=[pltpu.VMEM((D_MODEL, 2 * EXPERT_HIDDEN), BF16), pltpu.VMEM((EXPERT_HIDDEN, D_MODEL), BF16)],
        ),
        out_shape=jax.ShapeDtypeStruct((rows, D_MODEL), F32),
        compiler_params=_params(("arbitrary",)),
        name="experts",
    )(block_e, n_used, xs, w_gate, w_up, w_down)


def _final_kernel(h_ref, r_ref, wgu_ref, wd_ref, g_ref, b_ref, y_ref):
    h = h_ref[...]
    gu = jnp.dot(h.astype(BF16), wgu_ref[...], preferred_element_type=F32)
    hid = (_silu(gu[:, :SHARED_HIDDEN]) * gu[:, SHARED_HIDDEN:]).astype(BF16)
    shared = jnp.dot(hid, wd_ref[...], preferred_element_type=F32)
    y_ref[...] = _layer_norm(DN_ALPHA * h + (r_ref[...] + shared), g_ref[...], b_ref[...])


def _final(h1, routed, w_sh_gu, w_sh_down, ln_g, ln_b):
    n = h1.shape[0]
    t = TILE_FINAL
    row = lambda i: (i, 0)
    return pl.pallas_call(
        _final_kernel,
        grid=(n // t,),
        in_specs=[
            pl.BlockSpec((t, D_MODEL), row), pl.BlockSpec((t, D_MODEL), row),
            _const_spec((D_MODEL, 2 * SHARED_HIDDEN)), _const_spec((SHARED_HIDDEN, D_MODEL)),
            _const_spec((1, D_MODEL)), _const_spec((1, D_MODEL)),
        ],
        out_specs=pl.BlockSpec((t, D_MODEL), row),
        out_shape=jax.ShapeDtypeStruct((n, D_MODEL), F32),
        compiler_params=_params(("parallel",)),
        name="final",
    )(h1, routed, w_sh_gu, w_sh_down, ln_g, ln_b)


def _block_diag(w):
    g, a, b = w.shape
    out = jnp.zeros((g * a, g * b), w.dtype)
    for i in range(g):
        out = out.at[i * a:(i + 1) * a, i * b:(i + 1) * b].set(w[i])
    return out


def kernel(x_prompt, x_sample, ln0_g, ln0_b, w_in, pool_w, pool_scale, w_pool_up, gate_w2_fwd, gate_b_fwd,
           gate_w2_bwd, gate_b_bwd, gla_norm_g, w_gla_up, w_out, ln1_g, ln1_b, w_router, router_bias,
           w_exp_gate, w_exp_up, w_exp_down, w_sh_gate, w_sh_up, w_sh_down, ln2_g, ln2_b):
    bp, lp, d = x_prompt.shape
    bs, ls, _ = x_sample.shape
    n_p, n_s = bp * lp, bs * ls
    n = n_p + n_s
    starts = tuple(b * lp for b in range(bp)) + tuple(n_p + b * ls for b in range(bs))
    ends = tuple((b + 1) * lp for b in range(bp)) + tuple(n_p + (b + 1) * ls for b in range(bs))
    vec = lambda a: a.reshape(1, -1).astype(F32)

    w = w_in[0]
    c_u, c_q, c_k, c_v, c_og = 0, 512, 1024, 1536, 2560
    c_zf, c_zb, c_gp, c_gg = 3584, 3600, 3616, 4640
    w_main = jnp.concatenate([w[:, c_u:c_zf], w[:, c_gp:]], axis=1).astype(BF16)
    w_z = jnp.zeros((d, Z_PAD), F32).at[:, :2 * GATE_RANK].set(w[:, c_zf:c_gp]).astype(BF16)
    w2f = jnp.zeros((Z_PAD, GLA_K_W), F32).at[:GATE_RANK].set(gate_w2_fwd[0]).astype(BF16)
    w2b = jnp.zeros((Z_PAD, GLA_K_W), F32).at[GATE_RANK:2 * GATE_RANK].set(gate_w2_bwd[0]).astype(BF16)

    h, u, q, k, v, og, gp, gg, z = _inproj(x_prompt.reshape(n_p, d), x_sample.reshape(n_s, d),
                                           vec(ln0_g), vec(ln0_b), w_main, w_z)
    pm = _pool(u, gp, _block_diag(pool_w[0]).astype(BF16), vec(pool_scale[0]), w_pool_up[0].astype(BF16),
               starts, ends)
    ob = _gla_bwd(q, k, v, z, w2b, vec(gate_b_bwd[0]), ends)
    h1 = _gla_fwd_mix(q, k, v, z, w2f, vec(gate_b_fwd[0]), ob, og, gg, pm, h, vec(gla_norm_g[0]),
                      w_gla_up[0].astype(BF16), w_out[0].astype(BF16), vec(ln1_g[0]), vec(ln1_b[0]), starts)

    wr_t = w_router[0].T
    wr_hi = wr_t.astype(BF16)
    wr_lo = (wr_t - wr_hi.astype(F32)).astype(BF16)
    rbias = jnp.broadcast_to(router_bias[0].astype(F32)[:, None], (N_EXPERTS, 128))
    idx, wts, rank, counts = _router(h1, wr_hi, wr_lo, rbias)

    padded = (counts + MOE_BLOCK - 1) // MOE_BLOCK * MOE_BLOCK
    pend = jnp.cumsum(padded)
    pstart = pend - padded
    n_rows = n * TOP_K + N_EXPERTS * MOE_BLOCK
    n_blocks = n_rows // MOE_BLOCK
    block_e = jnp.minimum(jnp.searchsorted(pend, jnp.arange(n_blocks, dtype=jnp.int32) * MOE_BLOCK, side='right'),
                          N_EXPERTS - 1).astype(jnp.int32)
    n_used = (pend[-1:] // MOE_BLOCK).astype(jnp.int32)
    dest = (pstart[idx] + rank).astype(jnp.int32)

    tok = jnp.broadcast_to(jnp.arange(n, dtype=jnp.int32)[None, :], (TOP_K, n))
    row_tok = jnp.zeros((n_rows,), jnp.int32).at[dest.reshape(-1)].set(tok.reshape(-1))
    xs = h1.astype(BF16)[row_tok]
    ys = _experts(block_e, n_used, xs, w_exp_gate[0], w_exp_up[0], w_exp_down[0])
    routed = jnp.einsum('ktd,kt->td', ys[dest.reshape(-1)].reshape(TOP_K, n, d), wts)

    y = _final(h1, routed, jnp.concatenate([w_sh_gate[0], w_sh_up[0]], axis=1).astype(BF16),
               w_sh_down[0].astype(BF16), vec(ln2_g[0]), vec(ln2_b[0]))
    return y[:n_p].reshape(bp, lp, d), y[n_p:].reshape(bs, ls, d)
```

```python
import functools

import jax
import jax.numpy as jnp
from jax import lax
from jax.experimental import pallas as pl
from jax.experimental.pallas import tpu as pltpu
from jax.experimental.pallas import tpu_sc as plsc

F32 = jnp.float32
BF16 = jnp.bfloat16

D_MODEL = 1024
POOL_GROUPS = 4
POOL_GROUP_W = 128
POOL_W = POOL_GROUPS * POOL_GROUP_W
POOL_WINDOWS = (2, 4, 8, 16)
POOL_HALO = 16
GLA_HEADS = 4
GLA_K_W = 512
GLA_V_W = 1024
GLA_DK = GLA_K_W // GLA_HEADS
GLA_DV = GLA_V_W // GLA_HEADS
GATE_RANK = 16
GATE_TAU = 16.0
GLA_CHUNK = 64
Z_PAD = 128
N_EXPERTS = 256
TOP_K = 8
N_GROUPS = 8
TOPK_GROUPS = 4
EXPERTS_PER_GROUP = N_EXPERTS // N_GROUPS
EXPERT_HIDDEN = 256
SHARED_HIDDEN = 256
ROUTED_SCALE = 2.5
DEPTH = 1
DN_ALPHA = (2 * DEPTH) ** 0.25
LN_EPS = 1e-5
RMS_EPS = 1e-6

VMEM_LIMIT_BYTES = 56 * 1024 * 1024

TILE_INPROJ = 512
TILE_POOL = 512
TILE_GLA = 256
TILE_ROUTER = 512
TILE_FINAL = 256
MOE_BLOCK = 256
LANES = 128
ROW_TILES = D_MODEL // LANES
SC_WINDOW = 32


def _params(semantics):
    return pltpu.CompilerParams(dimension_semantics=semantics, vmem_limit_bytes=VMEM_LIMIT_BYTES)


def _const_spec(shape):
    nd = len(shape)
    return pl.BlockSpec(shape, lambda *_: (0,) * nd)


def _layer_norm(x, g, b):
    mu = jnp.mean(x, axis=-1, keepdims=True)
    xc = x - mu
    var = jnp.mean(xc * xc, axis=-1, keepdims=True)
    return xc * lax.rsqrt(var + LN_EPS) * g + b


def _sigmoid(x):
    return 1.0 / (1.0 + jnp.exp(-x))


def _log_sigmoid(x):
    return jnp.minimum(x, 0.0) - jnp.log(1.0 + jnp.exp(-jnp.abs(x)))


def _silu(x):
    return x * _sigmoid(x)


def _tile_hits(i, tile, offsets):
    hit = i < 0
    for off in offsets:
        if off % tile == 0:
            hit = jnp.logical_or(hit, i == off // tile)
    return hit


_MAIN_SPLITS = (POOL_W, GLA_K_W, GLA_K_W, GLA_V_W, GLA_V_W, D_MODEL, D_MODEL)


def _inproj_kernel(xp_ref, xs_ref, g_ref, b_ref, w_ref, wz_ref,
                   h_ref, u_ref, q_ref, k_ref, v_ref, og_ref, gp_ref, gg_ref, z_ref, *, n_prompt_tiles):
    i = pl.program_id(0)
    x = jnp.where(i < n_prompt_tiles, xp_ref[...], xs_ref[...])
    h = _layer_norm(x, g_ref[...], b_ref[...])
    h_ref[...] = h
    hb = h.astype(BF16)
    off = 0
    for ref, width in zip((u_ref, q_ref, k_ref, v_ref, og_ref, gp_ref, gg_ref), _MAIN_SPLITS):
        ref[...] = jnp.dot(hb, w_ref[:, off:off + width], preferred_element_type=F32).astype(BF16)
        off += width
    z_ref[...] = jnp.dot(hb, wz_ref[...], preferred_element_type=F32).astype(BF16)


def _inproj(xp, xs, ln_g, ln_b, w_main, w_z):
    n_p, n_s = xp.shape[0], xs.shape[0]
    n = n_p + n_s
    t = TILE_INPROJ
    npt, nst = n_p // t, n_s // t
    row = lambda i: (i, 0)
    widths = _MAIN_SPLITS + (Z_PAD,)
    out_shape = [jax.ShapeDtypeStruct((n, D_MODEL), F32)] + [jax.ShapeDtypeStruct((n, w), BF16) for w in widths]
    out_specs = [pl.BlockSpec((t, D_MODEL), row)] + [pl.BlockSpec((t, w), row) for w in widths]
    return pl.pallas_call(
        functools.partial(_inproj_kernel, n_prompt_tiles=npt),
        grid=(npt + nst,),
        in_specs=[
            pl.BlockSpec((t, D_MODEL), lambda i: (jnp.minimum(i, npt - 1), 0)),
            pl.BlockSpec((t, D_MODEL), lambda i: (jnp.maximum(i - npt, 0), 0)),
            _const_spec((1, D_MODEL)), _const_spec((1, D_MODEL)),
            _const_spec(w_main.shape), _const_spec(w_z.shape),
        ],
        out_specs=out_specs,
        out_shape=out_shape,
        compiler_params=_params(("parallel",)),
        name="inproj",
    )(xp, xs, ln_g, ln_b, w_main, w_z)


def _pool_kernel(u_ref, up_ref, un_ref, gp_ref, band_ref, pw_ref, sc_ref, wup_ref, pm_ref, *, starts, ends):
    t = u_ref.shape[0]
    i = pl.program_id(0)
    is_start = _tile_hits(i, t, starts)
    is_end = _tile_hits(i + 1, t, ends)
    cur = u_ref[...]
    prev = jnp.where(is_start, jnp.zeros_like(up_ref[...]), up_ref[...])
    nxt = jnp.where(is_end, jnp.zeros_like(un_ref[...]), un_ref[...])
    ext = jnp.concatenate([prev, cur, nxt], axis=0)
    r = lax.broadcasted_iota(jnp.int32, (t, POOL_GROUP_W), 0).astype(F32)
    fs = is_start.astype(F32)
    fe = is_end.astype(F32)
    parts = []
    for gi, w in enumerate(POOL_WINDOWS):
        sl = slice(gi * POOL_GROUP_W, (gi + 1) * POOL_GROUP_W)
        s = jnp.dot(band_ref[gi], ext[:, sl], preferred_element_type=F32)
        cnt = w - fs * jnp.maximum(w // 2 - r, 0.0) - fe * jnp.maximum(r + (w // 2 - t), 0.0)
        parts.append(s / cnt - cur[:, sl].astype(F32))
    p = jnp.concatenate(parts, axis=1).astype(BF16)
    p = (jnp.dot(p, pw_ref[...], preferred_element_type=F32) * sc_ref[...]).astype(BF16)
    pool_out = jnp.dot(p, wup_ref[...], preferred_element_type=F32)
    pm_ref[...] = (_sigmoid(gp_ref[...].astype(F32)) * pool_out).astype(BF16)


def _pool_bands(t):
    r = jnp.arange(t)[:, None]
    c = jnp.arange(t + 2 * POOL_HALO)[None, :] - POOL_HALO
    return jnp.stack([((c >= r - w // 2) & (c <= r + w // 2 - 1)).astype(BF16) for w in POOL_WINDOWS])


def _pool(u, gp, pool_w_bd, pool_scale, w_pool_up, starts, ends):
    n = u.shape[0]
    t = TILE_POOL
    hb = t // POOL_HALO
    last = n // POOL_HALO - 1
    return pl.pallas_call(
        functools.partial(_pool_kernel, starts=starts, ends=ends),
        grid=(n // t,),
        in_specs=[
            pl.BlockSpec((t, POOL_W), lambda i: (i, 0)),
            pl.BlockSpec((POOL_HALO, POOL_W), lambda i: (jnp.maximum(i * hb - 1, 0), 0)),
            pl.BlockSpec((POOL_HALO, POOL_W), lambda i: (jnp.minimum((i + 1) * hb, last), 0)),
            pl.BlockSpec((t, D_MODEL), lambda i: (i, 0)),
            _const_spec((POOL_GROUPS, t, t + 2 * POOL_HALO)),
            _const_spec((POOL_W, POOL_W)), _const_spec((1, POOL_W)), _const_spec((POOL_W, D_MODEL)),
        ],
        out_specs=pl.BlockSpec((t, D_MODEL), lambda i: (i, 0)),
        out_shape=jax.ShapeDtypeStruct((n, D_MODEL), BF16),
        compiler_params=_params(("parallel",)),
        name="pool",
    )(u, u, u, gp, _pool_bands(t), pool_w_bd, pool_scale, w_pool_up)


def _gla_tile(q, k, v, la, tri, st_ref, *, reverse):
    t = q.shape[0]
    c_sz = GLA_CHUNK
    hi = la.astype(BF16)
    r1 = la - hi.astype(F32)
    mid = r1.astype(BF16)
    lo = (r1 - mid.astype(F32)).astype(BF16)
    b = (jnp.dot(tri, hi, preferred_element_type=F32) + jnp.dot(tri, mid, preferred_element_type=F32)
         + jnp.dot(tri, lo, preferred_element_type=F32))
    kf = k.astype(F32)
    qd = (q.astype(F32) * jnp.exp(b) * (GLA_DK ** -0.5)).astype(BF16)
    kd = (kf * jnp.exp(-b)).astype(BF16)
    row = lax.broadcasted_iota(jnp.int32, (c_sz, c_sz), 0)
    col = lax.broadcasted_iota(jnp.int32, (c_sz, c_sz), 1)
    mask = (col >= row) if reverse else (col <= row)
    n_chunks = t // c_sz
    outs = [None] * n_chunks
    nt = (((1,), (1,)), ((), ()))
    tn = (((0,), (0,)), ((), ()))
    for c in (range(n_chunks - 1, -1, -1) if reverse else range(n_chunks)):
        rows = slice(c * c_sz, (c + 1) * c_sz)
        b_c = b[rows]
        b_edge = b_c[0:1] if reverse else b_c[c_sz - 1:c_sz]
        k2 = (kf[rows] * jnp.exp(b_edge - b_c)).astype(BF16)
        dec = jnp.exp(b_edge)
        heads = []
        for h in range(GLA_HEADS):
            ls = slice(h * GLA_DK, (h + 1) * GLA_DK)
            vs = slice(h * GLA_DV, (h + 1) * GLA_DV)
            qh = qd[rows, ls]
            att = lax.dot_general(qh, kd[rows, ls], nt, preferred_element_type=F32)
            att = jnp.where(mask, att, 0.0).astype(BF16)
            vh = v[rows, vs]
            st = st_ref[h]
            o = jnp.dot(att, vh, preferred_element_type=F32)
            o = o + lax.dot_general(qh, st.astype(BF16), nt, preferred_element_type=F32)
            st_ref[h] = st * dec[:, ls] + lax.dot_general(vh, k2[:, ls], tn, preferred_element_type=F32)
            heads.append(o)
        outs[c] = jnp.concatenate(heads, axis=1)
    return jnp.concatenate(outs, axis=0)


def _log_decay(z, w2_ref, gb_ref):
    return _log_sigmoid(jnp.dot(z, w2_ref[...], preferred_element_type=F32) + gb_ref[...]) * (1.0 / GATE_TAU)


def _gla_bwd_kernel(q_ref, k_ref, v_ref, z_ref, w2_ref, gb_ref, tri_ref, ob_ref, st_ref, *, n_tiles, ends):
    t = q_ref.shape[0]
    j = n_tiles - 1 - pl.program_id(0)

    @pl.when(_tile_hits(j + 1, t, ends))
    def _():
        st_ref[...] = jnp.zeros_like(st_ref)

    la = _log_decay(z_ref[...], w2_ref, gb_ref)
    ob_ref[...] = _gla_tile(q_ref[...], k_ref[...], v_ref[...], la, tri_ref[...], st_ref, reverse=True)


def _gla_fwd_kernel(q_ref, k_ref, v_ref, z_ref, w2_ref, gb_ref, tri_ref, ob_ref, og_ref, gg_ref, pm_ref, h_ref,
                    ng_ref, wgu_ref, wo_ref, l1g_ref, l1b_ref, h1_ref, h1rc_ref, st_ref, *, starts):
    t = q_ref.shape[0]
    i = pl.program_id(0)

    @pl.when(_tile_hits(i, t, starts))
    def _():
        st_ref[...] = jnp.zeros_like(st_ref)

    la = _log_decay(z_ref[...], w2_ref, gb_ref)
    o = _gla_tile(q_ref[...], k_ref[...], v_ref[...], la, tri_ref[...], st_ref, reverse=False) + ob_ref[...]
    heads = []
    for h in range(GLA_HEADS):
        oh = o[:, h * GLA_DV:(h + 1) * GLA_DV]
        heads.append(oh * lax.rsqrt(jnp.mean(oh * oh, axis=-1, keepdims=True) + RMS_EPS))
    o = jnp.concatenate(heads, axis=1) * ng_ref[...]
    o = (o * _silu(og_ref[...].astype(F32))).astype(BF16)
    gla_out = jnp.dot(o, wgu_ref[...], preferred_element_type=F32)
    merged = _sigmoid(gg_ref[...].astype(F32)) * gla_out + pm_ref[...].astype(F32)
    mix = jnp.dot(merged.astype(BF16), wo_ref[...], preferred_element_type=F32)
    h1 = _layer_norm(DN_ALPHA * h_ref[...] + mix, l1g_ref[...], l1b_ref[...])
    h1_ref[...] = h1
    _store_row_contiguous(h1rc_ref, h1)


def _store_row_contiguous(ref, x):
    t = x.shape[0]
    for s in range(ROW_TILES):
        ref[pl.ds(s, t, stride=ROW_TILES), :] = x[:, s * LANES:(s + 1) * LANES]


def _load_row_contiguous(ref, t):
    return jnp.concatenate([ref[pl.ds(s, t, stride=ROW_TILES), :] for s in range(ROW_TILES)], axis=1)


def _chunk_tri(t, reverse):
    r = jnp.arange(t)[:, None]
    c = jnp.arange(t)[None, :]
    same = (r // GLA_CHUNK) == (c // GLA_CHUNK)
    return (same & ((c >= r) if reverse else (c <= r))).astype(BF16)


def _gla_bwd(q, k, v, z, w2b_pad, gate_b, ends):
    n = q.shape[0]
    t = TILE_GLA
    nt = n // t
    rev = lambda i: (nt - 1 - i, 0)
    return pl.pallas_call(
        functools.partial(_gla_bwd_kernel, n_tiles=nt, ends=ends),
        grid=(nt,),
        in_specs=[
            pl.BlockSpec((t, GLA_K_W), rev), pl.BlockSpec((t, GLA_K_W), rev), pl.BlockSpec((t, GLA_V_W), rev),
            pl.BlockSpec((t, Z_PAD), rev),
            _const_spec((Z_PAD, GLA_K_W)), _const_spec((1, GLA_K_W)), _const_spec((t, t)),
        ],
        out_specs=pl.BlockSpec((t, GLA_V_W), rev),
        out_shape=jax.ShapeDtypeStruct((n, GLA_V_W), F32),
        scratch_shapes=[pltpu.VMEM((GLA_HEADS, GLA_DV, GLA_DK), F32)],
        compiler_params=_params(("arbitrary",)),
        name="gla_bwd",
    )(q, k, v, z, w2b_pad, gate_b, _chunk_tri(t, True))


def _gla_fwd_mix(q, k, v, z, w2f_pad, gate_b, ob, og, gg, pm, h, norm_g, w_gla_up, w_out, ln1_g, ln1_b, starts):
    n = q.shape[0]
    t = TILE_GLA
    row = lambda i: (i, 0)
    return pl.pallas_call(
        functools.partial(_gla_fwd_kernel, starts=starts),
        grid=(n // t,),
        in_specs=[
            pl.BlockSpec((t, GLA_K_W), row), pl.BlockSpec((t, GLA_K_W), row), pl.BlockSpec((t, GLA_V_W), row),
            pl.BlockSpec((t, Z_PAD), row),
            _const_spec((Z_PAD, GLA_K_W)), _const_spec((1, GLA_K_W)), _const_spec((t, t)),
            pl.BlockSpec((t, GLA_V_W), row), pl.BlockSpec((t, GLA_V_W), row), pl.BlockSpec((t, D_MODEL), row),
            pl.BlockSpec((t, D_MODEL), row), pl.BlockSpec((t, D_MODEL), row),
            _const_spec((1, GLA_V_W)), _const_spec((GLA_V_W, D_MODEL)), _const_spec((D_MODEL, D_MODEL)),
            _const_spec((1, D_MODEL)), _const_spec((1, D_MODEL)),
        ],
        out_specs=[pl.BlockSpec((t, D_MODEL), row), pl.BlockSpec((t * ROW_TILES, LANES), row)],
        out_shape=[jax.ShapeDtypeStruct((n, D_MODEL), F32), jax.ShapeDtypeStruct((n * ROW_TILES, LANES), F32)],
        scratch_shapes=[pltpu.VMEM((GLA_HEADS, GLA_DV, GLA_DK), F32)],
        compiler_params=_params(("arbitrary",)),
        name="gla_fwd_mix",
    )(q, k, v, z, w2f_pad, gate_b, _chunk_tri(t, False), ob, og, gg, pm, h, norm_g, w_gla_up, w_out, ln1_g, ln1_b)


def _router_kernel(h_ref, wrh_ref, wrl_ref, rb_ref, tri_ref, idx_ref, wts_ref, rank_ref, cnt_ref, base_ref):
    t = h_ref.shape[0]

    @pl.when(pl.program_id(0) == 0)
    def _():
        base_ref[...] = jnp.zeros_like(base_ref)

    h = h_ref[...]
    hh = h.astype(BF16)
    hl = (h - hh.astype(F32)).astype(BF16)
    nt = (((1,), (1,)), ((), ()))
    logits = (lax.dot_general(wrh_ref[...], hh, nt, preferred_element_type=F32)
              + lax.dot_general(wrl_ref[...], hh, nt, preferred_element_type=F32)
              + lax.dot_general(wrh_ref[...], hl, nt, preferred_element_type=F32))
    scores = _sigmoid(logits)
    sel = scores + rb_ref[:, 0:1]
    neg = jnp.float32(-jnp.inf)

    sub = lax.broadcasted_iota(jnp.int32, (EXPERTS_PER_GROUP, t), 0)
    gscore = []
    for g in range(N_GROUPS):
        sg = sel[g * EXPERTS_PER_GROUP:(g + 1) * EXPERTS_PER_GROUP]
        m1 = jnp.max(sg, axis=0, keepdims=True)
        first = jnp.min(jnp.where(sg == m1, sub, EXPERTS_PER_GROUP), axis=0, keepdims=True)
        m2 = jnp.max(jnp.where(sub == first, neg, sg), axis=0, keepdims=True)
        gscore.append(m1 + m2)
    masked = []
    for g in range(N_GROUPS):
        beaten = jnp.zeros((1, t), jnp.int32)
        for g2 in range(N_GROUPS):
            if g2 == g:
                continue
            wins = (gscore[g2] >= gscore[g]) if g2 < g else (gscore[g2] > gscore[g])
            beaten = beaten + wins.astype(jnp.int32)
        keep = beaten < TOPK_GROUPS
        masked.append(jnp.where(keep, sel[g * EXPERTS_PER_GROUP:(g + 1) * EXPERTS_PER_GROUP], neg))
    cand = jnp.concatenate(masked, axis=0)

    eid = lax.broadcasted_iota(jnp.int32, (N_EXPERTS, t), 0)
    picked = []
    assign = jnp.zeros((N_EXPERTS, t), F32)
    wsum = jnp.zeros((1, t), F32)
    for kk in range(TOP_K):
        m = jnp.max(cand, axis=0, keepdims=True)
        e_k = jnp.min(jnp.where(cand == m, eid, N_EXPERTS), axis=0, keepdims=True)
        hit = eid == e_k
        w_k = jnp.sum(jnp.where(hit, scores, 0.0), axis=0, keepdims=True)
        cand = jnp.where(hit, neg, cand)
        assign = jnp.where(hit, 1.0, assign)
        wsum = wsum + w_k
        picked.append((e_k, w_k))
        idx_ref[kk:kk + 1, :] = e_k

    a16 = assign.astype(BF16)
    before = jnp.dot(a16, tri_ref[...], preferred_element_type=F32)
    base = base_ref[...]
    before = before + jnp.concatenate([base] * (t // 128), axis=1)
    inv = ROUTED_SCALE / wsum
    for kk, (e_k, w_k) in enumerate(picked):
        wts_ref[kk:kk + 1, :] = w_k * inv
        rank_ref[kk:kk + 1, :] = jnp.sum(jnp.where(eid == e_k, before, 0.0), axis=0, keepdims=True).astype(jnp.int32)
    total = base + jnp.dot(a16, jnp.ones((t, 128), BF16), preferred_element_type=F32)
    base_ref[...] = total
    cnt_ref[...] = total


def _router(h1, wr_hi, wr_lo, rbias):
    n = h1.shape[0]
    t = TILE_ROUTER
    r = jnp.arange(t)
    tri = (r[:, None] < r[None, :]).astype(BF16)
    col = lambda i: (0, i)
    idx, wts, rank, cnt = pl.pallas_call(
        _router_kernel,
        grid=(n // t,),
        in_specs=[
            pl.BlockSpec((t, D_MODEL), lambda i: (i, 0)),
            _const_spec((N_EXPERTS, D_MODEL)), _const_spec((N_EXPERTS, D_MODEL)),
            _const_spec((N_EXPERTS, 128)), _const_spec((t, t)),
        ],
        out_specs=[pl.BlockSpec((TOP_K, t), col), pl.BlockSpec((TOP_K, t), col), pl.BlockSpec((TOP_K, t), col),
                   _const_spec((N_EXPERTS, 128))],
        out_shape=[jax.ShapeDtypeStruct((TOP_K, n), jnp.int32), jax.ShapeDtypeStruct((TOP_K, n), F32),
                   jax.ShapeDtypeStruct((TOP_K, n), jnp.int32), jax.ShapeDtypeStruct((N_EXPERTS, 128), F32)],
        scratch_shapes=[pltpu.VMEM((N_EXPERTS, 128), F32)],
        compiler_params=_params(("arbitrary",)),
        name="router",
    )(h1, wr_hi, wr_lo, rbias, tri)
    return idx, wts, rank, cnt[:, 0].astype(jnp.int32)


def _expert_kernel(be_ref, nu_ref, x_ref, wg_ref, wu_ref, wd_ref, y_ref, wgu_s, wd_s):
    b = pl.program_id(0)
    prev = be_ref[jnp.maximum(b - 1, 0)]
    fresh = jnp.logical_or(b == 0, be_ref[b] != prev)

    @pl.when(jnp.logical_and(fresh, b < nu_ref[0]))
    def _():
        wgu_s[:, :EXPERT_HIDDEN] = wg_ref[0].astype(BF16)
        wgu_s[:, EXPERT_HIDDEN:] = wu_ref[0].astype(BF16)
        wd_s[...] = wd_ref[0].astype(BF16)

    @pl.when(b < nu_ref[0])
    def _():
        x = _load_row_contiguous(x_ref, MOE_BLOCK).astype(BF16)
        gu = jnp.dot(x, wgu_s[...], preferred_element_type=F32)
        hid = (_silu(gu[:, :EXPERT_HIDDEN]) * gu[:, EXPERT_HIDDEN:]).astype(BF16)
        _store_row_contiguous(y_ref, jnp.dot(hid, wd_s[...], preferred_element_type=F32))


def _experts(block_e, n_used, xs, w_gate, w_up, w_down):
    blk = MOE_BLOCK * ROW_TILES
    nb = xs.shape[0] // blk
    return pl.pallas_call(
        _expert_kernel,
        grid_spec=pltpu.PrefetchScalarGridSpec(
            num_scalar_prefetch=2,
            grid=(nb,),
            in_specs=[
                pl.BlockSpec((blk, LANES), lambda b, be, nu: (jnp.minimum(b, nu[0] - 1), 0)),
                pl.BlockSpec((1, D_MODEL, EXPERT_HIDDEN), lambda b, be, nu: (be[b], 0, 0)),
                pl.BlockSpec((1, D_MODEL, EXPERT_HIDDEN), lambda b, be, nu: (be[b], 0, 0)),
                pl.BlockSpec((1, EXPERT_HIDDEN, D_MODEL), lambda b, be, nu: (be[b], 0, 0)),
            ],
            out_specs=pl.BlockSpec((blk, LANES), lambda b, be, nu: (b, 0)),
            scratch_shapes=[pltpu.VMEM((D_MODEL, 2 * EXPERT_HIDDEN), BF16), pltpu.VMEM((EXPERT_HIDDEN, D_MODEL), BF16)],
        ),
        out_shape=jax.ShapeDtypeStruct(xs.shape, F32),
        compiler_params=_params(("arbitrary",)),
        name="experts",
    )(block_e, n_used, xs, w_gate, w_up, w_down)


def _sc_mesh():
    return plsc.VectorSubcoreMesh(core_axis_name="c", subcore_axis_name="s")


def _sc_dispatch(x3, dest_w, n_rows):
    n = x3.shape[0]

    @functools.partial(pl.kernel, out_type=jax.ShapeDtypeStruct((n_rows, ROW_TILES, LANES), F32), mesh=_sc_mesh(),
                       name="sc_dispatch")
    def run(x_hbm, d_hbm, o_hbm):
        def body(x_vmem, d_vmem):
            for kk in range(TOP_K):
                pltpu.sync_copy(x_vmem, o_hbm.at[d_vmem.at[0].at[kk]])

        pltpu.emit_pipeline(
            body,
            grid=(n // SC_WINDOW,),
            in_specs=[pl.BlockSpec((SC_WINDOW, ROW_TILES, LANES), lambda i: (i, 0, 0)),
                      pl.BlockSpec((1, TOP_K, SC_WINDOW), lambda i: (i, 0, 0))],
            out_specs=[],
            core_axis_name=("c", "s"),
            dimension_semantics=(pltpu.PARALLEL,),
        )(x_hbm, d_hbm)

    return run(x3, dest_w)


def _sc_gather(y3, dest_w):
    nw = dest_w.shape[0]
    n = nw * SC_WINDOW

    @functools.partial(pl.kernel, out_type=jax.ShapeDtypeStruct((TOP_K, n, ROW_TILES, LANES), F32), mesh=_sc_mesh(),
                       name="sc_gather")
    def run(y_hbm, d_hbm, o_hbm):
        def body(d_vmem, g_vmem):
            pltpu.sync_copy(y_hbm.at[d_vmem.at[0].at[0]], g_vmem.at[0])

        pltpu.emit_pipeline(
            body,
            grid=(nw, TOP_K),
            in_specs=[pl.BlockSpec((1, 1, SC_WINDOW), lambda i, k: (i, k, 0))],
            out_specs=[pl.BlockSpec((1, SC_WINDOW, ROW_TILES, LANES), lambda i, k: (k, i, 0, 0))],
            core_axis_name=("c", "s"),
            dimension_semantics=(pltpu.PARALLEL, pltpu.PARALLEL),
        )(d_hbm, o_hbm)

    return run(y3, dest_w)


def _final_kernel(h_ref, g_ref, w_ref, wgu_ref, wd_ref, lg_ref, lb_ref, y_ref):
    t = h_ref.shape[0]
    h = h_ref[...]
    gu = jnp.dot(h.astype(BF16), wgu_ref[...], preferred_element_type=F32)
    hid = (_silu(gu[:, :SHARED_HIDDEN]) * gu[:, SHARED_HIDDEN:]).astype(BF16)
    ffn = jnp.dot(hid, wd_ref[...], preferred_element_type=F32)
    w = w_ref[...]
    for kk in range(TOP_K):
        ffn = ffn + _load_row_contiguous(g_ref.at[kk], t) * w[:, kk:kk + 1]
    y_ref[...] = _layer_norm(DN_ALPHA * h + ffn, lg_ref[...], lb_ref[...])


def _final(h1, gathered, wts_t, w_sh_gu, w_sh_down, ln_g, ln_b):
    n = h1.shape[0]
    t = TILE_FINAL
    row = lambda i: (i, 0)
    return pl.pallas_call(
        _final_kernel,
        grid=(n // t,),
        in_specs=[
            pl.BlockSpec((t, D_MODEL), row),
            pl.BlockSpec((TOP_K, t * ROW_TILES, LANES), lambda i: (0, i, 0)),
            pl.BlockSpec((t, TOP_K), row),
            _const_spec((D_MODEL, 2 * SHARED_HIDDEN)), _const_spec((SHARED_HIDDEN, D_MODEL)),
            _const_spec((1, D_MODEL)), _const_spec((1, D_MODEL)),
        ],
        out_specs=pl.BlockSpec((t, D_MODEL), row),
        out_shape=jax.ShapeDtypeStruct((n, D_MODEL), F32),
        compiler_params=_params(("parallel",)),
        name="final",
    )(h1, gathered, wts_t, w_sh_gu, w_sh_down, ln_g, ln_b)


def _block_diag(w):
    g, a, b = w.shape
    out = jnp.zeros((g * a, g * b), w.dtype)
    for i in range(g):
        out = out.at[i * a:(i + 1) * a, i * b:(i + 1) * b].set(w[i])
    return out


def kernel(x_prompt, x_sample, ln0_g, ln0_b, w_in, pool_w, pool_scale, w_pool_up, gate_w2_fwd, gate_b_fwd,
           gate_w2_bwd, gate_b_bwd, gla_norm_g, w_gla_up, w_out, ln1_g, ln1_b, w_router, router_bias,
           w_exp_gate, w_exp_up, w_exp_down, w_sh_gate, w_sh_up, w_sh_down, ln2_g, ln2_b):
    bp, lp, d = x_prompt.shape
    bs, ls, _ = x_sample.shape
    n_p, n_s = bp * lp, bs * ls
    n = n_p + n_s
    starts = tuple(b * lp for b in range(bp)) + tuple(n_p + b * ls for b in range(bs))
    ends = tuple((b + 1) * lp for b in range(bp)) + tuple(n_p + (b + 1) * ls for b in range(bs))
    vec = lambda a: a.reshape(1, -1).astype(F32)

    w = w_in[0]
    c_u, c_q, c_k, c_v, c_og = 0, 512, 1024, 1536, 2560
    c_zf, c_zb, c_gp, c_gg = 3584, 3600, 3616, 4640
    w_main = jnp.concatenate([w[:, c_u:c_zf], w[:, c_gp:]], axis=1).astype(BF16)
    w_z = jnp.zeros((d, Z_PAD), F32).at[:, :2 * GATE_RANK].set(w[:, c_zf:c_gp]).astype(BF16)
    w2f = jnp.zeros((Z_PAD, GLA_K_W), F32).at[:GATE_RANK].set(gate_w2_fwd[0]).astype(BF16)
    w2b = jnp.zeros((Z_PAD, GLA_K_W), F32).at[GATE_RANK:2 * GATE_RANK].set(gate_w2_bwd[0]).astype(BF16)

    h, u, q, k, v, og, gp, gg, z = _inproj(x_prompt.reshape(n_p, d), x_sample.reshape(n_s, d),
                                           vec(ln0_g), vec(ln0_b), w_main, w_z)
    pm = _pool(u, gp, _block_diag(pool_w[0]).astype(BF16), vec(pool_scale[0]), w_pool_up[0].astype(BF16),
               starts, ends)
    ob = _gla_bwd(q, k, v, z, w2b, vec(gate_b_bwd[0]), ends)
    h1, h1_rc = _gla_fwd_mix(q, k, v, z, w2f, vec(gate_b_fwd[0]), ob, og, gg, pm, h, vec(gla_norm_g[0]),
                             w_gla_up[0].astype(BF16), w_out[0].astype(BF16), vec(ln1_g[0]), vec(ln1_b[0]), starts)

    wr_t = w_router[0].T
    wr_hi = wr_t.astype(BF16)
    wr_lo = (wr_t - wr_hi.astype(F32)).astype(BF16)
    rbias = jnp.broadcast_to(router_bias[0].astype(F32)[:, None], (N_EXPERTS, 128))
    idx, wts, rank, counts = _router(h1, wr_hi, wr_lo, rbias)

    padded = (counts + MOE_BLOCK - 1) // MOE_BLOCK * MOE_BLOCK
    pend = jnp.cumsum(padded)
    pstart = pend - padded
    n_rows = n * TOP_K + N_EXPERTS * MOE_BLOCK
    n_blocks = n_rows // MOE_BLOCK
    block_e = jnp.minimum(jnp.searchsorted(pend, jnp.arange(n_blocks, dtype=jnp.int32) * MOE_BLOCK, side='right'),
                          N_EXPERTS - 1).astype(jnp.int32)
    n_used = (pend[-1:] // MOE_BLOCK).astype(jnp.int32)
    dest = (pstart[idx] + rank).astype(jnp.int32)

    dest_w = dest.reshape(TOP_K, n // SC_WINDOW, SC_WINDOW).transpose(1, 0, 2)
    xs = _sc_dispatch(h1_rc.reshape(n, ROW_TILES, LANES), dest_w, n_rows)
    ys = _experts(block_e, n_used, xs.reshape(n_rows * ROW_TILES, LANES), w_exp_gate[0], w_exp_up[0], w_exp_down[0])
    gathered = _sc_gather(ys.reshape(n_rows, ROW_TILES, LANES), dest_w)

    y = _final(h1, gathered.reshape(TOP_K, n * ROW_TILES, LANES), wts.T,
               jnp.concatenate([w_sh_gate[0], w_sh_up[0]], axis=1).astype(BF16),
               w_sh_down[0].astype(BF16), vec(ln2_g[0]), vec(ln2_b[0]))
    return y[:n_p].reshape(bp, lp, d), y[n_p:].reshape(bs, ls, d)
```

```python
import functools

import jax
import jax.numpy as jnp
from jax import lax
from jax.experimental import pallas as pl
from jax.experimental.pallas import tpu as pltpu
from jax.experimental.pallas import tpu_sc as plsc

F32 = jnp.float32
BF16 = jnp.bfloat16

D_MODEL = 1024
POOL_GROUPS = 4
POOL_GROUP_W = 128
POOL_W = POOL_GROUPS * POOL_GROUP_W
POOL_WINDOWS = (2, 4, 8, 16)
POOL_HALO = 16
GLA_HEADS = 4
GLA_K_W = 512
GLA_V_W = 1024
GLA_DK = GLA_K_W // GLA_HEADS
GLA_DV = GLA_V_W // GLA_HEADS
GATE_RANK = 16
GATE_TAU = 16.0
GLA_CHUNK = 64
Z_PAD = 128
N_EXPERTS = 256
TOP_K = 8
N_GROUPS = 8
TOPK_GROUPS = 4
EXPERTS_PER_GROUP = N_EXPERTS // N_GROUPS
EXPERT_HIDDEN = 256
SHARED_HIDDEN = 256
ROUTED_SCALE = 2.5
DEPTH = 1
DN_ALPHA = (2 * DEPTH) ** 0.25
LN_EPS = 1e-5
RMS_EPS = 1e-6

VMEM_LIMIT_BYTES = 56 * 1024 * 1024

TILE_INPROJ = 512
TILE_POOL = 512
TILE_GLA = 256
TILE_ROUTER = 512
TILE_FINAL = 256
TILE_SLOTS = 2048
MOE_BLOCK = 512
LANES = 128
ROW_TILES = D_MODEL // LANES
SC_WINDOW = 32


def _params(semantics):
    return pltpu.CompilerParams(dimension_semantics=semantics, vmem_limit_bytes=VMEM_LIMIT_BYTES)


def _const_spec(shape):
    nd = len(shape)
    return pl.BlockSpec(shape, lambda *_: (0,) * nd)


def _layer_norm(x, g, b):
    mu = jnp.mean(x, axis=-1, keepdims=True)
    xc = x - mu
    var = jnp.mean(xc * xc, axis=-1, keepdims=True)
    return xc * lax.rsqrt(var + LN_EPS) * g + b


def _sigmoid(x):
    return 1.0 / (1.0 + jnp.exp(-x))


def _log_sigmoid(x):
    return jnp.minimum(x, 0.0) - jnp.log(1.0 + jnp.exp(-jnp.abs(x)))


def _silu(x):
    return x * _sigmoid(x)


def _tile_hits(i, tile, offsets):
    hit = i < 0
    for off in offsets:
        if off % tile == 0:
            hit = jnp.logical_or(hit, i == off // tile)
    return hit


_MAIN_SPLITS = (POOL_W, GLA_K_W, GLA_K_W, GLA_V_W, GLA_V_W, D_MODEL, D_MODEL)


def _inproj_kernel(xp_ref, xs_ref, g_ref, b_ref, w_ref, wz_ref,
                   h_ref, u_ref, q_ref, k_ref, v_ref, og_ref, gp_ref, gg_ref, z_ref, *, n_prompt_tiles):
    i = pl.program_id(0)
    x = jnp.where(i < n_prompt_tiles, xp_ref[...], xs_ref[...])
    h = _layer_norm(x, g_ref[...], b_ref[...])
    h_ref[...] = h
    hb = h.astype(BF16)
    off = 0
    for ref, width in zip((u_ref, q_ref, k_ref, v_ref, og_ref, gp_ref, gg_ref), _MAIN_SPLITS):
        ref[...] = jnp.dot(hb, w_ref[:, off:off + width], preferred_element_type=F32).astype(BF16)
        off += width
    z_ref[...] = jnp.dot(hb, wz_ref[...], preferred_element_type=F32).astype(BF16)


def _inproj(xp, xs, ln_g, ln_b, w_main, w_z):
    n_p, n_s = xp.shape[0], xs.shape[0]
    n = n_p + n_s
    t = TILE_INPROJ
    npt, nst = n_p // t, n_s // t
    row = lambda i: (i, 0)
    widths = _MAIN_SPLITS + (Z_PAD,)
    out_shape = [jax.ShapeDtypeStruct((n, D_MODEL), F32)] + [jax.ShapeDtypeStruct((n, w), BF16) for w in widths]
    out_specs = [pl.BlockSpec((t, D_MODEL), row)] + [pl.BlockSpec((t, w), row) for w in widths]
    return pl.pallas_call(
        functools.partial(_inproj_kernel, n_prompt_tiles=npt),
        grid=(npt + nst,),
        in_specs=[
            pl.BlockSpec((t, D_MODEL), lambda i: (jnp.minimum(i, npt - 1), 0)),
            pl.BlockSpec((t, D_MODEL), lambda i: (jnp.maximum(i - npt, 0), 0)),
            _const_spec((1, D_MODEL)), _const_spec((1, D_MODEL)),
            _const_spec(w_main.shape), _const_spec(w_z.shape),
        ],
        out_specs=out_specs,
        out_shape=out_shape,
        compiler_params=_params(("parallel",)),
        name="inproj",
    )(xp, xs, ln_g, ln_b, w_main, w_z)


def _pool_kernel(u_ref, up_ref, un_ref, gp_ref, band_ref, pw_ref, sc_ref, wup_ref, pm_ref, *, starts, ends):
    t = u_ref.shape[0]
    i = pl.program_id(0)
    is_start = _tile_hits(i, t, starts)
    is_end = _tile_hits(i + 1, t, ends)
    cur = u_ref[...]
    prev = jnp.where(is_start, jnp.zeros_like(up_ref[...]), up_ref[...])
    nxt = jnp.where(is_end, jnp.zeros_like(un_ref[...]), un_ref[...])
    ext = jnp.concatenate([prev, cur, nxt], axis=0)
    r = lax.broadcasted_iota(jnp.int32, (t, POOL_GROUP_W), 0).astype(F32)
    fs = is_start.astype(F32)
    fe = is_end.astype(F32)
    parts = []
    for gi, w in enumerate(POOL_WINDOWS):
        sl = slice(gi * POOL_GROUP_W, (gi + 1) * POOL_GROUP_W)
        s = jnp.dot(band_ref[gi], ext[:, sl], preferred_element_type=F32)
        cnt = w - fs * jnp.maximum(w // 2 - r, 0.0) - fe * jnp.maximum(r + (w // 2 - t), 0.0)
        parts.append(s / cnt - cur[:, sl].astype(F32))
    p = jnp.concatenate(parts, axis=1).astype(BF16)
    p = (jnp.dot(p, pw_ref[...], preferred_element_type=F32) * sc_ref[...]).astype(BF16)
    pool_out = jnp.dot(p, wup_ref[...], preferred_element_type=F32)
    pm_ref[...] = (_sigmoid(gp_ref[...].astype(F32)) * pool_out).astype(BF16)


def _pool_bands(t):
    r = jnp.arange(t)[:, None]
    c = jnp.arange(t + 2 * POOL_HALO)[None, :] - POOL_HALO
    return jnp.stack([((c >= r - w // 2) & (c <= r + w // 2 - 1)).astype(BF16) for w in POOL_WINDOWS])


def _pool(u, gp, pool_w_bd, pool_scale, w_pool_up, starts, ends):
    n = u.shape[0]
    t = TILE_POOL
    hb = t // POOL_HALO
    last = n // POOL_HALO - 1
    return pl.pallas_call(
        functools.partial(_pool_kernel, starts=starts, ends=ends),
        grid=(n // t,),
        in_specs=[
            pl.BlockSpec((t, POOL_W), lambda i: (i, 0)),
            pl.BlockSpec((POOL_HALO, POOL_W), lambda i: (jnp.maximum(i * hb - 1, 0), 0)),
            pl.BlockSpec((POOL_HALO, POOL_W), lambda i: (jnp.minimum((i + 1) * hb, last), 0)),
            pl.BlockSpec((t, D_MODEL), lambda i: (i, 0)),
            _const_spec((POOL_GROUPS, t, t + 2 * POOL_HALO)),
            _const_spec((POOL_W, POOL_W)), _const_spec((1, POOL_W)), _const_spec((POOL_W, D_MODEL)),
        ],
        out_specs=pl.BlockSpec((t, D_MODEL), lambda i: (i, 0)),
        out_shape=jax.ShapeDtypeStruct((n, D_MODEL), BF16),
        compiler_params=_params(("parallel",)),
        name="pool",
    )(u, u, u, gp, _pool_bands(t), pool_w_bd, pool_scale, w_pool_up)


def _gla_tile(q, k, v, la, tri, st_ref, *, reverse):
    t = q.shape[0]
    c_sz = GLA_CHUNK
    hi = la.astype(BF16)
    r1 = la - hi.astype(F32)
    mid = r1.astype(BF16)
    lo = (r1 - mid.astype(F32)).astype(BF16)
    b = (jnp.dot(tri, hi, preferred_element_type=F32) + jnp.dot(tri, mid, preferred_element_type=F32)
         + jnp.dot(tri, lo, preferred_element_type=F32))
    kf = k.astype(F32)
    qd = (q.astype(F32) * jnp.exp(b) * (GLA_DK ** -0.5)).astype(BF16)
    kd = (kf * jnp.exp(-b)).astype(BF16)
    row = lax.broadcasted_iota(jnp.int32, (c_sz, c_sz), 0)
    col = lax.broadcasted_iota(jnp.int32, (c_sz, c_sz), 1)
    mask = (col >= row) if reverse else (col <= row)
    n_chunks = t // c_sz
    outs = [None] * n_chunks
    nt = (((1,), (1,)), ((), ()))
    tn = (((0,), (0,)), ((), ()))
    for c in (range(n_chunks - 1, -1, -1) if reverse else range(n_chunks)):
        rows = slice(c * c_sz, (c + 1) * c_sz)
        b_c = b[rows]
        b_edge = b_c[0:1] if reverse else b_c[c_sz - 1:c_sz]
        k2 = (kf[rows] * jnp.exp(b_edge - b_c)).astype(BF16)
        dec = jnp.exp(b_edge)
        heads = []
        for h in range(GLA_HEADS):
            ls = slice(h * GLA_DK, (h + 1) * GLA_DK)
            vs = slice(h * GLA_DV, (h + 1) * GLA_DV)
            qh = qd[rows, ls]
            att = lax.dot_general(qh, kd[rows, ls], nt, preferred_element_type=F32)
            att = jnp.where(mask, att, 0.0).astype(BF16)
            vh = v[rows, vs]
            st = st_ref[h]
            o = jnp.dot(att, vh, preferred_element_type=F32)
            o = o + lax.dot_general(qh, st.astype(BF16), nt, preferred_element_type=F32)
            st_ref[h] = st * dec[:, ls] + lax.dot_general(vh, k2[:, ls], tn, preferred_element_type=F32)
            heads.append(o)
        outs[c] = jnp.concatenate(heads, axis=1)
    return jnp.concatenate(outs, axis=0)


def _log_decay(z, w2_ref, gb_ref):
    return _log_sigmoid(jnp.dot(z, w2_ref[...], preferred_element_type=F32) + gb_ref[...]) * (1.0 / GATE_TAU)


def _gla_bwd_kernel(q_ref, k_ref, v_ref, z_ref, w2_ref, gb_ref, tri_ref, ob_ref, st_ref, *, n_tiles, ends):
    t = q_ref.shape[0]
    j = n_tiles - 1 - pl.program_id(0)

    @pl.when(_tile_hits(j + 1, t, ends))
    def _():
        st_ref[...] = jnp.zeros_like(st_ref)

    la = _log_decay(z_ref[...], w2_ref, gb_ref)
    ob_ref[...] = _gla_tile(q_ref[...], k_ref[...], v_ref[...], la, tri_ref[...], st_ref, reverse=True)


def _gla_fwd_kernel(q_ref, k_ref, v_ref, z_ref, w2_ref, gb_ref, tri_ref, ob_ref, og_ref, gg_ref, pm_ref, h_ref,
                    ng_ref, wgu_ref, wo_ref, l1g_ref, l1b_ref, h1_ref, h1rc_ref, st_ref, *, starts):
    t = q_ref.shape[0]
    i = pl.program_id(0)

    @pl.when(_tile_hits(i, t, starts))
    def _():
        st_ref[...] = jnp.zeros_like(st_ref)

    la = _log_decay(z_ref[...], w2_ref, gb_ref)
    o = _gla_tile(q_ref[...], k_ref[...], v_ref[...], la, tri_ref[...], st_ref, reverse=False) + ob_ref[...]
    heads = []
    for h in range(GLA_HEADS):
        oh = o[:, h * GLA_DV:(h + 1) * GLA_DV]
        heads.append(oh * lax.rsqrt(jnp.mean(oh * oh, axis=-1, keepdims=True) + RMS_EPS))
    o = jnp.concatenate(heads, axis=1) * ng_ref[...]
    o = (o * _silu(og_ref[...].astype(F32))).astype(BF16)
    gla_out = jnp.dot(o, wgu_ref[...], preferred_element_type=F32)
    merged = _sigmoid(gg_ref[...].astype(F32)) * gla_out + pm_ref[...].astype(F32)
    mix = jnp.dot(merged.astype(BF16), wo_ref[...], preferred_element_type=F32)
    h1 = _layer_norm(DN_ALPHA * h_ref[...] + mix, l1g_ref[...], l1b_ref[...])
    h1_ref[...] = h1
    _store_row_contiguous(h1rc_ref, h1)


def _store_row_contiguous(ref, x):
    t = x.shape[0]
    for s in range(ROW_TILES):
        ref[pl.ds(s, t, stride=ROW_TILES), :] = x[:, s * LANES:(s + 1) * LANES]


def _load_row_contiguous(ref, t):
    return jnp.concatenate([ref[pl.ds(s, t, stride=ROW_TILES), :] for s in range(ROW_TILES)], axis=1)


def _chunk_tri(t, reverse):
    r = jnp.arange(t)[:, None]
    c = jnp.arange(t)[None, :]
    same = (r // GLA_CHUNK) == (c // GLA_CHUNK)
    return (same & ((c >= r) if reverse else (c <= r))).astype(BF16)


def _gla_bwd(q, k, v, z, w2b_pad, gate_b, ends):
    n = q.shape[0]
    t = TILE_GLA
    nt = n // t
    rev = lambda i: (nt - 1 - i, 0)
    return pl.pallas_call(
        functools.partial(_gla_bwd_kernel, n_tiles=nt, ends=ends),
        grid=(nt,),
        in_specs=[
            pl.BlockSpec((t, GLA_K_W), rev), pl.BlockSpec((t, GLA_K_W), rev), pl.BlockSpec((t, GLA_V_W), rev),
            pl.BlockSpec((t, Z_PAD), rev),
            _const_spec((Z_PAD, GLA_K_W)), _const_spec((1, GLA_K_W)), _const_spec((t, t)),
        ],
        out_specs=pl.BlockSpec((t, GLA_V_W), rev),
        out_shape=jax.ShapeDtypeStruct((n, GLA_V_W), F32),
        scratch_shapes=[pltpu.VMEM((GLA_HEADS, GLA_DV, GLA_DK), F32)],
        compiler_params=_params(("arbitrary",)),
        name="gla_bwd",
    )(q, k, v, z, w2b_pad, gate_b, _chunk_tri(t, True))


def _gla_fwd_mix(q, k, v, z, w2f_pad, gate_b, ob, og, gg, pm, h, norm_g, w_gla_up, w_out, ln1_g, ln1_b, starts):
    n = q.shape[0]
    t = TILE_GLA
    row = lambda i: (i, 0)
    return pl.pallas_call(
        functools.partial(_gla_fwd_kernel, starts=starts),
        grid=(n // t,),
        in_specs=[
            pl.BlockSpec((t, GLA_K_W), row), pl.BlockSpec((t, GLA_K_W), row), pl.BlockSpec((t, GLA_V_W), row),
            pl.BlockSpec((t, Z_PAD), row),
            _const_spec((Z_PAD, GLA_K_W)), _const_spec((1, GLA_K_W)), _const_spec((t, t)),
            pl.BlockSpec((t, GLA_V_W), row), pl.BlockSpec((t, GLA_V_W), row), pl.BlockSpec((t, D_MODEL), row),
            pl.BlockSpec((t, D_MODEL), row), pl.BlockSpec((t, D_MODEL), row),
            _const_spec((1, GLA_V_W)), _const_spec((GLA_V_W, D_MODEL)), _const_spec((D_MODEL, D_MODEL)),
            _const_spec((1, D_MODEL)), _const_spec((1, D_MODEL)),
        ],
        out_specs=[pl.BlockSpec((t, D_MODEL), row), pl.BlockSpec((t * ROW_TILES, LANES), row)],
        out_shape=[jax.ShapeDtypeStruct((n, D_MODEL), F32), jax.ShapeDtypeStruct((n * ROW_TILES, LANES), F32)],
        scratch_shapes=[pltpu.VMEM((GLA_HEADS, GLA_DV, GLA_DK), F32)],
        compiler_params=_params(("arbitrary",)),
        name="gla_fwd_mix",
    )(q, k, v, z, w2f_pad, gate_b, _chunk_tri(t, False), ob, og, gg, pm, h, norm_g, w_gla_up, w_out, ln1_g, ln1_b)


def _router_kernel(h_ref, wrh_ref, wrl_ref, rb_ref, tri_ref, idx_ref, wts_ref, rank_ref, cnt_ref, base_ref):
    t = h_ref.shape[0]

    @pl.when(pl.program_id(0) == 0)
    def _():
        base_ref[...] = jnp.zeros_like(base_ref)

    h = h_ref[...]
    hh = h.astype(BF16)
    hl = (h - hh.astype(F32)).astype(BF16)
    nt = (((1,), (1,)), ((), ()))
    logits = (lax.dot_general(wrh_ref[...], hh, nt, preferred_element_type=F32)
              + lax.dot_general(wrl_ref[...], hh, nt, preferred_element_type=F32)
              + lax.dot_general(wrh_ref[...], hl, nt, preferred_element_type=F32))
    scores = _sigmoid(logits)
    sel = scores + rb_ref[:, 0:1]
    neg = jnp.float32(-jnp.inf)

    sub = lax.broadcasted_iota(jnp.int32, (EXPERTS_PER_GROUP, t), 0)
    gscore = []
    for g in range(N_GROUPS):
        sg = sel[g * EXPERTS_PER_GROUP:(g + 1) * EXPERTS_PER_GROUP]
        m1 = jnp.max(sg, axis=0, keepdims=True)
        first = jnp.min(jnp.where(sg == m1, sub, EXPERTS_PER_GROUP), axis=0, keepdims=True)
        m2 = jnp.max(jnp.where(sub == first, neg, sg), axis=0, keepdims=True)
        gscore.append(m1 + m2)
    masked = []
    for g in range(N_GROUPS):
        beaten = jnp.zeros((1, t), jnp.int32)
        for g2 in range(N_GROUPS):
            if g2 == g:
                continue
            wins = (gscore[g2] >= gscore[g]) if g2 < g else (gscore[g2] > gscore[g])
            beaten = beaten + wins.astype(jnp.int32)
        keep = beaten < TOPK_GROUPS
        masked.append(jnp.where(keep, sel[g * EXPERTS_PER_GROUP:(g + 1) * EXPERTS_PER_GROUP], neg))
    cand = jnp.concatenate(masked, axis=0)

    eid = lax.broadcasted_iota(jnp.int32, (N_EXPERTS, t), 0)
    picked = []
    assign = jnp.zeros((N_EXPERTS, t), F32)
    wsum = jnp.zeros((1, t), F32)
    for kk in range(TOP_K):
        m = jnp.max(cand, axis=0, keepdims=True)
        e_k = jnp.min(jnp.where(cand == m, eid, N_EXPERTS), axis=0, keepdims=True)
        hit = eid == e_k
        w_k = jnp.sum(jnp.where(hit, scores, 0.0), axis=0, keepdims=True)
        cand = jnp.where(hit, neg, cand)
        assign = jnp.where(hit, 1.0, assign)
        wsum = wsum + w_k
        picked.append((e_k, w_k))
        idx_ref[kk:kk + 1, :] = e_k

    a16 = assign.astype(BF16)
    before = jnp.dot(a16, tri_ref[...], preferred_element_type=F32)
    base = base_ref[...]
    before = before + jnp.concatenate([base] * (t // 128), axis=1)
    inv = ROUTED_SCALE / wsum
    for kk, (e_k, w_k) in enumerate(picked):
        wts_ref[kk:kk + 1, :] = w_k * inv
        rank_ref[kk:kk + 1, :] = jnp.sum(jnp.where(eid == e_k, before, 0.0), axis=0, keepdims=True).astype(jnp.int32)
    total = base + jnp.dot(a16, jnp.ones((t, 128), BF16), preferred_element_type=F32)
    base_ref[...] = total
    cnt_ref[...] = total


def _router(h1, wr_hi, wr_lo, rbias):
    n = h1.shape[0]
    t = TILE_ROUTER
    r = jnp.arange(t)
    tri = (r[:, None] < r[None, :]).astype(BF16)
    col = lambda i: (0, i)
    idx, wts, rank, cnt = pl.pallas_call(
        _router_kernel,
        grid=(n // t,),
        in_specs=[
            pl.BlockSpec((t, D_MODEL), lambda i: (i, 0)),
            _const_spec((N_EXPERTS, D_MODEL)), _const_spec((N_EXPERTS, D_MODEL)),
            _const_spec((N_EXPERTS, 128)), _const_spec((t, t)),
        ],
        out_specs=[pl.BlockSpec((TOP_K, t), col), pl.BlockSpec((TOP_K, t), col), pl.BlockSpec((TOP_K, t), col),
                   _const_spec((N_EXPERTS, 128))],
        out_shape=[jax.ShapeDtypeStruct((TOP_K, n), jnp.int32), jax.ShapeDtypeStruct((TOP_K, n), F32),
                   jax.ShapeDtypeStruct((TOP_K, n), jnp.int32), jax.ShapeDtypeStruct((N_EXPERTS, 128), F32)],
        scratch_shapes=[pltpu.VMEM((N_EXPERTS, 128), F32)],
        compiler_params=_params(("arbitrary",)),
        name="router",
    )(h1, wr_hi, wr_lo, rbias, tri)
    return idx, wts, rank, cnt[:, 0].astype(jnp.int32)


def _slot_kernel(ps_ref, idx_ref, rank_ref, dest_ref):
    idx = idx_ref[...]

    def add_expert(e, acc):
        return acc + jnp.where(idx == e, ps_ref[e], 0)

    dest_ref[...] = lax.fori_loop(0, N_EXPERTS, add_expert, rank_ref[...], unroll=8)


def _slots(pstart, idx, rank):
    n = idx.shape[1]
    t = TILE_SLOTS
    col = lambda i, ps: (0, i)
    return pl.pallas_call(
        _slot_kernel,
        grid_spec=pltpu.PrefetchScalarGridSpec(
            num_scalar_prefetch=1,
            grid=(n // t,),
            in_specs=[pl.BlockSpec((TOP_K, t), col), pl.BlockSpec((TOP_K, t), col)],
            out_specs=pl.BlockSpec((TOP_K, t), col),
        ),
        out_shape=jax.ShapeDtypeStruct((TOP_K, n), jnp.int32),
        compiler_params=_params(("parallel",)),
        name="slots",
    )(pstart, idx, rank)


def _expert_kernel(be_ref, nu_ref, x_ref, wg_ref, wu_ref, wd_ref, y_ref, wgu_s, wd_s):
    b = pl.program_id(0)
    prev = be_ref[jnp.maximum(b - 1, 0)]
    fresh = jnp.logical_or(b == 0, be_ref[b] != prev)

    @pl.when(jnp.logical_and(fresh, b < nu_ref[0]))
    def _():
        wgu_s[:, :EXPERT_HIDDEN] = wg_ref[0].astype(BF16)
        wgu_s[:, EXPERT_HIDDEN:] = wu_ref[0].astype(BF16)
        wd_s[...] = wd_ref[0].astype(BF16)

    @pl.when(b < nu_ref[0])
    def _():
        x = _load_row_contiguous(x_ref, MOE_BLOCK).astype(BF16)
        gu = jnp.dot(x, wgu_s[...], preferred_element_type=F32)
        hid = (_silu(gu[:, :EXPERT_HIDDEN]) * gu[:, EXPERT_HIDDEN:]).astype(BF16)
        _store_row_contiguous(y_ref, jnp.dot(hid, wd_s[...], preferred_element_type=F32))


def _experts(block_e, n_used, xs, w_gate, w_up, w_down):
    blk = MOE_BLOCK * ROW_TILES
    nb = xs.shape[0] // blk
    return pl.pallas_call(
        _expert_kernel,
        grid_spec=pltpu.PrefetchScalarGridSpec(
            num_scalar_prefetch=2,
            grid=(nb,),
            in_specs=[
                pl.BlockSpec((blk, LANES), lambda b, be, nu: (jnp.minimum(b, nu[0] - 1), 0)),
                pl.BlockSpec((1, D_MODEL, EXPERT_HIDDEN), lambda b, be, nu: (be[b], 0, 0)),
                pl.BlockSpec((1, D_MODEL, EXPERT_HIDDEN), lambda b, be, nu: (be[b], 0, 0)),
                pl.BlockSpec((1, EXPERT_HIDDEN, D_MODEL), lambda b, be, nu: (be[b], 0, 0)),
            ],
            out_specs=pl.BlockSpec((blk, LANES), lambda b, be, nu: (b, 0)),
            scratch_shapes=[pltpu.VMEM((D_MODEL, 2 * EXPERT_HIDDEN), BF16), pltpu.VMEM((EXPERT_HIDDEN, D_MODEL), BF16)],
        ),
        out_shape=jax.ShapeDtypeStruct(xs.shape, F32),
        compiler_params=_params(("arbitrary",)),
        name="experts",
    )(block_e, n_used, xs, w_gate, w_up, w_down)


def _sc_mesh():
    return plsc.VectorSubcoreMesh(core_axis_name="c", subcore_axis_name="s")


def _sc_dispatch(x3, dest_w, n_rows):
    n = x3.shape[0]

    @functools.partial(pl.kernel, out_type=jax.ShapeDtypeStruct((n_rows, ROW_TILES, LANES), F32), mesh=_sc_mesh(),
                       name="sc_dispatch")
    def run(x_hbm, d_hbm, o_hbm):
        def body(x_vmem, d_vmem):
            for kk in range(TOP_K):
                pltpu.sync_copy(x_vmem, o_hbm.at[d_vmem.at[0].at[kk]])

        pltpu.emit_pipeline(
            body,
            grid=(n // SC_WINDOW,),
            in_specs=[pl.BlockSpec((SC_WINDOW, ROW_TILES, LANES), lambda i: (i, 0, 0)),
                      pl.BlockSpec((1, TOP_K, SC_WINDOW), lambda i: (i, 0, 0))],
            out_specs=[],
            core_axis_name=("c", "s"),
            dimension_semantics=(pltpu.PARALLEL,),
        )(x_hbm, d_hbm)

    return run(x3, dest_w)


def _sc_gather(y3, dest_w):
    nw = dest_w.shape[0]
    n = nw * SC_WINDOW

    @functools.partial(pl.kernel, out_type=jax.ShapeDtypeStruct((TOP_K, n, ROW_TILES, LANES), F32), mesh=_sc_mesh(),
                       name="sc_gather")
    def run(y_hbm, d_hbm, o_hbm):
        def body(d_vmem, g_vmem):
            pltpu.sync_copy(y_hbm.at[d_vmem.at[0].at[0]], g_vmem.at[0])

        pltpu.emit_pipeline(
            body,
            grid=(nw, TOP_K),
            in_specs=[pl.BlockSpec((1, 1, SC_WINDOW), lambda i, k: (i, k, 0))],
            out_specs=[pl.BlockSpec((1, SC_WINDOW, ROW_TILES, LANES), lambda i, k: (k, i, 0, 0))],
            core_axis_name=("c", "s"),
            dimension_semantics=(pltpu.PARALLEL, pltpu.PARALLEL),
        )(d_hbm, o_hbm)

    return run(y3, dest_w)


def _final_kernel(h_ref, g_ref, w_ref, wgu_ref, wd_ref, lg_ref, lb_ref, y_ref):
    t = h_ref.shape[0]
    h = h_ref[...]
    gu = jnp.dot(h.astype(BF16), wgu_ref[...], preferred_element_type=F32)
    hid = (_silu(gu[:, :SHARED_HIDDEN]) * gu[:, SHARED_HIDDEN:]).astype(BF16)
    ffn = jnp.dot(hid, wd_ref[...], preferred_element_type=F32)
    w = w_ref[...]
    for kk in range(TOP_K):
        ffn = ffn + _load_row_contiguous(g_ref.at[kk], t) * w[:, kk:kk + 1]
    y_ref[...] = _layer_norm(DN_ALPHA * h + ffn, lg_ref[...], lb_ref[...])


def _final(h1, gathered, wts_t, w_sh_gu, w_sh_down, ln_g, ln_b):
    n = h1.shape[0]
    t = TILE_FINAL
    row = lambda i: (i, 0)
    return pl.pallas_call(
        _final_kernel,
        grid=(n // t,),
        in_specs=[
            pl.BlockSpec((t, D_MODEL), row),
            pl.BlockSpec((TOP_K, t * ROW_TILES, LANES), lambda i: (0, i, 0)),
            pl.BlockSpec((t, TOP_K), row),
            _const_spec((D_MODEL, 2 * SHARED_HIDDEN)), _const_spec((SHARED_HIDDEN, D_MODEL)),
            _const_spec((1, D_MODEL)), _const_spec((1, D_MODEL)),
        ],
        out_specs=pl.BlockSpec((t, D_MODEL), row),
        out_shape=jax.ShapeDtypeStruct((n, D_MODEL), F32),
        compiler_params=_params(("parallel",)),
        name="final",
    )(h1, gathered, wts_t, w_sh_gu, w_sh_down, ln_g, ln_b)


def _block_diag(w):
    g, a, b = w.shape
    out = jnp.zeros((g * a, g * b), w.dtype)
    for i in range(g):
        out = out.at[i * a:(i + 1) * a, i * b:(i + 1) * b].set(w[i])
    return out


def kernel(x_prompt, x_sample, ln0_g, ln0_b, w_in, pool_w, pool_scale, w_pool_up, gate_w2_fwd, gate_b_fwd,
           gate_w2_bwd, gate_b_bwd, gla_norm_g, w_gla_up, w_out, ln1_g, ln1_b, w_router, router_bias,
           w_exp_gate, w_exp_up, w_exp_down, w_sh_gate, w_sh_up, w_sh_down, ln2_g, ln2_b):
    bp, lp, d = x_prompt.shape
    bs, ls, _ = x_sample.shape
    n_p, n_s = bp * lp, bs * ls
    n = n_p + n_s
    starts = tuple(b * lp for b in range(bp)) + tuple(n_p + b * ls for b in range(bs))
    ends = tuple((b + 1) * lp for b in range(bp)) + tuple(n_p + (b + 1) * ls for b in range(bs))
    vec = lambda a: a.reshape(1, -1).astype(F32)

    w = w_in[0]
    c_u, c_q, c_k, c_v, c_og = 0, 512, 1024, 1536, 2560
    c_zf, c_zb, c_gp, c_gg = 3584, 3600, 3616, 4640
    w_main = jnp.concatenate([w[:, c_u:c_zf], w[:, c_gp:]], axis=1).astype(BF16)
    w_z = jnp.zeros((d, Z_PAD), F32).at[:, :2 * GATE_RANK].set(w[:, c_zf:c_gp]).astype(BF16)
    w2f = jnp.zeros((Z_PAD, GLA_K_W), F32).at[:GATE_RANK].set(gate_w2_fwd[0]).astype(BF16)
    w2b = jnp.zeros((Z_PAD, GLA_K_W), F32).at[GATE_RANK:2 * GATE_RANK].set(gate_w2_bwd[0]).astype(BF16)

    h, u, q, k, v, og, gp, gg, z = _inproj(x_prompt.reshape(n_p, d), x_sample.reshape(n_s, d),
                                           vec(ln0_g), vec(ln0_b), w_main, w_z)
    pm = _pool(u, gp, _block_diag(pool_w[0]).astype(BF16), vec(pool_scale[0]), w_pool_up[0].astype(BF16),
               starts, ends)
    ob = _gla_bwd(q, k, v, z, w2b, vec(gate_b_bwd[0]), ends)
    h1, h1_rc = _gla_fwd_mix(q, k, v, z, w2f, vec(gate_b_fwd[0]), ob, og, gg, pm, h, vec(gla_norm_g[0]),
                             w_gla_up[0].astype(BF16), w_out[0].astype(BF16), vec(ln1_g[0]), vec(ln1_b[0]), starts)

    wr_t = w_router[0].T
    wr_hi = wr_t.astype(BF16)
    wr_lo = (wr_t - wr_hi.astype(F32)).astype(BF16)
    rbias = jnp.broadcast_to(router_bias[0].astype(F32)[:, None], (N_EXPERTS, 128))
    idx, wts, rank, counts = _router(h1, wr_hi, wr_lo, rbias)

    padded = (counts + MOE_BLOCK - 1) // MOE_BLOCK * MOE_BLOCK
    pend = jnp.cumsum(padded)
    pstart = pend - padded
    n_rows = n * TOP_K + N_EXPERTS * MOE_BLOCK
    n_blocks = n_rows // MOE_BLOCK
    block_row = jnp.arange(n_blocks, dtype=jnp.int32) * MOE_BLOCK
    block_e = jnp.minimum(jnp.sum((pend[None, :] <= block_row[:, None]).astype(jnp.int32), axis=1), N_EXPERTS - 1)
    n_used = (pend[-1:] // MOE_BLOCK).astype(jnp.int32)
    dest = _slots(pstart.astype(jnp.int32), idx, rank)

    dest_w = dest.reshape(TOP_K, n // SC_WINDOW, SC_WINDOW).transpose(1, 0, 2)
    xs = _sc_dispatch(h1_rc.reshape(n, ROW_TILES, LANES), dest_w, n_rows)
    ys = _experts(block_e, n_used, xs.reshape(n_rows * ROW_TILES, LANES), w_exp_gate[0], w_exp_up[0], w_exp_down[0])
    gathered = _sc_gather(ys.reshape(n_rows, ROW_TILES, LANES), dest_w)

    y = _final(h1, gathered.reshape(TOP_K, n * ROW_TILES, LANES), wts.T,
               jnp.concatenate([w_sh_gate[0], w_sh_up[0]], axis=1).astype(BF16),
               w_sh_down[0].astype(BF16), vec(ln2_g[0]), vec(ln2_b[0]))
    return y[:n_p].reshape(bp, lp, d), y[n_p:].reshape(bs, ls, d)
```

```python
import functools

import jax
import jax.numpy as jnp
from jax import lax
from jax.experimental import pallas as pl
from jax.experimental.pallas import tpu as pltpu
from jax.experimental.pallas import tpu_sc as plsc

F32 = jnp.float32
BF16 = jnp.bfloat16

D_MODEL = 1024
POOL_GROUPS = 4
POOL_GROUP_W = 128
POOL_W = POOL_GROUPS * POOL_GROUP_W
POOL_WINDOWS = (2, 4, 8, 16)
POOL_HALO = 16
GLA_HEADS = 4
GLA_K_W = 512
GLA_V_W = 1024
GLA_DK = GLA_K_W // GLA_HEADS
GLA_DV = GLA_V_W // GLA_HEADS
GATE_RANK = 16
GATE_TAU = 16.0
GLA_CHUNK = 64
Z_PAD = 128
N_EXPERTS = 256
TOP_K = 8
N_GROUPS = 8
TOPK_GROUPS = 4
EXPERTS_PER_GROUP = N_EXPERTS // N_GROUPS
EXPERT_HIDDEN = 256
SHARED_HIDDEN = 256
ROUTED_SCALE = 2.5
DEPTH = 1
DN_ALPHA = (2 * DEPTH) ** 0.25
LN_EPS = 1e-5
RMS_EPS = 1e-6

VMEM_LIMIT_BYTES = 56 * 1024 * 1024

TILE_INPROJ = 512
TILE_POOL = 512
TILE_GLA = 256
TILE_ROUTER = 512
TILE_FINAL = 512
TILE_SLOTS = 2048
MOE_BLOCK = 512
LANES = 128
ROW_WORDS = D_MODEL // 2
ROW_TILES = ROW_WORDS // LANES
SC_WINDOW = 64


def _params(semantics):
    return pltpu.CompilerParams(dimension_semantics=semantics, vmem_limit_bytes=VMEM_LIMIT_BYTES)


def _const_spec(shape):
    nd = len(shape)
    return pl.BlockSpec(shape, lambda *_: (0,) * nd)


def _layer_norm(x, g, b):
    mu = jnp.mean(x, axis=-1, keepdims=True)
    xc = x - mu
    var = jnp.mean(xc * xc, axis=-1, keepdims=True)
    return xc * lax.rsqrt(var + LN_EPS) * g + b


def _sigmoid(x):
    return 1.0 / (1.0 + jnp.exp(-x))


def _log_sigmoid(x):
    return jnp.minimum(x, 0.0) - jnp.log(1.0 + jnp.exp(-jnp.abs(x)))


def _silu(x):
    return x * _sigmoid(x)


def _tile_hits(i, tile, offsets):
    hit = i < 0
    for off in offsets:
        if off % tile == 0:
            hit = jnp.logical_or(hit, i == off // tile)
    return hit


_MAIN_SPLITS = (POOL_W, GLA_K_W, GLA_K_W, GLA_V_W, GLA_V_W, D_MODEL, D_MODEL)


def _inproj_kernel(xp_ref, xs_ref, g_ref, b_ref, w_ref, wz_ref,
                   h_ref, u_ref, q_ref, k_ref, v_ref, og_ref, gp_ref, gg_ref, z_ref, *, n_prompt_tiles):
    i = pl.program_id(0)
    x = jnp.where(i < n_prompt_tiles, xp_ref[...], xs_ref[...])
    h = _layer_norm(x, g_ref[...], b_ref[...])
    h_ref[...] = h
    hb = h.astype(BF16)
    off = 0
    for ref, width in zip((u_ref, q_ref, k_ref, v_ref, og_ref, gp_ref, gg_ref), _MAIN_SPLITS):
        ref[...] = jnp.dot(hb, w_ref[:, off:off + width], preferred_element_type=F32).astype(BF16)
        off += width
    z_ref[...] = jnp.dot(hb, wz_ref[...], preferred_element_type=F32).astype(BF16)


def _inproj(xp, xs, ln_g, ln_b, w_main, w_z):
    n_p, n_s = xp.shape[0], xs.shape[0]
    n = n_p + n_s
    t = TILE_INPROJ
    npt, nst = n_p // t, n_s // t
    row = lambda i: (i, 0)
    widths = _MAIN_SPLITS + (Z_PAD,)
    out_shape = [jax.ShapeDtypeStruct((n, D_MODEL), F32)] + [jax.ShapeDtypeStruct((n, w), BF16) for w in widths]
    out_specs = [pl.BlockSpec((t, D_MODEL), row)] + [pl.BlockSpec((t, w), row) for w in widths]
    return pl.pallas_call(
        functools.partial(_inproj_kernel, n_prompt_tiles=npt),
        grid=(npt + nst,),
        in_specs=[
            pl.BlockSpec((t, D_MODEL), lambda i: (jnp.minimum(i, npt - 1), 0)),
            pl.BlockSpec((t, D_MODEL), lambda i: (jnp.maximum(i - npt, 0), 0)),
            _const_spec((1, D_MODEL)), _const_spec((1, D_MODEL)),
            _const_spec(w_main.shape), _const_spec(w_z.shape),
        ],
        out_specs=out_specs,
        out_shape=out_shape,
        compiler_params=_params(("parallel",)),
        name="inproj",
    )(xp, xs, ln_g, ln_b, w_main, w_z)


def _pool_kernel(u_ref, up_ref, un_ref, gp_ref, band_ref, pw_ref, sc_ref, wup_ref, pm_ref, *, starts, ends):
    t = u_ref.shape[0]
    i = pl.program_id(0)
    is_start = _tile_hits(i, t, starts)
    is_end = _tile_hits(i + 1, t, ends)
    cur = u_ref[...]
    prev = jnp.where(is_start, jnp.zeros_like(up_ref[...]), up_ref[...])
    nxt = jnp.where(is_end, jnp.zeros_like(un_ref[...]), un_ref[...])
    ext = jnp.concatenate([prev, cur, nxt], axis=0)
    r = lax.broadcasted_iota(jnp.int32, (t, POOL_GROUP_W), 0).astype(F32)
    fs = is_start.astype(F32)
    fe = is_end.astype(F32)
    parts = []
    for gi, w in enumerate(POOL_WINDOWS):
        sl = slice(gi * POOL_GROUP_W, (gi + 1) * POOL_GROUP_W)
        s = jnp.dot(band_ref[gi], ext[:, sl], preferred_element_type=F32)
        cnt = w - fs * jnp.maximum(w // 2 - r, 0.0) - fe * jnp.maximum(r + (w // 2 - t), 0.0)
        parts.append(s / cnt - cur[:, sl].astype(F32))
    p = jnp.concatenate(parts, axis=1).astype(BF16)
    p = (jnp.dot(p, pw_ref[...], preferred_element_type=F32) * sc_ref[...]).astype(BF16)
    pool_out = jnp.dot(p, wup_ref[...], preferred_element_type=F32)
    pm_ref[...] = (_sigmoid(gp_ref[...].astype(F32)) * pool_out).astype(BF16)


def _pool_bands(t):
    r = jnp.arange(t)[:, None]
    c = jnp.arange(t + 2 * POOL_HALO)[None, :] - POOL_HALO
    return jnp.stack([((c >= r - w // 2) & (c <= r + w // 2 - 1)).astype(BF16) for w in POOL_WINDOWS])


def _pool(u, gp, pool_w_bd, pool_scale, w_pool_up, starts, ends):
    n = u.shape[0]
    t = TILE_POOL
    hb = t // POOL_HALO
    last = n // POOL_HALO - 1
    return pl.pallas_call(
        functools.partial(_pool_kernel, starts=starts, ends=ends),
        grid=(n // t,),
        in_specs=[
            pl.BlockSpec((t, POOL_W), lambda i: (i, 0)),
            pl.BlockSpec((POOL_HALO, POOL_W), lambda i: (jnp.maximum(i * hb - 1, 0), 0)),
            pl.BlockSpec((POOL_HALO, POOL_W), lambda i: (jnp.minimum((i + 1) * hb, last), 0)),
            pl.BlockSpec((t, D_MODEL), lambda i: (i, 0)),
            _const_spec((POOL_GROUPS, t, t + 2 * POOL_HALO)),
            _const_spec((POOL_W, POOL_W)), _const_spec((1, POOL_W)), _const_spec((POOL_W, D_MODEL)),
        ],
        out_specs=pl.BlockSpec((t, D_MODEL), lambda i: (i, 0)),
        out_shape=jax.ShapeDtypeStruct((n, D_MODEL), BF16),
        compiler_params=_params(("parallel",)),
        name="pool",
    )(u, u, u, gp, _pool_bands(t), pool_w_bd, pool_scale, w_pool_up)


def _gla_tile(q, k, v, la, tri, st_ref, *, reverse):
    t = q.shape[0]
    c_sz = GLA_CHUNK
    hi = la.astype(BF16)
    r1 = la - hi.astype(F32)
    mid = r1.astype(BF16)
    lo = (r1 - mid.astype(F32)).astype(BF16)
    b = (jnp.dot(tri, hi, preferred_element_type=F32) + jnp.dot(tri, mid, preferred_element_type=F32)
         + jnp.dot(tri, lo, preferred_element_type=F32))
    kf = k.astype(F32)
    qd = (q.astype(F32) * jnp.exp(b) * (GLA_DK ** -0.5)).astype(BF16)
    kd = (kf * jnp.exp(-b)).astype(BF16)
    row = lax.broadcasted_iota(jnp.int32, (c_sz, c_sz), 0)
    col = lax.broadcasted_iota(jnp.int32, (c_sz, c_sz), 1)
    mask = (col >= row) if reverse else (col <= row)
    n_chunks = t // c_sz
    outs = [None] * n_chunks
    nt = (((1,), (1,)), ((), ()))
    tn = (((0,), (0,)), ((), ()))
    for c in (range(n_chunks - 1, -1, -1) if reverse else range(n_chunks)):
        rows = slice(c * c_sz, (c + 1) * c_sz)
        b_c = b[rows]
        b_edge = b_c[0:1] if reverse else b_c[c_sz - 1:c_sz]
        k2 = (kf[rows] * jnp.exp(b_edge - b_c)).astype(BF16)
        dec = jnp.exp(b_edge)
        heads = []
        for h in range(GLA_HEADS):
            ls = slice(h * GLA_DK, (h + 1) * GLA_DK)
            vs = slice(h * GLA_DV, (h + 1) * GLA_DV)
            qh = qd[rows, ls]
            att = lax.dot_general(qh, kd[rows, ls], nt, preferred_element_type=F32)
            att = jnp.where(mask, att, 0.0).astype(BF16)
            vh = v[rows, vs]
            st = st_ref[h]
            o = jnp.dot(att, vh, preferred_element_type=F32)
            o = o + lax.dot_general(qh, st.astype(BF16), nt, preferred_element_type=F32)
            st_ref[h] = st * dec[:, ls] + lax.dot_general(vh, k2[:, ls], tn, preferred_element_type=F32)
            heads.append(o)
        outs[c] = jnp.concatenate(heads, axis=1)
    return jnp.concatenate(outs, axis=0)


def _log_decay(z, w2_ref, gb_ref):
    return _log_sigmoid(jnp.dot(z, w2_ref[...], preferred_element_type=F32) + gb_ref[...]) * (1.0 / GATE_TAU)


def _gla_bwd_kernel(q_ref, k_ref, v_ref, z_ref, w2_ref, gb_ref, tri_ref, ob_ref, st_ref, *, n_tiles, ends):
    t = q_ref.shape[0]
    j = n_tiles - 1 - pl.program_id(0)

    @pl.when(_tile_hits(j + 1, t, ends))
    def _():
        st_ref[...] = jnp.zeros_like(st_ref)

    la = _log_decay(z_ref[...], w2_ref, gb_ref)
    ob_ref[...] = _gla_tile(q_ref[...], k_ref[...], v_ref[...], la, tri_ref[...], st_ref, reverse=True)


def _gla_fwd_kernel(q_ref, k_ref, v_ref, z_ref, w2_ref, gb_ref, tri_ref, ob_ref, og_ref, gg_ref, pm_ref, h_ref,
                    ng_ref, wgu_ref, wo_ref, l1g_ref, l1b_ref, h1_ref, h1rc_ref, st_ref, *, starts):
    t = q_ref.shape[0]
    i = pl.program_id(0)

    @pl.when(_tile_hits(i, t, starts))
    def _():
        st_ref[...] = jnp.zeros_like(st_ref)

    la = _log_decay(z_ref[...], w2_ref, gb_ref)
    o = _gla_tile(q_ref[...], k_ref[...], v_ref[...], la, tri_ref[...], st_ref, reverse=False) + ob_ref[...]
    heads = []
    for h in range(GLA_HEADS):
        oh = o[:, h * GLA_DV:(h + 1) * GLA_DV]
        heads.append(oh * lax.rsqrt(jnp.mean(oh * oh, axis=-1, keepdims=True) + RMS_EPS))
    o = jnp.concatenate(heads, axis=1) * ng_ref[...]
    o = (o * _silu(og_ref[...].astype(F32))).astype(BF16)
    gla_out = jnp.dot(o, wgu_ref[...], preferred_element_type=F32)
    merged = _sigmoid(gg_ref[...].astype(F32)) * gla_out + pm_ref[...].astype(F32)
    mix = jnp.dot(merged.astype(BF16), wo_ref[...], preferred_element_type=F32)
    h1 = _layer_norm(DN_ALPHA * h_ref[...] + mix, l1g_ref[...], l1b_ref[...])
    h1_ref[...] = h1
    _store_row_contiguous(h1rc_ref, h1)


def _store_row_contiguous(ref, x):
    t = x.shape[0]
    bits = pltpu.bitcast(x.astype(BF16).astype(F32), jnp.uint32)
    words = (bits[:, :ROW_WORDS] >> 16) | (bits[:, ROW_WORDS:] & jnp.uint32(0xFFFF0000))
    words = pltpu.bitcast(words, F32)
    for s in range(ROW_TILES):
        ref[pl.ds(s, t, stride=ROW_TILES), :] = words[:, s * LANES:(s + 1) * LANES]


def _load_row_contiguous(ref, t):
    words = jnp.concatenate([ref[pl.ds(s, t, stride=ROW_TILES), :] for s in range(ROW_TILES)], axis=1)
    bits = pltpu.bitcast(words, jnp.uint32)
    lo = pltpu.bitcast(bits << 16, F32)
    hi = pltpu.bitcast(bits & jnp.uint32(0xFFFF0000), F32)
    return jnp.concatenate([lo, hi], axis=1)


def _chunk_tri(t, reverse):
    r = jnp.arange(t)[:, None]
    c = jnp.arange(t)[None, :]
    same = (r // GLA_CHUNK) == (c // GLA_CHUNK)
    return (same & ((c >= r) if reverse else (c <= r))).astype(BF16)


def _gla_bwd(q, k, v, z, w2b_pad, gate_b, ends):
    n = q.shape[0]
    t = TILE_GLA
    nt = n // t
    rev = lambda i: (nt - 1 - i, 0)
    return pl.pallas_call(
        functools.partial(_gla_bwd_kernel, n_tiles=nt, ends=ends),
        grid=(nt,),
        in_specs=[
            pl.BlockSpec((t, GLA_K_W), rev), pl.BlockSpec((t, GLA_K_W), rev), pl.BlockSpec((t, GLA_V_W), rev),
            pl.BlockSpec((t, Z_PAD), rev),
            _const_spec((Z_PAD, GLA_K_W)), _const_spec((1, GLA_K_W)), _const_spec((t, t)),
        ],
        out_specs=pl.BlockSpec((t, GLA_V_W), rev),
        out_shape=jax.ShapeDtypeStruct((n, GLA_V_W), F32),
        scratch_shapes=[pltpu.VMEM((GLA_HEADS, GLA_DV, GLA_DK), F32)],
        compiler_params=_params(("arbitrary",)),
        name="gla_bwd",
    )(q, k, v, z, w2b_pad, gate_b, _chunk_tri(t, True))


def _gla_fwd_mix(q, k, v, z, w2f_pad, gate_b, ob, og, gg, pm, h, norm_g, w_gla_up, w_out, ln1_g, ln1_b, starts):
    n = q.shape[0]
    t = TILE_GLA
    row = lambda i: (i, 0)
    return pl.pallas_call(
        functools.partial(_gla_fwd_kernel, starts=starts),
        grid=(n // t,),
        in_specs=[
            pl.BlockSpec((t, GLA_K_W), row), pl.BlockSpec((t, GLA_K_W), row), pl.BlockSpec((t, GLA_V_W), row),
            pl.BlockSpec((t, Z_PAD), row),
            _const_spec((Z_PAD, GLA_K_W)), _const_spec((1, GLA_K_W)), _const_spec((t, t)),
            pl.BlockSpec((t, GLA_V_W), row), pl.BlockSpec((t, GLA_V_W), row), pl.BlockSpec((t, D_MODEL), row),
            pl.BlockSpec((t, D_MODEL), row), pl.BlockSpec((t, D_MODEL), row),
            _const_spec((1, GLA_V_W)), _const_spec((GLA_V_W, D_MODEL)), _const_spec((D_MODEL, D_MODEL)),
            _const_spec((1, D_MODEL)), _const_spec((1, D_MODEL)),
        ],
        out_specs=[pl.BlockSpec((t, D_MODEL), row), pl.BlockSpec((t * ROW_TILES, LANES), row)],
        out_shape=[jax.ShapeDtypeStruct((n, D_MODEL), F32), jax.ShapeDtypeStruct((n * ROW_TILES, LANES), F32)],
        scratch_shapes=[pltpu.VMEM((GLA_HEADS, GLA_DV, GLA_DK), F32)],
        compiler_params=_params(("arbitrary",)),
        name="gla_fwd_mix",
    )(q, k, v, z, w2f_pad, gate_b, _chunk_tri(t, False), ob, og, gg, pm, h, norm_g, w_gla_up, w_out, ln1_g, ln1_b)


def _router_kernel(h_ref, wrh_ref, wrl_ref, rb_ref, tri_ref, idx_ref, wts_ref, rank_ref, cnt_ref, base_ref):
    t = h_ref.shape[0]

    @pl.when(pl.program_id(0) == 0)
    def _():
        base_ref[...] = jnp.zeros_like(base_ref)

    h = h_ref[...]
    hh = h.astype(BF16)
    hl = (h - hh.astype(F32)).astype(BF16)
    nt = (((1,), (1,)), ((), ()))
    logits = (lax.dot_general(wrh_ref[...], hh, nt, preferred_element_type=F32)
              + lax.dot_general(wrl_ref[...], hh, nt, preferred_element_type=F32)
              + lax.dot_general(wrh_ref[...], hl, nt, preferred_element_type=F32))
    scores = _sigmoid(logits)
    sel = scores + rb_ref[:, 0:1]
    neg = jnp.float32(-jnp.inf)

    sub = lax.broadcasted_iota(jnp.int32, (EXPERTS_PER_GROUP, t), 0)
    gscore = []
    for g in range(N_GROUPS):
        sg = sel[g * EXPERTS_PER_GROUP:(g + 1) * EXPERTS_PER_GROUP]
        m1 = jnp.max(sg, axis=0, keepdims=True)
        first = jnp.min(jnp.where(sg == m1, sub, EXPERTS_PER_GROUP), axis=0, keepdims=True)
        m2 = jnp.max(jnp.where(sub == first, neg, sg), axis=0, keepdims=True)
        gscore.append(m1 + m2)
    masked = []
    for g in range(N_GROUPS):
        beaten = jnp.zeros((1, t), jnp.int32)
        for g2 in range(N_GROUPS):
            if g2 == g:
                continue
            wins = (gscore[g2] >= gscore[g]) if g2 < g else (gscore[g2] > gscore[g])
            beaten = beaten + wins.astype(jnp.int32)
        keep = beaten < TOPK_GROUPS
        masked.append(jnp.where(keep, sel[g * EXPERTS_PER_GROUP:(g + 1) * EXPERTS_PER_GROUP], neg))
    cand = jnp.concatenate(masked, axis=0)

    eid = lax.broadcasted_iota(jnp.int32, (N_EXPERTS, t), 0)
    picked = []
    assign = jnp.zeros((N_EXPERTS, t), F32)
    wsum = jnp.zeros((1, t), F32)
    for kk in range(TOP_K):
        m = jnp.max(cand, axis=0, keepdims=True)
        e_k = jnp.min(jnp.where(cand == m, eid, N_EXPERTS), axis=0, keepdims=True)
        hit = eid == e_k
        w_k = jnp.sum(jnp.where(hit, scores, 0.0), axis=0, keepdims=True)
        cand = jnp.where(hit, neg, cand)
        assign = jnp.where(hit, 1.0, assign)
        wsum = wsum + w_k
        picked.append((e_k, w_k))
        idx_ref[kk:kk + 1, :] = e_k

    a16 = assign.astype(BF16)
    before = jnp.dot(a16, tri_ref[...], preferred_element_type=F32)
    base = base_ref[...]
    before = before + jnp.concatenate([base] * (t // 128), axis=1)
    inv = ROUTED_SCALE / wsum
    for kk, (e_k, w_k) in enumerate(picked):
        wts_ref[kk:kk + 1, :] = w_k * inv
        rank_ref[kk:kk + 1, :] = jnp.sum(jnp.where(eid == e_k, before, 0.0), axis=0, keepdims=True).astype(jnp.int32)
    total = base + jnp.dot(a16, jnp.ones((t, 128), BF16), preferred_element_type=F32)
    base_ref[...] = total
    cnt_ref[...] = total


def _router(h1, wr_hi, wr_lo, rbias):
    n = h1.shape[0]
    t = TILE_ROUTER
    r = jnp.arange(t)
    tri = (r[:, None] < r[None, :]).astype(BF16)
    col = lambda i: (0, i)
    idx, wts, rank, cnt = pl.pallas_call(
        _router_kernel,
        grid=(n // t,),
        in_specs=[
            pl.BlockSpec((t, D_MODEL), lambda i: (i, 0)),
            _const_spec((N_EXPERTS, D_MODEL)), _const_spec((N_EXPERTS, D_MODEL)),
            _const_spec((N_EXPERTS, 128)), _const_spec((t, t)),
        ],
        out_specs=[pl.BlockSpec((TOP_K, t), col), pl.BlockSpec((TOP_K, t), col), pl.BlockSpec((TOP_K, t), col),
                   _const_spec((N_EXPERTS, 128))],
        out_shape=[jax.ShapeDtypeStruct((TOP_K, n), jnp.int32), jax.ShapeDtypeStruct((TOP_K, n), F32),
                   jax.ShapeDtypeStruct((TOP_K, n), jnp.int32), jax.ShapeDtypeStruct((N_EXPERTS, 128), F32)],
        scratch_shapes=[pltpu.VMEM((N_EXPERTS, 128), F32)],
        compiler_params=_params(("arbitrary",)),
        name="router",
    )(h1, wr_hi, wr_lo, rbias, tri)
    return idx, wts, rank, cnt[:, 0].astype(jnp.int32)


def _slot_kernel(ps_ref, idx_ref, rank_ref, dest_ref):
    idx = idx_ref[...]

    def add_expert(e, acc):
        return acc + jnp.where(idx == e, ps_ref[e], 0)

    dest_ref[...] = lax.fori_loop(0, N_EXPERTS, add_expert, rank_ref[...], unroll=8)


def _slots(pstart, idx, rank):
    n = idx.shape[1]
    t = TILE_SLOTS
    col = lambda i, ps: (0, i)
    return pl.pallas_call(
        _slot_kernel,
        grid_spec=pltpu.PrefetchScalarGridSpec(
            num_scalar_prefetch=1,
            grid=(n // t,),
            in_specs=[pl.BlockSpec((TOP_K, t), col), pl.BlockSpec((TOP_K, t), col)],
            out_specs=pl.BlockSpec((TOP_K, t), col),
        ),
        out_shape=jax.ShapeDtypeStruct((TOP_K, n), jnp.int32),
        compiler_params=_params(("parallel",)),
        name="slots",
    )(pstart, idx, rank)


def _expert_kernel(be_ref, nu_ref, x_ref, wg_ref, wu_ref, wd_ref, y_ref, wgu_s, wd_s):
    b = pl.program_id(0)
    prev = be_ref[jnp.maximum(b - 1, 0)]
    fresh = jnp.logical_or(b == 0, be_ref[b] != prev)

    @pl.when(jnp.logical_and(fresh, b < nu_ref[0]))
    def _():
        wgu_s[:, :EXPERT_HIDDEN] = wg_ref[0].astype(BF16)
        wgu_s[:, EXPERT_HIDDEN:] = wu_ref[0].astype(BF16)
        wd_s[...] = wd_ref[0].astype(BF16)

    @pl.when(b < nu_ref[0])
    def _():
        x = _load_row_contiguous(x_ref, MOE_BLOCK).astype(BF16)
        gu = jnp.dot(x, wgu_s[...], preferred_element_type=F32)
        hid = (_silu(gu[:, :EXPERT_HIDDEN]) * gu[:, EXPERT_HIDDEN:]).astype(BF16)
        _store_row_contiguous(y_ref, jnp.dot(hid, wd_s[...], preferred_element_type=F32))


def _experts(block_e, n_used, xs, w_gate, w_up, w_down):
    blk = MOE_BLOCK * ROW_TILES
    nb = xs.shape[0] // blk
    return pl.pallas_call(
        _expert_kernel,
        grid_spec=pltpu.PrefetchScalarGridSpec(
            num_scalar_prefetch=2,
            grid=(nb,),
            in_specs=[
                pl.BlockSpec((blk, LANES), lambda b, be, nu: (jnp.minimum(b, nu[0] - 1), 0)),
                pl.BlockSpec((1, D_MODEL, EXPERT_HIDDEN), lambda b, be, nu: (be[b], 0, 0)),
                pl.BlockSpec((1, D_MODEL, EXPERT_HIDDEN), lambda b, be, nu: (be[b], 0, 0)),
                pl.BlockSpec((1, EXPERT_HIDDEN, D_MODEL), lambda b, be, nu: (be[b], 0, 0)),
            ],
            out_specs=pl.BlockSpec((blk, LANES), lambda b, be, nu: (b, 0)),
            scratch_shapes=[pltpu.VMEM((D_MODEL, 2 * EXPERT_HIDDEN), BF16), pltpu.VMEM((EXPERT_HIDDEN, D_MODEL), BF16)],
        ),
        out_shape=jax.ShapeDtypeStruct(xs.shape, F32),
        compiler_params=_params(("arbitrary",)),
        name="experts",
    )(block_e, n_used, xs, w_gate, w_up, w_down)


def _sc_mesh():
    return plsc.VectorSubcoreMesh(core_axis_name="c", subcore_axis_name="s")


def _sc_dispatch(x3, dest_w, n_rows):
    n = x3.shape[0]

    @functools.partial(pl.kernel, out_type=jax.ShapeDtypeStruct((n_rows, ROW_TILES, LANES), F32), mesh=_sc_mesh(),
                       name="sc_dispatch")
    def run(x_hbm, d_hbm, o_hbm):
        def body(x_vmem, d_vmem):
            for kk in range(TOP_K):
                pltpu.sync_copy(x_vmem, o_hbm.at[d_vmem.at[0].at[kk]])

        pltpu.emit_pipeline(
            body,
            grid=(n // SC_WINDOW,),
            in_specs=[pl.BlockSpec((SC_WINDOW, ROW_TILES, LANES), lambda i: (i, 0, 0)),
                      pl.BlockSpec((1, TOP_K, SC_WINDOW), lambda i: (i, 0, 0))],
            out_specs=[],
            core_axis_name=("c", "s"),
            dimension_semantics=(pltpu.PARALLEL,),
        )(x_hbm, d_hbm)

    return run(x3, dest_w)


def _sc_gather(y3, dest_w):
    nw = dest_w.shape[0]
    n = nw * SC_WINDOW

    @functools.partial(pl.kernel, out_type=jax.ShapeDtypeStruct((TOP_K, n, ROW_TILES, LANES), F32), mesh=_sc_mesh(),
                       name="sc_gather")
    def run(y_hbm, d_hbm, o_hbm):
        def body(d_vmem, g_vmem):
            pltpu.sync_copy(y_hbm.at[d_vmem.at[0].at[0]], g_vmem.at[0])

        pltpu.emit_pipeline(
            body,
            grid=(nw, TOP_K),
            in_specs=[pl.BlockSpec((1, 1, SC_WINDOW), lambda i, k: (i, k, 0))],
            out_specs=[pl.BlockSpec((1, SC_WINDOW, ROW_TILES, LANES), lambda i, k: (k, i, 0, 0))],
            core_axis_name=("c", "s"),
            dimension_semantics=(pltpu.PARALLEL, pltpu.PARALLEL),
        )(d_hbm, o_hbm)

    return run(y3, dest_w)


def _final_kernel(h_ref, g_ref, w_ref, wgu_ref, wd_ref, lg_ref, lb_ref, yp_ref, ys_ref, *, n_prompt_tiles):
    t = h_ref.shape[0]
    h = h_ref[...]
    gu = jnp.dot(h.astype(BF16), wgu_ref[...], preferred_element_type=F32)
    hid = (_silu(gu[:, :SHARED_HIDDEN]) * gu[:, SHARED_HIDDEN:]).astype(BF16)
    ffn = jnp.dot(hid, wd_ref[...], preferred_element_type=F32)
    w = w_ref[...]
    for kk in range(TOP_K):
        ffn = ffn + _load_row_contiguous(g_ref.at[kk], t) * w[:, kk:kk + 1]
    y = _layer_norm(DN_ALPHA * h + ffn, lg_ref[...], lb_ref[...])
    is_prompt = pl.program_id(0) < n_prompt_tiles

    @pl.when(is_prompt)
    def _():
        yp_ref[...] = y

    @pl.when(jnp.logical_not(is_prompt))
    def _():
        ys_ref[...] = y


def _final(h1, gathered, wts_t, w_sh_gu, w_sh_down, ln_g, ln_b, n_p):
    n = h1.shape[0]
    t = TILE_FINAL
    npt = n_p // t
    row = lambda i: (i, 0)
    return pl.pallas_call(
        functools.partial(_final_kernel, n_prompt_tiles=npt),
        grid=(n // t,),
        in_specs=[
            pl.BlockSpec((t, D_MODEL), row),
            pl.BlockSpec((TOP_K, t * ROW_TILES, LANES), lambda i: (0, i, 0)),
            pl.BlockSpec((t, TOP_K), row),
            _const_spec((D_MODEL, 2 * SHARED_HIDDEN)), _const_spec((SHARED_HIDDEN, D_MODEL)),
            _const_spec((1, D_MODEL)), _const_spec((1, D_MODEL)),
        ],
        out_specs=[pl.BlockSpec((t, D_MODEL), lambda i: (jnp.minimum(i, npt - 1), 0)),
                   pl.BlockSpec((t, D_MODEL), lambda i: (jnp.maximum(i - npt, 0), 0))],
        out_shape=[jax.ShapeDtypeStruct((n_p, D_MODEL), F32), jax.ShapeDtypeStruct((n - n_p, D_MODEL), F32)],
        compiler_params=_params(("arbitrary",)),
        name="final",
    )(h1, gathered, wts_t, w_sh_gu, w_sh_down, ln_g, ln_b)


def _block_diag(w):
    g, a, b = w.shape
    out = jnp.zeros((g * a, g * b), w.dtype)
    for i in range(g):
        out = out.at[i * a:(i + 1) * a, i * b:(i + 1) * b].set(w[i])
    return out


def kernel(x_prompt, x_sample, ln0_g, ln0_b, w_in, pool_w, pool_scale, w_pool_up, gate_w2_fwd, gate_b_fwd,
           gate_w2_bwd, gate_b_bwd, gla_norm_g, w_gla_up, w_out, ln1_g, ln1_b, w_router, router_bias,
           w_exp_gate, w_exp_up, w_exp_down, w_sh_gate, w_sh_up, w_sh_down, ln2_g, ln2_b):
    bp, lp, d = x_prompt.shape
    bs, ls, _ = x_sample.shape
    n_p, n_s = bp * lp, bs * ls
    n = n_p + n_s
    starts = tuple(b * lp for b in range(bp)) + tuple(n_p + b * ls for b in range(bs))
    ends = tuple((b + 1) * lp for b in range(bp)) + tuple(n_p + (b + 1) * ls for b in range(bs))
    vec = lambda a: a.reshape(1, -1).astype(F32)

    w = w_in[0]
    c_u, c_q, c_k, c_v, c_og = 0, 512, 1024, 1536, 2560
    c_zf, c_zb, c_gp, c_gg = 3584, 3600, 3616, 4640
    w_main = jnp.concatenate([w[:, c_u:c_zf], w[:, c_gp:]], axis=1).astype(BF16)
    w_z = jnp.zeros((d, Z_PAD), F32).at[:, :2 * GATE_RANK].set(w[:, c_zf:c_gp]).astype(BF16)
    w2f = jnp.zeros((Z_PAD, GLA_K_W), F32).at[:GATE_RANK].set(gate_w2_fwd[0]).astype(BF16)
    w2b = jnp.zeros((Z_PAD, GLA_K_W), F32).at[GATE_RANK:2 * GATE_RANK].set(gate_w2_bwd[0]).astype(BF16)

    h, u, q, k, v, og, gp, gg, z = _inproj(x_prompt.reshape(n_p, d), x_sample.reshape(n_s, d),
                                           vec(ln0_g), vec(ln0_b), w_main, w_z)
    pm = _pool(u, gp, _block_diag(pool_w[0]).astype(BF16), vec(pool_scale[0]), w_pool_up[0].astype(BF16),
               starts, ends)
    ob = _gla_bwd(q, k, v, z, w2b, vec(gate_b_bwd[0]), ends)
    h1, h1_rc = _gla_fwd_mix(q, k, v, z, w2f, vec(gate_b_fwd[0]), ob, og, gg, pm, h, vec(gla_norm_g[0]),
                             w_gla_up[0].astype(BF16), w_out[0].astype(BF16), vec(ln1_g[0]), vec(ln1_b[0]), starts)

    wr_t = w_router[0].T
    wr_hi = wr_t.astype(BF16)
    wr_lo = (wr_t - wr_hi.astype(F32)).astype(BF16)
    rbias = jnp.broadcast_to(router_bias[0].astype(F32)[:, None], (N_EXPERTS, 128))
    idx, wts, rank, counts = _router(h1, wr_hi, wr_lo, rbias)

    padded = (counts + MOE_BLOCK - 1) // MOE_BLOCK * MOE_BLOCK
    pend = jnp.cumsum(padded)
    pstart = pend - padded
    n_rows = n * TOP_K + N_EXPERTS * MOE_BLOCK
    n_blocks = n_rows // MOE_BLOCK
    block_row = jnp.arange(n_blocks, dtype=jnp.int32) * MOE_BLOCK
    block_e = jnp.minimum(jnp.sum((pend[None, :] <= block_row[:, None]).astype(jnp.int32), axis=1), N_EXPERTS - 1)
    n_used = (pend[-1:] // MOE_BLOCK).astype(jnp.int32)
    dest = _slots(pstart.astype(jnp.int32), idx, rank)

    dest_w = dest.reshape(TOP_K, n // SC_WINDOW, SC_WINDOW).transpose(1, 0, 2)
    xs = _sc_dispatch(h1_rc.reshape(n, ROW_TILES, LANES), dest_w, n_rows)
    ys = _experts(block_e, n_used, xs.reshape(n_rows * ROW_TILES, LANES), w_exp_gate[0], w_exp_up[0], w_exp_down[0])
    gathered = _sc_gather(ys.reshape(n_rows, ROW_TILES, LANES), dest_w)

    y_p, y_s = _final(h1, gathered.reshape(TOP_K, n * ROW_TILES, LANES), wts.T,
                      jnp.concatenate([w_sh_gate[0], w_sh_up[0]], axis=1).astype(BF16),
                      w_sh_down[0].astype(BF16), vec(ln2_g[0]), vec(ln2_b[0]), n_p)
    return y_p.reshape(bp, lp, d), y_s.reshape(bs, ls, d)
```

```python
import functools

import jax
import jax.numpy as jnp
from jax import lax
from jax.experimental import pallas as pl
from jax.experimental.pallas import tpu as pltpu
from jax.experimental.pallas import tpu_sc as plsc

F32 = jnp.float32
BF16 = jnp.bfloat16

D_MODEL = 1024
POOL_GROUPS = 4
POOL_GROUP_W = 128
POOL_W = POOL_GROUPS * POOL_GROUP_W
POOL_WINDOWS = (2, 4, 8, 16)
POOL_HALO = 16
GLA_HEADS = 4
GLA_K_W = 512
GLA_V_W = 1024
GLA_DK = GLA_K_W // GLA_HEADS
GLA_DV = GLA_V_W // GLA_HEADS
GATE_RANK = 16
GATE_TAU = 16.0
GLA_CHUNK = 64
Z_PAD = 128
N_EXPERTS = 256
TOP_K = 8
N_GROUPS = 8
TOPK_GROUPS = 4
EXPERTS_PER_GROUP = N_EXPERTS // N_GROUPS
EXPERT_HIDDEN = 256
SHARED_HIDDEN = 256
ROUTED_SCALE = 2.5
DEPTH = 1
DN_ALPHA = (2 * DEPTH) ** 0.25
LN_EPS = 1e-5
RMS_EPS = 1e-6

VMEM_LIMIT_BYTES = 56 * 1024 * 1024

TILE_INPROJ = 512
TILE_POOL = 512
TILE_GLA = 256
TILE_ROUTER = 512
TILE_FINAL = 512
TILE_SLOTS = 2048
MOE_BLOCK = 512
X_SLOTS = 3
Y_SLOTS = 2
LANES = 128
ROW_WORDS = D_MODEL // 2
ROW_TILES = ROW_WORDS // LANES
SC_WINDOW = 64


def _params(semantics):
    return pltpu.CompilerParams(dimension_semantics=semantics, vmem_limit_bytes=VMEM_LIMIT_BYTES)


def _const_spec(shape):
    nd = len(shape)
    return pl.BlockSpec(shape, lambda *_: (0,) * nd)


def _layer_norm(x, g, b):
    mu = jnp.mean(x, axis=-1, keepdims=True)
    xc = x - mu
    var = jnp.mean(xc * xc, axis=-1, keepdims=True)
    return xc * lax.rsqrt(var + LN_EPS) * g + b


def _sigmoid(x):
    return 1.0 / (1.0 + jnp.exp(-x))


def _log_sigmoid(x):
    return jnp.minimum(x, 0.0) - jnp.log(1.0 + jnp.exp(-jnp.abs(x)))


def _silu(x):
    return x * _sigmoid(x)


def _tile_hits(i, tile, offsets):
    hit = i < 0
    for off in offsets:
        if off % tile == 0:
            hit = jnp.logical_or(hit, i == off // tile)
    return hit


_MAIN_SPLITS = (POOL_W, GLA_K_W, GLA_K_W, GLA_V_W, GLA_V_W, D_MODEL, D_MODEL)


def _inproj_kernel(xp_ref, xs_ref, g_ref, b_ref, w_ref, wz_ref,
                   h_ref, u_ref, q_ref, k_ref, v_ref, og_ref, gp_ref, gg_ref, z_ref, *, n_prompt_tiles):
    i = pl.program_id(0)
    x = jnp.where(i < n_prompt_tiles, xp_ref[...], xs_ref[...])
    h = _layer_norm(x, g_ref[...], b_ref[...])
    h_ref[...] = h
    hb = h.astype(BF16)
    off = 0
    for ref, width in zip((u_ref, q_ref, k_ref, v_ref, og_ref, gp_ref, gg_ref), _MAIN_SPLITS):
        ref[...] = jnp.dot(hb, w_ref[:, off:off + width], preferred_element_type=F32).astype(BF16)
        off += width
    z_ref[...] = jnp.dot(hb, wz_ref[...], preferred_element_type=F32).astype(BF16)


def _inproj(xp, xs, ln_g, ln_b, w_main, w_z):
    n_p, n_s = xp.shape[0], xs.shape[0]
    n = n_p + n_s
    t = TILE_INPROJ
    npt, nst = n_p // t, n_s // t
    row = lambda i: (i, 0)
    widths = _MAIN_SPLITS + (Z_PAD,)
    out_shape = [jax.ShapeDtypeStruct((n, D_MODEL), F32)] + [jax.ShapeDtypeStruct((n, w), BF16) for w in widths]
    out_specs = [pl.BlockSpec((t, D_MODEL), row)] + [pl.BlockSpec((t, w), row) for w in widths]
    return pl.pallas_call(
        functools.partial(_inproj_kernel, n_prompt_tiles=npt),
        grid=(npt + nst,),
        in_specs=[
            pl.BlockSpec((t, D_MODEL), lambda i: (jnp.minimum(i, npt - 1), 0)),
            pl.BlockSpec((t, D_MODEL), lambda i: (jnp.maximum(i - npt, 0), 0)),
            _const_spec((1, D_MODEL)), _const_spec((1, D_MODEL)),
            _const_spec(w_main.shape), _const_spec(w_z.shape),
        ],
        out_specs=out_specs,
        out_shape=out_shape,
        compiler_params=_params(("parallel",)),
        name="inproj",
    )(xp, xs, ln_g, ln_b, w_main, w_z)


def _pool_kernel(u_ref, up_ref, un_ref, gp_ref, band_ref, pw_ref, sc_ref, wup_ref, pm_ref, *, starts, ends):
    t = u_ref.shape[0]
    i = pl.program_id(0)
    is_start = _tile_hits(i, t, starts)
    is_end = _tile_hits(i + 1, t, ends)
    cur = u_ref[...]
    prev = jnp.where(is_start, jnp.zeros_like(up_ref[...]), up_ref[...])
    nxt = jnp.where(is_end, jnp.zeros_like(un_ref[...]), un_ref[...])
    ext = jnp.concatenate([prev, cur, nxt], axis=0)
    r = lax.broadcasted_iota(jnp.int32, (t, POOL_GROUP_W), 0).astype(F32)
    fs = is_start.astype(F32)
    fe = is_end.astype(F32)
    parts = []
    for gi, w in enumerate(POOL_WINDOWS):
        sl = slice(gi * POOL_GROUP_W, (gi + 1) * POOL_GROUP_W)
        s = jnp.dot(band_ref[gi], ext[:, sl], preferred_element_type=F32)
        cnt = w - fs * jnp.maximum(w // 2 - r, 0.0) - fe * jnp.maximum(r + (w // 2 - t), 0.0)
        parts.append(s / cnt - cur[:, sl].astype(F32))
    p = jnp.concatenate(parts, axis=1).astype(BF16)
    p = (jnp.dot(p, pw_ref[...], preferred_element_type=F32) * sc_ref[...]).astype(BF16)
    pool_out = jnp.dot(p, wup_ref[...], preferred_element_type=F32)
    pm_ref[...] = (_sigmoid(gp_ref[...].astype(F32)) * pool_out).astype(BF16)


def _pool_bands(t):
    r = jnp.arange(t)[:, None]
    c = jnp.arange(t + 2 * POOL_HALO)[None, :] - POOL_HALO
    return jnp.stack([((c >= r - w // 2) & (c <= r + w // 2 - 1)).astype(BF16) for w in POOL_WINDOWS])


def _pool(u, gp, pool_w_bd, pool_scale, w_pool_up, starts, ends):
    n = u.shape[0]
    t = TILE_POOL
    hb = t // POOL_HALO
    last = n // POOL_HALO - 1
    return pl.pallas_call(
        functools.partial(_pool_kernel, starts=starts, ends=ends),
        grid=(n // t,),
        in_specs=[
            pl.BlockSpec((t, POOL_W), lambda i: (i, 0)),
            pl.BlockSpec((POOL_HALO, POOL_W), lambda i: (jnp.maximum(i * hb - 1, 0), 0)),
            pl.BlockSpec((POOL_HALO, POOL_W), lambda i: (jnp.minimum((i + 1) * hb, last), 0)),
            pl.BlockSpec((t, D_MODEL), lambda i: (i, 0)),
            _const_spec((POOL_GROUPS, t, t + 2 * POOL_HALO)),
            _const_spec((POOL_W, POOL_W)), _const_spec((1, POOL_W)), _const_spec((POOL_W, D_MODEL)),
        ],
        out_specs=pl.BlockSpec((t, D_MODEL), lambda i: (i, 0)),
        out_shape=jax.ShapeDtypeStruct((n, D_MODEL), BF16),
        compiler_params=_params(("parallel",)),
        name="pool",
    )(u, u, u, gp, _pool_bands(t), pool_w_bd, pool_scale, w_pool_up)


def _gla_tile(q, k, v, la, tri, st_ref, *, reverse):
    t = q.shape[0]
    c_sz = GLA_CHUNK
    hi = la.astype(BF16)
    r1 = la - hi.astype(F32)
    mid = r1.astype(BF16)
    lo = (r1 - mid.astype(F32)).astype(BF16)
    b = (jnp.dot(tri, hi, preferred_element_type=F32) + jnp.dot(tri, mid, preferred_element_type=F32)
         + jnp.dot(tri, lo, preferred_element_type=F32))
    kf = k.astype(F32)
    qd = (q.astype(F32) * jnp.exp(b) * (GLA_DK ** -0.5)).astype(BF16)
    kd = (kf * jnp.exp(-b)).astype(BF16)
    row = lax.broadcasted_iota(jnp.int32, (c_sz, c_sz), 0)
    col = lax.broadcasted_iota(jnp.int32, (c_sz, c_sz), 1)
    mask = (col >= row) if reverse else (col <= row)
    n_chunks = t // c_sz
    outs = [None] * n_chunks
    nt = (((1,), (1,)), ((), ()))
    tn = (((0,), (0,)), ((), ()))
    for c in (range(n_chunks - 1, -1, -1) if reverse else range(n_chunks)):
        rows = slice(c * c_sz, (c + 1) * c_sz)
        b_c = b[rows]
        b_edge = b_c[0:1] if reverse else b_c[c_sz - 1:c_sz]
        k2 = (kf[rows] * jnp.exp(b_edge - b_c)).astype(BF16)
        dec = jnp.exp(b_edge)
        heads = []
        for h in range(GLA_HEADS):
            ls = slice(h * GLA_DK, (h + 1) * GLA_DK)
            vs = slice(h * GLA_DV, (h + 1) * GLA_DV)
            qh = qd[rows, ls]
            att = lax.dot_general(qh, kd[rows, ls], nt, preferred_element_type=F32)
            att = jnp.where(mask, att, 0.0).astype(BF16)
            vh = v[rows, vs]
            st = st_ref[h]
            o = jnp.dot(att, vh, preferred_element_type=F32)
            o = o + lax.dot_general(qh, st.astype(BF16), nt, preferred_element_type=F32)
            st_ref[h] = st * dec[:, ls] + lax.dot_general(vh, k2[:, ls], tn, preferred_element_type=F32)
            heads.append(o)
        outs[c] = jnp.concatenate(heads, axis=1)
    return jnp.concatenate(outs, axis=0)


def _log_decay(z, w2_ref, gb_ref):
    return _log_sigmoid(jnp.dot(z, w2_ref[...], preferred_element_type=F32) + gb_ref[...]) * (1.0 / GATE_TAU)


def _gla_bwd_kernel(q_ref, k_ref, v_ref, z_ref, w2_ref, gb_ref, tri_ref, ob_ref, st_ref, *, n_tiles, ends):
    t = q_ref.shape[0]
    j = n_tiles - 1 - pl.program_id(0)

    @pl.when(_tile_hits(j + 1, t, ends))
    def _():
        st_ref[...] = jnp.zeros_like(st_ref)

    la = _log_decay(z_ref[...], w2_ref, gb_ref)
    ob_ref[...] = _gla_tile(q_ref[...], k_ref[...], v_ref[...], la, tri_ref[...], st_ref, reverse=True)


def _gla_fwd_kernel(q_ref, k_ref, v_ref, z_ref, w2_ref, gb_ref, tri_ref, ob_ref, og_ref, gg_ref, pm_ref, h_ref,
                    ng_ref, wgu_ref, wo_ref, l1g_ref, l1b_ref, h1_ref, h1rc_ref, st_ref, *, starts):
    t = q_ref.shape[0]
    i = pl.program_id(0)

    @pl.when(_tile_hits(i, t, starts))
    def _():
        st_ref[...] = jnp.zeros_like(st_ref)

    la = _log_decay(z_ref[...], w2_ref, gb_ref)
    o = _gla_tile(q_ref[...], k_ref[...], v_ref[...], la, tri_ref[...], st_ref, reverse=False) + ob_ref[...]
    heads = []
    for h in range(GLA_HEADS):
        oh = o[:, h * GLA_DV:(h + 1) * GLA_DV]
        heads.append(oh * lax.rsqrt(jnp.mean(oh * oh, axis=-1, keepdims=True) + RMS_EPS))
    o = jnp.concatenate(heads, axis=1) * ng_ref[...]
    o = (o * _silu(og_ref[...].astype(F32))).astype(BF16)
    gla_out = jnp.dot(o, wgu_ref[...], preferred_element_type=F32)
    merged = _sigmoid(gg_ref[...].astype(F32)) * gla_out + pm_ref[...].astype(F32)
    mix = jnp.dot(merged.astype(BF16), wo_ref[...], preferred_element_type=F32)
    h1 = _layer_norm(DN_ALPHA * h_ref[...] + mix, l1g_ref[...], l1b_ref[...])
    h1_ref[...] = h1
    _store_row_contiguous(h1rc_ref, h1)


def _store_row_contiguous(ref, x):
    t = x.shape[0]
    bits = pltpu.bitcast(x.astype(BF16).astype(F32), jnp.uint32)
    words = (bits[:, :ROW_WORDS] >> 16) | (bits[:, ROW_WORDS:] & jnp.uint32(0xFFFF0000))
    words = pltpu.bitcast(words, F32)
    for s in range(ROW_TILES):
        ref[pl.ds(s, t, stride=ROW_TILES), :] = words[:, s * LANES:(s + 1) * LANES]


def _load_row_contiguous(ref, t):
    words = jnp.concatenate([ref[pl.ds(s, t, stride=ROW_TILES), :] for s in range(ROW_TILES)], axis=1)
    bits = pltpu.bitcast(words, jnp.uint32)
    lo = pltpu.bitcast(bits << 16, F32)
    hi = pltpu.bitcast(bits & jnp.uint32(0xFFFF0000), F32)
    return jnp.concatenate([lo, hi], axis=1)


def _chunk_tri(t, reverse):
    r = jnp.arange(t)[:, None]
    c = jnp.arange(t)[None, :]
    same = (r // GLA_CHUNK) == (c // GLA_CHUNK)
    return (same & ((c >= r) if reverse else (c <= r))).astype(BF16)


def _gla_bwd(q, k, v, z, w2b_pad, gate_b, ends):
    n = q.shape[0]
    t = TILE_GLA
    nt = n // t
    rev = lambda i: (nt - 1 - i, 0)
    return pl.pallas_call(
        functools.partial(_gla_bwd_kernel, n_tiles=nt, ends=ends),
        grid=(nt,),
        in_specs=[
            pl.BlockSpec((t, GLA_K_W), rev), pl.BlockSpec((t, GLA_K_W), rev), pl.BlockSpec((t, GLA_V_W), rev),
            pl.BlockSpec((t, Z_PAD), rev),
            _const_spec((Z_PAD, GLA_K_W)), _const_spec((1, GLA_K_W)), _const_spec((t, t)),
        ],
        out_specs=pl.BlockSpec((t, GLA_V_W), rev),
        out_shape=jax.ShapeDtypeStruct((n, GLA_V_W), F32),
        scratch_shapes=[pltpu.VMEM((GLA_HEADS, GLA_DV, GLA_DK), F32)],
        compiler_params=_params(("arbitrary",)),
        name="gla_bwd",
    )(q, k, v, z, w2b_pad, gate_b, _chunk_tri(t, True))


def _gla_fwd_mix(q, k, v, z, w2f_pad, gate_b, ob, og, gg, pm, h, norm_g, w_gla_up, w_out, ln1_g, ln1_b, starts):
    n = q.shape[0]
    t = TILE_GLA
    row = lambda i: (i, 0)
    return pl.pallas_call(
        functools.partial(_gla_fwd_kernel, starts=starts),
        grid=(n // t,),
        in_specs=[
            pl.BlockSpec((t, GLA_K_W), row), pl.BlockSpec((t, GLA_K_W), row), pl.BlockSpec((t, GLA_V_W), row),
            pl.BlockSpec((t, Z_PAD), row),
            _const_spec((Z_PAD, GLA_K_W)), _const_spec((1, GLA_K_W)), _const_spec((t, t)),
            pl.BlockSpec((t, GLA_V_W), row), pl.BlockSpec((t, GLA_V_W), row), pl.BlockSpec((t, D_MODEL), row),
            pl.BlockSpec((t, D_MODEL), row), pl.BlockSpec((t, D_MODEL), row),
            _const_spec((1, GLA_V_W)), _const_spec((GLA_V_W, D_MODEL)), _const_spec((D_MODEL, D_MODEL)),
            _const_spec((1, D_MODEL)), _const_spec((1, D_MODEL)),
        ],
        out_specs=[pl.BlockSpec((t, D_MODEL), row), pl.BlockSpec((t * ROW_TILES, LANES), row)],
        out_shape=[jax.ShapeDtypeStruct((n, D_MODEL), F32), jax.ShapeDtypeStruct((n * ROW_TILES, LANES), F32)],
        scratch_shapes=[pltpu.VMEM((GLA_HEADS, GLA_DV, GLA_DK), F32)],
        compiler_params=_params(("arbitrary",)),
        name="gla_fwd_mix",
    )(q, k, v, z, w2f_pad, gate_b, _chunk_tri(t, False), ob, og, gg, pm, h, norm_g, w_gla_up, w_out, ln1_g, ln1_b)


def _router_kernel(h_ref, wrh_ref, wrl_ref, rb_ref, tri_ref, idx_ref, wts_ref, rank_ref, cnt_ref, base_ref):
    t = h_ref.shape[0]

    @pl.when(pl.program_id(0) == 0)
    def _():
        base_ref[...] = jnp.zeros_like(base_ref)

    h = h_ref[...]
    hh = h.astype(BF16)
    hl = (h - hh.astype(F32)).astype(BF16)
    nt = (((1,), (1,)), ((), ()))
    logits = (lax.dot_general(wrh_ref[...], hh, nt, preferred_element_type=F32)
              + lax.dot_general(wrl_ref[...], hh, nt, preferred_element_type=F32)
              + lax.dot_general(wrh_ref[...], hl, nt, preferred_element_type=F32))
    scores = _sigmoid(logits)
    sel = scores + rb_ref[:, 0:1]
    neg = jnp.float32(-jnp.inf)

    sub = lax.broadcasted_iota(jnp.int32, (EXPERTS_PER_GROUP, t), 0)
    gscore = []
    for g in range(N_GROUPS):
        sg = sel[g * EXPERTS_PER_GROUP:(g + 1) * EXPERTS_PER_GROUP]
        m1 = jnp.max(sg, axis=0, keepdims=True)
        first = jnp.min(jnp.where(sg == m1, sub, EXPERTS_PER_GROUP), axis=0, keepdims=True)
        m2 = jnp.max(jnp.where(sub == first, neg, sg), axis=0, keepdims=True)
        gscore.append(m1 + m2)
    masked = []
    for g in range(N_GROUPS):
        beaten = jnp.zeros((1, t), jnp.int32)
        for g2 in range(N_GROUPS):
            if g2 == g:
                continue
            wins = (gscore[g2] >= gscore[g]) if g2 < g else (gscore[g2] > gscore[g])
            beaten = beaten + wins.astype(jnp.int32)
        keep = beaten < TOPK_GROUPS
        masked.append(jnp.where(keep, sel[g * EXPERTS_PER_GROUP:(g + 1) * EXPERTS_PER_GROUP], neg))
    cand = jnp.concatenate(masked, axis=0)

    eid = lax.broadcasted_iota(jnp.int32, (N_EXPERTS, t), 0)
    picked = []
    assign = jnp.zeros((N_EXPERTS, t), F32)
    wsum = jnp.zeros((1, t), F32)
    for kk in range(TOP_K):
        m = jnp.max(cand, axis=0, keepdims=True)
        e_k = jnp.min(jnp.where(cand == m, eid, N_EXPERTS), axis=0, keepdims=True)
        hit = eid == e_k
        w_k = jnp.sum(jnp.where(hit, scores, 0.0), axis=0, keepdims=True)
        cand = jnp.where(hit, neg, cand)
        assign = jnp.where(hit, 1.0, assign)
        wsum = wsum + w_k
        picked.append((e_k, w_k))
        idx_ref[kk:kk + 1, :] = e_k

    a16 = assign.astype(BF16)
    before = jnp.dot(a16, tri_ref[...], preferred_element_type=F32)
    base = base_ref[...]
    before = before + jnp.concatenate([base] * (t // 128), axis=1)
    inv = ROUTED_SCALE / wsum
    for kk, (e_k, w_k) in enumerate(picked):
        wts_ref[kk:kk + 1, :] = w_k * inv
        rank_ref[kk:kk + 1, :] = jnp.sum(jnp.where(eid == e_k, before, 0.0), axis=0, keepdims=True).astype(jnp.int32)
    total = base + jnp.dot(a16, jnp.ones((t, 128), BF16), preferred_element_type=F32)
    base_ref[...] = total
    cnt_ref[...] = total


def _router(h1, wr_hi, wr_lo, rbias):
    n = h1.shape[0]
    t = TILE_ROUTER
    r = jnp.arange(t)
    tri = (r[:, None] < r[None, :]).astype(BF16)
    col = lambda i: (0, i)
    idx, wts, rank, cnt = pl.pallas_call(
        _router_kernel,
        grid=(n // t,),
        in_specs=[
            pl.BlockSpec((t, D_MODEL), lambda i: (i, 0)),
            _const_spec((N_EXPERTS, D_MODEL)), _const_spec((N_EXPERTS, D_MODEL)),
            _const_spec((N_EXPERTS, 128)), _const_spec((t, t)),
        ],
        out_specs=[pl.BlockSpec((TOP_K, t), col), pl.BlockSpec((TOP_K, t), col), pl.BlockSpec((TOP_K, t), col),
                   _const_spec((N_EXPERTS, 128))],
        out_shape=[jax.ShapeDtypeStruct((TOP_K, n), jnp.int32), jax.ShapeDtypeStruct((TOP_K, n), F32),
                   jax.ShapeDtypeStruct((TOP_K, n), jnp.int32), jax.ShapeDtypeStruct((N_EXPERTS, 128), F32)],
        scratch_shapes=[pltpu.VMEM((N_EXPERTS, 128), F32)],
        compiler_params=_params(("arbitrary",)),
        name="router",
    )(h1, wr_hi, wr_lo, rbias, tri)
    return idx, wts, rank, cnt[:, 0].astype(jnp.int32)


def _slot_kernel(ps_ref, idx_ref, rank_ref, dest_ref):
    idx = idx_ref[...]

    def add_expert(e, acc):
        return acc + jnp.where(idx == e, ps_ref[e], 0)

    dest_ref[...] = lax.fori_loop(0, N_EXPERTS, add_expert, rank_ref[...], unroll=8)


def _slots(pstart, idx, rank):
    n = idx.shape[1]
    t = TILE_SLOTS
    col = lambda i, ps: (0, i)
    return pl.pallas_call(
        _slot_kernel,
        grid_spec=pltpu.PrefetchScalarGridSpec(
            num_scalar_prefetch=1,
            grid=(n // t,),
            in_specs=[pl.BlockSpec((TOP_K, t), col), pl.BlockSpec((TOP_K, t), col)],
            out_specs=pl.BlockSpec((TOP_K, t), col),
        ),
        out_shape=jax.ShapeDtypeStruct((TOP_K, n), jnp.int32),
        compiler_params=_params(("parallel",)),
        name="slots",
    )(pstart, idx, rank)


def _expert_kernel(first_ref, count_ref, used_ref, wg_ref, wu_ref, wd_ref, x_hbm, y_hbm,
                   wgu_s, wd_s, xbuf, ybuf, xsem, ysem):
    e = pl.program_id(0)
    rows = MOE_BLOCK * ROW_TILES
    used = used_ref[0]

    def x_copy(b, slot):
        return pltpu.make_async_copy(x_hbm.at[pl.ds(b * rows, rows)], xbuf.at[slot], xsem.at[slot])

    def y_copy(b, slot):
        return pltpu.make_async_copy(ybuf.at[slot], y_hbm.at[pl.ds(b * rows, rows)], ysem.at[slot])

    @pl.when(e == 0)
    def _():
        for b in range(X_SLOTS - 1):
            @pl.when(b < used)
            def _():
                x_copy(b, b).start()

    @pl.when(count_ref[e] > 0)
    def _():
        wgu_s[:, :EXPERT_HIDDEN] = wg_ref[0].astype(BF16)
        wgu_s[:, EXPERT_HIDDEN:] = wu_ref[0].astype(BF16)
        wd_s[...] = wd_ref[0].astype(BF16)

    def block(j, carry):
        b = first_ref[e] + j
        ahead = b + (X_SLOTS - 1)

        @pl.when(ahead < used)
        def _():
            x_copy(ahead, ahead % X_SLOTS).start()

        xs = b % X_SLOTS
        x_copy(b, xs).wait()
        x = _load_row_contiguous(xbuf.at[xs], MOE_BLOCK).astype(BF16)
        gu = jnp.dot(x, wgu_s[...], preferred_element_type=F32)
        hid = (_silu(gu[:, :EXPERT_HIDDEN]) * gu[:, EXPERT_HIDDEN:]).astype(BF16)
        y = jnp.dot(hid, wd_s[...], preferred_element_type=F32)
        ys = b % Y_SLOTS

        @pl.when(b >= Y_SLOTS)
        def _():
            y_copy(b - Y_SLOTS, ys).wait()

        _store_row_contiguous(ybuf.at[ys], y)
        y_copy(b, ys).start()
        return carry

    lax.fori_loop(0, count_ref[e], block, 0)

    @pl.when(e == pl.num_programs(0) - 1)
    def _():
        for back in range(1, Y_SLOTS + 1):
            @pl.when(used - back >= 0)
            def _():
                y_copy(used - back, (used - back) % Y_SLOTS).wait()


def _experts(first_block, block_count, n_used, xs, w_gate, w_up, w_down):
    rows = MOE_BLOCK * ROW_TILES
    n_exp = w_gate.shape[0]
    any_spec = pl.BlockSpec(memory_space=pl.ANY)
    return pl.pallas_call(
        _expert_kernel,
        grid_spec=pltpu.PrefetchScalarGridSpec(
            num_scalar_prefetch=3,
            grid=(n_exp,),
            in_specs=[
                pl.BlockSpec((1, D_MODEL, EXPERT_HIDDEN), lambda e, *_: (e, 0, 0)),
                pl.BlockSpec((1, D_MODEL, EXPERT_HIDDEN), lambda e, *_: (e, 0, 0)),
                pl.BlockSpec((1, EXPERT_HIDDEN, D_MODEL), lambda e, *_: (e, 0, 0)),
                any_spec,
            ],
            out_specs=any_spec,
            scratch_shapes=[
                pltpu.VMEM((D_MODEL, 2 * EXPERT_HIDDEN), BF16), pltpu.VMEM((EXPERT_HIDDEN, D_MODEL), BF16),
                pltpu.VMEM((X_SLOTS, rows, LANES), F32), pltpu.VMEM((Y_SLOTS, rows, LANES), F32),
                pltpu.SemaphoreType.DMA((X_SLOTS,)), pltpu.SemaphoreType.DMA((Y_SLOTS,)),
            ],
        ),
        out_shape=jax.ShapeDtypeStruct(xs.shape, F32),
        compiler_params=_params(("arbitrary",)),
        name="experts",
    )(first_block, block_count, n_used, w_gate, w_up, w_down, xs)


def _sc_mesh():
    return plsc.VectorSubcoreMesh(core_axis_name="c", subcore_axis_name="s")


def _sc_dispatch(x3, dest_w, n_rows):
    n = x3.shape[0]

    @functools.partial(pl.kernel, out_type=jax.ShapeDtypeStruct((n_rows, ROW_TILES, LANES), F32), mesh=_sc_mesh(),
                       name="sc_dispatch")
    def run(x_hbm, d_hbm, o_hbm):
        def body(x_vmem, d_vmem):
            for kk in range(TOP_K):
                pltpu.sync_copy(x_vmem, o_hbm.at[d_vmem.at[0].at[kk]])

        pltpu.emit_pipeline(
            body,
            grid=(n // SC_WINDOW,),
            in_specs=[pl.BlockSpec((SC_WINDOW, ROW_TILES, LANES), lambda i: (i, 0, 0)),
                      pl.BlockSpec((1, TOP_K, SC_WINDOW), lambda i: (i, 0, 0))],
            out_specs=[],
            core_axis_name=("c", "s"),
            dimension_semantics=(pltpu.PARALLEL,),
        )(x_hbm, d_hbm)

    return run(x3, dest_w)


def _sc_gather(y3, dest_w):
    nw = dest_w.shape[0]
    n = nw * SC_WINDOW

    @functools.partial(pl.kernel, out_type=jax.ShapeDtypeStruct((TOP_K, n, ROW_TILES, LANES), F32), mesh=_sc_mesh(),
                       name="sc_gather")
    def run(y_hbm, d_hbm, o_hbm):
        def body(d_vmem, g_vmem):
            pltpu.sync_copy(y_hbm.at[d_vmem.at[0].at[0]], g_vmem.at[0])

        pltpu.emit_pipeline(
            body,
            grid=(nw, TOP_K),
            in_specs=[pl.BlockSpec((1, 1, SC_WINDOW), lambda i, k: (i, k, 0))],
            out_specs=[pl.BlockSpec((1, SC_WINDOW, ROW_TILES, LANES), lambda i, k: (k, i, 0, 0))],
            core_axis_name=("c", "s"),
            dimension_semantics=(pltpu.PARALLEL, pltpu.PARALLEL),
        )(d_hbm, o_hbm)

    return run(y3, dest_w)


def _final_kernel(h_ref, g_ref, w_ref, wgu_ref, wd_ref, lg_ref, lb_ref, yp_ref, ys_ref, *, n_prompt_tiles):
    t = h_ref.shape[0]
    h = h_ref[...]
    gu = jnp.dot(h.astype(BF16), wgu_ref[...], preferred_element_type=F32)
    hid = (_silu(gu[:, :SHARED_HIDDEN]) * gu[:, SHARED_HIDDEN:]).astype(BF16)
    ffn = jnp.dot(hid, wd_ref[...], preferred_element_type=F32)
    w = w_ref[...]
    for kk in range(TOP_K):
        ffn = ffn + _load_row_contiguous(g_ref.at[kk], t) * w[:, kk:kk + 1]
    y = _layer_norm(DN_ALPHA * h + ffn, lg_ref[...], lb_ref[...])
    is_prompt = pl.program_id(0) < n_prompt_tiles

    @pl.when(is_prompt)
    def _():
        yp_ref[...] = y

    @pl.when(jnp.logical_not(is_prompt))
    def _():
        ys_ref[...] = y


def _final(h1, gathered, wts_t, w_sh_gu, w_sh_down, ln_g, ln_b, n_p):
    n = h1.shape[0]
    t = TILE_FINAL
    npt = n_p // t
    row = lambda i: (i, 0)
    return pl.pallas_call(
        functools.partial(_final_kernel, n_prompt_tiles=npt),
        grid=(n // t,),
        in_specs=[
            pl.BlockSpec((t, D_MODEL), row),
            pl.BlockSpec((TOP_K, t * ROW_TILES, LANES), lambda i: (0, i, 0)),
            pl.BlockSpec((t, TOP_K), row),
            _const_spec((D_MODEL, 2 * SHARED_HIDDEN)), _const_spec((SHARED_HIDDEN, D_MODEL)),
            _const_spec((1, D_MODEL)), _const_spec((1, D_MODEL)),
        ],
        out_specs=[pl.BlockSpec((t, D_MODEL), lambda i: (jnp.minimum(i, npt - 1), 0)),
                   pl.BlockSpec((t, D_MODEL), lambda i: (jnp.maximum(i - npt, 0), 0))],
        out_shape=[jax.ShapeDtypeStruct((n_p, D_MODEL), F32), jax.ShapeDtypeStruct((n - n_p, D_MODEL), F32)],
        compiler_params=_params(("arbitrary",)),
        name="final",
    )(h1, gathered, wts_t, w_sh_gu, w_sh_down, ln_g, ln_b)


def _block_diag(w):
    g, a, b = w.shape
    out = jnp.zeros((g * a, g * b), w.dtype)
    for i in range(g):
        out = out.at[i * a:(i + 1) * a, i * b:(i + 1) * b].set(w[i])
    return out


def kernel(x_prompt, x_sample, ln0_g, ln0_b, w_in, pool_w, pool_scale, w_pool_up, gate_w2_fwd, gate_b_fwd,
           gate_w2_bwd, gate_b_bwd, gla_norm_g, w_gla_up, w_out, ln1_g, ln1_b, w_router, router_bias,
           w_exp_gate, w_exp_up, w_exp_down, w_sh_gate, w_sh_up, w_sh_down, ln2_g, ln2_b):
    bp, lp, d = x_prompt.shape
    bs, ls, _ = x_sample.shape
    n_p, n_s = bp * lp, bs * ls
    n = n_p + n_s
    starts = tuple(b * lp for b in range(bp)) + tuple(n_p + b * ls for b in range(bs))
    ends = tuple((b + 1) * lp for b in range(bp)) + tuple(n_p + (b + 1) * ls for b in range(bs))
    vec = lambda a: a.reshape(1, -1).astype(F32)

    w = w_in[0]
    c_u, c_q, c_k, c_v, c_og = 0, 512, 1024, 1536, 2560
    c_zf, c_zb, c_gp, c_gg = 3584, 3600, 3616, 4640
    w_main = jnp.concatenate([w[:, c_u:c_zf], w[:, c_gp:]], axis=1).astype(BF16)
    w_z = jnp.zeros((d, Z_PAD), F32).at[:, :2 * GATE_RANK].set(w[:, c_zf:c_gp]).astype(BF16)
    w2f = jnp.zeros((Z_PAD, GLA_K_W), F32).at[:GATE_RANK].set(gate_w2_fwd[0]).astype(BF16)
    w2b = jnp.zeros((Z_PAD, GLA_K_W), F32).at[GATE_RANK:2 * GATE_RANK].set(gate_w2_bwd[0]).astype(BF16)

    h, u, q, k, v, og, gp, gg, z = _inproj(x_prompt.reshape(n_p, d), x_sample.reshape(n_s, d),
                                           vec(ln0_g), vec(ln0_b), w_main, w_z)
    pm = _pool(u, gp, _block_diag(pool_w[0]).astype(BF16), vec(pool_scale[0]), w_pool_up[0].astype(BF16),
               starts, ends)
    ob = _gla_bwd(q, k, v, z, w2b, vec(gate_b_bwd[0]), ends)
    h1, h1_rc = _gla_fwd_mix(q, k, v, z, w2f, vec(gate_b_fwd[0]), ob, og, gg, pm, h, vec(gla_norm_g[0]),
                             w_gla_up[0].astype(BF16), w_out[0].astype(BF16), vec(ln1_g[0]), vec(ln1_b[0]), starts)

    wr_t = w_router[0].T
    wr_hi = wr_t.astype(BF16)
    wr_lo = (wr_t - wr_hi.astype(F32)).astype(BF16)
    rbias = jnp.broadcast_to(router_bias[0].astype(F32)[:, None], (N_EXPERTS, 128))
    idx, wts, rank, counts = _router(h1, wr_hi, wr_lo, rbias)

    padded = (counts + MOE_BLOCK - 1) // MOE_BLOCK * MOE_BLOCK
    pend = jnp.cumsum(padded)
    pstart = pend - padded
    n_rows = n * TOP_K + N_EXPERTS * MOE_BLOCK
    first_block = (pstart // MOE_BLOCK).astype(jnp.int32)
    block_count = (padded // MOE_BLOCK).astype(jnp.int32)
    n_used = (pend[-1:] // MOE_BLOCK).astype(jnp.int32)
    dest = _slots(pstart.astype(jnp.int32), idx, rank)

    dest_w = dest.reshape(TOP_K, n // SC_WINDOW, SC_WINDOW).transpose(1, 0, 2)
    xs = _sc_dispatch(h1_rc.reshape(n, ROW_TILES, LANES), dest_w, n_rows)
    ys = _experts(first_block, block_count, n_used, xs.reshape(n_rows * ROW_TILES, LANES),
                  w_exp_gate[0], w_exp_up[0], w_exp_down[0])
    gathered = _sc_gather(ys.reshape(n_rows, ROW_TILES, LANES), dest_w)

    y_p, y_s = _final(h1, gathered.reshape(TOP_K, n * ROW_TILES, LANES), wts.T,
                      jnp.concatenate([w_sh_gate[0], w_sh_up[0]], axis=1).astype(BF16),
                      w_sh_down[0].astype(BF16), vec(ln2_g[0]), vec(ln2_b[0]), n_p)
    return y_p.reshape(bp, lp, d), y_s.reshape(bs, ls, d)
```

```python
import functools

import jax
import jax.numpy as jnp
from jax import lax
from jax.experimental import pallas as pl
from jax.experimental.pallas import tpu as pltpu
from jax.experimental.pallas import tpu_sc as plsc

F32 = jnp.float32
BF16 = jnp.bfloat16

D_MODEL = 1024
POOL_GROUPS = 4
POOL_GROUP_W = 128
POOL_W = POOL_GROUPS * POOL_GROUP_W
POOL_WINDOWS = (2, 4, 8, 16)
POOL_HALO = 16
GLA_HEADS = 4
GLA_K_W = 512
GLA_V_W = 1024
GLA_DK = GLA_K_W // GLA_HEADS
GLA_DV = GLA_V_W // GLA_HEADS
GATE_RANK = 16
GATE_TAU = 16.0
GLA_CHUNK = 64
Z_PAD = 128
N_EXPERTS = 256
TOP_K = 8
N_GROUPS = 8
TOPK_GROUPS = 4
EXPERTS_PER_GROUP = N_EXPERTS // N_GROUPS
EXPERT_HIDDEN = 256
SHARED_HIDDEN = 256
ROUTED_SCALE = 2.5
DEPTH = 1
DN_ALPHA = (2 * DEPTH) ** 0.25
LN_EPS = 1e-5
RMS_EPS = 1e-6

VMEM_LIMIT_BYTES = 56 * 1024 * 1024

TILE_INPROJ = 512
TILE_POOL = 512
TILE_GLA = 256
TILE_ROUTER = 512
TILE_FINAL = 512
TILE_SLOTS = 2048
MOE_BLOCK = 256
X_SLOTS = 3
Y_SLOTS = 2
LANES = 128
ROW_WORDS = D_MODEL // 2
ROW_TILES = ROW_WORDS // LANES
SC_WINDOW = 64


def _params(semantics):
    return pltpu.CompilerParams(dimension_semantics=semantics, vmem_limit_bytes=VMEM_LIMIT_BYTES)


def _const_spec(shape):
    nd = len(shape)
    return pl.BlockSpec(shape, lambda *_: (0,) * nd)


def _layer_norm(x, g, b):
    mu = jnp.mean(x, axis=-1, keepdims=True)
    xc = x - mu
    var = jnp.mean(xc * xc, axis=-1, keepdims=True)
    return xc * lax.rsqrt(var + LN_EPS) * g + b


def _sigmoid(x):
    return 1.0 / (1.0 + jnp.exp(-x))


def _log_sigmoid(x):
    return jnp.minimum(x, 0.0) - jnp.log(1.0 + jnp.exp(-jnp.abs(x)))


def _silu(x):
    return x * _sigmoid(x)


def _tile_hits(i, tile, offsets):
    hit = i < 0
    for off in offsets:
        if off % tile == 0:
            hit = jnp.logical_or(hit, i == off // tile)
    return hit


_MAIN_SPLITS = (POOL_W, GLA_K_W, GLA_K_W, GLA_V_W, GLA_V_W, D_MODEL, D_MODEL)


def _inproj_kernel(xp_ref, xs_ref, g_ref, b_ref, w_ref, wz_ref,
                   h_ref, u_ref, q_ref, k_ref, v_ref, og_ref, gp_ref, gg_ref, z_ref, *, n_prompt_tiles):
    i = pl.program_id(0)
    x = jnp.where(i < n_prompt_tiles, xp_ref[...], xs_ref[...])
    h = _layer_norm(x, g_ref[...], b_ref[...])
    h_ref[...] = h
    hb = h.astype(BF16)
    off = 0
    for ref, width in zip((u_ref, q_ref, k_ref, v_ref, og_ref, gp_ref, gg_ref), _MAIN_SPLITS):
        ref[...] = jnp.dot(hb, w_ref[:, off:off + width], preferred_element_type=F32).astype(BF16)
        off += width
    z_ref[...] = jnp.dot(hb, wz_ref[...], preferred_element_type=F32).astype(BF16)


def _inproj(xp, xs, ln_g, ln_b, w_main, w_z):
    n_p, n_s = xp.shape[0], xs.shape[0]
    n = n_p + n_s
    t = TILE_INPROJ
    npt, nst = n_p // t, n_s // t
    row = lambda i: (i, 0)
    widths = _MAIN_SPLITS + (Z_PAD,)
    out_shape = [jax.ShapeDtypeStruct((n, D_MODEL), F32)] + [jax.ShapeDtypeStruct((n, w), BF16) for w in widths]
    out_specs = [pl.BlockSpec((t, D_MODEL), row)] + [pl.BlockSpec((t, w), row) for w in widths]
    return pl.pallas_call(
        functools.partial(_inproj_kernel, n_prompt_tiles=npt),
        grid=(npt + nst,),
        in_specs=[
            pl.BlockSpec((t, D_MODEL), lambda i: (jnp.minimum(i, npt - 1), 0)),
            pl.BlockSpec((t, D_MODEL), lambda i: (jnp.maximum(i - npt, 0), 0)),
            _const_spec((1, D_MODEL)), _const_spec((1, D_MODEL)),
            _const_spec(w_main.shape), _const_spec(w_z.shape),
        ],
        out_specs=out_specs,
        out_shape=out_shape,
        compiler_params=_params(("parallel",)),
        name="inproj",
    )(xp, xs, ln_g, ln_b, w_main, w_z)


def _pool_kernel(u_ref, up_ref, un_ref, gp_ref, band_ref, pw_ref, sc_ref, wup_ref, pm_ref, *, starts, ends):
    t = u_ref.shape[0]
    i = pl.program_id(0)
    is_start = _tile_hits(i, t, starts)
    is_end = _tile_hits(i + 1, t, ends)
    cur = u_ref[...]
    prev = jnp.where(is_start, jnp.zeros_like(up_ref[...]), up_ref[...])
    nxt = jnp.where(is_end, jnp.zeros_like(un_ref[...]), un_ref[...])
    ext = jnp.concatenate([prev, cur, nxt], axis=0)
    r = lax.broadcasted_iota(jnp.int32, (t, POOL_GROUP_W), 0).astype(F32)
    fs = is_start.astype(F32)
    fe = is_end.astype(F32)
    parts = []
    for gi, w in enumerate(POOL_WINDOWS):
        sl = slice(gi * POOL_GROUP_W, (gi + 1) * POOL_GROUP_W)
        s = jnp.dot(band_ref[gi], ext[:, sl], preferred_element_type=F32)
        cnt = w - fs * jnp.maximum(w // 2 - r, 0.0) - fe * jnp.maximum(r + (w // 2 - t), 0.0)
        parts.append(s / cnt - cur[:, sl].astype(F32))
    p = jnp.concatenate(parts, axis=1).astype(BF16)
    p = (jnp.dot(p, pw_ref[...], preferred_element_type=F32) * sc_ref[...]).astype(BF16)
    pool_out = jnp.dot(p, wup_ref[...], preferred_element_type=F32)
    pm_ref[...] = (_sigmoid(gp_ref[...].astype(F32)) * pool_out).astype(BF16)


def _pool_bands(t):
    r = jnp.arange(t)[:, None]
    c = jnp.arange(t + 2 * POOL_HALO)[None, :] - POOL_HALO
    return jnp.stack([((c >= r - w // 2) & (c <= r + w // 2 - 1)).astype(BF16) for w in POOL_WINDOWS])


def _pool(u, gp, pool_w_bd, pool_scale, w_pool_up, starts, ends):
    n = u.shape[0]
    t = TILE_POOL
    hb = t // POOL_HALO
    last = n // POOL_HALO - 1
    return pl.pallas_call(
        functools.partial(_pool_kernel, starts=starts, ends=ends),
        grid=(n // t,),
        in_specs=[
            pl.BlockSpec((t, POOL_W), lambda i: (i, 0)),
            pl.BlockSpec((POOL_HALO, POOL_W), lambda i: (jnp.maximum(i * hb - 1, 0), 0)),
            pl.BlockSpec((POOL_HALO, POOL_W), lambda i: (jnp.minimum((i + 1) * hb, last), 0)),
            pl.BlockSpec((t, D_MODEL), lambda i: (i, 0)),
            _const_spec((POOL_GROUPS, t, t + 2 * POOL_HALO)),
            _const_spec((POOL_W, POOL_W)), _const_spec((1, POOL_W)), _const_spec((POOL_W, D_MODEL)),
        ],
        out_specs=pl.BlockSpec((t, D_MODEL), lambda i: (i, 0)),
        out_shape=jax.ShapeDtypeStruct((n, D_MODEL), BF16),
        compiler_params=_params(("parallel",)),
        name="pool",
    )(u, u, u, gp, _pool_bands(t), pool_w_bd, pool_scale, w_pool_up)


def _gla_tile(q, k, v, la, tri, st_ref, *, reverse):
    t = q.shape[0]
    c_sz = GLA_CHUNK
    hi = la.astype(BF16)
    r1 = la - hi.astype(F32)
    mid = r1.astype(BF16)
    lo = (r1 - mid.astype(F32)).astype(BF16)
    b = (jnp.dot(tri, hi, preferred_element_type=F32) + jnp.dot(tri, mid, preferred_element_type=F32)
         + jnp.dot(tri, lo, preferred_element_type=F32))
    kf = k.astype(F32)
    qd = (q.astype(F32) * jnp.exp(b) * (GLA_DK ** -0.5)).astype(BF16)
    kd = (kf * jnp.exp(-b)).astype(BF16)
    row = lax.broadcasted_iota(jnp.int32, (c_sz, c_sz), 0)
    col = lax.broadcasted_iota(jnp.int32, (c_sz, c_sz), 1)
    mask = (col >= row) if reverse else (col <= row)
    n_chunks = t // c_sz
    order = list(range(n_chunks - 1, -1, -1) if reverse else range(n_chunks))
    nt = (((1,), (1,)), ((), ()))
    tn = (((0,), (0,)), ((), ()))
    rows = [slice(c * c_sz, (c + 1) * c_sz) for c in range(n_chunks)]
    lanes = [slice(h * GLA_DK, (h + 1) * GLA_DK) for h in range(GLA_HEADS)]
    vcols = [slice(h * GLA_DV, (h + 1) * GLA_DV) for h in range(GLA_HEADS)]

    att, kv, dec = {}, {}, {}
    for c in order:
        b_c = b[rows[c]]
        b_edge = b_c[0:1] if reverse else b_c[c_sz - 1:c_sz]
        k2 = (kf[rows[c]] * jnp.exp(b_edge - b_c)).astype(BF16)
        dec[c] = jnp.exp(b_edge)
        for h in range(GLA_HEADS):
            att[c, h] = lax.dot_general(qd[rows[c], lanes[h]], kd[rows[c], lanes[h]], nt, preferred_element_type=F32)
            kv[c, h] = lax.dot_general(v[rows[c], vcols[h]], k2[:, lanes[h]], tn, preferred_element_type=F32)

    state_before = {}
    for h in range(GLA_HEADS):
        st = st_ref[h]
        for c in order:
            state_before[c, h] = st.astype(BF16)
            st = st * dec[c][:, lanes[h]] + kv[c, h]
        st_ref[h] = st

    outs = []
    for c in range(n_chunks):
        heads = []
        for h in range(GLA_HEADS):
            a = jnp.where(mask, att[c, h], 0.0).astype(BF16)
            o = jnp.dot(a, v[rows[c], vcols[h]], preferred_element_type=F32)
            heads.append(o + lax.dot_general(qd[rows[c], lanes[h]], state_before[c, h], nt,
                                             preferred_element_type=F32))
        outs.append(jnp.concatenate(heads, axis=1))
    return jnp.concatenate(outs, axis=0)


def _log_decay(z, w2_ref, gb_ref):
    return _log_sigmoid(jnp.dot(z, w2_ref[...], preferred_element_type=F32) + gb_ref[...]) * (1.0 / GATE_TAU)


def _gla_bwd_kernel(q_ref, k_ref, v_ref, z_ref, w2_ref, gb_ref, tri_ref, ob_ref, st_ref, *, n_tiles, ends):
    t = q_ref.shape[0]
    j = n_tiles - 1 - pl.program_id(0)

    @pl.when(_tile_hits(j + 1, t, ends))
    def _():
        st_ref[...] = jnp.zeros_like(st_ref)

    la = _log_decay(z_ref[...], w2_ref, gb_ref)
    ob_ref[...] = _gla_tile(q_ref[...], k_ref[...], v_ref[...], la, tri_ref[...], st_ref, reverse=True)


def _gla_fwd_kernel(q_ref, k_ref, v_ref, z_ref, w2_ref, gb_ref, tri_ref, ob_ref, og_ref, gg_ref, pm_ref, h_ref,
                    ng_ref, wgu_ref, wo_ref, l1g_ref, l1b_ref, h1_ref, h1rc_ref, st_ref, *, starts):
    t = q_ref.shape[0]
    i = pl.program_id(0)

    @pl.when(_tile_hits(i, t, starts))
    def _():
        st_ref[...] = jnp.zeros_like(st_ref)

    la = _log_decay(z_ref[...], w2_ref, gb_ref)
    o = _gla_tile(q_ref[...], k_ref[...], v_ref[...], la, tri_ref[...], st_ref, reverse=False) + ob_ref[...]
    heads = []
    for h in range(GLA_HEADS):
        oh = o[:, h * GLA_DV:(h + 1) * GLA_DV]
        heads.append(oh * lax.rsqrt(jnp.mean(oh * oh, axis=-1, keepdims=True) + RMS_EPS))
    o = jnp.concatenate(heads, axis=1) * ng_ref[...]
    o = (o * _silu(og_ref[...].astype(F32))).astype(BF16)
    gla_out = jnp.dot(o, wgu_ref[...], preferred_element_type=F32)
    merged = _sigmoid(gg_ref[...].astype(F32)) * gla_out + pm_ref[...].astype(F32)
    mix = jnp.dot(merged.astype(BF16), wo_ref[...], preferred_element_type=F32)
    h1 = _layer_norm(DN_ALPHA * h_ref[...] + mix, l1g_ref[...], l1b_ref[...])
    h1_ref[...] = h1
    _store_row_contiguous(h1rc_ref, h1)


def _store_row_contiguous(ref, x):
    t = x.shape[0]
    bits = pltpu.bitcast(x.astype(BF16).astype(F32), jnp.uint32)
    words = (bits[:, :ROW_WORDS] >> 16) | (bits[:, ROW_WORDS:] & jnp.uint32(0xFFFF0000))
    words = pltpu.bitcast(words, F32)
    for s in range(ROW_TILES):
        ref[pl.ds(s, t, stride=ROW_TILES), :] = words[:, s * LANES:(s + 1) * LANES]


def _load_row_contiguous(ref, t):
    words = jnp.concatenate([ref[pl.ds(s, t, stride=ROW_TILES), :] for s in range(ROW_TILES)], axis=1)
    bits = pltpu.bitcast(words, jnp.uint32)
    lo = pltpu.bitcast(bits << 16, F32)
    hi = pltpu.bitcast(bits & jnp.uint32(0xFFFF0000), F32)
    return jnp.concatenate([lo, hi], axis=1)


def _chunk_tri(t, reverse):
    r = jnp.arange(t)[:, None]
    c = jnp.arange(t)[None, :]
    same = (r // GLA_CHUNK) == (c // GLA_CHUNK)
    return (same & ((c >= r) if reverse else (c <= r))).astype(BF16)


def _gla_bwd(q, k, v, z, w2b_pad, gate_b, ends):
    n = q.shape[0]
    t = TILE_GLA
    nt = n // t
    rev = lambda i: (nt - 1 - i, 0)
    return pl.pallas_call(
        functools.partial(_gla_bwd_kernel, n_tiles=nt, ends=ends),
        grid=(nt,),
        in_specs=[
            pl.BlockSpec((t, GLA_K_W), rev), pl.BlockSpec((t, GLA_K_W), rev), pl.BlockSpec((t, GLA_V_W), rev),
            pl.BlockSpec((t, Z_PAD), rev),
            _const_spec((Z_PAD, GLA_K_W)), _const_spec((1, GLA_K_W)), _const_spec((t, t)),
        ],
        out_specs=pl.BlockSpec((t, GLA_V_W), rev),
        out_shape=jax.ShapeDtypeStruct((n, GLA_V_W), F32),
        scratch_shapes=[pltpu.VMEM((GLA_HEADS, GLA_DV, GLA_DK), F32)],
        compiler_params=_params(("arbitrary",)),
        name="gla_bwd",
    )(q, k, v, z, w2b_pad, gate_b, _chunk_tri(t, True))


def _gla_fwd_mix(q, k, v, z, w2f_pad, gate_b, ob, og, gg, pm, h, norm_g, w_gla_up, w_out, ln1_g, ln1_b, starts):
    n = q.shape[0]
    t = TILE_GLA
    row = lambda i: (i, 0)
    return pl.pallas_call(
        functools.partial(_gla_fwd_kernel, starts=starts),
        grid=(n // t,),
        in_specs=[
            pl.BlockSpec((t, GLA_K_W), row), pl.BlockSpec((t, GLA_K_W), row), pl.BlockSpec((t, GLA_V_W), row),
            pl.BlockSpec((t, Z_PAD), row),
            _const_spec((Z_PAD, GLA_K_W)), _const_spec((1, GLA_K_W)), _const_spec((t, t)),
            pl.BlockSpec((t, GLA_V_W), row), pl.BlockSpec((t, GLA_V_W), row), pl.BlockSpec((t, D_MODEL), row),
            pl.BlockSpec((t, D_MODEL), row), pl.BlockSpec((t, D_MODEL), row),
            _const_spec((1, GLA_V_W)), _const_spec((GLA_V_W, D_MODEL)), _const_spec((D_MODEL, D_MODEL)),
            _const_spec((1, D_MODEL)), _const_spec((1, D_MODEL)),
        ],
        out_specs=[pl.BlockSpec((t, D_MODEL), row), pl.BlockSpec((t * ROW_TILES, LANES), row)],
        out_shape=[jax.ShapeDtypeStruct((n, D_MODEL), F32), jax.ShapeDtypeStruct((n * ROW_TILES, LANES), F32)],
        scratch_shapes=[pltpu.VMEM((GLA_HEADS, GLA_DV, GLA_DK), F32)],
        compiler_params=_params(("arbitrary",)),
        name="gla_fwd_mix",
    )(q, k, v, z, w2f_pad, gate_b, _chunk_tri(t, False), ob, og, gg, pm, h, norm_g, w_gla_up, w_out, ln1_g, ln1_b)


def _router_kernel(h_ref, wrh_ref, wrl_ref, rb_ref, tri_ref, idx_ref, wts_ref, rank_ref, cnt_ref, base_ref):
    t = h_ref.shape[0]

    @pl.when(pl.program_id(0) == 0)
    def _():
        base_ref[...] = jnp.zeros_like(base_ref)

    h = h_ref[...]
    hh = h.astype(BF16)
    hl = (h - hh.astype(F32)).astype(BF16)
    nt = (((1,), (1,)), ((), ()))
    logits = (lax.dot_general(wrh_ref[...], hh, nt, preferred_element_type=F32)
              + lax.dot_general(wrl_ref[...], hh, nt, preferred_element_type=F32)
              + lax.dot_general(wrh_ref[...], hl, nt, preferred_element_type=F32))
    scores = _sigmoid(logits)
    sel = scores + rb_ref[:, 0:1]
    neg = jnp.float32(-jnp.inf)

    sub = lax.broadcasted_iota(jnp.int32, (EXPERTS_PER_GROUP, t), 0)
    gscore = []
    for g in range(N_GROUPS):
        sg = sel[g * EXPERTS_PER_GROUP:(g + 1) * EXPERTS_PER_GROUP]
        m1 = jnp.max(sg, axis=0, keepdims=True)
        first = jnp.min(jnp.where(sg == m1, sub, EXPERTS_PER_GROUP), axis=0, keepdims=True)
        m2 = jnp.max(jnp.where(sub == first, neg, sg), axis=0, keepdims=True)
        gscore.append(m1 + m2)
    masked = []
    for g in range(N_GROUPS):
        beaten = jnp.zeros((1, t), jnp.int32)
        for g2 in range(N_GROUPS):
            if g2 == g:
                continue
            wins = (gscore[g2] >= gscore[g]) if g2 < g else (gscore[g2] > gscore[g])
            beaten = beaten + wins.astype(jnp.int32)
        keep = beaten < TOPK_GROUPS
        masked.append(jnp.where(keep, sel[g * EXPERTS_PER_GROUP:(g + 1) * EXPERTS_PER_GROUP], neg))
    cand = jnp.concatenate(masked, axis=0)

    eid = lax.broadcasted_iota(jnp.int32, (N_EXPERTS, t), 0)
    picked = []
    assign = jnp.zeros((N_EXPERTS, t), F32)
    wsum = jnp.zeros((1, t), F32)
    for kk in range(TOP_K):
        m = jnp.max(cand, axis=0, keepdims=True)
        e_k = jnp.min(jnp.where(cand == m, eid, N_EXPERTS), axis=0, keepdims=True)
        hit = eid == e_k
        w_k = jnp.sum(jnp.where(hit, scores, 0.0), axis=0, keepdims=True)
        cand = jnp.where(hit, neg, cand)
        assign = jnp.where(hit, 1.0, assign)
        wsum = wsum + w_k
        picked.append((e_k, w_k))
        idx_ref[kk:kk + 1, :] = e_k

    a16 = assign.astype(BF16)
    before = jnp.dot(a16, tri_ref[...], preferred_element_type=F32)
    base = base_ref[...]
    before = before + jnp.concatenate([base] * (t // 128), axis=1)
    inv = ROUTED_SCALE / wsum
    for kk, (e_k, w_k) in enumerate(picked):
        wts_ref[kk:kk + 1, :] = w_k * inv
        rank_ref[kk:kk + 1, :] = jnp.sum(jnp.where(eid == e_k, before, 0.0), axis=0, keepdims=True).astype(jnp.int32)
    total = base + jnp.dot(a16, jnp.ones((t, 128), BF16), preferred_element_type=F32)
    base_ref[...] = total
    cnt_ref[...] = total


def _router(h1, wr_hi, wr_lo, rbias):
    n = h1.shape[0]
    t = TILE_ROUTER
    r = jnp.arange(t)
    tri = (r[:, None] < r[None, :]).astype(BF16)
    col = lambda i: (0, i)
    idx, wts, rank, cnt = pl.pallas_call(
        _router_kernel,
        grid=(n // t,),
        in_specs=[
            pl.BlockSpec((t, D_MODEL), lambda i: (i, 0)),
            _const_spec((N_EXPERTS, D_MODEL)), _const_spec((N_EXPERTS, D_MODEL)),
            _const_spec((N_EXPERTS, 128)), _const_spec((t, t)),
        ],
        out_specs=[pl.BlockSpec((TOP_K, t), col), pl.BlockSpec((TOP_K, t), col), pl.BlockSpec((TOP_K, t), col),
                   _const_spec((N_EXPERTS, 128))],
        out_shape=[jax.ShapeDtypeStruct((TOP_K, n), jnp.int32), jax.ShapeDtypeStruct((TOP_K, n), F32),
                   jax.ShapeDtypeStruct((TOP_K, n), jnp.int32), jax.ShapeDtypeStruct((N_EXPERTS, 128), F32)],
        scratch_shapes=[pltpu.VMEM((N_EXPERTS, 128), F32)],
        compiler_params=_params(("arbitrary",)),
        name="router",
    )(h1, wr_hi, wr_lo, rbias, tri)
    return idx, wts, rank, cnt[:, 0].astype(jnp.int32)


def _slot_kernel(ps_ref, idx_ref, rank_ref, dest_ref):
    idx = idx_ref[...]

    def add_expert(e, acc):
        return acc + jnp.where(idx == e, ps_ref[e], 0)

    dest_ref[...] = lax.fori_loop(0, N_EXPERTS, add_expert, rank_ref[...], unroll=8)


def _slots(pstart, idx, rank):
    n = idx.shape[1]
    t = TILE_SLOTS
    col = lambda i, ps: (0, i)
    return pl.pallas_call(
        _slot_kernel,
        grid_spec=pltpu.PrefetchScalarGridSpec(
            num_scalar_prefetch=1,
            grid=(n // t,),
            in_specs=[pl.BlockSpec((TOP_K, t), col), pl.BlockSpec((TOP_K, t), col)],
            out_specs=pl.BlockSpec((TOP_K, t), col),
        ),
        out_shape=jax.ShapeDtypeStruct((TOP_K, n), jnp.int32),
        compiler_params=_params(("parallel",)),
        name="slots",
    )(pstart, idx, rank)


def _expert_kernel(first_ref, count_ref, used_ref, wg_ref, wu_ref, wd_ref, x_hbm, y_hbm,
                   wgu_s, wd_s, xbuf, ybuf, xsem, ysem):
    e = pl.program_id(0)
    rows = MOE_BLOCK * ROW_TILES
    used = used_ref[0]

    def x_copy(b, slot):
        return pltpu.make_async_copy(x_hbm.at[pl.ds(b * rows, rows)], xbuf.at[slot], xsem.at[slot])

    def y_copy(b, slot):
        return pltpu.make_async_copy(ybuf.at[slot], y_hbm.at[pl.ds(b * rows, rows)], ysem.at[slot])

    @pl.when(e == 0)
    def _():
        for b in range(X_SLOTS - 1):
            @pl.when(b < used)
            def _():
                x_copy(b, b).start()

    @pl.when(count_ref[e] > 0)
    def _():
        wgu_s[:, :EXPERT_HIDDEN] = wg_ref[0].astype(BF16)
        wgu_s[:, EXPERT_HIDDEN:] = wu_ref[0].astype(BF16)
        wd_s[...] = wd_ref[0].astype(BF16)

    def block(j, carry):
        b = first_ref[e] + j
        ahead = b + (X_SLOTS - 1)

        @pl.when(ahead < used)
        def _():
            x_copy(ahead, ahead % X_SLOTS).start()

        xs = b % X_SLOTS
        x_copy(b, xs).wait()
        x = _load_row_contiguous(xbuf.at[xs], MOE_BLOCK).astype(BF16)
        gu = jnp.dot(x, wgu_s[...], preferred_element_type=F32)
        hid = (_silu(gu[:, :EXPERT_HIDDEN]) * gu[:, EXPERT_HIDDEN:]).astype(BF16)
        y = jnp.dot(hid, wd_s[...], preferred_element_type=F32)
        ys = b % Y_SLOTS

        @pl.when(b >= Y_SLOTS)
        def _():
            y_copy(b - Y_SLOTS, ys).wait()

        _store_row_contiguous(ybuf.at[ys], y)
        y_copy(b, ys).start()
        return carry

    lax.fori_loop(0, count_ref[e], block, 0)

    @pl.when(e == pl.num_programs(0) - 1)
    def _():
        for back in range(1, Y_SLOTS + 1):
            @pl.when(used - back >= 0)
            def _():
                y_copy(used - back, (used - back) % Y_SLOTS).wait()


def _experts(first_block, block_count, n_used, xs, w_gate, w_up, w_down):
    rows = MOE_BLOCK * ROW_TILES
    n_exp = w_gate.shape[0]
    any_spec = pl.BlockSpec(memory_space=pl.ANY)
    return pl.pallas_call(
        _expert_kernel,
        grid_spec=pltpu.PrefetchScalarGridSpec(
            num_scalar_prefetch=3,
            grid=(n_exp,),
            in_specs=[
                pl.BlockSpec((1, D_MODEL, EXPERT_HIDDEN), lambda e, *_: (e, 0, 0)),
                pl.BlockSpec((1, D_MODEL, EXPERT_HIDDEN), lambda e, *_: (e, 0, 0)),
                pl.BlockSpec((1, EXPERT_HIDDEN, D_MODEL), lambda e, *_: (e, 0, 0)),
                any_spec,
            ],
            out_specs=any_spec,
            scratch_shapes=[
                pltpu.VMEM((D_MODEL, 2 * EXPERT_HIDDEN), BF16), pltpu.VMEM((EXPERT_HIDDEN, D_MODEL), BF16),
                pltpu.VMEM((X_SLOTS, rows, LANES), F32), pltpu.VMEM((Y_SLOTS, rows, LANES), F32),
                pltpu.SemaphoreType.DMA((X_SLOTS,)), pltpu.SemaphoreType.DMA((Y_SLOTS,)),
            ],
        ),
        out_shape=jax.ShapeDtypeStruct(xs.shape, F32),
        compiler_params=_params(("arbitrary",)),
        name="experts",
    )(first_block, block_count, n_used, w_gate, w_up, w_down, xs)


def _sc_mesh():
    return plsc.VectorSubcoreMesh(core_axis_name="c", subcore_axis_name="s")


def _sc_dispatch(x3, dest_w, n_rows):
    n = x3.shape[0]

    @functools.partial(pl.kernel, out_type=jax.ShapeDtypeStruct((n_rows, ROW_TILES, LANES), F32), mesh=_sc_mesh(),
                       name="sc_dispatch")
    def run(x_hbm, d_hbm, o_hbm):
        def body(x_vmem, d_vmem):
            for kk in range(TOP_K):
                pltpu.sync_copy(x_vmem, o_hbm.at[d_vmem.at[0].at[kk]])

        pltpu.emit_pipeline(
            body,
            grid=(n // SC_WINDOW,),
            in_specs=[pl.BlockSpec((SC_WINDOW, ROW_TILES, LANES), lambda i: (i, 0, 0)),
                      pl.BlockSpec((1, TOP_K, SC_WINDOW), lambda i: (i, 0, 0))],
            out_specs=[],
            core_axis_name=("c", "s"),
            dimension_semantics=(pltpu.PARALLEL,),
        )(x_hbm, d_hbm)

    return run(x3, dest_w)


def _sc_gather(y3, dest_w):
    nw = dest_w.shape[0]
    n = nw * SC_WINDOW

    @functools.partial(pl.kernel, out_type=jax.ShapeDtypeStruct((TOP_K, n, ROW_TILES, LANES), F32), mesh=_sc_mesh(),
                       name="sc_gather")
    def run(y_hbm, d_hbm, o_hbm):
        def body(d_vmem, g_vmem):
            pltpu.sync_copy(y_hbm.at[d_vmem.at[0].at[0]], g_vmem.at[0])

        pltpu.emit_pipeline(
            body,
            grid=(nw, TOP_K),
            in_specs=[pl.BlockSpec((1, 1, SC_WINDOW), lambda i, k: (i, k, 0))],
            out_specs=[pl.BlockSpec((1, SC_WINDOW, ROW_TILES, LANES), lambda i, k: (k, i, 0, 0))],
            core_axis_name=("c", "s"),
            dimension_semantics=(pltpu.PARALLEL, pltpu.PARALLEL),
        )(d_hbm, o_hbm)

    return run(y3, dest_w)


def _final_kernel(h_ref, g_ref, w_ref, wgu_ref, wd_ref, lg_ref, lb_ref, yp_ref, ys_ref, *, n_prompt_tiles):
    t = h_ref.shape[0]
    h = h_ref[...]
    gu = jnp.dot(h.astype(BF16), wgu_ref[...], preferred_element_type=F32)
    hid = (_silu(gu[:, :SHARED_HIDDEN]) * gu[:, SHARED_HIDDEN:]).astype(BF16)
    ffn = jnp.dot(hid, wd_ref[...], preferred_element_type=F32)
    w = w_ref[...]
    for kk in range(TOP_K):
        ffn = ffn + _load_row_contiguous(g_ref.at[kk], t) * w[:, kk:kk + 1]
    y = _layer_norm(DN_ALPHA * h + ffn, lg_ref[...], lb_ref[...])
    is_prompt = pl.program_id(0) < n_prompt_tiles

    @pl.when(is_prompt)
    def _():
        yp_ref[...] = y

    @pl.when(jnp.logical_not(is_prompt))
    def _():
        ys_ref[...] = y


def _final(h1, gathered, wts_t, w_sh_gu, w_sh_down, ln_g, ln_b, n_p):
    n = h1.shape[0]
    t = TILE_FINAL
    npt = n_p // t
    row = lambda i: (i, 0)
    return pl.pallas_call(
        functools.partial(_final_kernel, n_prompt_tiles=npt),
        grid=(n // t,),
        in_specs=[
            pl.BlockSpec((t, D_MODEL), row),
            pl.BlockSpec((TOP_K, t * ROW_TILES, LANES), lambda i: (0, i, 0)),
            pl.BlockSpec((t, TOP_K), row),
            _const_spec((D_MODEL, 2 * SHARED_HIDDEN)), _const_spec((SHARED_HIDDEN, D_MODEL)),
            _const_spec((1, D_MODEL)), _const_spec((1, D_MODEL)),
        ],
        out_specs=[pl.BlockSpec((t, D_MODEL), lambda i: (jnp.minimum(i, npt - 1), 0)),
                   pl.BlockSpec((t, D_MODEL), lambda i: (jnp.maximum(i - npt, 0), 0))],
        out_shape=[jax.ShapeDtypeStruct((n_p, D_MODEL), F32), jax.ShapeDtypeStruct((n - n_p, D_MODEL), F32)],
        compiler_params=_params(("arbitrary",)),
        name="final",
    )(h1, gathered, wts_t, w_sh_gu, w_sh_down, ln_g, ln_b)


def _block_diag(w):
    g, a, b = w.shape
    out = jnp.zeros((g * a, g * b), w.dtype)
    for i in range(g):
        out = out.at[i * a:(i + 1) * a, i * b:(i + 1) * b].set(w[i])
    return out


def kernel(x_prompt, x_sample, ln0_g, ln0_b, w_in, pool_w, pool_scale, w_pool_up, gate_w2_fwd, gate_b_fwd,
           gate_w2_bwd, gate_b_bwd, gla_norm_g, w_gla_up, w_out, ln1_g, ln1_b, w_router, router_bias,
           w_exp_gate, w_exp_up, w_exp_down, w_sh_gate, w_sh_up, w_sh_down, ln2_g, ln2_b):
    bp, lp, d = x_prompt.shape
    bs, ls, _ = x_sample.shape
    n_p, n_s = bp * lp, bs * ls
    n = n_p + n_s
    starts = tuple(b * lp for b in range(bp)) + tuple(n_p + b * ls for b in range(bs))
    ends = tuple((b + 1) * lp for b in range(bp)) + tuple(n_p + (b + 1) * ls for b in range(bs))
    vec = lambda a: a.reshape(1, -1).astype(F32)

    w = w_in[0]
    c_u, c_q, c_k, c_v, c_og = 0, 512, 1024, 1536, 2560
    c_zf, c_zb, c_gp, c_gg = 3584, 3600, 3616, 4640
    w_main = jnp.concatenate([w[:, c_u:c_zf], w[:, c_gp:]], axis=1).astype(BF16)
    w_z = jnp.zeros((d, Z_PAD), F32).at[:, :2 * GATE_RANK].set(w[:, c_zf:c_gp]).astype(BF16)
    w2f = jnp.zeros((Z_PAD, GLA_K_W), F32).at[:GATE_RANK].set(gate_w2_fwd[0]).astype(BF16)
    w2b = jnp.zeros((Z_PAD, GLA_K_W), F32).at[GATE_RANK:2 * GATE_RANK].set(gate_w2_bwd[0]).astype(BF16)

    h, u, q, k, v, og, gp, gg, z = _inproj(x_prompt.reshape(n_p, d), x_sample.reshape(n_s, d),
                                           vec(ln0_g), vec(ln0_b), w_main, w_z)
    pm = _pool(u, gp, _block_diag(pool_w[0]).astype(BF16), vec(pool_scale[0]), w_pool_up[0].astype(BF16),
               starts, ends)
    ob = _gla_bwd(q, k, v, z, w2b, vec(gate_b_bwd[0]), ends)
    h1, h1_rc = _gla_fwd_mix(q, k, v, z, w2f, vec(gate_b_fwd[0]), ob, og, gg, pm, h, vec(gla_norm_g[0]),
                             w_gla_up[0].astype(BF16), w_out[0].astype(BF16), vec(ln1_g[0]), vec(ln1_b[0]), starts)

    wr_t = w_router[0].T
    wr_hi = wr_t.astype(BF16)
    wr_lo = (wr_t - wr_hi.astype(F32)).astype(BF16)
    rbias = jnp.broadcast_to(router_bias[0].astype(F32)[:, None], (N_EXPERTS, 128))
    idx, wts, rank, counts = _router(h1, wr_hi, wr_lo, rbias)

    padded = (counts + MOE_BLOCK - 1) // MOE_BLOCK * MOE_BLOCK
    pend = jnp.cumsum(padded)
    pstart = pend - padded
    n_rows = n * TOP_K + N_EXPERTS * MOE_BLOCK
    first_block = (pstart // MOE_BLOCK).astype(jnp.int32)
    block_count = (padded // MOE_BLOCK).astype(jnp.int32)
    n_used = (pend[-1:] // MOE_BLOCK).astype(jnp.int32)
    dest = _slots(pstart.astype(jnp.int32), idx, rank)

    dest_w = dest.reshape(TOP_K, n // SC_WINDOW, SC_WINDOW).transpose(1, 0, 2)
    xs = _sc_dispatch(h1_rc.reshape(n, ROW_TILES, LANES), dest_w, n_rows)
    ys = _experts(first_block, block_count, n_used, xs.reshape(n_rows * ROW_TILES, LANES),
                  w_exp_gate[0], w_exp_up[0], w_exp_down[0])
    gathered = _sc_gather(ys.reshape(n_rows, ROW_TILES, LANES), dest_w)

    y_p, y_s = _final(h1, gathered.reshape(TOP_K, n * ROW_TILES, LANES), wts.T,
                      jnp.concatenate([w_sh_gate[0], w_sh_up[0]], axis=1).astype(BF16),
                      w_sh_down[0].astype(BF16), vec(ln2_g[0]), vec(ln2_b[0]), n_p)
    return y_p.reshape(bp, lp, d), y_s.reshape(bs, ls, d)
```

```python
import functools

import jax
import jax.numpy as jnp
from jax import lax
from jax.experimental import pallas as pl
from jax.experimental.pallas import tpu as pltpu
from jax.experimental.pallas import tpu_sc as plsc

F32 = jnp.float32
BF16 = jnp.bfloat16

D_MODEL = 1024
POOL_GROUPS = 4
POOL_GROUP_W = 128
POOL_W = POOL_GROUPS * POOL_GROUP_W
POOL_WINDOWS = (2, 4, 8, 16)
POOL_HALO = 16
GLA_HEADS = 4
GLA_K_W = 512
GLA_V_W = 1024
GLA_DK = GLA_K_W // GLA_HEADS
GLA_DV = GLA_V_W // GLA_HEADS
GATE_RANK = 16
GATE_TAU = 16.0
GLA_CHUNK = 64
Z_PAD = 128
N_EXPERTS = 256
TOP_K = 8
N_GROUPS = 8
TOPK_GROUPS = 4
EXPERTS_PER_GROUP = N_EXPERTS // N_GROUPS
EXPERT_HIDDEN = 256
SHARED_HIDDEN = 256
ROUTED_SCALE = 2.5
DEPTH = 1
DN_ALPHA = (2 * DEPTH) ** 0.25
LN_EPS = 1e-5
RMS_EPS = 1e-6

VMEM_LIMIT_BYTES = 56 * 1024 * 1024

TILE_INPROJ = 512
TILE_POOL = 512
TILE_GLA = 256
TILE_ROUTER = 512
TILE_FINAL = 512
TILE_SLOTS = 2048
MOE_BLOCK = 512
EXPERT_GROUP = 2
X_AHEAD = 2
X_SLOTS = X_AHEAD + EXPERT_GROUP
Y_SLOTS = 2 * EXPERT_GROUP
LANES = 128
ROW_WORDS = D_MODEL // 2
ROW_TILES = ROW_WORDS // LANES
SC_WINDOW = 64


def _params(semantics):
    return pltpu.CompilerParams(dimension_semantics=semantics, vmem_limit_bytes=VMEM_LIMIT_BYTES)


def _const_spec(shape):
    nd = len(shape)
    return pl.BlockSpec(shape, lambda *_: (0,) * nd)


def _layer_norm(x, g, b):
    mu = jnp.mean(x, axis=-1, keepdims=True)
    xc = x - mu
    var = jnp.mean(xc * xc, axis=-1, keepdims=True)
    return xc * lax.rsqrt(var + LN_EPS) * g + b


def _sigmoid(x):
    return 1.0 / (1.0 + jnp.exp(-x))


def _log_sigmoid(x):
    return jnp.minimum(x, 0.0) - jnp.log(1.0 + jnp.exp(-jnp.abs(x)))


def _silu(x):
    return x * _sigmoid(x)


def _tile_hits(i, tile, offsets):
    hit = i < 0
    for off in offsets:
        if off % tile == 0:
            hit = jnp.logical_or(hit, i == off // tile)
    return hit


_MAIN_SPLITS = (POOL_W, GLA_K_W, GLA_K_W, GLA_V_W, GLA_V_W, D_MODEL, D_MODEL)


def _inproj_kernel(xp_ref, xs_ref, g_ref, b_ref, w_ref, wz_ref,
                   h_ref, u_ref, q_ref, k_ref, v_ref, og_ref, gp_ref, gg_ref, z_ref, *, n_prompt_tiles):
    i = pl.program_id(0)
    x = jnp.where(i < n_prompt_tiles, xp_ref[...], xs_ref[...])
    h = _layer_norm(x, g_ref[...], b_ref[...])
    h_ref[...] = h
    hb = h.astype(BF16)
    off = 0
    for ref, width in zip((u_ref, q_ref, k_ref, v_ref, og_ref, gp_ref, gg_ref), _MAIN_SPLITS):
        ref[...] = jnp.dot(hb, w_ref[:, off:off + width], preferred_element_type=F32).astype(BF16)
        off += width
    z_ref[...] = jnp.dot(hb, wz_ref[...], preferred_element_type=F32).astype(BF16)


def _inproj(xp, xs, ln_g, ln_b, w_main, w_z):
    n_p, n_s = xp.shape[0], xs.shape[0]
    n = n_p + n_s
    t = TILE_INPROJ
    npt, nst = n_p // t, n_s // t
    row = lambda i: (i, 0)
    widths = _MAIN_SPLITS + (Z_PAD,)
    out_shape = [jax.ShapeDtypeStruct((n, D_MODEL), F32)] + [jax.ShapeDtypeStruct((n, w), BF16) for w in widths]
    out_specs = [pl.BlockSpec((t, D_MODEL), row)] + [pl.BlockSpec((t, w), row) for w in widths]
    return pl.pallas_call(
        functools.partial(_inproj_kernel, n_prompt_tiles=npt),
        grid=(npt + nst,),
        in_specs=[
            pl.BlockSpec((t, D_MODEL), lambda i: (jnp.minimum(i, npt - 1), 0)),
            pl.BlockSpec((t, D_MODEL), lambda i: (jnp.maximum(i - npt, 0), 0)),
            _const_spec((1, D_MODEL)), _const_spec((1, D_MODEL)),
            _const_spec(w_main.shape), _const_spec(w_z.shape),
        ],
        out_specs=out_specs,
        out_shape=out_shape,
        compiler_params=_params(("parallel",)),
        name="inproj",
    )(xp, xs, ln_g, ln_b, w_main, w_z)


def _pool_kernel(u_ref, up_ref, un_ref, gp_ref, band_ref, pw_ref, sc_ref, wup_ref, pm_ref, *, starts, ends):
    t = u_ref.shape[0]
    i = pl.program_id(0)
    is_start = _tile_hits(i, t, starts)
    is_end = _tile_hits(i + 1, t, ends)
    cur = u_ref[...]
    prev = jnp.where(is_start, jnp.zeros_like(up_ref[...]), up_ref[...])
    nxt = jnp.where(is_end, jnp.zeros_like(un_ref[...]), un_ref[...])
    ext = jnp.concatenate([prev, cur, nxt], axis=0)
    r = lax.broadcasted_iota(jnp.int32, (t, POOL_GROUP_W), 0).astype(F32)
    fs = is_start.astype(F32)
    fe = is_end.astype(F32)
    parts = []
    for gi, w in enumerate(POOL_WINDOWS):
        sl = slice(gi * POOL_GROUP_W, (gi + 1) * POOL_GROUP_W)
        s = jnp.dot(band_ref[gi], ext[:, sl], preferred_element_type=F32)
        cnt = w - fs * jnp.maximum(w // 2 - r, 0.0) - fe * jnp.maximum(r + (w // 2 - t), 0.0)
        parts.append(s / cnt - cur[:, sl].astype(F32))
    p = jnp.concatenate(parts, axis=1).astype(BF16)
    p = (jnp.dot(p, pw_ref[...], preferred_element_type=F32) * sc_ref[...]).astype(BF16)
    pool_out = jnp.dot(p, wup_ref[...], preferred_element_type=F32)
    pm_ref[...] = (_sigmoid(gp_ref[...].astype(F32)) * pool_out).astype(BF16)


def _pool_bands(t):
    r = jnp.arange(t)[:, None]
    c = jnp.arange(t + 2 * POOL_HALO)[None, :] - POOL_HALO
    return jnp.stack([((c >= r - w // 2) & (c <= r + w // 2 - 1)).astype(BF16) for w in POOL_WINDOWS])


def _pool(u, gp, pool_w_bd, pool_scale, w_pool_up, starts, ends):
    n = u.shape[0]
    t = TILE_POOL
    hb = t // POOL_HALO
    last = n // POOL_HALO - 1
    return pl.pallas_call(
        functools.partial(_pool_kernel, starts=starts, ends=ends),
        grid=(n // t,),
        in_specs=[
            pl.BlockSpec((t, POOL_W), lambda i: (i, 0)),
            pl.BlockSpec((POOL_HALO, POOL_W), lambda i: (jnp.maximum(i * hb - 1, 0), 0)),
            pl.BlockSpec((POOL_HALO, POOL_W), lambda i: (jnp.minimum((i + 1) * hb, last), 0)),
            pl.BlockSpec((t, D_MODEL), lambda i: (i, 0)),
            _const_spec((POOL_GROUPS, t, t + 2 * POOL_HALO)),
            _const_spec((POOL_W, POOL_W)), _const_spec((1, POOL_W)), _const_spec((POOL_W, D_MODEL)),
        ],
        out_specs=pl.BlockSpec((t, D_MODEL), lambda i: (i, 0)),
        out_shape=jax.ShapeDtypeStruct((n, D_MODEL), BF16),
        compiler_params=_params(("parallel",)),
        name="pool",
    )(u, u, u, gp, _pool_bands(t), pool_w_bd, pool_scale, w_pool_up)


def _gla_tile(q, k, v, la, tri, st_ref, *, reverse):
    t = q.shape[0]
    c_sz = GLA_CHUNK
    hi = la.astype(BF16)
    r1 = la - hi.astype(F32)
    mid = r1.astype(BF16)
    lo = (r1 - mid.astype(F32)).astype(BF16)
    b = (jnp.dot(tri, hi, preferred_element_type=F32) + jnp.dot(tri, mid, preferred_element_type=F32)
         + jnp.dot(tri, lo, preferred_element_type=F32))
    kf = k.astype(F32)
    qd = (q.astype(F32) * jnp.exp(b) * (GLA_DK ** -0.5)).astype(BF16)
    kd = (kf * jnp.exp(-b)).astype(BF16)
    row = lax.broadcasted_iota(jnp.int32, (c_sz, c_sz), 0)
    col = lax.broadcasted_iota(jnp.int32, (c_sz, c_sz), 1)
    mask = (col >= row) if reverse else (col <= row)
    n_chunks = t // c_sz
    order = list(range(n_chunks - 1, -1, -1) if reverse else range(n_chunks))
    nt = (((1,), (1,)), ((), ()))
    tn = (((0,), (0,)), ((), ()))
    rows = [slice(c * c_sz, (c + 1) * c_sz) for c in range(n_chunks)]
    lanes = [slice(h * GLA_DK, (h + 1) * GLA_DK) for h in range(GLA_HEADS)]
    vcols = [slice(h * GLA_DV, (h + 1) * GLA_DV) for h in range(GLA_HEADS)]

    att, kv, dec = {}, {}, {}
    for c in order:
        b_c = b[rows[c]]
        b_edge = b_c[0:1] if reverse else b_c[c_sz - 1:c_sz]
        k2 = (kf[rows[c]] * jnp.exp(b_edge - b_c)).astype(BF16)
        dec[c] = jnp.exp(b_edge)
        for h in range(GLA_HEADS):
            att[c, h] = lax.dot_general(qd[rows[c], lanes[h]], kd[rows[c], lanes[h]], nt, preferred_element_type=F32)
            kv[c, h] = lax.dot_general(v[rows[c], vcols[h]], k2[:, lanes[h]], tn, preferred_element_type=F32)

    state_before = {}
    for h in range(GLA_HEADS):
        st = st_ref[h]
        for c in order:
            state_before[c, h] = st.astype(BF16)
            st = st * dec[c][:, lanes[h]] + kv[c, h]
        st_ref[h] = st

    outs = []
    for c in range(n_chunks):
        heads = []
        for h in range(GLA_HEADS):
            a = jnp.where(mask, att[c, h], 0.0).astype(BF16)
            o = jnp.dot(a, v[rows[c], vcols[h]], preferred_element_type=F32)
            heads.append(o + lax.dot_general(qd[rows[c], lanes[h]], state_before[c, h], nt,
                                             preferred_element_type=F32))
        outs.append(jnp.concatenate(heads, axis=1))
    return jnp.concatenate(outs, axis=0)


def _log_decay(z, w2_ref, gb_ref):
    return _log_sigmoid(jnp.dot(z, w2_ref[...], preferred_element_type=F32) + gb_ref[...]) * (1.0 / GATE_TAU)


def _gla_bwd_kernel(q_ref, k_ref, v_ref, z_ref, w2_ref, gb_ref, tri_ref, ob_ref, st_ref, *, n_tiles, ends):
    t = q_ref.shape[0]
    j = n_tiles - 1 - pl.program_id(0)

    @pl.when(_tile_hits(j + 1, t, ends))
    def _():
        st_ref[...] = jnp.zeros_like(st_ref)

    la = _log_decay(z_ref[...], w2_ref, gb_ref)
    ob_ref[...] = _gla_tile(q_ref[...], k_ref[...], v_ref[...], la, tri_ref[...], st_ref, reverse=True)


def _gla_fwd_kernel(q_ref, k_ref, v_ref, z_ref, w2_ref, gb_ref, tri_ref, ob_ref, og_ref, gg_ref, pm_ref, h_ref,
                    ng_ref, wgu_ref, wo_ref, l1g_ref, l1b_ref, h1_ref, h1rc_ref, st_ref, *, starts):
    t = q_ref.shape[0]
    i = pl.program_id(0)

    @pl.when(_tile_hits(i, t, starts))
    def _():
        st_ref[...] = jnp.zeros_like(st_ref)

    la = _log_decay(z_ref[...], w2_ref, gb_ref)
    o = _gla_tile(q_ref[...], k_ref[...], v_ref[...], la, tri_ref[...], st_ref, reverse=False) + ob_ref[...]
    heads = []
    for h in range(GLA_HEADS):
        oh = o[:, h * GLA_DV:(h + 1) * GLA_DV]
        heads.append(oh * lax.rsqrt(jnp.mean(oh * oh, axis=-1, keepdims=True) + RMS_EPS))
    o = jnp.concatenate(heads, axis=1) * ng_ref[...]
    o = (o * _silu(og_ref[...].astype(F32))).astype(BF16)
    gla_out = jnp.dot(o, wgu_ref[...], preferred_element_type=F32)
    merged = _sigmoid(gg_ref[...].astype(F32)) * gla_out + pm_ref[...].astype(F32)
    mix = jnp.dot(merged.astype(BF16), wo_ref[...], preferred_element_type=F32)
    h1 = _layer_norm(DN_ALPHA * h_ref[...] + mix, l1g_ref[...], l1b_ref[...])
    h1_ref[...] = h1
    _store_row_contiguous(h1rc_ref, h1)


def _store_row_contiguous(ref, x):
    t = x.shape[0]
    bits = pltpu.bitcast(x.astype(BF16).astype(F32), jnp.uint32)
    words = (bits[:, :ROW_WORDS] >> 16) | (bits[:, ROW_WORDS:] & jnp.uint32(0xFFFF0000))
    words = pltpu.bitcast(words, F32)
    for s in range(ROW_TILES):
        ref[pl.ds(s, t, stride=ROW_TILES), :] = words[:, s * LANES:(s + 1) * LANES]


def _load_row_contiguous(ref, t):
    words = jnp.concatenate([ref[pl.ds(s, t, stride=ROW_TILES), :] for s in range(ROW_TILES)], axis=1)
    bits = pltpu.bitcast(words, jnp.uint32)
    lo = pltpu.bitcast(bits << 16, F32)
    hi = pltpu.bitcast(bits & jnp.uint32(0xFFFF0000), F32)
    return jnp.concatenate([lo, hi], axis=1)


def _chunk_tri(t, reverse):
    r = jnp.arange(t)[:, None]
    c = jnp.arange(t)[None, :]
    same = (r // GLA_CHUNK) == (c // GLA_CHUNK)
    return (same & ((c >= r) if reverse else (c <= r))).astype(BF16)


def _gla_bwd(q, k, v, z, w2b_pad, gate_b, ends):
    n = q.shape[0]
    t = TILE_GLA
    nt = n // t
    rev = lambda i: (nt - 1 - i, 0)
    return pl.pallas_call(
        functools.partial(_gla_bwd_kernel, n_tiles=nt, ends=ends),
        grid=(nt,),
        in_specs=[
            pl.BlockSpec((t, GLA_K_W), rev), pl.BlockSpec((t, GLA_K_W), rev), pl.BlockSpec((t, GLA_V_W), rev),
            pl.BlockSpec((t, Z_PAD), rev),
            _const_spec((Z_PAD, GLA_K_W)), _const_spec((1, GLA_K_W)), _const_spec((t, t)),
        ],
        out_specs=pl.BlockSpec((t, GLA_V_W), rev),
        out_shape=jax.ShapeDtypeStruct((n, GLA_V_W), F32),
        scratch_shapes=[pltpu.VMEM((GLA_HEADS, GLA_DV, GLA_DK), F32)],
        compiler_params=_params(("arbitrary",)),
        name="gla_bwd",
    )(q, k, v, z, w2b_pad, gate_b, _chunk_tri(t, True))


def _gla_fwd_mix(q, k, v, z, w2f_pad, gate_b, ob, og, gg, pm, h, norm_g, w_gla_up, w_out, ln1_g, ln1_b, starts):
    n = q.shape[0]
    t = TILE_GLA
    row = lambda i: (i, 0)
    return pl.pallas_call(
        functools.partial(_gla_fwd_kernel, starts=starts),
        grid=(n // t,),
        in_specs=[
            pl.BlockSpec((t, GLA_K_W), row), pl.BlockSpec((t, GLA_K_W), row), pl.BlockSpec((t, GLA_V_W), row),
            pl.BlockSpec((t, Z_PAD), row),
            _const_spec((Z_PAD, GLA_K_W)), _const_spec((1, GLA_K_W)), _const_spec((t, t)),
            pl.BlockSpec((t, GLA_V_W), row), pl.BlockSpec((t, GLA_V_W), row), pl.BlockSpec((t, D_MODEL), row),
            pl.BlockSpec((t, D_MODEL), row), pl.BlockSpec((t, D_MODEL), row),
            _const_spec((1, GLA_V_W)), _const_spec((GLA_V_W, D_MODEL)), _const_spec((D_MODEL, D_MODEL)),
            _const_spec((1, D_MODEL)), _const_spec((1, D_MODEL)),
        ],
        out_specs=[pl.BlockSpec((t, D_MODEL), row), pl.BlockSpec((t * ROW_TILES, LANES), row)],
        out_shape=[jax.ShapeDtypeStruct((n, D_MODEL), F32), jax.ShapeDtypeStruct((n * ROW_TILES, LANES), F32)],
        scratch_shapes=[pltpu.VMEM((GLA_HEADS, GLA_DV, GLA_DK), F32)],
        compiler_params=_params(("arbitrary",)),
        name="gla_fwd_mix",
    )(q, k, v, z, w2f_pad, gate_b, _chunk_tri(t, False), ob, og, gg, pm, h, norm_g, w_gla_up, w_out, ln1_g, ln1_b)


def _router_kernel(h_ref, wrh_ref, wrl_ref, rb_ref, tri_ref, idx_ref, wts_ref, rank_ref, cnt_ref, base_ref):
    t = h_ref.shape[0]

    @pl.when(pl.program_id(0) == 0)
    def _():
        base_ref[...] = jnp.zeros_like(base_ref)

    h = h_ref[...]
    hh = h.astype(BF16)
    hl = (h - hh.astype(F32)).astype(BF16)
    nt = (((1,), (1,)), ((), ()))
    logits = (lax.dot_general(wrh_ref[...], hh, nt, preferred_element_type=F32)
              + lax.dot_general(wrl_ref[...], hh, nt, preferred_element_type=F32)
              + lax.dot_general(wrh_ref[...], hl, nt, preferred_element_type=F32))
    scores = _sigmoid(logits)
    sel = scores + rb_ref[:, 0:1]
    neg = jnp.float32(-jnp.inf)

    sub = lax.broadcasted_iota(jnp.int32, (EXPERTS_PER_GROUP, t), 0)
    gscore = []
    for g in range(N_GROUPS):
        sg = sel[g * EXPERTS_PER_GROUP:(g + 1) * EXPERTS_PER_GROUP]
        m1 = jnp.max(sg, axis=0, keepdims=True)
        first = jnp.min(jnp.where(sg == m1, sub, EXPERTS_PER_GROUP), axis=0, keepdims=True)
        m2 = jnp.max(jnp.where(sub == first, neg, sg), axis=0, keepdims=True)
        gscore.append(m1 + m2)
    masked = []
    for g in range(N_GROUPS):
        beaten = jnp.zeros((1, t), jnp.int32)
        for g2 in range(N_GROUPS):
            if g2 == g:
                continue
            wins = (gscore[g2] >= gscore[g]) if g2 < g else (gscore[g2] > gscore[g])
            beaten = beaten + wins.astype(jnp.int32)
        keep = beaten < TOPK_GROUPS
        masked.append(jnp.where(keep, sel[g * EXPERTS_PER_GROUP:(g + 1) * EXPERTS_PER_GROUP], neg))
    cand = jnp.concatenate(masked, axis=0)

    eid = lax.broadcasted_iota(jnp.int32, (N_EXPERTS, t), 0)
    picked = []
    assign = jnp.zeros((N_EXPERTS, t), F32)
    wsum = jnp.zeros((1, t), F32)
    for kk in range(TOP_K):
        m = jnp.max(cand, axis=0, keepdims=True)
        e_k = jnp.min(jnp.where(cand == m, eid, N_EXPERTS), axis=0, keepdims=True)
        hit = eid == e_k
        w_k = jnp.sum(jnp.where(hit, scores, 0.0), axis=0, keepdims=True)
        cand = jnp.where(hit, neg, cand)
        assign = jnp.where(hit, 1.0, assign)
        wsum = wsum + w_k
        picked.append((e_k, w_k))
        idx_ref[kk:kk + 1, :] = e_k

    a16 = assign.astype(BF16)
    before = jnp.dot(a16, tri_ref[...], preferred_element_type=F32)
    base = base_ref[...]
    before = before + jnp.concatenate([base] * (t // 128), axis=1)
    inv = ROUTED_SCALE / wsum
    for kk, (e_k, w_k) in enumerate(picked):
        wts_ref[kk:kk + 1, :] = w_k * inv
        rank_ref[kk:kk + 1, :] = jnp.sum(jnp.where(eid == e_k, before, 0.0), axis=0, keepdims=True).astype(jnp.int32)
    total = base + jnp.dot(a16, jnp.ones((t, 128), BF16), preferred_element_type=F32)
    base_ref[...] = total
    cnt_ref[...] = total


def _router(h1, wr_hi, wr_lo, rbias):
    n = h1.shape[0]
    t = TILE_ROUTER
    r = jnp.arange(t)
    tri = (r[:, None] < r[None, :]).astype(BF16)
    col = lambda i: (0, i)
    idx, wts, rank, cnt = pl.pallas_call(
        _router_kernel,
        grid=(n // t,),
        in_specs=[
            pl.BlockSpec((t, D_MODEL), lambda i: (i, 0)),
            _const_spec((N_EXPERTS, D_MODEL)), _const_spec((N_EXPERTS, D_MODEL)),
            _const_spec((N_EXPERTS, 128)), _const_spec((t, t)),
        ],
        out_specs=[pl.BlockSpec((TOP_K, t), col), pl.BlockSpec((TOP_K, t), col), pl.BlockSpec((TOP_K, t), col),
                   _const_spec((N_EXPERTS, 128))],
        out_shape=[jax.ShapeDtypeStruct((TOP_K, n), jnp.int32), jax.ShapeDtypeStruct((TOP_K, n), F32),
                   jax.ShapeDtypeStruct((TOP_K, n), jnp.int32), jax.ShapeDtypeStruct((N_EXPERTS, 128), F32)],
        scratch_shapes=[pltpu.VMEM((N_EXPERTS, 128), F32)],
        compiler_params=_params(("arbitrary",)),
        name="router",
    )(h1, wr_hi, wr_lo, rbias, tri)
    return idx, wts, rank, cnt[:, 0].astype(jnp.int32)


def _slot_kernel(ps_ref, idx_ref, rank_ref, dest_ref):
    idx = idx_ref[...]

    def add_expert(e, acc):
        return acc + jnp.where(idx == e, ps_ref[e], 0)

    dest_ref[...] = lax.fori_loop(0, N_EXPERTS, add_expert, rank_ref[...], unroll=8)


def _slots(pstart, idx, rank):
    n = idx.shape[1]
    t = TILE_SLOTS
    col = lambda i, ps: (0, i)
    return pl.pallas_call(
        _slot_kernel,
        grid_spec=pltpu.PrefetchScalarGridSpec(
            num_scalar_prefetch=1,
            grid=(n // t,),
            in_specs=[pl.BlockSpec((TOP_K, t), col), pl.BlockSpec((TOP_K, t), col)],
            out_specs=pl.BlockSpec((TOP_K, t), col),
        ),
        out_shape=jax.ShapeDtypeStruct((TOP_K, n), jnp.int32),
        compiler_params=_params(("parallel",)),
        name="slots",
    )(pstart, idx, rank)


def _expert_kernel(first_ref, count_ref, used_ref, wg_ref, wu_ref, wd_ref, x_hbm, y_hbm,
                   wgu_s, wd_s, xbuf, ybuf, xsem, ysem):
    e = pl.program_id(0)
    rows = MOE_BLOCK * ROW_TILES
    used = used_ref[0]

    def x_copy(b, slot):
        return pltpu.make_async_copy(x_hbm.at[pl.ds(b * rows, rows)], xbuf.at[slot], xsem.at[slot])

    def y_copy(b, slot):
        return pltpu.make_async_copy(ybuf.at[slot], y_hbm.at[pl.ds(b * rows, rows)], ysem.at[slot])

    @pl.when(e == 0)
    def _():
        for b in range(X_AHEAD):
            @pl.when(b < used)
            def _():
                x_copy(b, b).start()

    @pl.when(count_ref[e] > 0)
    def _():
        wgu_s[:, :EXPERT_HIDDEN] = wg_ref[0].astype(BF16)
        wgu_s[:, EXPERT_HIDDEN:] = wu_ref[0].astype(BF16)
        wd_s[...] = wd_ref[0].astype(BF16)

    def run_blocks(b0, g):
        for i in range(g):
            ahead = b0 + i + X_AHEAD

            @pl.when(ahead < used)
            def _():
                x_copy(ahead, ahead % X_SLOTS).start()

        for i in range(g):
            done = b0 + i - Y_SLOTS

            @pl.when(done >= 0)
            def _():
                y_copy(done, done % Y_SLOTS).wait()

        x = []
        for i in range(g):
            xs = (b0 + i) % X_SLOTS
            x_copy(b0 + i, xs).wait()
            x.append(_load_row_contiguous(xbuf.at[xs], MOE_BLOCK).astype(BF16))
        gu = [jnp.dot(xi, wgu_s[...], preferred_element_type=F32) for xi in x]
        hid = [(_silu(gi[:, :EXPERT_HIDDEN]) * gi[:, EXPERT_HIDDEN:]).astype(BF16) for gi in gu]
        y = [jnp.dot(hi, wd_s[...], preferred_element_type=F32) for hi in hid]
        for i in range(g):
            ys = (b0 + i) % Y_SLOTS
            _store_row_contiguous(ybuf.at[ys], y[i])
            y_copy(b0 + i, ys).start()

    first = first_ref[e]
    count = count_ref[e]
    n_groups = lax.div(count, EXPERT_GROUP)

    def group(j, carry):
        run_blocks(first + j * EXPERT_GROUP, EXPERT_GROUP)
        return carry

    lax.fori_loop(0, n_groups, group, 0)
    for left in range(1, EXPERT_GROUP):
        @pl.when(count - n_groups * EXPERT_GROUP == left)
        def _():
            run_blocks(first + n_groups * EXPERT_GROUP, left)

    @pl.when(e == pl.num_programs(0) - 1)
    def _():
        for back in range(1, Y_SLOTS + 1):
            @pl.when(used - back >= 0)
            def _():
                y_copy(used - back, (used - back) % Y_SLOTS).wait()


def _experts(first_block, block_count, n_used, xs, w_gate, w_up, w_down):
    rows = MOE_BLOCK * ROW_TILES
    n_exp = w_gate.shape[0]
    any_spec = pl.BlockSpec(memory_space=pl.ANY)
    return pl.pallas_call(
        _expert_kernel,
        grid_spec=pltpu.PrefetchScalarGridSpec(
            num_scalar_prefetch=3,
            grid=(n_exp,),
            in_specs=[
                pl.BlockSpec((1, D_MODEL, EXPERT_HIDDEN), lambda e, *_: (e, 0, 0)),
                pl.BlockSpec((1, D_MODEL, EXPERT_HIDDEN), lambda e, *_: (e, 0, 0)),
                pl.BlockSpec((1, EXPERT_HIDDEN, D_MODEL), lambda e, *_: (e, 0, 0)),
                any_spec,
            ],
            out_specs=any_spec,
            scratch_shapes=[
                pltpu.VMEM((D_MODEL, 2 * EXPERT_HIDDEN), BF16), pltpu.VMEM((EXPERT_HIDDEN, D_MODEL), BF16),
                pltpu.VMEM((X_SLOTS, rows, LANES), F32), pltpu.VMEM((Y_SLOTS, rows, LANES), F32),
                pltpu.SemaphoreType.DMA((X_SLOTS,)), pltpu.SemaphoreType.DMA((Y_SLOTS,)),
            ],
        ),
        out_shape=jax.ShapeDtypeStruct(xs.shape, F32),
        compiler_params=_params(("arbitrary",)),
        name="experts",
    )(first_block, block_count, n_used, w_gate, w_up, w_down, xs)


def _sc_mesh():
    return plsc.VectorSubcoreMesh(core_axis_name="c", subcore_axis_name="s")


def _sc_dispatch(x3, dest_w, n_rows):
    n = x3.shape[0]

    @functools.partial(pl.kernel, out_type=jax.ShapeDtypeStruct((n_rows, ROW_TILES, LANES), F32), mesh=_sc_mesh(),
                       name="sc_dispatch")
    def run(x_hbm, d_hbm, o_hbm):
        def body(x_vmem, d_vmem):
            for kk in range(TOP_K):
                pltpu.sync_copy(x_vmem, o_hbm.at[d_vmem.at[0].at[kk]])

        pltpu.emit_pipeline(
            body,
            grid=(n // SC_WINDOW,),
            in_specs=[pl.BlockSpec((SC_WINDOW, ROW_TILES, LANES), lambda i: (i, 0, 0)),
                      pl.BlockSpec((1, TOP_K, SC_WINDOW), lambda i: (i, 0, 0))],
            out_specs=[],
            core_axis_name=("c", "s"),
            dimension_semantics=(pltpu.PARALLEL,),
        )(x_hbm, d_hbm)

    return run(x3, dest_w)


def _sc_gather(y3, dest_w):
    nw = dest_w.shape[0]
    n = nw * SC_WINDOW

    @functools.partial(pl.kernel, out_type=jax.ShapeDtypeStruct((TOP_K, n, ROW_TILES, LANES), F32), mesh=_sc_mesh(),
                       name="sc_gather")
    def run(y_hbm, d_hbm, o_hbm):
        def body(d_vmem, g_vmem):
            pltpu.sync_copy(y_hbm.at[d_vmem.at[0].at[0]], g_vmem.at[0])

        pltpu.emit_pipeline(
            body,
            grid=(nw, TOP_K),
            in_specs=[pl.BlockSpec((1, 1, SC_WINDOW), lambda i, k: (i, k, 0))],
            out_specs=[pl.BlockSpec((1, SC_WINDOW, ROW_TILES, LANES), lambda i, k: (k, i, 0, 0))],
            core_axis_name=("c", "s"),
            dimension_semantics=(pltpu.PARALLEL, pltpu.PARALLEL),
        )(d_hbm, o_hbm)

    return run(y3, dest_w)


def _final_kernel(h_ref, g_ref, w_ref, wgu_ref, wd_ref, lg_ref, lb_ref, yp_ref, ys_ref, *, n_prompt_tiles):
    t = h_ref.shape[0]
    h = h_ref[...]
    gu = jnp.dot(h.astype(BF16), wgu_ref[...], preferred_element_type=F32)
    hid = (_silu(gu[:, :SHARED_HIDDEN]) * gu[:, SHARED_HIDDEN:]).astype(BF16)
    ffn = jnp.dot(hid, wd_ref[...], preferred_element_type=F32)
    w = w_ref[...]
    for kk in range(TOP_K):
        ffn = ffn + _load_row_contiguous(g_ref.at[kk], t) * w[:, kk:kk + 1]
    y = _layer_norm(DN_ALPHA * h + ffn, lg_ref[...], lb_ref[...])
    is_prompt = pl.program_id(0) < n_prompt_tiles

    @pl.when(is_prompt)
    def _():
        yp_ref[...] = y

    @pl.when(jnp.logical_not(is_prompt))
    def _():
        ys_ref[...] = y


def _final(h1, gathered, wts_t, w_sh_gu, w_sh_down, ln_g, ln_b, n_p):
    n = h1.shape[0]
    t = TILE_FINAL
    npt = n_p // t
    row = lambda i: (i, 0)
    return pl.pallas_call(
        functools.partial(_final_kernel, n_prompt_tiles=npt),
        grid=(n // t,),
        in_specs=[
            pl.BlockSpec((t, D_MODEL), row),
            pl.BlockSpec((TOP_K, t * ROW_TILES, LANES), lambda i: (0, i, 0)),
            pl.BlockSpec((t, TOP_K), row),
            _const_spec((D_MODEL, 2 * SHARED_HIDDEN)), _const_spec((SHARED_HIDDEN, D_MODEL)),
            _const_spec((1, D_MODEL)), _const_spec((1, D_MODEL)),
        ],
        out_specs=[pl.BlockSpec((t, D_MODEL), lambda i: (jnp.minimum(i, npt - 1), 0)),
                   pl.BlockSpec((t, D_MODEL), lambda i: (jnp.maximum(i - npt, 0), 0))],
        out_shape=[jax.ShapeDtypeStruct((n_p, D_MODEL), F32), jax.ShapeDtypeStruct((n - n_p, D_MODEL), F32)],
        compiler_params=_params(("arbitrary",)),
        name="final",
    )(h1, gathered, wts_t, w_sh_gu, w_sh_down, ln_g, ln_b)


def _block_diag(w):
    g, a, b = w.shape
    out = jnp.zeros((g * a, g * b), w.dtype)
    for i in range(g):
        out = out.at[i * a:(i + 1) * a, i * b:(i + 1) * b].set(w[i])
    return out


def kernel(x_prompt, x_sample, ln0_g, ln0_b, w_in, pool_w, pool_scale, w_pool_up, gate_w2_fwd, gate_b_fwd,
           gate_w2_bwd, gate_b_bwd, gla_norm_g, w_gla_up, w_out, ln1_g, ln1_b, w_router, router_bias,
           w_exp_gate, w_exp_up, w_exp_down, w_sh_gate, w_sh_up, w_sh_down, ln2_g, ln2_b):
    bp, lp, d = x_prompt.shape
    bs, ls, _ = x_sample.shape
    n_p, n_s = bp * lp, bs * ls
    n = n_p + n_s
    starts = tuple(b * lp for b in range(bp)) + tuple(n_p + b * ls for b in range(bs))
    ends = tuple((b + 1) * lp for b in range(bp)) + tuple(n_p + (b + 1) * ls for b in range(bs))
    vec = lambda a: a.reshape(1, -1).astype(F32)

    w = w_in[0]
    c_u, c_q, c_k, c_v, c_og = 0, 512, 1024, 1536, 2560
    c_zf, c_zb, c_gp, c_gg = 3584, 3600, 3616, 4640
    w_main = jnp.concatenate([w[:, c_u:c_zf], w[:, c_gp:]], axis=1).astype(BF16)
    w_z = jnp.zeros((d, Z_PAD), F32).at[:, :2 * GATE_RANK].set(w[:, c_zf:c_gp]).astype(BF16)
    w2f = jnp.zeros((Z_PAD, GLA_K_W), F32).at[:GATE_RANK].set(gate_w2_fwd[0]).astype(BF16)
    w2b = jnp.zeros((Z_PAD, GLA_K_W), F32).at[GATE_RANK:2 * GATE_RANK].set(gate_w2_bwd[0]).astype(BF16)

    h, u, q, k, v, og, gp, gg, z = _inproj(x_prompt.reshape(n_p, d), x_sample.reshape(n_s, d),
                                           vec(ln0_g), vec(ln0_b), w_main, w_z)
    pm = _pool(u, gp, _block_diag(pool_w[0]).astype(BF16), vec(pool_scale[0]), w_pool_up[0].astype(BF16),
               starts, ends)
    ob = _gla_bwd(q, k, v, z, w2b, vec(gate_b_bwd[0]), ends)
    h1, h1_rc = _gla_fwd_mix(q, k, v, z, w2f, vec(gate_b_fwd[0]), ob, og, gg, pm, h, vec(gla_norm_g[0]),
                             w_gla_up[0].astype(BF16), w_out[0].astype(BF16), vec(ln1_g[0]), vec(ln1_b[0]), starts)

    wr_t = w_router[0].T
    wr_hi = wr_t.astype(BF16)
    wr_lo = (wr_t - wr_hi.astype(F32)).astype(BF16)
    rbias = jnp.broadcast_to(router_bias[0].astype(F32)[:, None], (N_EXPERTS, 128))
    idx, wts, rank, counts = _router(h1, wr_hi, wr_lo, rbias)

    padded = (counts + MOE_BLOCK - 1) // MOE_BLOCK * MOE_BLOCK
    pend = jnp.cumsum(padded)
    pstart = pend - padded
    n_rows = n * TOP_K + N_EXPERTS * MOE_BLOCK
    first_block = (pstart // MOE_BLOCK).astype(jnp.int32)
    block_count = (padded // MOE_BLOCK).astype(jnp.int32)
    n_used = (pend[-1:] // MOE_BLOCK).astype(jnp.int32)
    dest = _slots(pstart.astype(jnp.int32), idx, rank)

    dest_w = dest.reshape(TOP_K, n // SC_WINDOW, SC_WINDOW).transpose(1, 0, 2)
    xs = _sc_dispatch(h1_rc.reshape(n, ROW_TILES, LANES), dest_w, n_rows)
    ys = _experts(first_block, block_count, n_used, xs.reshape(n_rows * ROW_TILES, LANES),
                  w_exp_gate[0], w_exp_up[0], w_exp_down[0])
    gathered = _sc_gather(ys.reshape(n_rows, ROW_TILES, LANES), dest_w)

    y_p, y_s = _final(h1, gathered.reshape(TOP_K, n * ROW_TILES, LANES), wts.T,
                      jnp.concatenate([w_sh_gate[0], w_sh_up[0]], axis=1).astype(BF16),
                      w_sh_down[0].astype(BF16), vec(ln2_g[0]), vec(ln2_b[0]), n_p)
    return y_p.reshape(bp, lp, d), y_s.reshape(bs, ls, d)
```

```python
import functools

import jax
import jax.numpy as jnp
from jax import lax
from jax.experimental import pallas as pl
from jax.experimental.pallas import tpu as pltpu
from jax.experimental.pallas import tpu_sc as plsc

F32 = jnp.float32
BF16 = jnp.bfloat16

D_MODEL = 1024
POOL_GROUPS = 4
POOL_GROUP_W = 128
POOL_W = POOL_GROUPS * POOL_GROUP_W
POOL_WINDOWS = (2, 4, 8, 16)
POOL_HALO = 16
GLA_HEADS = 4
GLA_K_W = 512
GLA_V_W = 1024
GLA_DK = GLA_K_W // GLA_HEADS
GLA_DV = GLA_V_W // GLA_HEADS
GATE_RANK = 16
GATE_TAU = 16.0
GLA_CHUNK = 64
Z_PAD = 128
N_EXPERTS = 256
TOP_K = 8
N_GROUPS = 8
TOPK_GROUPS = 4
EXPERTS_PER_GROUP = N_EXPERTS // N_GROUPS
EXPERT_HIDDEN = 256
SHARED_HIDDEN = 256
ROUTED_SCALE = 2.5
DEPTH = 1
DN_ALPHA = (2 * DEPTH) ** 0.25
LN_EPS = 1e-5
RMS_EPS = 1e-6

VMEM_LIMIT_BYTES = 56 * 1024 * 1024

TILE_INPROJ = 512
TILE_POOL = 512
TILE_GLA = 256
TILE_ROUTER = 512
TILE_FINAL = 512
TILE_SLOTS = 2048
MOE_BLOCK = 512
EXPERT_GROUP = 2
X_AHEAD = 2
X_SLOTS = X_AHEAD + EXPERT_GROUP
Y_SLOTS = 2 * EXPERT_GROUP
LANES = 128
ROW_WORDS = D_MODEL // 2
ROW_TILES = ROW_WORDS // LANES
SC_WINDOW = 64


def _params(semantics):
    return pltpu.CompilerParams(dimension_semantics=semantics, vmem_limit_bytes=VMEM_LIMIT_BYTES)


def _const_spec(shape):
    nd = len(shape)
    return pl.BlockSpec(shape, lambda *_: (0,) * nd)


def _layer_norm(x, g, b):
    mu = jnp.mean(x, axis=-1, keepdims=True)
    xc = x - mu
    var = jnp.mean(xc * xc, axis=-1, keepdims=True)
    return xc * lax.rsqrt(var + LN_EPS) * g + b


def _sigmoid(x):
    return 1.0 / (1.0 + jnp.exp(-x))


def _log_sigmoid(x):
    return jnp.minimum(x, 0.0) - jnp.log(1.0 + jnp.exp(-jnp.abs(x)))


def _silu(x):
    return x * _sigmoid(x)


def _tile_hits(i, tile, offsets):
    hit = i < 0
    for off in offsets:
        if off % tile == 0:
            hit = jnp.logical_or(hit, i == off // tile)
    return hit


_MAIN_SPLITS = (POOL_W, GLA_K_W, GLA_K_W, GLA_V_W, GLA_V_W, D_MODEL, D_MODEL)


def _inproj_kernel(xp_ref, xs_ref, g_ref, b_ref, w_ref, wz_ref,
                   h_ref, u_ref, q_ref, k_ref, v_ref, og_ref, gp_ref, gg_ref, z_ref, *, n_prompt_tiles):
    i = pl.program_id(0)
    x = jnp.where(i < n_prompt_tiles, xp_ref[...], xs_ref[...])
    h = _layer_norm(x, g_ref[...], b_ref[...])
    h_ref[...] = h
    hb = h.astype(BF16)
    off = 0
    for ref, width in zip((u_ref, q_ref, k_ref, v_ref, og_ref, gp_ref, gg_ref), _MAIN_SPLITS):
        ref[...] = jnp.dot(hb, w_ref[:, off:off + width], preferred_element_type=F32).astype(BF16)
        off += width
    z_ref[...] = jnp.dot(hb, wz_ref[...], preferred_element_type=F32).astype(BF16)


def _inproj(xp, xs, ln_g, ln_b, w_main, w_z):
    n_p, n_s = xp.shape[0], xs.shape[0]
    n = n_p + n_s
    t = TILE_INPROJ
    npt, nst = n_p // t, n_s // t
    row = lambda i: (i, 0)
    widths = _MAIN_SPLITS + (Z_PAD,)
    out_shape = [jax.ShapeDtypeStruct((n, D_MODEL), F32)] + [jax.ShapeDtypeStruct((n, w), BF16) for w in widths]
    out_specs = [pl.BlockSpec((t, D_MODEL), row)] + [pl.BlockSpec((t, w), row) for w in widths]
    return pl.pallas_call(
        functools.partial(_inproj_kernel, n_prompt_tiles=npt),
        grid=(npt + nst,),
        in_specs=[
            pl.BlockSpec((t, D_MODEL), lambda i: (jnp.minimum(i, npt - 1), 0)),
            pl.BlockSpec((t, D_MODEL), lambda i: (jnp.maximum(i - npt, 0), 0)),
            _const_spec((1, D_MODEL)), _const_spec((1, D_MODEL)),
            _const_spec(w_main.shape), _const_spec(w_z.shape),
        ],
        out_specs=out_specs,
        out_shape=out_shape,
        compiler_params=_params(("parallel",)),
        name="inproj",
    )(xp, xs, ln_g, ln_b, w_main, w_z)


def _pool_kernel(u_ref, up_ref, un_ref, gp_ref, band_ref, pw_ref, sc_ref, wup_ref, pm_ref, *, starts, ends):
    t = u_ref.shape[0]
    i = pl.program_id(0)
    is_start = _tile_hits(i, t, starts)
    is_end = _tile_hits(i + 1, t, ends)
    cur = u_ref[...]
    prev = jnp.where(is_start, jnp.zeros_like(up_ref[...]), up_ref[...])
    nxt = jnp.where(is_end, jnp.zeros_like(un_ref[...]), un_ref[...])
    ext = jnp.concatenate([prev, cur, nxt], axis=0)
    r = lax.broadcasted_iota(jnp.int32, (t, POOL_GROUP_W), 0).astype(F32)
    fs = is_start.astype(F32)
    fe = is_end.astype(F32)
    parts = []
    for gi, w in enumerate(POOL_WINDOWS):
        sl = slice(gi * POOL_GROUP_W, (gi + 1) * POOL_GROUP_W)
        s = jnp.dot(band_ref[gi], ext[:, sl], preferred_element_type=F32)
        cnt = w - fs * jnp.maximum(w // 2 - r, 0.0) - fe * jnp.maximum(r + (w // 2 - t), 0.0)
        parts.append(s / cnt - cur[:, sl].astype(F32))
    p = jnp.concatenate(parts, axis=1).astype(BF16)
    p = (jnp.dot(p, pw_ref[...], preferred_element_type=F32) * sc_ref[...]).astype(BF16)
    pool_out = jnp.dot(p, wup_ref[...], preferred_element_type=F32)
    pm_ref[...] = (_sigmoid(gp_ref[...].astype(F32)) * pool_out).astype(BF16)


def _pool_bands(t):
    r = jnp.arange(t)[:, None]
    c = jnp.arange(t + 2 * POOL_HALO)[None, :] - POOL_HALO
    return jnp.stack([((c >= r - w // 2) & (c <= r + w // 2 - 1)).astype(BF16) for w in POOL_WINDOWS])


def _pool(u, gp, pool_w_bd, pool_scale, w_pool_up, starts, ends):
    n = u.shape[0]
    t = TILE_POOL
    hb = t // POOL_HALO
    last = n // POOL_HALO - 1
    return pl.pallas_call(
        functools.partial(_pool_kernel, starts=starts, ends=ends),
        grid=(n // t,),
        in_specs=[
            pl.BlockSpec((t, POOL_W), lambda i: (i, 0)),
            pl.BlockSpec((POOL_HALO, POOL_W), lambda i: (jnp.maximum(i * hb - 1, 0), 0)),
            pl.BlockSpec((POOL_HALO, POOL_W), lambda i: (jnp.minimum((i + 1) * hb, last), 0)),
            pl.BlockSpec((t, D_MODEL), lambda i: (i, 0)),
            _const_spec((POOL_GROUPS, t, t + 2 * POOL_HALO)),
            _const_spec((POOL_W, POOL_W)), _const_spec((1, POOL_W)), _const_spec((POOL_W, D_MODEL)),
        ],
        out_specs=pl.BlockSpec((t, D_MODEL), lambda i: (i, 0)),
        out_shape=jax.ShapeDtypeStruct((n, D_MODEL), BF16),
        compiler_params=_params(("parallel",)),
        name="pool",
    )(u, u, u, gp, _pool_bands(t), pool_w_bd, pool_scale, w_pool_up)


def _gla_tile(q, k, v, la, tri, st_ref, *, reverse):
    t = q.shape[0]
    c_sz = GLA_CHUNK
    hi = la.astype(BF16)
    r1 = la - hi.astype(F32)
    mid = r1.astype(BF16)
    lo = (r1 - mid.astype(F32)).astype(BF16)
    b = (jnp.dot(tri, hi, preferred_element_type=F32) + jnp.dot(tri, mid, preferred_element_type=F32)
         + jnp.dot(tri, lo, preferred_element_type=F32))
    kf = k.astype(F32)
    qd = (q.astype(F32) * jnp.exp(b) * (GLA_DK ** -0.5)).astype(BF16)
    kd = (kf * jnp.exp(-b)).astype(BF16)
    row = lax.broadcasted_iota(jnp.int32, (c_sz, c_sz), 0)
    col = lax.broadcasted_iota(jnp.int32, (c_sz, c_sz), 1)
    mask = (col >= row) if reverse else (col <= row)
    n_chunks = t // c_sz
    order = list(range(n_chunks - 1, -1, -1) if reverse else range(n_chunks))
    nt = (((1,), (1,)), ((), ()))
    tn = (((0,), (0,)), ((), ()))
    rows = [slice(c * c_sz, (c + 1) * c_sz) for c in range(n_chunks)]
    lanes = [slice(h * GLA_DK, (h + 1) * GLA_DK) for h in range(GLA_HEADS)]
    vcols = [slice(h * GLA_DV, (h + 1) * GLA_DV) for h in range(GLA_HEADS)]

    att, kv, dec = {}, {}, {}
    for c in order:
        b_c = b[rows[c]]
        b_edge = b_c[0:1] if reverse else b_c[c_sz - 1:c_sz]
        k2 = (kf[rows[c]] * jnp.exp(b_edge - b_c)).astype(BF16)
        dec[c] = jnp.exp(b_edge)
        for h in range(GLA_HEADS):
            att[c, h] = lax.dot_general(qd[rows[c], lanes[h]], kd[rows[c], lanes[h]], nt, preferred_element_type=F32)
            kv[c, h] = lax.dot_general(v[rows[c], vcols[h]], k2[:, lanes[h]], tn, preferred_element_type=F32)

    state_before = {}
    for h in range(GLA_HEADS):
        st = st_ref[h]
        for c in order:
            state_before[c, h] = st.astype(BF16)
            st = st * dec[c][:, lanes[h]] + kv[c, h]
        st_ref[h] = st

    outs = []
    for c in range(n_chunks):
        heads = []
        for h in range(GLA_HEADS):
            a = jnp.where(mask, att[c, h], 0.0).astype(BF16)
            o = jnp.dot(a, v[rows[c], vcols[h]], preferred_element_type=F32)
            heads.append(o + lax.dot_general(qd[rows[c], lanes[h]], state_before[c, h], nt,
                                             preferred_element_type=F32))
        outs.append(jnp.concatenate(heads, axis=1))
    return jnp.concatenate(outs, axis=0)


def _log_decay(z, w2_ref, gb_ref):
    return _log_sigmoid(jnp.dot(z, w2_ref[...], preferred_element_type=F32) + gb_ref[...]) * (1.0 / GATE_TAU)


def _gla_bwd_kernel(q_ref, k_ref, v_ref, z_ref, w2_ref, gb_ref, tri_ref, ob_ref, st_ref, *, n_tiles, ends):
    t = q_ref.shape[0]
    j = n_tiles - 1 - pl.program_id(0)

    @pl.when(_tile_hits(j + 1, t, ends))
    def _():
        st_ref[...] = jnp.zeros_like(st_ref)

    la = _log_decay(z_ref[...], w2_ref, gb_ref)
    ob_ref[...] = _gla_tile(q_ref[...], k_ref[...], v_ref[...], la, tri_ref[...], st_ref, reverse=True)


def _gla_fwd_kernel(q_ref, k_ref, v_ref, z_ref, w2_ref, gb_ref, tri_ref, ob_ref, og_ref, gg_ref, pm_ref, h_ref,
                    ng_ref, wgu_ref, wo_ref, l1g_ref, l1b_ref, h1_ref, h1rc_ref, st_ref, *, starts):
    t = q_ref.shape[0]
    i = pl.program_id(0)

    @pl.when(_tile_hits(i, t, starts))
    def _():
        st_ref[...] = jnp.zeros_like(st_ref)

    la = _log_decay(z_ref[...], w2_ref, gb_ref)
    o = _gla_tile(q_ref[...], k_ref[...], v_ref[...], la, tri_ref[...], st_ref, reverse=False) + ob_ref[...]
    heads = []
    for h in range(GLA_HEADS):
        oh = o[:, h * GLA_DV:(h + 1) * GLA_DV]
        heads.append(oh * lax.rsqrt(jnp.mean(oh * oh, axis=-1, keepdims=True) + RMS_EPS))
    o = jnp.concatenate(heads, axis=1) * ng_ref[...]
    o = (o * _silu(og_ref[...].astype(F32))).astype(BF16)
    gla_out = jnp.dot(o, wgu_ref[...], preferred_element_type=F32)
    merged = _sigmoid(gg_ref[...].astype(F32)) * gla_out + pm_ref[...].astype(F32)
    mix = jnp.dot(merged.astype(BF16), wo_ref[...], preferred_element_type=F32)
    h1 = _layer_norm(DN_ALPHA * h_ref[...] + mix, l1g_ref[...], l1b_ref[...])
    h1_ref[...] = h1
    _store_row_contiguous(h1rc_ref, h1)


def _store_row_contiguous(ref, x):
    t = x.shape[0]
    bits = pltpu.bitcast(x.astype(BF16).astype(F32), jnp.uint32)
    words = (bits[:, :ROW_WORDS] >> 16) | (bits[:, ROW_WORDS:] & jnp.uint32(0xFFFF0000))
    words = pltpu.bitcast(words, F32)
    for s in range(ROW_TILES):
        ref[pl.ds(s, t, stride=ROW_TILES), :] = words[:, s * LANES:(s + 1) * LANES]


def _load_row_contiguous(ref, t):
    words = jnp.concatenate([ref[pl.ds(s, t, stride=ROW_TILES), :] for s in range(ROW_TILES)], axis=1)
    bits = pltpu.bitcast(words, jnp.uint32)
    lo = pltpu.bitcast(bits << 16, F32)
    hi = pltpu.bitcast(bits & jnp.uint32(0xFFFF0000), F32)
    return jnp.concatenate([lo, hi], axis=1)


def _chunk_tri(t, reverse):
    r = jnp.arange(t)[:, None]
    c = jnp.arange(t)[None, :]
    same = (r // GLA_CHUNK) == (c // GLA_CHUNK)
    return (same & ((c >= r) if reverse else (c <= r))).astype(BF16)


def _gla_bwd(q, k, v, z, w2b_pad, gate_b, ends):
    n = q.shape[0]
    t = TILE_GLA
    nt = n // t
    rev = lambda i: (nt - 1 - i, 0)
    return pl.pallas_call(
        functools.partial(_gla_bwd_kernel, n_tiles=nt, ends=ends),
        grid=(nt,),
        in_specs=[
            pl.BlockSpec((t, GLA_K_W), rev), pl.BlockSpec((t, GLA_K_W), rev), pl.BlockSpec((t, GLA_V_W), rev),
            pl.BlockSpec((t, Z_PAD), rev),
            _const_spec((Z_PAD, GLA_K_W)), _const_spec((1, GLA_K_W)), _const_spec((t, t)),
        ],
        out_specs=pl.BlockSpec((t, GLA_V_W), rev),
        out_shape=jax.ShapeDtypeStruct((n, GLA_V_W), F32),
        scratch_shapes=[pltpu.VMEM((GLA_HEADS, GLA_DV, GLA_DK), F32)],
        compiler_params=_params(("arbitrary",)),
        name="gla_bwd",
    )(q, k, v, z, w2b_pad, gate_b, _chunk_tri(t, True))


def _gla_fwd_mix(q, k, v, z, w2f_pad, gate_b, ob, og, gg, pm, h, norm_g, w_gla_up, w_out, ln1_g, ln1_b, starts):
    n = q.shape[0]
    t = TILE_GLA
    row = lambda i: (i, 0)
    return pl.pallas_call(
        functools.partial(_gla_fwd_kernel, starts=starts),
        grid=(n // t,),
        in_specs=[
            pl.BlockSpec((t, GLA_K_W), row), pl.BlockSpec((t, GLA_K_W), row), pl.BlockSpec((t, GLA_V_W), row),
            pl.BlockSpec((t, Z_PAD), row),
            _const_spec((Z_PAD, GLA_K_W)), _const_spec((1, GLA_K_W)), _const_spec((t, t)),
            pl.BlockSpec((t, GLA_V_W), row), pl.BlockSpec((t, GLA_V_W), row), pl.BlockSpec((t, D_MODEL), row),
            pl.BlockSpec((t, D_MODEL), row), pl.BlockSpec((t, D_MODEL), row),
            _const_spec((1, GLA_V_W)), _const_spec((GLA_V_W, D_MODEL)), _const_spec((D_MODEL, D_MODEL)),
            _const_spec((1, D_MODEL)), _const_spec((1, D_MODEL)),
        ],
        out_specs=[pl.BlockSpec((t, D_MODEL), row), pl.BlockSpec((t * ROW_TILES, LANES), row)],
        out_shape=[jax.ShapeDtypeStruct((n, D_MODEL), F32), jax.ShapeDtypeStruct((n * ROW_TILES, LANES), F32)],
        scratch_shapes=[pltpu.VMEM((GLA_HEADS, GLA_DV, GLA_DK), F32)],
        compiler_params=_params(("arbitrary",)),
        name="gla_fwd_mix",
    )(q, k, v, z, w2f_pad, gate_b, _chunk_tri(t, False), ob, og, gg, pm, h, norm_g, w_gla_up, w_out, ln1_g, ln1_b)


def _router_kernel(h_ref, wrh_ref, wrl_ref, rb_ref, tri_ref, idx_ref, wts_ref, rank_ref, cnt_ref, base_ref):
    t = h_ref.shape[0]

    @pl.when(pl.program_id(0) == 0)
    def _():
        base_ref[...] = jnp.zeros_like(base_ref)

    h = h_ref[...]
    hh = h.astype(BF16)
    hl = (h - hh.astype(F32)).astype(BF16)
    nt = (((1,), (1,)), ((), ()))
    logits = (lax.dot_general(wrh_ref[...], hh, nt, preferred_element_type=F32)
              + lax.dot_general(wrl_ref[...], hh, nt, preferred_element_type=F32)
              + lax.dot_general(wrh_ref[...], hl, nt, preferred_element_type=F32))
    scores = _sigmoid(logits)
    sel = scores + rb_ref[:, 0:1]
    neg = jnp.float32(-jnp.inf)

    sub = lax.broadcasted_iota(jnp.int32, (EXPERTS_PER_GROUP, t), 0)
    gscore = []
    for g in range(N_GROUPS):
        sg = sel[g * EXPERTS_PER_GROUP:(g + 1) * EXPERTS_PER_GROUP]
        m1 = jnp.max(sg, axis=0, keepdims=True)
        first = jnp.min(jnp.where(sg == m1, sub, EXPERTS_PER_GROUP), axis=0, keepdims=True)
        m2 = jnp.max(jnp.where(sub == first, neg, sg), axis=0, keepdims=True)
        gscore.append(m1 + m2)
    masked = []
    for g in range(N_GROUPS):
        beaten = jnp.zeros((1, t), jnp.int32)
        for g2 in range(N_GROUPS):
            if g2 == g:
                continue
            wins = (gscore[g2] >= gscore[g]) if g2 < g else (gscore[g2] > gscore[g])
            beaten = beaten + wins.astype(jnp.int32)
        keep = beaten < TOPK_GROUPS
        masked.append(jnp.where(keep, sel[g * EXPERTS_PER_GROUP:(g + 1) * EXPERTS_PER_GROUP], neg))
    cand = jnp.concatenate(masked, axis=0)

    eid = lax.broadcasted_iota(jnp.int32, (N_EXPERTS, t), 0)
    picked = []
    assign = jnp.zeros((N_EXPERTS, t), F32)
    wsum = jnp.zeros((1, t), F32)
    for kk in range(TOP_K):
        m = jnp.max(cand, axis=0, keepdims=True)
        e_k = jnp.min(jnp.where(cand == m, eid, N_EXPERTS), axis=0, keepdims=True)
        hit = eid == e_k
        w_k = jnp.sum(jnp.where(hit, scores, 0.0), axis=0, keepdims=True)
        cand = jnp.where(hit, neg, cand)
        assign = jnp.where(hit, 1.0, assign)
        wsum = wsum + w_k
        picked.append((e_k, w_k))
        idx_ref[kk:kk + 1, :] = e_k

    a16 = assign.astype(BF16)
    before = jnp.dot(a16, tri_ref[...], preferred_element_type=F32)
    base = base_ref[...]
    before = before + jnp.concatenate([base] * (t // 128), axis=1)
    inv = ROUTED_SCALE / wsum
    for kk, (e_k, w_k) in enumerate(picked):
        wts_ref[kk:kk + 1, :] = w_k * inv
        rank_ref[kk:kk + 1, :] = jnp.sum(jnp.where(eid == e_k, before, 0.0), axis=0, keepdims=True).astype(jnp.int32)
    total = base + jnp.dot(a16, jnp.ones((t, 128), BF16), preferred_element_type=F32)
    base_ref[...] = total
    cnt_ref[...] = total


def _router(h1, wr_hi, wr_lo, rbias):
    n = h1.shape[0]
    t = TILE_ROUTER
    r = jnp.arange(t)
    tri = (r[:, None] < r[None, :]).astype(BF16)
    col = lambda i: (0, i)
    idx, wts, rank, cnt = pl.pallas_call(
        _router_kernel,
        grid=(n // t,),
        in_specs=[
            pl.BlockSpec((t, D_MODEL), lambda i: (i, 0)),
            _const_spec((N_EXPERTS, D_MODEL)), _const_spec((N_EXPERTS, D_MODEL)),
            _const_spec((N_EXPERTS, 128)), _const_spec((t, t)),
        ],
        out_specs=[pl.BlockSpec((TOP_K, t), col), pl.BlockSpec((TOP_K, t), col), pl.BlockSpec((TOP_K, t), col),
                   _const_spec((N_EXPERTS, 128))],
        out_shape=[jax.ShapeDtypeStruct((TOP_K, n), jnp.int32), jax.ShapeDtypeStruct((TOP_K, n), F32),
                   jax.ShapeDtypeStruct((TOP_K, n), jnp.int32), jax.ShapeDtypeStruct((N_EXPERTS, 128), F32)],
        scratch_shapes=[pltpu.VMEM((N_EXPERTS, 128), F32)],
        compiler_params=_params(("arbitrary",)),
        name="router",
    )(h1, wr_hi, wr_lo, rbias, tri)
    return idx, wts, rank, cnt[:, 0].astype(jnp.int32)


def _slot_kernel(ps_ref, idx_ref, rank_ref, dest_ref):
    idx = idx_ref[...]

    def add_expert(e, acc):
        return acc + jnp.where(idx == e, ps_ref[e], 0)

    dest_ref[...] = lax.fori_loop(0, N_EXPERTS, add_expert, rank_ref[...], unroll=8)


def _slots(pstart, idx, rank):
    n = idx.shape[1]
    t = TILE_SLOTS
    col = lambda i, ps: (0, i)
    return pl.pallas_call(
        _slot_kernel,
        grid_spec=pltpu.PrefetchScalarGridSpec(
            num_scalar_prefetch=1,
            grid=(n // t,),
            in_specs=[pl.BlockSpec((TOP_K, t), col), pl.BlockSpec((TOP_K, t), col)],
            out_specs=pl.BlockSpec((TOP_K, t), col),
        ),
        out_shape=jax.ShapeDtypeStruct((TOP_K, n), jnp.int32),
        compiler_params=_params(("parallel",)),
        name="slots",
    )(pstart, idx, rank)


def _expert_kernel(first_ref, count_ref, used_ref, wg_ref, wu_ref, wd_ref, x_hbm, y_hbm,
                   wgu_s, wd_s, xbuf, ybuf, xsem, ysem):
    e = pl.program_id(0)
    rows = MOE_BLOCK * ROW_TILES
    used = used_ref[0]

    def x_copy(b, slot):
        return pltpu.make_async_copy(x_hbm.at[pl.ds(b * rows, rows)], xbuf.at[slot], xsem.at[slot])

    def y_copy(b, slot):
        return pltpu.make_async_copy(ybuf.at[slot], y_hbm.at[pl.ds(b * rows, rows)], ysem.at[slot])

    @pl.when(e == 0)
    def _():
        for b in range(X_AHEAD):
            @pl.when(b < used)
            def _():
                x_copy(b, b).start()

    @pl.when(count_ref[e] > 0)
    def _():
        wgu_s[:, :EXPERT_HIDDEN] = wg_ref[0].astype(BF16)
        wgu_s[:, EXPERT_HIDDEN:] = wu_ref[0].astype(BF16)
        wd_s[...] = wd_ref[0].astype(BF16)

    def run_blocks(b0, g):
        for i in range(g):
            ahead = b0 + i + X_AHEAD

            @pl.when(ahead < used)
            def _():
                x_copy(ahead, ahead % X_SLOTS).start()

        for i in range(g):
            done = b0 + i - Y_SLOTS

            @pl.when(done >= 0)
            def _():
                y_copy(done, done % Y_SLOTS).wait()

        x = []
        for i in range(g):
            xs = (b0 + i) % X_SLOTS
            x_copy(b0 + i, xs).wait()
            x.append(_load_row_contiguous(xbuf.at[xs], MOE_BLOCK).astype(BF16))
        gu = [jnp.dot(xi, wgu_s[...], preferred_element_type=F32) for xi in x]
        hid = [(_silu(gi[:, :EXPERT_HIDDEN]) * gi[:, EXPERT_HIDDEN:]).astype(BF16) for gi in gu]
        y = [jnp.dot(hi, wd_s[...], preferred_element_type=F32) for hi in hid]
        for i in range(g):
            ys = (b0 + i) % Y_SLOTS
            _store_row_contiguous(ybuf.at[ys], y[i])
            y_copy(b0 + i, ys).start()

    first = first_ref[e]
    count = count_ref[e]
    n_groups = lax.div(count, EXPERT_GROUP)

    def group(j, carry):
        run_blocks(first + j * EXPERT_GROUP, EXPERT_GROUP)
        return carry

    lax.fori_loop(0, n_groups, group, 0)
    for left in range(1, EXPERT_GROUP):
        @pl.when(count - n_groups * EXPERT_GROUP == left)
        def _():
            run_blocks(first + n_groups * EXPERT_GROUP, left)

    @pl.when(e == pl.num_programs(0) - 1)
    def _():
        for back in range(1, Y_SLOTS + 1):
            @pl.when(used - back >= 0)
            def _():
                y_copy(used - back, (used - back) % Y_SLOTS).wait()


def _experts(first_block, block_count, n_used, xs, w_gate, w_up, w_down):
    rows = MOE_BLOCK * ROW_TILES
    n_exp = w_gate.shape[0]
    any_spec = pl.BlockSpec(memory_space=pl.ANY)
    return pl.pallas_call(
        _expert_kernel,
        grid_spec=pltpu.PrefetchScalarGridSpec(
            num_scalar_prefetch=3,
            grid=(n_exp,),
            in_specs=[
                pl.BlockSpec((1, D_MODEL, EXPERT_HIDDEN), lambda e, *_: (e, 0, 0)),
                pl.BlockSpec((1, D_MODEL, EXPERT_HIDDEN), lambda e, *_: (e, 0, 0)),
                pl.BlockSpec((1, EXPERT_HIDDEN, D_MODEL), lambda e, *_: (e, 0, 0)),
                any_spec,
            ],
            out_specs=any_spec,
            scratch_shapes=[
                pltpu.VMEM((D_MODEL, 2 * EXPERT_HIDDEN), BF16), pltpu.VMEM((EXPERT_HIDDEN, D_MODEL), BF16),
                pltpu.VMEM((X_SLOTS, rows, LANES), F32), pltpu.VMEM((Y_SLOTS, rows, LANES), F32),
                pltpu.SemaphoreType.DMA((X_SLOTS,)), pltpu.SemaphoreType.DMA((Y_SLOTS,)),
            ],
        ),
        out_shape=jax.ShapeDtypeStruct(xs.shape, F32),
        compiler_params=_params(("arbitrary",)),
        name="experts",
    )(first_block, block_count, n_used, w_gate, w_up, w_down, xs)


def _sc_mesh():
    return plsc.VectorSubcoreMesh(core_axis_name="c", subcore_axis_name="s")


def _sc_dispatch(x3, dest_w, n_rows):
    n = x3.shape[0]

    @functools.partial(pl.kernel, out_type=jax.ShapeDtypeStruct((n_rows, ROW_TILES, LANES), F32), mesh=_sc_mesh(),
                       name="sc_dispatch")
    def run(x_hbm, d_hbm, o_hbm):
        def body(x_vmem, d_vmem):
            for kk in range(TOP_K):
                pltpu.sync_copy(x_vmem, o_hbm.at[d_vmem.at[0].at[kk]])

        pltpu.emit_pipeline(
            body,
            grid=(n // SC_WINDOW,),
            in_specs=[pl.BlockSpec((SC_WINDOW, ROW_TILES, LANES), lambda i: (i, 0, 0)),
                      pl.BlockSpec((1, TOP_K, SC_WINDOW), lambda i: (i, 0, 0))],
            out_specs=[],
            core_axis_name=("c", "s"),
            dimension_semantics=(pltpu.PARALLEL,),
        )(x_hbm, d_hbm)

    return run(x3, dest_w)


def _sc_gather(y3, dest_w):
    nw = dest_w.shape[0]
    n = nw * SC_WINDOW

    @functools.partial(pl.kernel, out_type=jax.ShapeDtypeStruct((TOP_K, n, ROW_TILES, LANES), F32), mesh=_sc_mesh(),
                       name="sc_gather")
    def run(y_hbm, d_hbm, o_hbm):
        def body(d_vmem, g_vmem):
            pltpu.sync_copy(y_hbm.at[d_vmem.at[0].at[0]], g_vmem.at[0])

        pltpu.emit_pipeline(
            body,
            grid=(nw, TOP_K),
            in_specs=[pl.BlockSpec((1, 1, SC_WINDOW), lambda i, k: (i, k, 0))],
            out_specs=[pl.BlockSpec((1, SC_WINDOW, ROW_TILES, LANES), lambda i, k: (k, i, 0, 0))],
            core_axis_name=("c", "s"),
            dimension_semantics=(pltpu.PARALLEL, pltpu.PARALLEL),
        )(d_hbm, o_hbm)

    return run(y3, dest_w)


def _final_kernel(h_ref, g_ref, w_ref, wgu_ref, wd_ref, lg_ref, lb_ref, y_ref):
    t = h_ref.shape[0]
    h = h_ref[...]
    gu = jnp.dot(h.astype(BF16), wgu_ref[...], preferred_element_type=F32)
    hid = (_silu(gu[:, :SHARED_HIDDEN]) * gu[:, SHARED_HIDDEN:]).astype(BF16)
    ffn = jnp.dot(hid, wd_ref[...], preferred_element_type=F32)
    w = w_ref[...]
    for kk in range(TOP_K):
        ffn = ffn + _load_row_contiguous(g_ref.at[kk], t) * w[:, kk:kk + 1]
    y_ref[...] = _layer_norm(DN_ALPHA * h + ffn, lg_ref[...], lb_ref[...])


def _final(h1, gathered, wts_t, w_sh_gu, w_sh_down, ln_g, ln_b, first_token):
    m = gathered.shape[1] // ROW_TILES
    t = TILE_FINAL
    off = first_token // t
    part = lambda i: (i + off, 0)
    return pl.pallas_call(
        _final_kernel,
        grid=(m // t,),
        in_specs=[
            pl.BlockSpec((t, D_MODEL), part),
            pl.BlockSpec((TOP_K, t * ROW_TILES, LANES), lambda i: (0, i, 0)),
            pl.BlockSpec((t, TOP_K), part),
            _const_spec((D_MODEL, 2 * SHARED_HIDDEN)), _const_spec((SHARED_HIDDEN, D_MODEL)),
            _const_spec((1, D_MODEL)), _const_spec((1, D_MODEL)),
        ],
        out_specs=pl.BlockSpec((t, D_MODEL), lambda i: (i, 0)),
        out_shape=jax.ShapeDtypeStruct((m, D_MODEL), F32),
        compiler_params=_params(("parallel",)),
        name="final",
    )(h1, gathered, wts_t, w_sh_gu, w_sh_down, ln_g, ln_b)


def _block_diag(w):
    g, a, b = w.shape
    out = jnp.zeros((g * a, g * b), w.dtype)
    for i in range(g):
        out = out.at[i * a:(i + 1) * a, i * b:(i + 1) * b].set(w[i])
    return out


def kernel(x_prompt, x_sample, ln0_g, ln0_b, w_in, pool_w, pool_scale, w_pool_up, gate_w2_fwd, gate_b_fwd,
           gate_w2_bwd, gate_b_bwd, gla_norm_g, w_gla_up, w_out, ln1_g, ln1_b, w_router, router_bias,
           w_exp_gate, w_exp_up, w_exp_down, w_sh_gate, w_sh_up, w_sh_down, ln2_g, ln2_b):
    bp, lp, d = x_prompt.shape
    bs, ls, _ = x_sample.shape
    n_p, n_s = bp * lp, bs * ls
    n = n_p + n_s
    starts = tuple(b * lp for b in range(bp)) + tuple(n_p + b * ls for b in range(bs))
    ends = tuple((b + 1) * lp for b in range(bp)) + tuple(n_p + (b + 1) * ls for b in range(bs))
    vec = lambda a: a.reshape(1, -1).astype(F32)

    w = w_in[0]
    c_u, c_q, c_k, c_v, c_og = 0, 512, 1024, 1536, 2560
    c_zf, c_zb, c_gp, c_gg = 3584, 3600, 3616, 4640
    w_main = jnp.concatenate([w[:, c_u:c_zf], w[:, c_gp:]], axis=1).astype(BF16)
    w_z = jnp.zeros((d, Z_PAD), F32).at[:, :2 * GATE_RANK].set(w[:, c_zf:c_gp]).astype(BF16)
    w2f = jnp.zeros((Z_PAD, GLA_K_W), F32).at[:GATE_RANK].set(gate_w2_fwd[0]).astype(BF16)
    w2b = jnp.zeros((Z_PAD, GLA_K_W), F32).at[GATE_RANK:2 * GATE_RANK].set(gate_w2_bwd[0]).astype(BF16)

    h, u, q, k, v, og, gp, gg, z = _inproj(x_prompt.reshape(n_p, d), x_sample.reshape(n_s, d),
                                           vec(ln0_g), vec(ln0_b), w_main, w_z)
    pm = _pool(u, gp, _block_diag(pool_w[0]).astype(BF16), vec(pool_scale[0]), w_pool_up[0].astype(BF16),
               starts, ends)
    ob = _gla_bwd(q, k, v, z, w2b, vec(gate_b_bwd[0]), ends)
    h1, h1_rc = _gla_fwd_mix(q, k, v, z, w2f, vec(gate_b_fwd[0]), ob, og, gg, pm, h, vec(gla_norm_g[0]),
                             w_gla_up[0].astype(BF16), w_out[0].astype(BF16), vec(ln1_g[0]), vec(ln1_b[0]), starts)

    wr_t = w_router[0].T
    wr_hi = wr_t.astype(BF16)
    wr_lo = (wr_t - wr_hi.astype(F32)).astype(BF16)
    rbias = jnp.broadcast_to(router_bias[0].astype(F32)[:, None], (N_EXPERTS, 128))
    idx, wts, rank, counts = _router(h1, wr_hi, wr_lo, rbias)

    padded = (counts + MOE_BLOCK - 1) // MOE_BLOCK * MOE_BLOCK
    pend = jnp.cumsum(padded)
    pstart = pend - padded
    n_rows = n * TOP_K + N_EXPERTS * MOE_BLOCK
    first_block = (pstart // MOE_BLOCK).astype(jnp.int32)
    block_count = (padded // MOE_BLOCK).astype(jnp.int32)
    n_used = (pend[-1:] // MOE_BLOCK).astype(jnp.int32)
    dest = _slots(pstart.astype(jnp.int32), idx, rank)

    dest_w = dest.reshape(TOP_K, n // SC_WINDOW, SC_WINDOW).transpose(1, 0, 2)
    xs = _sc_dispatch(h1_rc.reshape(n, ROW_TILES, LANES), dest_w, n_rows)
    ys = _experts(first_block, block_count, n_used, xs.reshape(n_rows * ROW_TILES, LANES),
                  w_exp_gate[0], w_exp_up[0], w_exp_down[0])
    ys = ys.reshape(n_rows, ROW_TILES, LANES)
    w_sh_gu = jnp.concatenate([w_sh_gate[0], w_sh_up[0]], axis=1).astype(BF16)
    w_sh_d = w_sh_down[0].astype(BF16)
    wts_t = wts.T
    outs = []
    for first, m in ((0, n_p), (n_p, n_s)):
        windows = dest_w[first // SC_WINDOW:(first + m) // SC_WINDOW]
        gathered = _sc_gather(ys, windows).reshape(TOP_K, m * ROW_TILES, LANES)
        outs.append(_final(h1, gathered, wts_t, w_sh_gu, w_sh_d, vec(ln2_g[0]), vec(ln2_b[0]), first))
    return outs[0].reshape(bp, lp, d), outs[1].reshape(bs, ls, d)
```

```python
import functools

import jax
import jax.numpy as jnp
from jax import lax
from jax.experimental import pallas as pl
from jax.experimental.pallas import tpu as pltpu
from jax.experimental.pallas import tpu_sc as plsc

F32 = jnp.float32
BF16 = jnp.bfloat16

D_MODEL = 1024
POOL_GROUPS = 4
POOL_GROUP_W = 128
POOL_W = POOL_GROUPS * POOL_GROUP_W
POOL_WINDOWS = (2, 4, 8, 16)
POOL_HALO = 16
GLA_HEADS = 4
GLA_K_W = 512
GLA_V_W = 1024
GLA_DK = GLA_K_W // GLA_HEADS
GLA_DV = GLA_V_W // GLA_HEADS
GATE_RANK = 16
GATE_TAU = 16.0
GLA_CHUNK = 64
Z_PAD = 128
N_EXPERTS = 256
TOP_K = 8
N_GROUPS = 8
TOPK_GROUPS = 4
EXPERTS_PER_GROUP = N_EXPERTS // N_GROUPS
EXPERT_HIDDEN = 256
SHARED_HIDDEN = 256
ROUTED_SCALE = 2.5
DEPTH = 1
DN_ALPHA = (2 * DEPTH) ** 0.25
LN_EPS = 1e-5
RMS_EPS = 1e-6

VMEM_LIMIT_BYTES = 56 * 1024 * 1024

TILE_INPROJ = 512
TILE_POOL = 512
TILE_GLA = 512
GLA_SCAN_BLOCK = 256
TILE_ROUTER = 512
TILE_FINAL = 512
TILE_SLOTS = 2048
MOE_BLOCK = 512
EXPERT_GROUP = 2
X_AHEAD = 2
X_SLOTS = X_AHEAD + EXPERT_GROUP
Y_SLOTS = 2 * EXPERT_GROUP
LANES = 128
ROW_WORDS = D_MODEL // 2
ROW_TILES = ROW_WORDS // LANES
SC_WINDOW = 64


def _params(semantics):
    return pltpu.CompilerParams(dimension_semantics=semantics, vmem_limit_bytes=VMEM_LIMIT_BYTES)


def _const_spec(shape):
    nd = len(shape)
    return pl.BlockSpec(shape, lambda *_: (0,) * nd)


def _layer_norm(x, g, b):
    mu = jnp.mean(x, axis=-1, keepdims=True)
    xc = x - mu
    var = jnp.mean(xc * xc, axis=-1, keepdims=True)
    return xc * lax.rsqrt(var + LN_EPS) * g + b


def _sigmoid(x):
    return 1.0 / (1.0 + jnp.exp(-x))


def _log_sigmoid(x):
    return jnp.minimum(x, 0.0) - jnp.log(1.0 + jnp.exp(-jnp.abs(x)))


def _silu(x):
    return x * _sigmoid(x)


def _tile_hits(i, tile, offsets):
    hit = i < 0
    for off in offsets:
        if off % tile == 0:
            hit = jnp.logical_or(hit, i == off // tile)
    return hit


_MAIN_SPLITS = (POOL_W, GLA_K_W, GLA_K_W, GLA_V_W, GLA_V_W, D_MODEL, D_MODEL)


def _inproj_kernel(xp_ref, xs_ref, g_ref, b_ref, w_ref, wz_ref,
                   h_ref, u_ref, q_ref, k_ref, v_ref, og_ref, gp_ref, gg_ref, z_ref, *, n_prompt_tiles):
    i = pl.program_id(0)
    x = jnp.where(i < n_prompt_tiles, xp_ref[...], xs_ref[...])
    h = _layer_norm(x, g_ref[...], b_ref[...])
    h_ref[...] = h
    hb = h.astype(BF16)
    off = 0
    for ref, width in zip((u_ref, q_ref, k_ref, v_ref, og_ref, gp_ref, gg_ref), _MAIN_SPLITS):
        ref[...] = jnp.dot(hb, w_ref[:, off:off + width], preferred_element_type=F32).astype(BF16)
        off += width
    z_ref[...] = jnp.dot(hb, wz_ref[...], preferred_element_type=F32).astype(BF16)


def _inproj(xp, xs, ln_g, ln_b, w_main, w_z):
    n_p, n_s = xp.shape[0], xs.shape[0]
    n = n_p + n_s
    t = TILE_INPROJ
    npt, nst = n_p // t, n_s // t
    row = lambda i: (i, 0)
    widths = _MAIN_SPLITS + (Z_PAD,)
    out_shape = [jax.ShapeDtypeStruct((n, D_MODEL), F32)] + [jax.ShapeDtypeStruct((n, w), BF16) for w in widths]
    out_specs = [pl.BlockSpec((t, D_MODEL), row)] + [pl.BlockSpec((t, w), row) for w in widths]
    return pl.pallas_call(
        functools.partial(_inproj_kernel, n_prompt_tiles=npt),
        grid=(npt + nst,),
        in_specs=[
            pl.BlockSpec((t, D_MODEL), lambda i: (jnp.minimum(i, npt - 1), 0)),
            pl.BlockSpec((t, D_MODEL), lambda i: (jnp.maximum(i - npt, 0), 0)),
            _const_spec((1, D_MODEL)), _const_spec((1, D_MODEL)),
            _const_spec(w_main.shape), _const_spec(w_z.shape),
        ],
        out_specs=out_specs,
        out_shape=out_shape,
        compiler_params=_params(("parallel",)),
        name="inproj",
    )(xp, xs, ln_g, ln_b, w_main, w_z)


def _pool_kernel(u_ref, up_ref, un_ref, gp_ref, band_ref, pw_ref, sc_ref, wup_ref, pm_ref, *, starts, ends):
    t = u_ref.shape[0]
    i = pl.program_id(0)
    is_start = _tile_hits(i, t, starts)
    is_end = _tile_hits(i + 1, t, ends)
    cur = u_ref[...]
    prev = jnp.where(is_start, jnp.zeros_like(up_ref[...]), up_ref[...])
    nxt = jnp.where(is_end, jnp.zeros_like(un_ref[...]), un_ref[...])
    ext = jnp.concatenate([prev, cur, nxt], axis=0)
    r = lax.broadcasted_iota(jnp.int32, (t, POOL_GROUP_W), 0).astype(F32)
    fs = is_start.astype(F32)
    fe = is_end.astype(F32)
    parts = []
    for gi, w in enumerate(POOL_WINDOWS):
        sl = slice(gi * POOL_GROUP_W, (gi + 1) * POOL_GROUP_W)
        s = jnp.dot(band_ref[gi], ext[:, sl], preferred_element_type=F32)
        cnt = w - fs * jnp.maximum(w // 2 - r, 0.0) - fe * jnp.maximum(r + (w // 2 - t), 0.0)
        parts.append(s / cnt - cur[:, sl].astype(F32))
    p = jnp.concatenate(parts, axis=1).astype(BF16)
    p = (jnp.dot(p, pw_ref[...], preferred_element_type=F32) * sc_ref[...]).astype(BF16)
    pool_out = jnp.dot(p, wup_ref[...], preferred_element_type=F32)
    pm_ref[...] = (_sigmoid(gp_ref[...].astype(F32)) * pool_out).astype(BF16)


def _pool_bands(t):
    r = jnp.arange(t)[:, None]
    c = jnp.arange(t + 2 * POOL_HALO)[None, :] - POOL_HALO
    return jnp.stack([((c >= r - w // 2) & (c <= r + w // 2 - 1)).astype(BF16) for w in POOL_WINDOWS])


def _pool(u, gp, pool_w_bd, pool_scale, w_pool_up, starts, ends):
    n = u.shape[0]
    t = TILE_POOL
    hb = t // POOL_HALO
    last = n // POOL_HALO - 1
    return pl.pallas_call(
        functools.partial(_pool_kernel, starts=starts, ends=ends),
        grid=(n // t,),
        in_specs=[
            pl.BlockSpec((t, POOL_W), lambda i: (i, 0)),
            pl.BlockSpec((POOL_HALO, POOL_W), lambda i: (jnp.maximum(i * hb - 1, 0), 0)),
            pl.BlockSpec((POOL_HALO, POOL_W), lambda i: (jnp.minimum((i + 1) * hb, last), 0)),
            pl.BlockSpec((t, D_MODEL), lambda i: (i, 0)),
            _const_spec((POOL_GROUPS, t, t + 2 * POOL_HALO)),
            _const_spec((POOL_W, POOL_W)), _const_spec((1, POOL_W)), _const_spec((POOL_W, D_MODEL)),
        ],
        out_specs=pl.BlockSpec((t, D_MODEL), lambda i: (i, 0)),
        out_shape=jax.ShapeDtypeStruct((n, D_MODEL), BF16),
        compiler_params=_params(("parallel",)),
        name="pool",
    )(u, u, u, gp, _pool_bands(t), pool_w_bd, pool_scale, w_pool_up)


def _gla_tile(q, k, v, la, tri, st_ref, *, reverse):
    t = q.shape[0]
    c_sz = GLA_CHUNK
    hi = la.astype(BF16)
    r1 = la - hi.astype(F32)
    mid = r1.astype(BF16)
    lo = (r1 - mid.astype(F32)).astype(BF16)
    scan = lambda rs: (jnp.dot(tri, hi[rs], preferred_element_type=F32)
                       + jnp.dot(tri, mid[rs], preferred_element_type=F32)
                       + jnp.dot(tri, lo[rs], preferred_element_type=F32))
    sb = GLA_SCAN_BLOCK
    b = jnp.concatenate([scan(slice(r0, r0 + sb)) for r0 in range(0, t, sb)], axis=0)
    kf = k.astype(F32)
    qd = (q.astype(F32) * jnp.exp(b) * (GLA_DK ** -0.5)).astype(BF16)
    kd = (kf * jnp.exp(-b)).astype(BF16)
    row = lax.broadcasted_iota(jnp.int32, (c_sz, c_sz), 0)
    col = lax.broadcasted_iota(jnp.int32, (c_sz, c_sz), 1)
    mask = (col >= row) if reverse else (col <= row)
    n_chunks = t // c_sz
    order = list(range(n_chunks - 1, -1, -1) if reverse else range(n_chunks))
    nt = (((1,), (1,)), ((), ()))
    tn = (((0,), (0,)), ((), ()))
    rows = [slice(c * c_sz, (c + 1) * c_sz) for c in range(n_chunks)]
    lanes = [slice(h * GLA_DK, (h + 1) * GLA_DK) for h in range(GLA_HEADS)]
    vcols = [slice(h * GLA_DV, (h + 1) * GLA_DV) for h in range(GLA_HEADS)]

    att, kv, dec = {}, {}, {}
    for c in order:
        b_c = b[rows[c]]
        b_edge = b_c[0:1] if reverse else b_c[c_sz - 1:c_sz]
        k2 = (kf[rows[c]] * jnp.exp(b_edge - b_c)).astype(BF16)
        dec[c] = jnp.exp(b_edge)
        for h in range(GLA_HEADS):
            att[c, h] = lax.dot_general(qd[rows[c], lanes[h]], kd[rows[c], lanes[h]], nt, preferred_element_type=F32)
            kv[c, h] = lax.dot_general(v[rows[c], vcols[h]], k2[:, lanes[h]], tn, preferred_element_type=F32)

    state_before = {}
    for h in range(GLA_HEADS):
        st = st_ref[h]
        for c in order:
            state_before[c, h] = st.astype(BF16)
            st = st * dec[c][:, lanes[h]] + kv[c, h]
        st_ref[h] = st

    outs = []
    for c in range(n_chunks):
        heads = []
        for h in range(GLA_HEADS):
            a = jnp.where(mask, att[c, h], 0.0).astype(BF16)
            o = jnp.dot(a, v[rows[c], vcols[h]], preferred_element_type=F32)
            heads.append(o + lax.dot_general(qd[rows[c], lanes[h]], state_before[c, h], nt,
                                             preferred_element_type=F32))
        outs.append(jnp.concatenate(heads, axis=1))
    return jnp.concatenate(outs, axis=0)


def _log_decay(z, w2_ref, gb_ref):
    return _log_sigmoid(jnp.dot(z, w2_ref[...], preferred_element_type=F32) + gb_ref[...]) * (1.0 / GATE_TAU)


def _gla_bwd_kernel(q_ref, k_ref, v_ref, z_ref, w2_ref, gb_ref, tri_ref, ob_ref, st_ref, *, n_tiles, ends):
    t = q_ref.shape[0]
    j = n_tiles - 1 - pl.program_id(0)

    @pl.when(_tile_hits(j + 1, t, ends))
    def _():
        st_ref[...] = jnp.zeros_like(st_ref)

    la = _log_decay(z_ref[...], w2_ref, gb_ref)
    ob_ref[...] = _gla_tile(q_ref[...], k_ref[...], v_ref[...], la, tri_ref[...], st_ref, reverse=True)


def _gla_fwd_kernel(q_ref, k_ref, v_ref, z_ref, w2_ref, gb_ref, tri_ref, ob_ref, og_ref, gg_ref, pm_ref, h_ref,
                    ng_ref, wgu_ref, wo_ref, l1g_ref, l1b_ref, h1_ref, h1rc_ref, st_ref, *, starts):
    t = q_ref.shape[0]
    i = pl.program_id(0)

    @pl.when(_tile_hits(i, t, starts))
    def _():
        st_ref[...] = jnp.zeros_like(st_ref)

    la = _log_decay(z_ref[...], w2_ref, gb_ref)
    o = _gla_tile(q_ref[...], k_ref[...], v_ref[...], la, tri_ref[...], st_ref, reverse=False) + ob_ref[...]
    heads = []
    for h in range(GLA_HEADS):
        oh = o[:, h * GLA_DV:(h + 1) * GLA_DV]
        heads.append(oh * lax.rsqrt(jnp.mean(oh * oh, axis=-1, keepdims=True) + RMS_EPS))
    o = jnp.concatenate(heads, axis=1) * ng_ref[...]
    o = (o * _silu(og_ref[...].astype(F32))).astype(BF16)
    gla_out = jnp.dot(o, wgu_ref[...], preferred_element_type=F32)
    merged = _sigmoid(gg_ref[...].astype(F32)) * gla_out + pm_ref[...].astype(F32)
    mix = jnp.dot(merged.astype(BF16), wo_ref[...], preferred_element_type=F32)
    h1 = _layer_norm(DN_ALPHA * h_ref[...] + mix, l1g_ref[...], l1b_ref[...])
    h1_ref[...] = h1
    _store_row_contiguous(h1rc_ref, h1)


def _store_row_contiguous(ref, x):
    t = x.shape[0]
    bits = pltpu.bitcast(x.astype(BF16).astype(F32), jnp.uint32)
    words = (bits[:, :ROW_WORDS] >> 16) | (bits[:, ROW_WORDS:] & jnp.uint32(0xFFFF0000))
    words = pltpu.bitcast(words, F32)
    for s in range(ROW_TILES):
        ref[pl.ds(s, t, stride=ROW_TILES), :] = words[:, s * LANES:(s + 1) * LANES]


def _load_row_contiguous(ref, t):
    words = jnp.concatenate([ref[pl.ds(s, t, stride=ROW_TILES), :] for s in range(ROW_TILES)], axis=1)
    bits = pltpu.bitcast(words, jnp.uint32)
    lo = pltpu.bitcast(bits << 16, F32)
    hi = pltpu.bitcast(bits & jnp.uint32(0xFFFF0000), F32)
    return jnp.concatenate([lo, hi], axis=1)


def _chunk_tri(t, reverse):
    r = jnp.arange(t)[:, None]
    c = jnp.arange(t)[None, :]
    same = (r // GLA_CHUNK) == (c // GLA_CHUNK)
    return (same & ((c >= r) if reverse else (c <= r))).astype(BF16)


def _gla_bwd(q, k, v, z, w2b_pad, gate_b, ends):
    n = q.shape[0]
    t = TILE_GLA
    nt = n // t
    rev = lambda i: (nt - 1 - i, 0)
    return pl.pallas_call(
        functools.partial(_gla_bwd_kernel, n_tiles=nt, ends=ends),
        grid=(nt,),
        in_specs=[
            pl.BlockSpec((t, GLA_K_W), rev), pl.BlockSpec((t, GLA_K_W), rev), pl.BlockSpec((t, GLA_V_W), rev),
            pl.BlockSpec((t, Z_PAD), rev),
            _const_spec((Z_PAD, GLA_K_W)), _const_spec((1, GLA_K_W)),
            _const_spec((GLA_SCAN_BLOCK, GLA_SCAN_BLOCK)),
        ],
        out_specs=pl.BlockSpec((t, GLA_V_W), rev),
        out_shape=jax.ShapeDtypeStruct((n, GLA_V_W), F32),
        scratch_shapes=[pltpu.VMEM((GLA_HEADS, GLA_DV, GLA_DK), F32)],
        compiler_params=_params(("arbitrary",)),
        name="gla_bwd",
    )(q, k, v, z, w2b_pad, gate_b, _chunk_tri(GLA_SCAN_BLOCK, True))


def _gla_fwd_mix(q, k, v, z, w2f_pad, gate_b, ob, og, gg, pm, h, norm_g, w_gla_up, w_out, ln1_g, ln1_b, starts):
    n = q.shape[0]
    t = TILE_GLA
    row = lambda i: (i, 0)
    return pl.pallas_call(
        functools.partial(_gla_fwd_kernel, starts=starts),
        grid=(n // t,),
        in_specs=[
            pl.BlockSpec((t, GLA_K_W), row), pl.BlockSpec((t, GLA_K_W), row), pl.BlockSpec((t, GLA_V_W), row),
            pl.BlockSpec((t, Z_PAD), row),
            _const_spec((Z_PAD, GLA_K_W)), _const_spec((1, GLA_K_W)),
            _const_spec((GLA_SCAN_BLOCK, GLA_SCAN_BLOCK)),
            pl.BlockSpec((t, GLA_V_W), row), pl.BlockSpec((t, GLA_V_W), row), pl.BlockSpec((t, D_MODEL), row),
            pl.BlockSpec((t, D_MODEL), row), pl.BlockSpec((t, D_MODEL), row),
            _const_spec((1, GLA_V_W)), _const_spec((GLA_V_W, D_MODEL)), _const_spec((D_MODEL, D_MODEL)),
            _const_spec((1, D_MODEL)), _const_spec((1, D_MODEL)),
        ],
        out_specs=[pl.BlockSpec((t, D_MODEL), row), pl.BlockSpec((t * ROW_TILES, LANES), row)],
        out_shape=[jax.ShapeDtypeStruct((n, D_MODEL), F32), jax.ShapeDtypeStruct((n * ROW_TILES, LANES), F32)],
        scratch_shapes=[pltpu.VMEM((GLA_HEADS, GLA_DV, GLA_DK), F32)],
        compiler_params=_params(("arbitrary",)),
        name="gla_fwd_mix",
    )(q, k, v, z, w2f_pad, gate_b, _chunk_tri(GLA_SCAN_BLOCK, False), ob, og, gg, pm, h, norm_g, w_gla_up, w_out,
      ln1_g, ln1_b)


def _router_kernel(h_ref, wrh_ref, wrl_ref, rb_ref, tri_ref, idx_ref, wts_ref, rank_ref, cnt_ref, base_ref):
    t = h_ref.shape[0]

    @pl.when(pl.program_id(0) == 0)
    def _():
        base_ref[...] = jnp.zeros_like(base_ref)

    h = h_ref[...]
    hh = h.astype(BF16)
    hl = (h - hh.astype(F32)).astype(BF16)
    nt = (((1,), (1,)), ((), ()))
    logits = (lax.dot_general(wrh_ref[...], hh, nt, preferred_element_type=F32)
              + lax.dot_general(wrl_ref[...], hh, nt, preferred_element_type=F32)
              + lax.dot_general(wrh_ref[...], hl, nt, preferred_element_type=F32))
    scores = _sigmoid(logits)
    sel = scores + rb_ref[:, 0:1]
    neg = jnp.float32(-jnp.inf)

    sub = lax.broadcasted_iota(jnp.int32, (EXPERTS_PER_GROUP, t), 0)
    gscore = []
    for g in range(N_GROUPS):
        sg = sel[g * EXPERTS_PER_GROUP:(g + 1) * EXPERTS_PER_GROUP]
        m1 = jnp.max(sg, axis=0, keepdims=True)
        first = jnp.min(jnp.where(sg == m1, sub, EXPERTS_PER_GROUP), axis=0, keepdims=True)
        m2 = jnp.max(jnp.where(sub == first, neg, sg), axis=0, keepdims=True)
        gscore.append(m1 + m2)
    masked = []
    for g in range(N_GROUPS):
        beaten = jnp.zeros((1, t), jnp.int32)
        for g2 in range(N_GROUPS):
            if g2 == g:
                continue
            wins = (gscore[g2] >= gscore[g]) if g2 < g else (gscore[g2] > gscore[g])
            beaten = beaten + wins.astype(jnp.int32)
        keep = beaten < TOPK_GROUPS
        masked.append(jnp.where(keep, sel[g * EXPERTS_PER_GROUP:(g + 1) * EXPERTS_PER_GROUP], neg))
    cand = jnp.concatenate(masked, axis=0)

    eid = lax.broadcasted_iota(jnp.int32, (N_EXPERTS, t), 0)
    picked = []
    assign = jnp.zeros((N_EXPERTS, t), F32)
    wsum = jnp.zeros((1, t), F32)
    for kk in range(TOP_K):
        m = jnp.max(cand, axis=0, keepdims=True)
        e_k = jnp.min(jnp.where(cand == m, eid, N_EXPERTS), axis=0, keepdims=True)
        hit = eid == e_k
        w_k = jnp.sum(jnp.where(hit, scores, 0.0), axis=0, keepdims=True)
        cand = jnp.where(hit, neg, cand)
        assign = jnp.where(hit, 1.0, assign)
        wsum = wsum + w_k
        picked.append((e_k, w_k))
        idx_ref[kk:kk + 1, :] = e_k

    a16 = assign.astype(BF16)
    before = jnp.dot(a16, tri_ref[...], preferred_element_type=F32)
    base = base_ref[...]
    before = before + jnp.concatenate([base] * (t // 128), axis=1)
    inv = ROUTED_SCALE / wsum
    for kk, (e_k, w_k) in enumerate(picked):
        wts_ref[kk:kk + 1, :] = w_k * inv
        rank_ref[kk:kk + 1, :] = jnp.sum(jnp.where(eid == e_k, before, 0.0), axis=0, keepdims=True).astype(jnp.int32)
    total = base + jnp.dot(a16, jnp.ones((t, 128), BF16), preferred_element_type=F32)
    base_ref[...] = total
    cnt_ref[...] = total


def _router(h1, wr_hi, wr_lo, rbias):
    n = h1.shape[0]
    t = TILE_ROUTER
    r = jnp.arange(t)
    tri = (r[:, None] < r[None, :]).astype(BF16)
    col = lambda i: (0, i)
    idx, wts, rank, cnt = pl.pallas_call(
        _router_kernel,
        grid=(n // t,),
        in_specs=[
            pl.BlockSpec((t, D_MODEL), lambda i: (i, 0)),
            _const_spec((N_EXPERTS, D_MODEL)), _const_spec((N_EXPERTS, D_MODEL)),
            _const_spec((N_EXPERTS, 128)), _const_spec((t, t)),
        ],
        out_specs=[pl.BlockSpec((TOP_K, t), col), pl.BlockSpec((TOP_K, t), col), pl.BlockSpec((TOP_K, t), col),
                   _const_spec((N_EXPERTS, 128))],
        out_shape=[jax.ShapeDtypeStruct((TOP_K, n), jnp.int32), jax.ShapeDtypeStruct((TOP_K, n), F32),
                   jax.ShapeDtypeStruct((TOP_K, n), jnp.int32), jax.ShapeDtypeStruct((N_EXPERTS, 128), F32)],
        scratch_shapes=[pltpu.VMEM((N_EXPERTS, 128), F32)],
        compiler_params=_params(("arbitrary",)),
        name="router",
    )(h1, wr_hi, wr_lo, rbias, tri)
    return idx, wts, rank, cnt[:, 0].astype(jnp.int32)


def _slot_kernel(ps_ref, idx_ref, rank_ref, dest_ref):
    idx = idx_ref[...]

    def add_expert(e, acc):
        return acc + jnp.where(idx == e, ps_ref[e], 0)

    dest_ref[...] = lax.fori_loop(0, N_EXPERTS, add_expert, rank_ref[...], unroll=8)


def _slots(pstart, idx, rank):
    n = idx.shape[1]
    t = TILE_SLOTS
    col = lambda i, ps: (0, i)
    return pl.pallas_call(
        _slot_kernel,
        grid_spec=pltpu.PrefetchScalarGridSpec(
            num_scalar_prefetch=1,
            grid=(n // t,),
            in_specs=[pl.BlockSpec((TOP_K, t), col), pl.BlockSpec((TOP_K, t), col)],
            out_specs=pl.BlockSpec((TOP_K, t), col),
        ),
        out_shape=jax.ShapeDtypeStruct((TOP_K, n), jnp.int32),
        compiler_params=_params(("parallel",)),
        name="slots",
    )(pstart, idx, rank)


def _expert_kernel(first_ref, count_ref, used_ref, wg_ref, wu_ref, wd_ref, x_hbm, y_hbm,
                   wgu_s, wd_s, xbuf, ybuf, xsem, ysem):
    e = pl.program_id(0)
    rows = MOE_BLOCK * ROW_TILES
    used = used_ref[0]

    def x_copy(b, slot):
        return pltpu.make_async_copy(x_hbm.at[pl.ds(b * rows, rows)], xbuf.at[slot], xsem.at[slot])

    def y_copy(b, slot):
        return pltpu.make_async_copy(ybuf.at[slot], y_hbm.at[pl.ds(b * rows, rows)], ysem.at[slot])

    @pl.when(e == 0)
    def _():
        for b in range(X_AHEAD):
            @pl.when(b < used)
            def _():
                x_copy(b, b).start()

    @pl.when(count_ref[e] > 0)
    def _():
        wgu_s[:, :EXPERT_HIDDEN] = wg_ref[0].astype(BF16)
        wgu_s[:, EXPERT_HIDDEN:] = wu_ref[0].astype(BF16)
        wd_s[...] = wd_ref[0].astype(BF16)

    def run_blocks(b0, g):
        for i in range(g):
            ahead = b0 + i + X_AHEAD

            @pl.when(ahead < used)
            def _():
                x_copy(ahead, ahead % X_SLOTS).start()

        for i in range(g):
            done = b0 + i - Y_SLOTS

            @pl.when(done >= 0)
            def _():
                y_copy(done, done % Y_SLOTS).wait()

        x = []
        for i in range(g):
            xs = (b0 + i) % X_SLOTS
            x_copy(b0 + i, xs).wait()
            x.append(_load_row_contiguous(xbuf.at[xs], MOE_BLOCK).astype(BF16))
        gu = [jnp.dot(xi, wgu_s[...], preferred_element_type=F32) for xi in x]
        hid = [(_silu(gi[:, :EXPERT_HIDDEN]) * gi[:, EXPERT_HIDDEN:]).astype(BF16) for gi in gu]
        y = [jnp.dot(hi, wd_s[...], preferred_element_type=F32) for hi in hid]
        for i in range(g):
            ys = (b0 + i) % Y_SLOTS
            _store_row_contiguous(ybuf.at[ys], y[i])
            y_copy(b0 + i, ys).start()

    first = first_ref[e]
    count = count_ref[e]
    n_groups = lax.div(count, EXPERT_GROUP)

    def group(j, carry):
        run_blocks(first + j * EXPERT_GROUP, EXPERT_GROUP)
        return carry

    lax.fori_loop(0, n_groups, group, 0)
    for left in range(1, EXPERT_GROUP):
        @pl.when(count - n_groups * EXPERT_GROUP == left)
        def _():
            run_blocks(first + n_groups * EXPERT_GROUP, left)

    @pl.when(e == pl.num_programs(0) - 1)
    def _():
        for back in range(1, Y_SLOTS + 1):
            @pl.when(used - back >= 0)
            def _():
                y_copy(used - back, (used - back) % Y_SLOTS).wait()


def _experts(first_block, block_count, n_used, xs, w_gate, w_up, w_down):
    rows = MOE_BLOCK * ROW_TILES
    n_exp = w_gate.shape[0]
    any_spec = pl.BlockSpec(memory_space=pl.ANY)
    return pl.pallas_call(
        _expert_kernel,
        grid_spec=pltpu.PrefetchScalarGridSpec(
            num_scalar_prefetch=3,
            grid=(n_exp,),
            in_specs=[
                pl.BlockSpec((1, D_MODEL, EXPERT_HIDDEN), lambda e, *_: (e, 0, 0)),
                pl.BlockSpec((1, D_MODEL, EXPERT_HIDDEN), lambda e, *_: (e, 0, 0)),
                pl.BlockSpec((1, EXPERT_HIDDEN, D_MODEL), lambda e, *_: (e, 0, 0)),
                any_spec,
            ],
            out_specs=any_spec,
            scratch_shapes=[
                pltpu.VMEM((D_MODEL, 2 * EXPERT_HIDDEN), BF16), pltpu.VMEM((EXPERT_HIDDEN, D_MODEL), BF16),
                pltpu.VMEM((X_SLOTS, rows, LANES), F32), pltpu.VMEM((Y_SLOTS, rows, LANES), F32),
                pltpu.SemaphoreType.DMA((X_SLOTS,)), pltpu.SemaphoreType.DMA((Y_SLOTS,)),
            ],
        ),
        out_shape=jax.ShapeDtypeStruct(xs.shape, F32),
        compiler_params=_params(("arbitrary",)),
        name="experts",
    )(first_block, block_count, n_used, w_gate, w_up, w_down, xs)


def _sc_mesh():
    return plsc.VectorSubcoreMesh(core_axis_name="c", subcore_axis_name="s")


def _sc_dispatch(x3, dest_w, n_rows):
    n = x3.shape[0]

    @functools.partial(pl.kernel, out_type=jax.ShapeDtypeStruct((n_rows, ROW_TILES, LANES), F32), mesh=_sc_mesh(),
                       name="sc_dispatch")
    def run(x_hbm, d_hbm, o_hbm):
        def body(x_vmem, d_vmem):
            for kk in range(TOP_K):
                pltpu.sync_copy(x_vmem, o_hbm.at[d_vmem.at[0].at[kk]])

        pltpu.emit_pipeline(
            body,
            grid=(n // SC_WINDOW,),
            in_specs=[pl.BlockSpec((SC_WINDOW, ROW_TILES, LANES), lambda i: (i, 0, 0)),
                      pl.BlockSpec((1, TOP_K, SC_WINDOW), lambda i: (i, 0, 0))],
            out_specs=[],
            core_axis_name=("c", "s"),
            dimension_semantics=(pltpu.PARALLEL,),
        )(x_hbm, d_hbm)

    return run(x3, dest_w)


def _sc_gather(y3, dest_w):
    nw = dest_w.shape[0]
    n = nw * SC_WINDOW

    @functools.partial(pl.kernel, out_type=jax.ShapeDtypeStruct((TOP_K, n, ROW_TILES, LANES), F32), mesh=_sc_mesh(),
                       name="sc_gather")
    def run(y_hbm, d_hbm, o_hbm):
        def body(d_vmem, g_vmem):
            pltpu.sync_copy(y_hbm.at[d_vmem.at[0].at[0]], g_vmem.at[0])

        pltpu.emit_pipeline(
            body,
            grid=(nw, TOP_K),
            in_specs=[pl.BlockSpec((1, 1, SC_WINDOW), lambda i, k: (i, k, 0))],
            out_specs=[pl.BlockSpec((1, SC_WINDOW, ROW_TILES, LANES), lambda i, k: (k, i, 0, 0))],
            core_axis_name=("c", "s"),
            dimension_semantics=(pltpu.PARALLEL, pltpu.PARALLEL),
        )(d_hbm, o_hbm)

    return run(y3, dest_w)


def _final_kernel(h_ref, g_ref, w_ref, wgu_ref, wd_ref, lg_ref, lb_ref, y_ref):
    t = h_ref.shape[0]
    h = h_ref[...]
    gu = jnp.dot(h.astype(BF16), wgu_ref[...], preferred_element_type=F32)
    hid = (_silu(gu[:, :SHARED_HIDDEN]) * gu[:, SHARED_HIDDEN:]).astype(BF16)
    ffn = jnp.dot(hid, wd_ref[...], preferred_element_type=F32)
    w = w_ref[...]
    for kk in range(TOP_K):
        ffn = ffn + _load_row_contiguous(g_ref.at[kk], t) * w[:, kk:kk + 1]
    y_ref[...] = _layer_norm(DN_ALPHA * h + ffn, lg_ref[...], lb_ref[...])


def _final(h1, gathered, wts_t, w_sh_gu, w_sh_down, ln_g, ln_b, first_token):
    m = gathered.shape[1] // ROW_TILES
    t = TILE_FINAL
    off = first_token // t
    part = lambda i: (i + off, 0)
    return pl.pallas_call(
        _final_kernel,
        grid=(m // t,),
        in_specs=[
            pl.BlockSpec((t, D_MODEL), part),
            pl.BlockSpec((TOP_K, t * ROW_TILES, LANES), lambda i: (0, i, 0)),
            pl.BlockSpec((t, TOP_K), part),
            _const_spec((D_MODEL, 2 * SHARED_HIDDEN)), _const_spec((SHARED_HIDDEN, D_MODEL)),
            _const_spec((1, D_MODEL)), _const_spec((1, D_MODEL)),
        ],
        out_specs=pl.BlockSpec((t, D_MODEL), lambda i: (i, 0)),
        out_shape=jax.ShapeDtypeStruct((m, D_MODEL), F32),
        compiler_params=_params(("parallel",)),
        name="final",
    )(h1, gathered, wts_t, w_sh_gu, w_sh_down, ln_g, ln_b)


def _block_diag(w):
    g, a, b = w.shape
    out = jnp.zeros((g * a, g * b), w.dtype)
    for i in range(g):
        out = out.at[i * a:(i + 1) * a, i * b:(i + 1) * b].set(w[i])
    return out


def kernel(x_prompt, x_sample, ln0_g, ln0_b, w_in, pool_w, pool_scale, w_pool_up, gate_w2_fwd, gate_b_fwd,
           gate_w2_bwd, gate_b_bwd, gla_norm_g, w_gla_up, w_out, ln1_g, ln1_b, w_router, router_bias,
           w_exp_gate, w_exp_up, w_exp_down, w_sh_gate, w_sh_up, w_sh_down, ln2_g, ln2_b):
    bp, lp, d = x_prompt.shape
    bs, ls, _ = x_sample.shape
    n_p, n_s = bp * lp, bs * ls
    n = n_p + n_s
    starts = tuple(b * lp for b in range(bp)) + tuple(n_p + b * ls for b in range(bs))
    ends = tuple((b + 1) * lp for b in range(bp)) + tuple(n_p + (b + 1) * ls for b in range(bs))
    vec = lambda a: a.reshape(1, -1).astype(F32)

    w = w_in[0]
    c_u, c_q, c_k, c_v, c_og = 0, 512, 1024, 1536, 2560
    c_zf, c_zb, c_gp, c_gg = 3584, 3600, 3616, 4640
    w_main = jnp.concatenate([w[:, c_u:c_zf], w[:, c_gp:]], axis=1).astype(BF16)
    w_z = jnp.zeros((d, Z_PAD), F32).at[:, :2 * GATE_RANK].set(w[:, c_zf:c_gp]).astype(BF16)
    w2f = jnp.zeros((Z_PAD, GLA_K_W), F32).at[:GATE_RANK].set(gate_w2_fwd[0]).astype(BF16)
    w2b = jnp.zeros((Z_PAD, GLA_K_W), F32).at[GATE_RANK:2 * GATE_RANK].set(gate_w2_bwd[0]).astype(BF16)

    h, u, q, k, v, og, gp, gg, z = _inproj(x_prompt.reshape(n_p, d), x_sample.reshape(n_s, d),
                                           vec(ln0_g), vec(ln0_b), w_main, w_z)
    pm = _pool(u, gp, _block_diag(pool_w[0]).astype(BF16), vec(pool_scale[0]), w_pool_up[0].astype(BF16),
               starts, ends)
    ob = _gla_bwd(q, k, v, z, w2b, vec(gate_b_bwd[0]), ends)
    h1, h1_rc = _gla_fwd_mix(q, k, v, z, w2f, vec(gate_b_fwd[0]), ob, og, gg, pm, h, vec(gla_norm_g[0]),
                             w_gla_up[0].astype(BF16), w_out[0].astype(BF16), vec(ln1_g[0]), vec(ln1_b[0]), starts)

    wr_t = w_router[0].T
    wr_hi = wr_t.astype(BF16)
    wr_lo = (wr_t - wr_hi.astype(F32)).astype(BF16)
    rbias = jnp.broadcast_to(router_bias[0].astype(F32)[:, None], (N_EXPERTS, 128))
    idx, wts, rank, counts = _router(h1, wr_hi, wr_lo, rbias)

    padded = (counts + MOE_BLOCK - 1) // MOE_BLOCK * MOE_BLOCK
    pend = jnp.cumsum(padded)
    pstart = pend - padded
    n_rows = n * TOP_K + N_EXPERTS * MOE_BLOCK
    first_block = (pstart // MOE_BLOCK).astype(jnp.int32)
    block_count = (padded // MOE_BLOCK).astype(jnp.int32)
    n_used = (pend[-1:] // MOE_BLOCK).astype(jnp.int32)
    dest = _slots(pstart.astype(jnp.int32), idx, rank)

    dest_w = dest.reshape(TOP_K, n // SC_WINDOW, SC_WINDOW).transpose(1, 0, 2)
    xs = _sc_dispatch(h1_rc.reshape(n, ROW_TILES, LANES), dest_w, n_rows)
    ys = _experts(first_block, block_count, n_used, xs.reshape(n_rows * ROW_TILES, LANES),
                  w_exp_gate[0], w_exp_up[0], w_exp_down[0])
    ys = ys.reshape(n_rows, ROW_TILES, LANES)
    w_sh_gu = jnp.concatenate([w_sh_gate[0], w_sh_up[0]], axis=1).astype(BF16)
    w_sh_d = w_sh_down[0].astype(BF16)
    wts_t = wts.T
    outs = []
    for first, m in ((0, n_p), (n_p, n_s)):
        windows = dest_w[first // SC_WINDOW:(first + m) // SC_WINDOW]
        gathered = _sc_gather(ys, windows).reshape(TOP_K, m * ROW_TILES, LANES)
        outs.append(_final(h1, gathered, wts_t, w_sh_gu, w_sh_d, vec(ln2_g[0]), vec(ln2_b[0]), first))
    return outs[0].reshape(bp, lp, d), outs[1].reshape(bs, ls, d)
```

```python
import functools

import jax
import jax.numpy as jnp
from jax import lax
from jax.experimental import pallas as pl
from jax.experimental.pallas import tpu as pltpu
from jax.experimental.pallas import tpu_sc as plsc

F32 = jnp.float32
BF16 = jnp.bfloat16

D_MODEL = 1024
POOL_GROUPS = 4
POOL_GROUP_W = 128
POOL_W = POOL_GROUPS * POOL_GROUP_W
POOL_WINDOWS = (2, 4, 8, 16)
POOL_HALO = 16
GLA_HEADS = 4
GLA_K_W = 512
GLA_V_W = 1024
GLA_DK = GLA_K_W // GLA_HEADS
GLA_DV = GLA_V_W // GLA_HEADS
GATE_RANK = 16
GATE_TAU = 16.0
GLA_CHUNK = 64
Z_PAD = 128
N_EXPERTS = 256
TOP_K = 8
N_GROUPS = 8
TOPK_GROUPS = 4
EXPERTS_PER_GROUP = N_EXPERTS // N_GROUPS
EXPERT_HIDDEN = 256
SHARED_HIDDEN = 256
ROUTED_SCALE = 2.5
DEPTH = 1
DN_ALPHA = (2 * DEPTH) ** 0.25
LN_EPS = 1e-5
RMS_EPS = 1e-6

VMEM_LIMIT_BYTES = 56 * 1024 * 1024

TILE_INPROJ = 512
TILE_POOL = 512
TILE_GLA = 512
GLA_SCAN_BLOCK = 256
TILE_ROUTER = 512
TILE_FINAL = 512
TILE_SLOTS = 2048
MOE_BLOCK = 512
EXPERT_GROUP = 2
X_AHEAD = 2
X_SLOTS = X_AHEAD + EXPERT_GROUP
Y_SLOTS = 2 * EXPERT_GROUP
LANES = 128
ROW_WORDS = D_MODEL // 2
ROW_TILES = ROW_WORDS // LANES
SC_WINDOW = 64
SC_COMBINE_WINDOW = 16


def _params(semantics):
    return pltpu.CompilerParams(dimension_semantics=semantics, vmem_limit_bytes=VMEM_LIMIT_BYTES)


def _const_spec(shape):
    nd = len(shape)
    return pl.BlockSpec(shape, lambda *_: (0,) * nd)


def _layer_norm(x, g, b):
    mu = jnp.mean(x, axis=-1, keepdims=True)
    xc = x - mu
    var = jnp.mean(xc * xc, axis=-1, keepdims=True)
    return xc * lax.rsqrt(var + LN_EPS) * g + b


def _sigmoid(x):
    return 1.0 / (1.0 + jnp.exp(-x))


def _log_sigmoid(x):
    return jnp.minimum(x, 0.0) - jnp.log(1.0 + jnp.exp(-jnp.abs(x)))


def _silu(x):
    return x * _sigmoid(x)


def _tile_hits(i, tile, offsets):
    hit = i < 0
    for off in offsets:
        if off % tile == 0:
            hit = jnp.logical_or(hit, i == off // tile)
    return hit


_MAIN_SPLITS = (POOL_W, GLA_K_W, GLA_K_W, GLA_V_W, GLA_V_W, D_MODEL, D_MODEL)


def _inproj_kernel(xp_ref, xs_ref, g_ref, b_ref, w_ref, wz_ref,
                   h_ref, u_ref, q_ref, k_ref, v_ref, og_ref, gp_ref, gg_ref, z_ref, *, n_prompt_tiles):
    i = pl.program_id(0)
    x = jnp.where(i < n_prompt_tiles, xp_ref[...], xs_ref[...])
    h = _layer_norm(x, g_ref[...], b_ref[...])
    h_ref[...] = h
    hb = h.astype(BF16)
    off = 0
    for ref, width in zip((u_ref, q_ref, k_ref, v_ref, og_ref, gp_ref, gg_ref), _MAIN_SPLITS):
        ref[...] = jnp.dot(hb, w_ref[:, off:off + width], preferred_element_type=F32).astype(BF16)
        off += width
    z_ref[...] = jnp.dot(hb, wz_ref[...], preferred_element_type=F32).astype(BF16)


def _inproj(xp, xs, ln_g, ln_b, w_main, w_z):
    n_p, n_s = xp.shape[0], xs.shape[0]
    n = n_p + n_s
    t = TILE_INPROJ
    npt, nst = n_p // t, n_s // t
    row = lambda i: (i, 0)
    widths = _MAIN_SPLITS + (Z_PAD,)
    out_shape = [jax.ShapeDtypeStruct((n, D_MODEL), F32)] + [jax.ShapeDtypeStruct((n, w), BF16) for w in widths]
    out_specs = [pl.BlockSpec((t, D_MODEL), row)] + [pl.BlockSpec((t, w), row) for w in widths]
    return pl.pallas_call(
        functools.partial(_inproj_kernel, n_prompt_tiles=npt),
        grid=(npt + nst,),
        in_specs=[
            pl.BlockSpec((t, D_MODEL), lambda i: (jnp.minimum(i, npt - 1), 0)),
            pl.BlockSpec((t, D_MODEL), lambda i: (jnp.maximum(i - npt, 0), 0)),
            _const_spec((1, D_MODEL)), _const_spec((1, D_MODEL)),
            _const_spec(w_main.shape), _const_spec(w_z.shape),
        ],
        out_specs=out_specs,
        out_shape=out_shape,
        compiler_params=_params(("parallel",)),
        name="inproj",
    )(xp, xs, ln_g, ln_b, w_main, w_z)


def _pool_kernel(u_ref, up_ref, un_ref, gp_ref, band_ref, pw_ref, sc_ref, wup_ref, pm_ref, *, starts, ends):
    t = u_ref.shape[0]
    i = pl.program_id(0)
    is_start = _tile_hits(i, t, starts)
    is_end = _tile_hits(i + 1, t, ends)
    cur = u_ref[...]
    prev = jnp.where(is_start, jnp.zeros_like(up_ref[...]), up_ref[...])
    nxt = jnp.where(is_end, jnp.zeros_like(un_ref[...]), un_ref[...])
    ext = jnp.concatenate([prev, cur, nxt], axis=0)
    r = lax.broadcasted_iota(jnp.int32, (t, POOL_GROUP_W), 0).astype(F32)
    fs = is_start.astype(F32)
    fe = is_end.astype(F32)
    parts = []
    for gi, w in enumerate(POOL_WINDOWS):
        sl = slice(gi * POOL_GROUP_W, (gi + 1) * POOL_GROUP_W)
        s = jnp.dot(band_ref[gi], ext[:, sl], preferred_element_type=F32)
        cnt = w - fs * jnp.maximum(w // 2 - r, 0.0) - fe * jnp.maximum(r + (w // 2 - t), 0.0)
        parts.append(s / cnt - cur[:, sl].astype(F32))
    p = jnp.concatenate(parts, axis=1).astype(BF16)
    p = (jnp.dot(p, pw_ref[...], preferred_element_type=F32) * sc_ref[...]).astype(BF16)
    pool_out = jnp.dot(p, wup_ref[...], preferred_element_type=F32)
    pm_ref[...] = (_sigmoid(gp_ref[...].astype(F32)) * pool_out).astype(BF16)


def _pool_bands(t):
    r = jnp.arange(t)[:, None]
    c = jnp.arange(t + 2 * POOL_HALO)[None, :] - POOL_HALO
    return jnp.stack([((c >= r - w // 2) & (c <= r + w // 2 - 1)).astype(BF16) for w in POOL_WINDOWS])


def _pool(u, gp, pool_w_bd, pool_scale, w_pool_up, starts, ends):
    n = u.shape[0]
    t = TILE_POOL
    hb = t // POOL_HALO
    last = n // POOL_HALO - 1
    return pl.pallas_call(
        functools.partial(_pool_kernel, starts=starts, ends=ends),
        grid=(n // t,),
        in_specs=[
            pl.BlockSpec((t, POOL_W), lambda i: (i, 0)),
            pl.BlockSpec((POOL_HALO, POOL_W), lambda i: (jnp.maximum(i * hb - 1, 0), 0)),
            pl.BlockSpec((POOL_HALO, POOL_W), lambda i: (jnp.minimum((i + 1) * hb, last), 0)),
            pl.BlockSpec((t, D_MODEL), lambda i: (i, 0)),
            _const_spec((POOL_GROUPS, t, t + 2 * POOL_HALO)),
            _const_spec((POOL_W, POOL_W)), _const_spec((1, POOL_W)), _const_spec((POOL_W, D_MODEL)),
        ],
        out_specs=pl.BlockSpec((t, D_MODEL), lambda i: (i, 0)),
        out_shape=jax.ShapeDtypeStruct((n, D_MODEL), BF16),
        compiler_params=_params(("parallel",)),
        name="pool",
    )(u, u, u, gp, _pool_bands(t), pool_w_bd, pool_scale, w_pool_up)


def _gla_tile(q, k, v, la, tri, st_ref, *, reverse):
    t = q.shape[0]
    c_sz = GLA_CHUNK
    hi = la.astype(BF16)
    r1 = la - hi.astype(F32)
    mid = r1.astype(BF16)
    lo = (r1 - mid.astype(F32)).astype(BF16)
    scan = lambda rs: (jnp.dot(tri, hi[rs], preferred_element_type=F32)
                       + jnp.dot(tri, mid[rs], preferred_element_type=F32)
                       + jnp.dot(tri, lo[rs], preferred_element_type=F32))
    sb = GLA_SCAN_BLOCK
    b = jnp.concatenate([scan(slice(r0, r0 + sb)) for r0 in range(0, t, sb)], axis=0)
    kf = k.astype(F32)
    qd = (q.astype(F32) * jnp.exp(b) * (GLA_DK ** -0.5)).astype(BF16)
    kd = (kf * jnp.exp(-b)).astype(BF16)
    row = lax.broadcasted_iota(jnp.int32, (c_sz, c_sz), 0)
    col = lax.broadcasted_iota(jnp.int32, (c_sz, c_sz), 1)
    mask = (col >= row) if reverse else (col <= row)
    n_chunks = t // c_sz
    order = list(range(n_chunks - 1, -1, -1) if reverse else range(n_chunks))
    nt = (((1,), (1,)), ((), ()))
    tn = (((0,), (0,)), ((), ()))
    rows = [slice(c * c_sz, (c + 1) * c_sz) for c in range(n_chunks)]
    lanes = [slice(h * GLA_DK, (h + 1) * GLA_DK) for h in range(GLA_HEADS)]
    vcols = [slice(h * GLA_DV, (h + 1) * GLA_DV) for h in range(GLA_HEADS)]

    att, kv, dec = {}, {}, {}
    for c in order:
        b_c = b[rows[c]]
        b_edge = b_c[0:1] if reverse else b_c[c_sz - 1:c_sz]
        k2 = (kf[rows[c]] * jnp.exp(b_edge - b_c)).astype(BF16)
        dec[c] = jnp.exp(b_edge)
        for h in range(GLA_HEADS):
            att[c, h] = lax.dot_general(qd[rows[c], lanes[h]], kd[rows[c], lanes[h]], nt, preferred_element_type=F32)
            kv[c, h] = lax.dot_general(v[rows[c], vcols[h]], k2[:, lanes[h]], tn, preferred_element_type=F32)

    state_before = {}
    for h in range(GLA_HEADS):
        st = st_ref[h]
        for c in order:
            state_before[c, h] = st.astype(BF16)
            st = st * dec[c][:, lanes[h]] + kv[c, h]
        st_ref[h] = st

    outs = []
    for c in range(n_chunks):
        heads = []
        for h in range(GLA_HEADS):
            a = jnp.where(mask, att[c, h], 0.0).astype(BF16)
            o = jnp.dot(a, v[rows[c], vcols[h]], preferred_element_type=F32)
            heads.append(o + lax.dot_general(qd[rows[c], lanes[h]], state_before[c, h], nt,
                                             preferred_element_type=F32))
        outs.append(jnp.concatenate(heads, axis=1))
    return jnp.concatenate(outs, axis=0)


def _log_decay(z, w2_ref, gb_ref):
    return _log_sigmoid(jnp.dot(z, w2_ref[...], preferred_element_type=F32) + gb_ref[...]) * (1.0 / GATE_TAU)


def _gla_bwd_kernel(q_ref, k_ref, v_ref, z_ref, w2_ref, gb_ref, tri_ref, ob_ref, st_ref, *, n_tiles, ends):
    t = q_ref.shape[0]
    j = n_tiles - 1 - pl.program_id(0)

    @pl.when(_tile_hits(j + 1, t, ends))
    def _():
        st_ref[...] = jnp.zeros_like(st_ref)

    la = _log_decay(z_ref[...], w2_ref, gb_ref)
    ob_ref[...] = _gla_tile(q_ref[...], k_ref[...], v_ref[...], la, tri_ref[...], st_ref, reverse=True)


def _gla_fwd_kernel(q_ref, k_ref, v_ref, z_ref, w2_ref, gb_ref, tri_ref, ob_ref, og_ref, gg_ref, pm_ref, h_ref,
                    ng_ref, wgu_ref, wo_ref, l1g_ref, l1b_ref, h1_ref, h1rc_ref, st_ref, *, starts):
    t = q_ref.shape[0]
    i = pl.program_id(0)

    @pl.when(_tile_hits(i, t, starts))
    def _():
        st_ref[...] = jnp.zeros_like(st_ref)

    la = _log_decay(z_ref[...], w2_ref, gb_ref)
    o = _gla_tile(q_ref[...], k_ref[...], v_ref[...], la, tri_ref[...], st_ref, reverse=False) + ob_ref[...]
    heads = []
    for h in range(GLA_HEADS):
        oh = o[:, h * GLA_DV:(h + 1) * GLA_DV]
        heads.append(oh * lax.rsqrt(jnp.mean(oh * oh, axis=-1, keepdims=True) + RMS_EPS))
    o = jnp.concatenate(heads, axis=1) * ng_ref[...]
    o = (o * _silu(og_ref[...].astype(F32))).astype(BF16)
    gla_out = jnp.dot(o, wgu_ref[...], preferred_element_type=F32)
    merged = _sigmoid(gg_ref[...].astype(F32)) * gla_out + pm_ref[...].astype(F32)
    mix = jnp.dot(merged.astype(BF16), wo_ref[...], preferred_element_type=F32)
    h1 = _layer_norm(DN_ALPHA * h_ref[...] + mix, l1g_ref[...], l1b_ref[...])
    h1_ref[...] = h1
    _store_row_contiguous(h1rc_ref, h1)


def _store_row_contiguous(ref, x):
    t = x.shape[0]
    bits = pltpu.bitcast(x.astype(BF16).astype(F32), jnp.uint32)
    words = (bits[:, :ROW_WORDS] >> 16) | (bits[:, ROW_WORDS:] & jnp.uint32(0xFFFF0000))
    words = pltpu.bitcast(words, F32)
    for s in range(ROW_TILES):
        ref[pl.ds(s, t, stride=ROW_TILES), :] = words[:, s * LANES:(s + 1) * LANES]


def _load_row_contiguous(ref, t):
    words = jnp.concatenate([ref[pl.ds(s, t, stride=ROW_TILES), :] for s in range(ROW_TILES)], axis=1)
    bits = pltpu.bitcast(words, jnp.uint32)
    lo = pltpu.bitcast(bits << 16, F32)
    hi = pltpu.bitcast(bits & jnp.uint32(0xFFFF0000), F32)
    return jnp.concatenate([lo, hi], axis=1)


def _chunk_tri(t, reverse):
    r = jnp.arange(t)[:, None]
    c = jnp.arange(t)[None, :]
    same = (r // GLA_CHUNK) == (c // GLA_CHUNK)
    return (same & ((c >= r) if reverse else (c <= r))).astype(BF16)


def _gla_bwd(q, k, v, z, w2b_pad, gate_b, ends):
    n = q.shape[0]
    t = TILE_GLA
    nt = n // t
    rev = lambda i: (nt - 1 - i, 0)
    return pl.pallas_call(
        functools.partial(_gla_bwd_kernel, n_tiles=nt, ends=ends),
        grid=(nt,),
        in_specs=[
            pl.BlockSpec((t, GLA_K_W), rev), pl.BlockSpec((t, GLA_K_W), rev), pl.BlockSpec((t, GLA_V_W), rev),
            pl.BlockSpec((t, Z_PAD), rev),
            _const_spec((Z_PAD, GLA_K_W)), _const_spec((1, GLA_K_W)),
            _const_spec((GLA_SCAN_BLOCK, GLA_SCAN_BLOCK)),
        ],
        out_specs=pl.BlockSpec((t, GLA_V_W), rev),
        out_shape=jax.ShapeDtypeStruct((n, GLA_V_W), F32),
        scratch_shapes=[pltpu.VMEM((GLA_HEADS, GLA_DV, GLA_DK), F32)],
        compiler_params=_params(("arbitrary",)),
        name="gla_bwd",
    )(q, k, v, z, w2b_pad, gate_b, _chunk_tri(GLA_SCAN_BLOCK, True))


def _gla_fwd_mix(q, k, v, z, w2f_pad, gate_b, ob, og, gg, pm, h, norm_g, w_gla_up, w_out, ln1_g, ln1_b, starts):
    n = q.shape[0]
    t = TILE_GLA
    row = lambda i: (i, 0)
    return pl.pallas_call(
        functools.partial(_gla_fwd_kernel, starts=starts),
        grid=(n // t,),
        in_specs=[
            pl.BlockSpec((t, GLA_K_W), row), pl.BlockSpec((t, GLA_K_W), row), pl.BlockSpec((t, GLA_V_W), row),
            pl.BlockSpec((t, Z_PAD), row),
            _const_spec((Z_PAD, GLA_K_W)), _const_spec((1, GLA_K_W)),
            _const_spec((GLA_SCAN_BLOCK, GLA_SCAN_BLOCK)),
            pl.BlockSpec((t, GLA_V_W), row), pl.BlockSpec((t, GLA_V_W), row), pl.BlockSpec((t, D_MODEL), row),
            pl.BlockSpec((t, D_MODEL), row), pl.BlockSpec((t, D_MODEL), row),
            _const_spec((1, GLA_V_W)), _const_spec((GLA_V_W, D_MODEL)), _const_spec((D_MODEL, D_MODEL)),
            _const_spec((1, D_MODEL)), _const_spec((1, D_MODEL)),
        ],
        out_specs=[pl.BlockSpec((t, D_MODEL), row), pl.BlockSpec((t * ROW_TILES, LANES), row)],
        out_shape=[jax.ShapeDtypeStruct((n, D_MODEL), F32), jax.ShapeDtypeStruct((n * ROW_TILES, LANES), F32)],
        scratch_shapes=[pltpu.VMEM((GLA_HEADS, GLA_DV, GLA_DK), F32)],
        compiler_params=_params(("arbitrary",)),
        name="gla_fwd_mix",
    )(q, k, v, z, w2f_pad, gate_b, _chunk_tri(GLA_SCAN_BLOCK, False), ob, og, gg, pm, h, norm_g, w_gla_up, w_out,
      ln1_g, ln1_b)


def _router_kernel(h_ref, wrh_ref, wrl_ref, rb_ref, tri_ref, idx_ref, wts_ref, rank_ref, cnt_ref, base_ref):
    t = h_ref.shape[0]

    @pl.when(pl.program_id(0) == 0)
    def _():
        base_ref[...] = jnp.zeros_like(base_ref)

    h = h_ref[...]
    hh = h.astype(BF16)
    hl = (h - hh.astype(F32)).astype(BF16)
    nt = (((1,), (1,)), ((), ()))
    logits = (lax.dot_general(wrh_ref[...], hh, nt, preferred_element_type=F32)
              + lax.dot_general(wrl_ref[...], hh, nt, preferred_element_type=F32)
              + lax.dot_general(wrh_ref[...], hl, nt, preferred_element_type=F32))
    scores = _sigmoid(logits)
    sel = scores + rb_ref[:, 0:1]
    neg = jnp.float32(-jnp.inf)

    sub = lax.broadcasted_iota(jnp.int32, (EXPERTS_PER_GROUP, t), 0)
    gscore = []
    for g in range(N_GROUPS):
        sg = sel[g * EXPERTS_PER_GROUP:(g + 1) * EXPERTS_PER_GROUP]
        m1 = jnp.max(sg, axis=0, keepdims=True)
        first = jnp.min(jnp.where(sg == m1, sub, EXPERTS_PER_GROUP), axis=0, keepdims=True)
        m2 = jnp.max(jnp.where(sub == first, neg, sg), axis=0, keepdims=True)
        gscore.append(m1 + m2)
    masked = []
    for g in range(N_GROUPS):
        beaten = jnp.zeros((1, t), jnp.int32)
        for g2 in range(N_GROUPS):
            if g2 == g:
                continue
            wins = (gscore[g2] >= gscore[g]) if g2 < g else (gscore[g2] > gscore[g])
            beaten = beaten + wins.astype(jnp.int32)
        keep = beaten < TOPK_GROUPS
        masked.append(jnp.where(keep, sel[g * EXPERTS_PER_GROUP:(g + 1) * EXPERTS_PER_GROUP], neg))
    cand = jnp.concatenate(masked, axis=0)

    eid = lax.broadcasted_iota(jnp.int32, (N_EXPERTS, t), 0)
    picked = []
    assign = jnp.zeros((N_EXPERTS, t), F32)
    wsum = jnp.zeros((1, t), F32)
    for kk in range(TOP_K):
        m = jnp.max(cand, axis=0, keepdims=True)
        e_k = jnp.min(jnp.where(cand == m, eid, N_EXPERTS), axis=0, keepdims=True)
        hit = eid == e_k
        w_k = jnp.sum(jnp.where(hit, scores, 0.0), axis=0, keepdims=True)
        cand = jnp.where(hit, neg, cand)
        assign = jnp.where(hit, 1.0, assign)
        wsum = wsum + w_k
        picked.append((e_k, w_k))
        idx_ref[kk:kk + 1, :] = e_k

    a16 = assign.astype(BF16)
    before = jnp.dot(a16, tri_ref[...], preferred_element_type=F32)
    base = base_ref[...]
    before = before + jnp.concatenate([base] * (t // 128), axis=1)
    inv = ROUTED_SCALE / wsum
    for kk, (e_k, w_k) in enumerate(picked):
        wts_ref[kk:kk + 1, :] = w_k * inv
        rank_ref[kk:kk + 1, :] = jnp.sum(jnp.where(eid == e_k, before, 0.0), axis=0, keepdims=True).astype(jnp.int32)
    total = base + jnp.dot(a16, jnp.ones((t, 128), BF16), preferred_element_type=F32)
    base_ref[...] = total
    cnt_ref[...] = total


def _router(h1, wr_hi, wr_lo, rbias):
    n = h1.shape[0]
    t = TILE_ROUTER
    r = jnp.arange(t)
    tri = (r[:, None] < r[None, :]).astype(BF16)
    col = lambda i: (0, i)
    idx, wts, rank, cnt = pl.pallas_call(
        _router_kernel,
        grid=(n // t,),
        in_specs=[
            pl.BlockSpec((t, D_MODEL), lambda i: (i, 0)),
            _const_spec((N_EXPERTS, D_MODEL)), _const_spec((N_EXPERTS, D_MODEL)),
            _const_spec((N_EXPERTS, 128)), _const_spec((t, t)),
        ],
        out_specs=[pl.BlockSpec((TOP_K, t), col), pl.BlockSpec((TOP_K, t), col), pl.BlockSpec((TOP_K, t), col),
                   _const_spec((N_EXPERTS, 128))],
        out_shape=[jax.ShapeDtypeStruct((TOP_K, n), jnp.int32), jax.ShapeDtypeStruct((TOP_K, n), F32),
                   jax.ShapeDtypeStruct((TOP_K, n), jnp.int32), jax.ShapeDtypeStruct((N_EXPERTS, 128), F32)],
        scratch_shapes=[pltpu.VMEM((N_EXPERTS, 128), F32)],
        compiler_params=_params(("arbitrary",)),
        name="router",
    )(h1, wr_hi, wr_lo, rbias, tri)
    return idx, wts, rank, cnt[:, 0].astype(jnp.int32)


def _slot_kernel(ps_ref, idx_ref, rank_ref, dest_ref):
    idx = idx_ref[...]

    def add_expert(e, acc):
        return acc + jnp.where(idx == e, ps_ref[e], 0)

    dest_ref[...] = lax.fori_loop(0, N_EXPERTS, add_expert, rank_ref[...], unroll=8)


def _slots(pstart, idx, rank):
    n = idx.shape[1]
    t = TILE_SLOTS
    col = lambda i, ps: (0, i)
    return pl.pallas_call(
        _slot_kernel,
        grid_spec=pltpu.PrefetchScalarGridSpec(
            num_scalar_prefetch=1,
            grid=(n // t,),
            in_specs=[pl.BlockSpec((TOP_K, t), col), pl.BlockSpec((TOP_K, t), col)],
            out_specs=pl.BlockSpec((TOP_K, t), col),
        ),
        out_shape=jax.ShapeDtypeStruct((TOP_K, n), jnp.int32),
        compiler_params=_params(("parallel",)),
        name="slots",
    )(pstart, idx, rank)


def _expert_kernel(first_ref, count_ref, used_ref, wg_ref, wu_ref, wd_ref, x_hbm, y_hbm,
                   wgu_s, wd_s, xbuf, ybuf, xsem, ysem):
    e = pl.program_id(0)
    rows = MOE_BLOCK * ROW_TILES
    used = used_ref[0]

    def x_copy(b, slot):
        return pltpu.make_async_copy(x_hbm.at[pl.ds(b * rows, rows)], xbuf.at[slot], xsem.at[slot])

    def y_copy(b, slot):
        return pltpu.make_async_copy(ybuf.at[slot], y_hbm.at[pl.ds(b * rows, rows)], ysem.at[slot])

    @pl.when(e == 0)
    def _():
        for b in range(X_AHEAD):
            @pl.when(b < used)
            def _():
                x_copy(b, b).start()

    @pl.when(count_ref[e] > 0)
    def _():
        wgu_s[:, :EXPERT_HIDDEN] = wg_ref[0].astype(BF16)
        wgu_s[:, EXPERT_HIDDEN:] = wu_ref[0].astype(BF16)
        wd_s[...] = wd_ref[0].astype(BF16)

    def run_blocks(b0, g):
        for i in range(g):
            ahead = b0 + i + X_AHEAD

            @pl.when(ahead < used)
            def _():
                x_copy(ahead, ahead % X_SLOTS).start()

        for i in range(g):
            done = b0 + i - Y_SLOTS

            @pl.when(done >= 0)
            def _():
                y_copy(done, done % Y_SLOTS).wait()

        x = []
        for i in range(g):
            xs = (b0 + i) % X_SLOTS
            x_copy(b0 + i, xs).wait()
            x.append(_load_row_contiguous(xbuf.at[xs], MOE_BLOCK).astype(BF16))
        gu = [jnp.dot(xi, wgu_s[...], preferred_element_type=F32) for xi in x]
        hid = [(_silu(gi[:, :EXPERT_HIDDEN]) * gi[:, EXPERT_HIDDEN:]).astype(BF16) for gi in gu]
        y = [jnp.dot(hi, wd_s[...], preferred_element_type=F32) for hi in hid]
        for i in range(g):
            ys = (b0 + i) % Y_SLOTS
            _store_row_contiguous(ybuf.at[ys], y[i])
            y_copy(b0 + i, ys).start()

    first = first_ref[e]
    count = count_ref[e]
    n_groups = lax.div(count, EXPERT_GROUP)

    def group(j, carry):
        run_blocks(first + j * EXPERT_GROUP, EXPERT_GROUP)
        return carry

    lax.fori_loop(0, n_groups, group, 0)
    for left in range(1, EXPERT_GROUP):
        @pl.when(count - n_groups * EXPERT_GROUP == left)
        def _():
            run_blocks(first + n_groups * EXPERT_GROUP, left)

    @pl.when(e == pl.num_programs(0) - 1)
    def _():
        for back in range(1, Y_SLOTS + 1):
            @pl.when(used - back >= 0)
            def _():
                y_copy(used - back, (used - back) % Y_SLOTS).wait()


def _experts(first_block, block_count, n_used, xs, w_gate, w_up, w_down):
    rows = MOE_BLOCK * ROW_TILES
    n_exp = w_gate.shape[0]
    any_spec = pl.BlockSpec(memory_space=pl.ANY)
    return pl.pallas_call(
        _expert_kernel,
        grid_spec=pltpu.PrefetchScalarGridSpec(
            num_scalar_prefetch=3,
            grid=(n_exp,),
            in_specs=[
                pl.BlockSpec((1, D_MODEL, EXPERT_HIDDEN), lambda e, *_: (e, 0, 0)),
                pl.BlockSpec((1, D_MODEL, EXPERT_HIDDEN), lambda e, *_: (e, 0, 0)),
                pl.BlockSpec((1, EXPERT_HIDDEN, D_MODEL), lambda e, *_: (e, 0, 0)),
                any_spec,
            ],
            out_specs=any_spec,
            scratch_shapes=[
                pltpu.VMEM((D_MODEL, 2 * EXPERT_HIDDEN), BF16), pltpu.VMEM((EXPERT_HIDDEN, D_MODEL), BF16),
                pltpu.VMEM((X_SLOTS, rows, LANES), F32), pltpu.VMEM((Y_SLOTS, rows, LANES), F32),
                pltpu.SemaphoreType.DMA((X_SLOTS,)), pltpu.SemaphoreType.DMA((Y_SLOTS,)),
            ],
        ),
        out_shape=jax.ShapeDtypeStruct(xs.shape, F32),
        compiler_params=_params(("arbitrary",)),
        name="experts",
    )(first_block, block_count, n_used, w_gate, w_up, w_down, xs)


def _sc_mesh():
    return plsc.VectorSubcoreMesh(core_axis_name="c", subcore_axis_name="s")


def _sc_dispatch(x3, dest_w, n_rows):
    n = x3.shape[0]

    @functools.partial(pl.kernel, out_type=jax.ShapeDtypeStruct((n_rows, ROW_TILES, LANES), F32), mesh=_sc_mesh(),
                       name="sc_dispatch")
    def run(x_hbm, d_hbm, o_hbm):
        def body(x_vmem, d_vmem):
            for kk in range(TOP_K):
                pltpu.sync_copy(x_vmem, o_hbm.at[d_vmem.at[0].at[kk]])

        pltpu.emit_pipeline(
            body,
            grid=(n // SC_WINDOW,),
            in_specs=[pl.BlockSpec((SC_WINDOW, ROW_TILES, LANES), lambda i: (i, 0, 0)),
                      pl.BlockSpec((1, TOP_K, SC_WINDOW), lambda i: (i, 0, 0))],
            out_specs=[],
            core_axis_name=("c", "s"),
            dimension_semantics=(pltpu.PARALLEL,),
        )(x_hbm, d_hbm)

    return run(x3, dest_w)


def _sc_gather(y3, dest_w):
    nw = dest_w.shape[0]
    n = nw * SC_WINDOW

    @functools.partial(pl.kernel, out_type=jax.ShapeDtypeStruct((TOP_K, n, ROW_TILES, LANES), F32), mesh=_sc_mesh(),
                       name="sc_gather")
    def run(y_hbm, d_hbm, o_hbm):
        def body(d_vmem, g_vmem):
            pltpu.sync_copy(y_hbm.at[d_vmem.at[0].at[0]], g_vmem.at[0])

        pltpu.emit_pipeline(
            body,
            grid=(nw, TOP_K),
            in_specs=[pl.BlockSpec((1, 1, SC_WINDOW), lambda i, k: (i, k, 0))],
            out_specs=[pl.BlockSpec((1, SC_WINDOW, ROW_TILES, LANES), lambda i, k: (k, i, 0, 0))],
            core_axis_name=("c", "s"),
            dimension_semantics=(pltpu.PARALLEL, pltpu.PARALLEL),
        )(d_hbm, o_hbm)

    return run(y3, dest_w)


def _sc_combine(y3, dest_w, wts_w):
    nw = dest_w.shape[0]
    w_rows = SC_COMBINE_WINDOW
    lanes_sc = 16
    per_tile = LANES // lanes_sc

    @functools.partial(
        pl.kernel, out_type=jax.ShapeDtypeStruct((nw * w_rows, 2 * ROW_TILES, LANES), F32), mesh=_sc_mesh(),
        scratch_types=[pltpu.VMEM((2, w_rows, ROW_TILES, LANES), F32), pltpu.SemaphoreType.DMA((2,))],
        compiler_params=pltpu.CompilerParams(needs_layout_passes=False),
        name="sc_combine")
    def run(y_hbm, d_hbm, w_hbm, o_hbm, buf, sem):
        def body(d_vmem, w_vmem, acc_vmem):
            copies = [pltpu.make_async_copy(y_hbm.at[d_vmem.at[0].at[kk]], buf.at[kk % 2], sem.at[kk % 2])
                      for kk in range(TOP_K)]
            copies[0].start()
            for kk in range(TOP_K):
                copies[kk].wait()
                if kk + 1 < TOP_K:
                    copies[kk + 1].start()

                @plsc.parallel_loop(0, w_rows)
                def _(t):
                    wt = w_vmem[0, kk, t, :]
                    for j in range(ROW_WORDS // lanes_sc):
                        r, c = j // per_tile, (j % per_tile) * lanes_sc
                        word = plsc.bitcast(buf[kk % 2, t, r, pl.ds(c, lanes_sc)], jnp.int32)
                        lo = plsc.bitcast(word << 16, F32) * wt
                        hi = plsc.bitcast(word & jnp.int32(-65536), F32) * wt
                        if kk == 0:
                            acc_vmem[t, r, pl.ds(c, lanes_sc)] = lo
                            acc_vmem[t, ROW_TILES + r, pl.ds(c, lanes_sc)] = hi
                        else:
                            plsc.addupdate(acc_vmem.at[t, r, pl.ds(c, lanes_sc)], lo)
                            plsc.addupdate(acc_vmem.at[t, ROW_TILES + r, pl.ds(c, lanes_sc)], hi)

        pltpu.emit_pipeline(
            body,
            grid=(nw,),
            in_specs=[pl.BlockSpec((1, TOP_K, w_rows), lambda i: (i, 0, 0)),
                      pl.BlockSpec((1, TOP_K, w_rows, lanes_sc), lambda i: (i, 0, 0, 0))],
            out_specs=[pl.BlockSpec((w_rows, 2 * ROW_TILES, LANES), lambda i: (i, 0, 0))],
            core_axis_name=("c", "s"),
            dimension_semantics=(pltpu.PARALLEL,),
        )(d_hbm, w_hbm, o_hbm)

    return run(y3, dest_w, wts_w)


def _final_kernel(h_ref, r_ref, wgu_ref, wd_ref, lg_ref, lb_ref, y_ref):
    t = h_ref.shape[0]
    h = h_ref[...]
    gu = jnp.dot(h.astype(BF16), wgu_ref[...], preferred_element_type=F32)
    hid = (_silu(gu[:, :SHARED_HIDDEN]) * gu[:, SHARED_HIDDEN:]).astype(BF16)
    shared = jnp.dot(hid, wd_ref[...], preferred_element_type=F32)
    tiles = D_MODEL // LANES
    routed = jnp.concatenate([r_ref[pl.ds(s, t, stride=tiles), :] for s in range(tiles)], axis=1)
    y_ref[...] = _layer_norm(DN_ALPHA * h + (routed + shared), lg_ref[...], lb_ref[...])


def _final(h1, routed, w_sh_gu, w_sh_down, ln_g, ln_b, first_token):
    tiles = D_MODEL // LANES
    m = routed.shape[0] // tiles
    t = TILE_FINAL
    off = first_token // t
    return pl.pallas_call(
        _final_kernel,
        grid=(m // t,),
        in_specs=[
            pl.BlockSpec((t, D_MODEL), lambda i: (i + off, 0)),
            pl.BlockSpec((t * tiles, LANES), lambda i: (i, 0)),
            _const_spec((D_MODEL, 2 * SHARED_HIDDEN)), _const_spec((SHARED_HIDDEN, D_MODEL)),
            _const_spec((1, D_MODEL)), _const_spec((1, D_MODEL)),
        ],
        out_specs=pl.BlockSpec((t, D_MODEL), lambda i: (i, 0)),
        out_shape=jax.ShapeDtypeStruct((m, D_MODEL), F32),
        compiler_params=_params(("parallel",)),
        name="final",
    )(h1, routed, w_sh_gu, w_sh_down, ln_g, ln_b)


def _block_diag(w):
    g, a, b = w.shape
    out = jnp.zeros((g * a, g * b), w.dtype)
    for i in range(g):
        out = out.at[i * a:(i + 1) * a, i * b:(i + 1) * b].set(w[i])
    return out


def kernel(x_prompt, x_sample, ln0_g, ln0_b, w_in, pool_w, pool_scale, w_pool_up, gate_w2_fwd, gate_b_fwd,
           gate_w2_bwd, gate_b_bwd, gla_norm_g, w_gla_up, w_out, ln1_g, ln1_b, w_router, router_bias,
           w_exp_gate, w_exp_up, w_exp_down, w_sh_gate, w_sh_up, w_sh_down, ln2_g, ln2_b):
    bp, lp, d = x_prompt.shape
    bs, ls, _ = x_sample.shape
    n_p, n_s = bp * lp, bs * ls
    n = n_p + n_s
    starts = tuple(b * lp for b in range(bp)) + tuple(n_p + b * ls for b in range(bs))
    ends = tuple((b + 1) * lp for b in range(bp)) + tuple(n_p + (b + 1) * ls for b in range(bs))
    vec = lambda a: a.reshape(1, -1).astype(F32)

    w = w_in[0]
    c_u, c_q, c_k, c_v, c_og = 0, 512, 1024, 1536, 2560
    c_zf, c_zb, c_gp, c_gg = 3584, 3600, 3616, 4640
    w_main = jnp.concatenate([w[:, c_u:c_zf], w[:, c_gp:]], axis=1).astype(BF16)
    w_z = jnp.zeros((d, Z_PAD), F32).at[:, :2 * GATE_RANK].set(w[:, c_zf:c_gp]).astype(BF16)
    w2f = jnp.zeros((Z_PAD, GLA_K_W), F32).at[:GATE_RANK].set(gate_w2_fwd[0]).astype(BF16)
    w2b = jnp.zeros((Z_PAD, GLA_K_W), F32).at[GATE_RANK:2 * GATE_RANK].set(gate_w2_bwd[0]).astype(BF16)

    h, u, q, k, v, og, gp, gg, z = _inproj(x_prompt.reshape(n_p, d), x_sample.reshape(n_s, d),
                                           vec(ln0_g), vec(ln0_b), w_main, w_z)
    pm = _pool(u, gp, _block_diag(pool_w[0]).astype(BF16), vec(pool_scale[0]), w_pool_up[0].astype(BF16),
               starts, ends)
    ob = _gla_bwd(q, k, v, z, w2b, vec(gate_b_bwd[0]), ends)
    h1, h1_rc = _gla_fwd_mix(q, k, v, z, w2f, vec(gate_b_fwd[0]), ob, og, gg, pm, h, vec(gla_norm_g[0]),
                             w_gla_up[0].astype(BF16), w_out[0].astype(BF16), vec(ln1_g[0]), vec(ln1_b[0]), starts)

    wr_t = w_router[0].T
    wr_hi = wr_t.astype(BF16)
    wr_lo = (wr_t - wr_hi.astype(F32)).astype(BF16)
    rbias = jnp.broadcast_to(router_bias[0].astype(F32)[:, None], (N_EXPERTS, 128))
    idx, wts, rank, counts = _router(h1, wr_hi, wr_lo, rbias)

    padded = (counts + MOE_BLOCK - 1) // MOE_BLOCK * MOE_BLOCK
    pend = jnp.cumsum(padded)
    pstart = pend - padded
    n_rows = n * TOP_K + N_EXPERTS * MOE_BLOCK
    first_block = (pstart // MOE_BLOCK).astype(jnp.int32)
    block_count = (padded // MOE_BLOCK).astype(jnp.int32)
    n_used = (pend[-1:] // MOE_BLOCK).astype(jnp.int32)
    dest = _slots(pstart.astype(jnp.int32), idx, rank)

    dest_w = dest.reshape(TOP_K, n // SC_WINDOW, SC_WINDOW).transpose(1, 0, 2)
    xs = _sc_dispatch(h1_rc.reshape(n, ROW_TILES, LANES), dest_w, n_rows)
    ys = _experts(first_block, block_count, n_used, xs.reshape(n_rows * ROW_TILES, LANES),
                  w_exp_gate[0], w_exp_up[0], w_exp_down[0])
    ys = ys.reshape(n_rows, ROW_TILES, LANES)
    w_sh_gu = jnp.concatenate([w_sh_gate[0], w_sh_up[0]], axis=1).astype(BF16)
    w_sh_d = w_sh_down[0].astype(BF16)
    cw = SC_COMBINE_WINDOW
    dest_c = dest.reshape(TOP_K, n // cw, cw).transpose(1, 0, 2)
    wts_c = jnp.broadcast_to(wts.reshape(TOP_K, n // cw, cw).transpose(1, 0, 2)[..., None], (n // cw, TOP_K, cw, 16))
    outs = []
    for first, m in ((0, n_p), (n_p, n_s)):
        win = slice(first // cw, (first + m) // cw)
        routed = _sc_combine(ys, dest_c[win], wts_c[win]).reshape(m * (D_MODEL // LANES), LANES)
        outs.append(_final(h1, routed, w_sh_gu, w_sh_d, vec(ln2_g[0]), vec(ln2_b[0]), first))
    return outs[0].reshape(bp, lp, d), outs[1].reshape(bs, ls, d)
```

```python
import functools

import jax
import jax.numpy as jnp
from jax import lax
from jax.experimental import pallas as pl
from jax.experimental.pallas import tpu as pltpu
from jax.experimental.pallas import tpu_sc as plsc

F32 = jnp.float32
BF16 = jnp.bfloat16

D_MODEL = 1024
POOL_GROUPS = 4
POOL_GROUP_W = 128
POOL_W = POOL_GROUPS * POOL_GROUP_W
POOL_WINDOWS = (2, 4, 8, 16)
POOL_HALO = 16
GLA_HEADS = 4
GLA_K_W = 512
GLA_V_W = 1024
GLA_DK = GLA_K_W // GLA_HEADS
GLA_DV = GLA_V_W // GLA_HEADS
GATE_RANK = 16
GATE_TAU = 16.0
GLA_CHUNK = 64
Z_PAD = 128
N_EXPERTS = 256
TOP_K = 8
N_GROUPS = 8
TOPK_GROUPS = 4
EXPERTS_PER_GROUP = N_EXPERTS // N_GROUPS
EXPERT_HIDDEN = 256
SHARED_HIDDEN = 256
ROUTED_SCALE = 2.5
DEPTH = 1
DN_ALPHA = (2 * DEPTH) ** 0.25
LN_EPS = 1e-5
RMS_EPS = 1e-6

VMEM_LIMIT_BYTES = 56 * 1024 * 1024

TILE_INPROJ = 512
TILE_POOL = 512
TILE_GLA = 512
GLA_SCAN_BLOCK = 256
TILE_ROUTER = 512
TILE_FINAL = 512
TILE_SLOTS = 2048
MOE_BLOCK = 512
EXPERT_GROUP = 2
X_AHEAD = 3
X_SLOTS = X_AHEAD + EXPERT_GROUP
Y_SLOTS = 2 * EXPERT_GROUP
LANES = 128
ROW_WORDS = D_MODEL // 2
ROW_TILES = ROW_WORDS // LANES
SC_WINDOW = 64


def _params(semantics):
    return pltpu.CompilerParams(dimension_semantics=semantics, vmem_limit_bytes=VMEM_LIMIT_BYTES)


def _const_spec(shape):
    nd = len(shape)
    return pl.BlockSpec(shape, lambda *_: (0,) * nd)


def _layer_norm(x, g, b):
    mu = jnp.mean(x, axis=-1, keepdims=True)
    xc = x - mu
    var = jnp.mean(xc * xc, axis=-1, keepdims=True)
    return xc * lax.rsqrt(var + LN_EPS) * g + b


def _sigmoid(x):
    return 1.0 / (1.0 + jnp.exp(-x))


def _log_sigmoid(x):
    return jnp.minimum(x, 0.0) - jnp.log(1.0 + jnp.exp(-jnp.abs(x)))


def _silu(x):
    return x * _sigmoid(x)


def _tile_hits(i, tile, offsets):
    hit = i < 0
    for off in offsets:
        if off % tile == 0:
            hit = jnp.logical_or(hit, i == off // tile)
    return hit


_MAIN_SPLITS = (POOL_W, GLA_K_W, GLA_K_W, GLA_V_W, GLA_V_W, D_MODEL, D_MODEL)


def _inproj_kernel(xp_ref, xs_ref, g_ref, b_ref, w_ref, wz_ref,
                   h_ref, u_ref, q_ref, k_ref, v_ref, og_ref, gp_ref, gg_ref, z_ref, *, n_prompt_tiles):
    i = pl.program_id(0)
    x = jnp.where(i < n_prompt_tiles, xp_ref[...], xs_ref[...])
    h = _layer_norm(x, g_ref[...], b_ref[...])
    h_ref[...] = h
    hb = h.astype(BF16)
    off = 0
    for ref, width in zip((u_ref, q_ref, k_ref, v_ref, og_ref, gp_ref, gg_ref), _MAIN_SPLITS):
        ref[...] = jnp.dot(hb, w_ref[:, off:off + width], preferred_element_type=F32).astype(BF16)
        off += width
    z_ref[...] = jnp.dot(hb, wz_ref[...], preferred_element_type=F32).astype(BF16)


def _inproj(xp, xs, ln_g, ln_b, w_main, w_z):
    n_p, n_s = xp.shape[0], xs.shape[0]
    n = n_p + n_s
    t = TILE_INPROJ
    npt, nst = n_p // t, n_s // t
    row = lambda i: (i, 0)
    widths = _MAIN_SPLITS + (Z_PAD,)
    out_shape = [jax.ShapeDtypeStruct((n, D_MODEL), F32)] + [jax.ShapeDtypeStruct((n, w), BF16) for w in widths]
    out_specs = [pl.BlockSpec((t, D_MODEL), row)] + [pl.BlockSpec((t, w), row) for w in widths]
    return pl.pallas_call(
        functools.partial(_inproj_kernel, n_prompt_tiles=npt),
        grid=(npt + nst,),
        in_specs=[
            pl.BlockSpec((t, D_MODEL), lambda i: (jnp.minimum(i, npt - 1), 0)),
            pl.BlockSpec((t, D_MODEL), lambda i: (jnp.maximum(i - npt, 0), 0)),
            _const_spec((1, D_MODEL)), _const_spec((1, D_MODEL)),
            _const_spec(w_main.shape), _const_spec(w_z.shape),
        ],
        out_specs=out_specs,
        out_shape=out_shape,
        compiler_params=_params(("parallel",)),
        name="inproj",
    )(xp, xs, ln_g, ln_b, w_main, w_z)


def _pool_kernel(u_ref, up_ref, un_ref, gp_ref, band_ref, pw_ref, sc_ref, wup_ref, pm_ref, *, starts, ends):
    t = u_ref.shape[0]
    i = pl.program_id(0)
    is_start = _tile_hits(i, t, starts)
    is_end = _tile_hits(i + 1, t, ends)
    cur = u_ref[...]
    prev = jnp.where(is_start, jnp.zeros_like(up_ref[...]), up_ref[...])
    nxt = jnp.where(is_end, jnp.zeros_like(un_ref[...]), un_ref[...])
    ext = jnp.concatenate([prev, cur, nxt], axis=0)
    r = lax.broadcasted_iota(jnp.int32, (t, POOL_GROUP_W), 0).astype(F32)
    fs = is_start.astype(F32)
    fe = is_end.astype(F32)
    parts = []
    for gi, w in enumerate(POOL_WINDOWS):
        sl = slice(gi * POOL_GROUP_W, (gi + 1) * POOL_GROUP_W)
        s = jnp.dot(band_ref[gi], ext[:, sl], preferred_element_type=F32)
        cnt = w - fs * jnp.maximum(w // 2 - r, 0.0) - fe * jnp.maximum(r + (w // 2 - t), 0.0)
        parts.append(s / cnt - cur[:, sl].astype(F32))
    p = jnp.concatenate(parts, axis=1).astype(BF16)
    p = (jnp.dot(p, pw_ref[...], preferred_element_type=F32) * sc_ref[...]).astype(BF16)
    pool_out = jnp.dot(p, wup_ref[...], preferred_element_type=F32)
    pm_ref[...] = (_sigmoid(gp_ref[...].astype(F32)) * pool_out).astype(BF16)


def _pool_bands(t):
    r = jnp.arange(t)[:, None]
    c = jnp.arange(t + 2 * POOL_HALO)[None, :] - POOL_HALO
    return jnp.stack([((c >= r - w // 2) & (c <= r + w // 2 - 1)).astype(BF16) for w in POOL_WINDOWS])


def _pool(u, gp, pool_w_bd, pool_scale, w_pool_up, starts, ends):
    n = u.shape[0]
    t = TILE_POOL
    hb = t // POOL_HALO
    last = n // POOL_HALO - 1
    return pl.pallas_call(
        functools.partial(_pool_kernel, starts=starts, ends=ends),
        grid=(n // t,),
        in_specs=[
            pl.BlockSpec((t, POOL_W), lambda i: (i, 0)),
            pl.BlockSpec((POOL_HALO, POOL_W), lambda i: (jnp.maximum(i * hb - 1, 0), 0)),
            pl.BlockSpec((POOL_HALO, POOL_W), lambda i: (jnp.minimum((i + 1) * hb, last), 0)),
            pl.BlockSpec((t, D_MODEL), lambda i: (i, 0)),
            _const_spec((POOL_GROUPS, t, t + 2 * POOL_HALO)),
            _const_spec((POOL_W, POOL_W)), _const_spec((1, POOL_W)), _const_spec((POOL_W, D_MODEL)),
        ],
        out_specs=pl.BlockSpec((t, D_MODEL), lambda i: (i, 0)),
        out_shape=jax.ShapeDtypeStruct((n, D_MODEL), BF16),
        compiler_params=_params(("parallel",)),
        name="pool",
    )(u, u, u, gp, _pool_bands(t), pool_w_bd, pool_scale, w_pool_up)


def _gla_tile(q, k, v, la, tri, st_ref, *, reverse):
    t = q.shape[0]
    c_sz = GLA_CHUNK
    hi = la.astype(BF16)
    r1 = la - hi.astype(F32)
    mid = r1.astype(BF16)
    lo = (r1 - mid.astype(F32)).astype(BF16)
    scan = lambda rs: (jnp.dot(tri, hi[rs], preferred_element_type=F32)
                       + jnp.dot(tri, mid[rs], preferred_element_type=F32)
                       + jnp.dot(tri, lo[rs], preferred_element_type=F32))
    sb = GLA_SCAN_BLOCK
    b = jnp.concatenate([scan(slice(r0, r0 + sb)) for r0 in range(0, t, sb)], axis=0)
    kf = k.astype(F32)
    qd = (q.astype(F32) * jnp.exp(b) * (GLA_DK ** -0.5)).astype(BF16)
    kd = (kf * jnp.exp(-b)).astype(BF16)
    row = lax.broadcasted_iota(jnp.int32, (c_sz, c_sz), 0)
    col = lax.broadcasted_iota(jnp.int32, (c_sz, c_sz), 1)
    mask = (col >= row) if reverse else (col <= row)
    n_chunks = t // c_sz
    order = list(range(n_chunks - 1, -1, -1) if reverse else range(n_chunks))
    nt = (((1,), (1,)), ((), ()))
    tn = (((0,), (0,)), ((), ()))
    rows = [slice(c * c_sz, (c + 1) * c_sz) for c in range(n_chunks)]
    lanes = [slice(h * GLA_DK, (h + 1) * GLA_DK) for h in range(GLA_HEADS)]
    vcols = [slice(h * GLA_DV, (h + 1) * GLA_DV) for h in range(GLA_HEADS)]

    att, kv, dec = {}, {}, {}
    for c in order:
        b_c = b[rows[c]]
        b_edge = b_c[0:1] if reverse else b_c[c_sz - 1:c_sz]
        k2 = (kf[rows[c]] * jnp.exp(b_edge - b_c)).astype(BF16)
        dec[c] = jnp.exp(b_edge)
        for h in range(GLA_HEADS):
            att[c, h] = lax.dot_general(qd[rows[c], lanes[h]], kd[rows[c], lanes[h]], nt, preferred_element_type=F32)
            kv[c, h] = lax.dot_general(v[rows[c], vcols[h]], k2[:, lanes[h]], tn, preferred_element_type=F32)

    state_before = {}
    for h in range(GLA_HEADS):
        st = st_ref[h]
        for c in order:
            state_before[c, h] = st.astype(BF16)
            st = st * dec[c][:, lanes[h]] + kv[c, h]
        st_ref[h] = st

    outs = []
    for c in range(n_chunks):
        heads = []
        for h in range(GLA_HEADS):
            a = jnp.where(mask, att[c, h], 0.0).astype(BF16)
            o = jnp.dot(a, v[rows[c], vcols[h]], preferred_element_type=F32)
            heads.append(o + lax.dot_general(qd[rows[c], lanes[h]], state_before[c, h], nt,
                                             preferred_element_type=F32))
        outs.append(jnp.concatenate(heads, axis=1))
    return jnp.concatenate(outs, axis=0)


def _log_decay(z, w2_ref, gb_ref):
    return _log_sigmoid(jnp.dot(z, w2_ref[...], preferred_element_type=F32) + gb_ref[...]) * (1.0 / GATE_TAU)


def _gla_bwd_kernel(q_ref, k_ref, v_ref, z_ref, w2_ref, gb_ref, tri_ref, ob_ref, st_ref, *, n_tiles, ends):
    t = q_ref.shape[0]
    j = n_tiles - 1 - pl.program_id(0)

    @pl.when(_tile_hits(j + 1, t, ends))
    def _():
        st_ref[...] = jnp.zeros_like(st_ref)

    la = _log_decay(z_ref[...], w2_ref, gb_ref)
    ob_ref[...] = _gla_tile(q_ref[...], k_ref[...], v_ref[...], la, tri_ref[...], st_ref, reverse=True)


def _gla_fwd_kernel(q_ref, k_ref, v_ref, z_ref, w2_ref, gb_ref, tri_ref, ob_ref, og_ref, gg_ref, pm_ref, h_ref,
                    ng_ref, wgu_ref, wo_ref, l1g_ref, l1b_ref, h1_ref, h1rc_ref, st_ref, *, starts):
    t = q_ref.shape[0]
    i = pl.program_id(0)

    @pl.when(_tile_hits(i, t, starts))
    def _():
        st_ref[...] = jnp.zeros_like(st_ref)

    la = _log_decay(z_ref[...], w2_ref, gb_ref)
    o = _gla_tile(q_ref[...], k_ref[...], v_ref[...], la, tri_ref[...], st_ref, reverse=False) + ob_ref[...]
    heads = []
    for h in range(GLA_HEADS):
        oh = o[:, h * GLA_DV:(h + 1) * GLA_DV]
        heads.append(oh * lax.rsqrt(jnp.mean(oh * oh, axis=-1, keepdims=True) + RMS_EPS))
    o = jnp.concatenate(heads, axis=1) * ng_ref[...]
    o = (o * _silu(og_ref[...].astype(F32))).astype(BF16)
    gla_out = jnp.dot(o, wgu_ref[...], preferred_element_type=F32)
    merged = _sigmoid(gg_ref[...].astype(F32)) * gla_out + pm_ref[...].astype(F32)
    mix = jnp.dot(merged.astype(BF16), wo_ref[...], preferred_element_type=F32)
    h1 = _layer_norm(DN_ALPHA * h_ref[...] + mix, l1g_ref[...], l1b_ref[...])
    h1_ref[...] = h1
    _store_row_contiguous(h1rc_ref, h1)


def _store_row_contiguous(ref, x):
    t = x.shape[0]
    bits = pltpu.bitcast(x.astype(BF16).astype(F32), jnp.uint32)
    words = (bits[:, :ROW_WORDS] >> 16) | (bits[:, ROW_WORDS:] & jnp.uint32(0xFFFF0000))
    words = pltpu.bitcast(words, F32)
    for s in range(ROW_TILES):
        ref[pl.ds(s, t, stride=ROW_TILES), :] = words[:, s * LANES:(s + 1) * LANES]


def _load_row_contiguous(ref, t):
    words = jnp.concatenate([ref[pl.ds(s, t, stride=ROW_TILES), :] for s in range(ROW_TILES)], axis=1)
    bits = pltpu.bitcast(words, jnp.uint32)
    lo = pltpu.bitcast(bits << 16, F32)
    hi = pltpu.bitcast(bits & jnp.uint32(0xFFFF0000), F32)
    return jnp.concatenate([lo, hi], axis=1)


def _chunk_tri(t, reverse):
    r = jnp.arange(t)[:, None]
    c = jnp.arange(t)[None, :]
    same = (r // GLA_CHUNK) == (c // GLA_CHUNK)
    return (same & ((c >= r) if reverse else (c <= r))).astype(BF16)


def _gla_bwd(q, k, v, z, w2b_pad, gate_b, ends):
    n = q.shape[0]
    t = TILE_GLA
    nt = n // t
    rev = lambda i: (nt - 1 - i, 0)
    return pl.pallas_call(
        functools.partial(_gla_bwd_kernel, n_tiles=nt, ends=ends),
        grid=(nt,),
        in_specs=[
            pl.BlockSpec((t, GLA_K_W), rev), pl.BlockSpec((t, GLA_K_W), rev), pl.BlockSpec((t, GLA_V_W), rev),
            pl.BlockSpec((t, Z_PAD), rev),
            _const_spec((Z_PAD, GLA_K_W)), _const_spec((1, GLA_K_W)),
            _const_spec((GLA_SCAN_BLOCK, GLA_SCAN_BLOCK)),
        ],
        out_specs=pl.BlockSpec((t, GLA_V_W), rev),
        out_shape=jax.ShapeDtypeStruct((n, GLA_V_W), F32),
        scratch_shapes=[pltpu.VMEM((GLA_HEADS, GLA_DV, GLA_DK), F32)],
        compiler_params=_params(("arbitrary",)),
        name="gla_bwd",
    )(q, k, v, z, w2b_pad, gate_b, _chunk_tri(GLA_SCAN_BLOCK, True))


def _gla_fwd_mix(q, k, v, z, w2f_pad, gate_b, ob, og, gg, pm, h, norm_g, w_gla_up, w_out, ln1_g, ln1_b, starts):
    n = q.shape[0]
    t = TILE_GLA
    row = lambda i: (i, 0)
    return pl.pallas_call(
        functools.partial(_gla_fwd_kernel, starts=starts),
        grid=(n // t,),
        in_specs=[
            pl.BlockSpec((t, GLA_K_W), row), pl.BlockSpec((t, GLA_K_W), row), pl.BlockSpec((t, GLA_V_W), row),
            pl.BlockSpec((t, Z_PAD), row),
            _const_spec((Z_PAD, GLA_K_W)), _const_spec((1, GLA_K_W)),
            _const_spec((GLA_SCAN_BLOCK, GLA_SCAN_BLOCK)),
            pl.BlockSpec((t, GLA_V_W), row), pl.BlockSpec((t, GLA_V_W), row), pl.BlockSpec((t, D_MODEL), row),
            pl.BlockSpec((t, D_MODEL), row), pl.BlockSpec((t, D_MODEL), row),
            _const_spec((1, GLA_V_W)), _const_spec((GLA_V_W, D_MODEL)), _const_spec((D_MODEL, D_MODEL)),
            _const_spec((1, D_MODEL)), _const_spec((1, D_MODEL)),
        ],
        out_specs=[pl.BlockSpec((t, D_MODEL), row), pl.BlockSpec((t * ROW_TILES, LANES), row)],
        out_shape=[jax.ShapeDtypeStruct((n, D_MODEL), F32), jax.ShapeDtypeStruct((n * ROW_TILES, LANES), F32)],
        scratch_shapes=[pltpu.VMEM((GLA_HEADS, GLA_DV, GLA_DK), F32)],
        compiler_params=_params(("arbitrary",)),
        name="gla_fwd_mix",
    )(q, k, v, z, w2f_pad, gate_b, _chunk_tri(GLA_SCAN_BLOCK, False), ob, og, gg, pm, h, norm_g, w_gla_up, w_out,
      ln1_g, ln1_b)


def _router_kernel(h_ref, wrh_ref, wrl_ref, rb_ref, tri_ref, idx_ref, wts_ref, rank_ref, cnt_ref, base_ref):
    t = h_ref.shape[0]

    @pl.when(pl.program_id(0) == 0)
    def _():
        base_ref[...] = jnp.zeros_like(base_ref)

    h = h_ref[...]
    hh = h.astype(BF16)
    hl = (h - hh.astype(F32)).astype(BF16)
    nt = (((1,), (1,)), ((), ()))
    logits = (lax.dot_general(wrh_ref[...], hh, nt, preferred_element_type=F32)
              + lax.dot_general(wrl_ref[...], hh, nt, preferred_element_type=F32)
              + lax.dot_general(wrh_ref[...], hl, nt, preferred_element_type=F32))
    scores = _sigmoid(logits)
    sel = scores + rb_ref[:, 0:1]
    neg = jnp.float32(-jnp.inf)

    sub = lax.broadcasted_iota(jnp.int32, (EXPERTS_PER_GROUP, t), 0)
    gscore = []
    for g in range(N_GROUPS):
        sg = sel[g * EXPERTS_PER_GROUP:(g + 1) * EXPERTS_PER_GROUP]
        m1 = jnp.max(sg, axis=0, keepdims=True)
        first = jnp.min(jnp.where(sg == m1, sub, EXPERTS_PER_GROUP), axis=0, keepdims=True)
        m2 = jnp.max(jnp.where(sub == first, neg, sg), axis=0, keepdims=True)
        gscore.append(m1 + m2)
    keep = []
    for g in range(N_GROUPS):
        beaten = jnp.zeros((1, t), jnp.int32)
        for g2 in range(N_GROUPS):
            if g2 == g:
                continue
            wins = (gscore[g2] >= gscore[g]) if g2 < g else (gscore[g2] > gscore[g])
            beaten = beaten + wins.astype(jnp.int32)
        keep.append(beaten < TOPK_GROUPS)
    slot = []
    ahead = jnp.zeros((1, t), jnp.int32)
    for g in range(N_GROUPS):
        slot.append([jnp.logical_and(keep[g], ahead == j) for j in range(TOPK_GROUPS)])
        ahead = ahead + keep[g].astype(jnp.int32)
    groups_of = lambda j: range(j, j + N_GROUPS - TOPK_GROUPS + 1)
    slots_of = lambda g: range(max(0, g - (N_GROUPS - TOPK_GROUPS)), min(TOPK_GROUPS - 1, g) + 1)
    rows_of = lambda x, g: x[g * EXPERTS_PER_GROUP:(g + 1) * EXPERTS_PER_GROUP]

    def compact(x, fill):
        parts = []
        for j in range(TOPK_GROUPS):
            acc = jnp.full((EXPERTS_PER_GROUP, t), fill, x.dtype)
            for g in groups_of(j):
                acc = jnp.where(slot[g][j], rows_of(x, g), acc)
            parts.append(acc)
        return jnp.concatenate(parts, axis=0)

    cand = compact(sel, neg)
    kept_scores = compact(scores, 0.0)
    gid = []
    for j in range(TOPK_GROUPS):
        acc = jnp.zeros((1, t), jnp.int32)
        for g in groups_of(j):
            acc = jnp.where(slot[g][j], g, acc)
        gid.append(acc * EXPERTS_PER_GROUP + sub)
    cid = jnp.concatenate(gid, axis=0)

    picked = []
    wsum = jnp.zeros((1, t), F32)
    for kk in range(TOP_K):
        m = jnp.max(cand, axis=0, keepdims=True)
        e_k = jnp.min(jnp.where(cand == m, cid, N_EXPERTS), axis=0, keepdims=True)
        hit = cid == e_k
        w_k = jnp.sum(jnp.where(hit, kept_scores, 0.0), axis=0, keepdims=True)
        cand = jnp.where(hit, neg, cand)
        wsum = wsum + w_k
        picked.append((e_k, w_k))
        idx_ref[kk:kk + 1, :] = e_k

    chosen = jnp.where(cand == neg, 1.0, 0.0)
    full = []
    for g in range(N_GROUPS):
        acc = jnp.zeros((EXPERTS_PER_GROUP, t), F32)
        for j in slots_of(g):
            acc = jnp.where(slot[g][j], rows_of(chosen, j), acc)
        full.append(acc)
    a16 = jnp.concatenate(full, axis=0).astype(BF16)

    before = jnp.dot(a16, tri_ref[...], preferred_element_type=F32)
    base = base_ref[...]
    before = compact(before + jnp.concatenate([base] * (t // 128), axis=1), 0.0)
    inv = ROUTED_SCALE / wsum
    for kk, (e_k, w_k) in enumerate(picked):
        wts_ref[kk:kk + 1, :] = w_k * inv
        rank_ref[kk:kk + 1, :] = jnp.sum(jnp.where(cid == e_k, before, 0.0), axis=0, keepdims=True).astype(jnp.int32)
    total = base + jnp.dot(a16, jnp.ones((t, 128), BF16), preferred_element_type=F32)
    base_ref[...] = total
    cnt_ref[...] = total


def _router(h1, wr_hi, wr_lo, rbias):
    n = h1.shape[0]
    t = TILE_ROUTER
    r = jnp.arange(t)
    tri = (r[:, None] < r[None, :]).astype(BF16)
    col = lambda i: (0, i)
    idx, wts, rank, cnt = pl.pallas_call(
        _router_kernel,
        grid=(n // t,),
        in_specs=[
            pl.BlockSpec((t, D_MODEL), lambda i: (i, 0)),
            _const_spec((N_EXPERTS, D_MODEL)), _const_spec((N_EXPERTS, D_MODEL)),
            _const_spec((N_EXPERTS, 128)), _const_spec((t, t)),
        ],
        out_specs=[pl.BlockSpec((TOP_K, t), col), pl.BlockSpec((TOP_K, t), col), pl.BlockSpec((TOP_K, t), col),
                   _const_spec((N_EXPERTS, 128))],
        out_shape=[jax.ShapeDtypeStruct((TOP_K, n), jnp.int32), jax.ShapeDtypeStruct((TOP_K, n), F32),
                   jax.ShapeDtypeStruct((TOP_K, n), jnp.int32), jax.ShapeDtypeStruct((N_EXPERTS, 128), F32)],
        scratch_shapes=[pltpu.VMEM((N_EXPERTS, 128), F32)],
        compiler_params=_params(("arbitrary",)),
        name="router",
    )(h1, wr_hi, wr_lo, rbias, tri)
    return idx, wts, rank, cnt[:, 0].astype(jnp.int32)


def _slot_kernel(ps_ref, idx_ref, rank_ref, dest_ref):
    idx = idx_ref[...]

    def add_expert(e, acc):
        return acc + jnp.where(idx == e, ps_ref[e], 0)

    dest_ref[...] = lax.fori_loop(0, N_EXPERTS, add_expert, rank_ref[...], unroll=8)


def _slots(pstart, idx, rank):
    n = idx.shape[1]
    t = TILE_SLOTS
    col = lambda i, ps: (0, i)
    return pl.pallas_call(
        _slot_kernel,
        grid_spec=pltpu.PrefetchScalarGridSpec(
            num_scalar_prefetch=1,
            grid=(n // t,),
            in_specs=[pl.BlockSpec((TOP_K, t), col), pl.BlockSpec((TOP_K, t), col)],
            out_specs=pl.BlockSpec((TOP_K, t), col),
        ),
        out_shape=jax.ShapeDtypeStruct((TOP_K, n), jnp.int32),
        compiler_params=_params(("parallel",)),
        name="slots",
    )(pstart, idx, rank)


def _expert_kernel(first_ref, count_ref, used_ref, wg_ref, wu_ref, wd_ref, x_hbm, y_hbm,
                   wgu_s, wd_s, xbuf, ybuf, xsem, ysem):
    e = pl.program_id(0)
    rows = MOE_BLOCK * ROW_TILES
    used = used_ref[0]

    def x_copy(b, slot):
        return pltpu.make_async_copy(x_hbm.at[pl.ds(b * rows, rows)], xbuf.at[slot], xsem.at[slot])

    def y_copy(b, slot):
        return pltpu.make_async_copy(ybuf.at[slot], y_hbm.at[pl.ds(b * rows, rows)], ysem.at[slot])

    @pl.when(e == 0)
    def _():
        for b in range(X_AHEAD):
            @pl.when(b < used)
            def _():
                x_copy(b, b).start()

    @pl.when(count_ref[e] > 0)
    def _():
        wgu_s[:, :EXPERT_HIDDEN] = wg_ref[0].astype(BF16)
        wgu_s[:, EXPERT_HIDDEN:] = wu_ref[0].astype(BF16)
        wd_s[...] = wd_ref[0].astype(BF16)

    def run_blocks(b0, g):
        for i in range(g):
            ahead = b0 + i + X_AHEAD

            @pl.when(ahead < used)
            def _():
                x_copy(ahead, ahead % X_SLOTS).start()

        for i in range(g):
            done = b0 + i - Y_SLOTS

            @pl.when(done >= 0)
            def _():
                y_copy(done, done % Y_SLOTS).wait()

        x = []
        for i in range(g):
            xs = (b0 + i) % X_SLOTS
            x_copy(b0 + i, xs).wait()
            x.append(_load_row_contiguous(xbuf.at[xs], MOE_BLOCK).astype(BF16))
        gu = [jnp.dot(xi, wgu_s[...], preferred_element_type=F32) for xi in x]
        hid = [(_silu(gi[:, :EXPERT_HIDDEN]) * gi[:, EXPERT_HIDDEN:]).astype(BF16) for gi in gu]
        y = [jnp.dot(hi, wd_s[...], preferred_element_type=F32) for hi in hid]
        for i in range(g):
            ys = (b0 + i) % Y_SLOTS
            _store_row_contiguous(ybuf.at[ys], y[i])
            y_copy(b0 + i, ys).start()

    first = first_ref[e]
    count = count_ref[e]
    n_groups = lax.div(count, EXPERT_GROUP)

    def group(j, carry):
        run_blocks(first + j * EXPERT_GROUP, EXPERT_GROUP)
        return carry

    lax.fori_loop(0, n_groups, group, 0)
    for left in range(1, EXPERT_GROUP):
        @pl.when(count - n_groups * EXPERT_GROUP == left)
        def _():
            run_blocks(first + n_groups * EXPERT_GROUP, left)

    @pl.when(e == pl.num_programs(0) - 1)
    def _():
        for back in range(1, Y_SLOTS + 1):
            @pl.when(used - back >= 0)
            def _():
                y_copy(used - back, (used - back) % Y_SLOTS).wait()


def _experts(first_block, block_count, n_used, xs, w_gate, w_up, w_down):
    rows = MOE_BLOCK * ROW_TILES
    n_exp = w_gate.shape[0]
    any_spec = pl.BlockSpec(memory_space=pl.ANY)
    return pl.pallas_call(
        _expert_kernel,
        grid_spec=pltpu.PrefetchScalarGridSpec(
            num_scalar_prefetch=3,
            grid=(n_exp,),
            in_specs=[
                pl.BlockSpec((1, D_MODEL, EXPERT_HIDDEN), lambda e, *_: (e, 0, 0)),
                pl.BlockSpec((1, D_MODEL, EXPERT_HIDDEN), lambda e, *_: (e, 0, 0)),
                pl.BlockSpec((1, EXPERT_HIDDEN, D_MODEL), lambda e, *_: (e, 0, 0)),
                any_spec,
            ],
            out_specs=any_spec,
            scratch_shapes=[
                pltpu.VMEM((D_MODEL, 2 * EXPERT_HIDDEN), BF16), pltpu.VMEM((EXPERT_HIDDEN, D_MODEL), BF16),
                pltpu.VMEM((X_SLOTS, rows, LANES), F32), pltpu.VMEM((Y_SLOTS, rows, LANES), F32),
                pltpu.SemaphoreType.DMA((X_SLOTS,)), pltpu.SemaphoreType.DMA((Y_SLOTS,)),
            ],
        ),
        out_shape=jax.ShapeDtypeStruct(xs.shape, F32),
        compiler_params=_params(("arbitrary",)),
        name="experts",
    )(first_block, block_count, n_used, w_gate, w_up, w_down, xs)


def _sc_mesh():
    return plsc.VectorSubcoreMesh(core_axis_name="c", subcore_axis_name="s")


def _sc_dispatch(x3, dest_w, n_rows):
    n = x3.shape[0]

    @functools.partial(pl.kernel, out_type=jax.ShapeDtypeStruct((n_rows, ROW_TILES, LANES), F32), mesh=_sc_mesh(),
                       name="sc_dispatch")
    def run(x_hbm, d_hbm, o_hbm):
        def body(x_vmem, d_vmem):
            for kk in range(TOP_K):
                pltpu.sync_copy(x_vmem, o_hbm.at[d_vmem.at[0].at[kk]])

        pltpu.emit_pipeline(
            body,
            grid=(n // SC_WINDOW,),
            in_specs=[pl.BlockSpec((SC_WINDOW, ROW_TILES, LANES), lambda i: (i, 0, 0)),
                      pl.BlockSpec((1, TOP_K, SC_WINDOW), lambda i: (i, 0, 0))],
            out_specs=[],
            core_axis_name=("c", "s"),
            dimension_semantics=(pltpu.PARALLEL,),
        )(x_hbm, d_hbm)

    return run(x3, dest_w)


def _sc_gather(y3, dest_w):
    nw = dest_w.shape[0]
    n = nw * SC_WINDOW

    @functools.partial(pl.kernel, out_type=jax.ShapeDtypeStruct((TOP_K, n, ROW_TILES, LANES), F32), mesh=_sc_mesh(),
                       name="sc_gather")
    def run(y_hbm, d_hbm, o_hbm):
        def body(d_vmem, g_vmem):
            pltpu.sync_copy(y_hbm.at[d_vmem.at[0].at[0]], g_vmem.at[0])

        pltpu.emit_pipeline(
            body,
            grid=(nw, TOP_K),
            in_specs=[pl.BlockSpec((1, 1, SC_WINDOW), lambda i, k: (i, k, 0))],
            out_specs=[pl.BlockSpec((1, SC_WINDOW, ROW_TILES, LANES), lambda i, k: (k, i, 0, 0))],
            core_axis_name=("c", "s"),
            dimension_semantics=(pltpu.PARALLEL, pltpu.PARALLEL),
        )(d_hbm, o_hbm)

    return run(y3, dest_w)


def _final_kernel(h_ref, g_ref, w_ref, wgu_ref, wd_ref, lg_ref, lb_ref, y_ref):
    t = h_ref.shape[0]
    h = h_ref[...]
    gu = jnp.dot(h.astype(BF16), wgu_ref[...], preferred_element_type=F32)
    hid = (_silu(gu[:, :SHARED_HIDDEN]) * gu[:, SHARED_HIDDEN:]).astype(BF16)
    ffn = jnp.dot(hid, wd_ref[...], preferred_element_type=F32)
    w = w_ref[...]
    for kk in range(TOP_K):
        ffn = ffn + _load_row_contiguous(g_ref.at[kk], t) * w[:, kk:kk + 1]
    y_ref[...] = _layer_norm(DN_ALPHA * h + ffn, lg_ref[...], lb_ref[...])


def _final(h1, gathered, wts_t, w_sh_gu, w_sh_down, ln_g, ln_b, first_token):
    m = gathered.shape[1] // ROW_TILES
    t = TILE_FINAL
    off = first_token // t
    part = lambda i: (i + off, 0)
    return pl.pallas_call(
        _final_kernel,
        grid=(m // t,),
        in_specs=[
            pl.BlockSpec((t, D_MODEL), part),
            pl.BlockSpec((TOP_K, t * ROW_TILES, LANES), lambda i: (0, i, 0)),
            pl.BlockSpec((t, TOP_K), part),
            _const_spec((D_MODEL, 2 * SHARED_HIDDEN)), _const_spec((SHARED_HIDDEN, D_MODEL)),
            _const_spec((1, D_MODEL)), _const_spec((1, D_MODEL)),
        ],
        out_specs=pl.BlockSpec((t, D_MODEL), lambda i: (i, 0)),
        out_shape=jax.ShapeDtypeStruct((m, D_MODEL), F32),
        compiler_params=_params(("parallel",)),
        name="final",
    )(h1, gathered, wts_t, w_sh_gu, w_sh_down, ln_g, ln_b)


def _block_diag(w):
    g, a, b = w.shape
    out = jnp.zeros((g * a, g * b), w.dtype)
    for i in range(g):
        out = out.at[i * a:(i + 1) * a, i * b:(i + 1) * b].set(w[i])
    return out


def kernel(x_prompt, x_sample, ln0_g, ln0_b, w_in, pool_w, pool_scale, w_pool_up, gate_w2_fwd, gate_b_fwd,
           gate_w2_bwd, gate_b_bwd, gla_norm_g, w_gla_up, w_out, ln1_g, ln1_b, w_router, router_bias,
           w_exp_gate, w_exp_up, w_exp_down, w_sh_gate, w_sh_up, w_sh_down, ln2_g, ln2_b):
    bp, lp, d = x_prompt.shape
    bs, ls, _ = x_sample.shape
    n_p, n_s = bp * lp, bs * ls
    n = n_p + n_s
    starts = tuple(b * lp for b in range(bp)) + tuple(n_p + b * ls for b in range(bs))
    ends = tuple((b + 1) * lp for b in range(bp)) + tuple(n_p + (b + 1) * ls for b in range(bs))
    vec = lambda a: a.reshape(1, -1).astype(F32)

    w = w_in[0]
    c_u, c_q, c_k, c_v, c_og = 0, 512, 1024, 1536, 2560
    c_zf, c_zb, c_gp, c_gg = 3584, 3600, 3616, 4640
    w_main = jnp.concatenate([w[:, c_u:c_zf], w[:, c_gp:]], axis=1).astype(BF16)
    w_z = jnp.zeros((d, Z_PAD), F32).at[:, :2 * GATE_RANK].set(w[:, c_zf:c_gp]).astype(BF16)
    w2f = jnp.zeros((Z_PAD, GLA_K_W), F32).at[:GATE_RANK].set(gate_w2_fwd[0]).astype(BF16)
    w2b = jnp.zeros((Z_PAD, GLA_K_W), F32).at[GATE_RANK:2 * GATE_RANK].set(gate_w2_bwd[0]).astype(BF16)

    h, u, q, k, v, og, gp, gg, z = _inproj(x_prompt.reshape(n_p, d), x_sample.reshape(n_s, d),
                                           vec(ln0_g), vec(ln0_b), w_main, w_z)
    pm = _pool(u, gp, _block_diag(pool_w[0]).astype(BF16), vec(pool_scale[0]), w_pool_up[0].astype(BF16),
               starts, ends)
    ob = _gla_bwd(q, k, v, z, w2b, vec(gate_b_bwd[0]), ends)
    h1, h1_rc = _gla_fwd_mix(q, k, v, z, w2f, vec(gate_b_fwd[0]), ob, og, gg, pm, h, vec(gla_norm_g[0]),
                             w_gla_up[0].astype(BF16), w_out[0].astype(BF16), vec(ln1_g[0]), vec(ln1_b[0]), starts)

    wr_t = w_router[0].T
    wr_hi = wr_t.astype(BF16)
    wr_lo = (wr_t - wr_hi.astype(F32)).astype(BF16)
    rbias = jnp.broadcast_to(router_bias[0].astype(F32)[:, None], (N_EXPERTS, 128))
    idx, wts, rank, counts = _router(h1, wr_hi, wr_lo, rbias)

    padded = (counts + MOE_BLOCK - 1) // MOE_BLOCK * MOE_BLOCK
    pend = jnp.cumsum(padded)
    pstart = pend - padded
    n_rows = n * TOP_K + N_EXPERTS * MOE_BLOCK
    first_block = (pstart // MOE_BLOCK).astype(jnp.int32)
    block_count = (padded // MOE_BLOCK).astype(jnp.int32)
    n_used = (pend[-1:] // MOE_BLOCK).astype(jnp.int32)
    dest = _slots(pstart.astype(jnp.int32), idx, rank)

    dest_w = dest.reshape(TOP_K, n // SC_WINDOW, SC_WINDOW).transpose(1, 0, 2)
    xs = _sc_dispatch(h1_rc.reshape(n, ROW_TILES, LANES), dest_w, n_rows)
    ys = _experts(first_block, block_count, n_used, xs.reshape(n_rows * ROW_TILES, LANES),
                  w_exp_gate[0], w_exp_up[0], w_exp_down[0])
    ys = ys.reshape(n_rows, ROW_TILES, LANES)
    w_sh_gu = jnp.concatenate([w_sh_gate[0], w_sh_up[0]], axis=1).astype(BF16)
    w_sh_d = w_sh_down[0].astype(BF16)
    wts_t = wts.T
    outs = []
    for first, m in ((0, n_p), (n_p, n_s)):
        windows = dest_w[first // SC_WINDOW:(first + m) // SC_WINDOW]
        gathered = _sc_gather(ys, windows).reshape(TOP_K, m * ROW_TILES, LANES)
        outs.append(_final(h1, gathered, wts_t, w_sh_gu, w_sh_d, vec(ln2_g[0]), vec(ln2_b[0]), first))
    return outs[0].reshape(bp, lp, d), outs[1].reshape(bs, ls, d)
```

```python
import functools

import jax
import jax.numpy as jnp
from jax import lax
from jax.experimental import pallas as pl
from jax.experimental.pallas import tpu as pltpu
from jax.experimental.pallas import tpu_sc as plsc

F32 = jnp.float32
BF16 = jnp.bfloat16

D_MODEL = 1024
POOL_GROUPS = 4
POOL_GROUP_W = 128
POOL_W = POOL_GROUPS * POOL_GROUP_W
POOL_WINDOWS = (2, 4, 8, 16)
POOL_HALO = 16
GLA_HEADS = 4
GLA_K_W = 512
GLA_V_W = 1024
GLA_DK = GLA_K_W // GLA_HEADS
GLA_DV = GLA_V_W // GLA_HEADS
GATE_RANK = 16
GATE_TAU = 16.0
GLA_CHUNK = 64
Z_PAD = 128
N_EXPERTS = 256
TOP_K = 8
N_GROUPS = 8
TOPK_GROUPS = 4
EXPERTS_PER_GROUP = N_EXPERTS // N_GROUPS
EXPERT_HIDDEN = 256
SHARED_HIDDEN = 256
ROUTED_SCALE = 2.5
DEPTH = 1
DN_ALPHA = (2 * DEPTH) ** 0.25
LN_EPS = 1e-5
RMS_EPS = 1e-6

VMEM_LIMIT_BYTES = 56 * 1024 * 1024

TILE_INPROJ = 512
TILE_POOL = 512
TILE_GLA = 512
GLA_SCAN_BLOCK = 256
TILE_ROUTER = 512
TILE_FINAL = 512
TILE_SLOTS = 2048
MOE_BLOCK = 512
EXPERT_GROUP = 2
X_AHEAD = 5
X_SLOTS = X_AHEAD + EXPERT_GROUP
Y_SLOTS = 3 * EXPERT_GROUP
LANES = 128
ROW_WORDS = D_MODEL // 2
ROW_TILES = ROW_WORDS // LANES
SC_WINDOW = 64


def _params(semantics):
    return pltpu.CompilerParams(dimension_semantics=semantics, vmem_limit_bytes=VMEM_LIMIT_BYTES)


def _const_spec(shape):
    nd = len(shape)
    return pl.BlockSpec(shape, lambda *_: (0,) * nd)


def _layer_norm(x, g, b):
    mu = jnp.mean(x, axis=-1, keepdims=True)
    xc = x - mu
    var = jnp.mean(xc * xc, axis=-1, keepdims=True)
    return xc * lax.rsqrt(var + LN_EPS) * g + b


def _sigmoid(x):
    return 1.0 / (1.0 + jnp.exp(-x))


def _log_sigmoid(x):
    return jnp.minimum(x, 0.0) - jnp.log(1.0 + jnp.exp(-jnp.abs(x)))


def _silu(x):
    return x * _sigmoid(x)


def _tile_hits(i, tile, offsets):
    hit = i < 0
    for off in offsets:
        if off % tile == 0:
            hit = jnp.logical_or(hit, i == off // tile)
    return hit


_MAIN_SPLITS = (POOL_W, GLA_K_W, GLA_K_W, GLA_V_W, GLA_V_W, D_MODEL, D_MODEL)


def _inproj_kernel(xp_ref, xs_ref, g_ref, b_ref, w_ref, wz_ref,
                   h_ref, u_ref, q_ref, k_ref, v_ref, og_ref, gp_ref, gg_ref, z_ref, *, n_prompt_tiles):
    i = pl.program_id(0)
    x = jnp.where(i < n_prompt_tiles, xp_ref[...], xs_ref[...])
    h = _layer_norm(x, g_ref[...], b_ref[...])
    h_ref[...] = h
    hb = h.astype(BF16)
    off = 0
    for ref, width in zip((u_ref, q_ref, k_ref, v_ref, og_ref, gp_ref, gg_ref), _MAIN_SPLITS):
        ref[...] = jnp.dot(hb, w_ref[:, off:off + width], preferred_element_type=F32).astype(BF16)
        off += width
    z_ref[...] = jnp.dot(hb, wz_ref[...], preferred_element_type=F32).astype(BF16)


def _inproj(xp, xs, ln_g, ln_b, w_main, w_z):
    n_p, n_s = xp.shape[0], xs.shape[0]
    n = n_p + n_s
    t = TILE_INPROJ
    npt, nst = n_p // t, n_s // t
    row = lambda i: (i, 0)
    widths = _MAIN_SPLITS + (Z_PAD,)
    out_shape = [jax.ShapeDtypeStruct((n, D_MODEL), F32)] + [jax.ShapeDtypeStruct((n, w), BF16) for w in widths]
    out_specs = [pl.BlockSpec((t, D_MODEL), row)] + [pl.BlockSpec((t, w), row) for w in widths]
    return pl.pallas_call(
        functools.partial(_inproj_kernel, n_prompt_tiles=npt),
        grid=(npt + nst,),
        in_specs=[
            pl.BlockSpec((t, D_MODEL), lambda i: (jnp.minimum(i, npt - 1), 0)),
            pl.BlockSpec((t, D_MODEL), lambda i: (jnp.maximum(i - npt, 0), 0)),
            _const_spec((1, D_MODEL)), _const_spec((1, D_MODEL)),
            _const_spec(w_main.shape), _const_spec(w_z.shape),
        ],
        out_specs=out_specs,
        out_shape=out_shape,
        compiler_params=_params(("parallel",)),
        name="inproj",
    )(xp, xs, ln_g, ln_b, w_main, w_z)


def _pool_kernel(u_ref, up_ref, un_ref, gp_ref, band_ref, pw_ref, sc_ref, wup_ref, pm_ref, *, starts, ends):
    t = u_ref.shape[0]
    i = pl.program_id(0)
    is_start = _tile_hits(i, t, starts)
    is_end = _tile_hits(i + 1, t, ends)
    cur = u_ref[...]
    prev = jnp.where(is_start, jnp.zeros_like(up_ref[...]), up_ref[...])
    nxt = jnp.where(is_end, jnp.zeros_like(un_ref[...]), un_ref[...])
    ext = jnp.concatenate([prev, cur, nxt], axis=0)
    r = lax.broadcasted_iota(jnp.int32, (t, POOL_GROUP_W), 0).astype(F32)
    fs = is_start.astype(F32)
    fe = is_end.astype(F32)
    parts = []
    for gi, w in enumerate(POOL_WINDOWS):
        sl = slice(gi * POOL_GROUP_W, (gi + 1) * POOL_GROUP_W)
        s = jnp.dot(band_ref[gi], ext[:, sl], preferred_element_type=F32)
        cnt = w - fs * jnp.maximum(w // 2 - r, 0.0) - fe * jnp.maximum(r + (w // 2 - t), 0.0)
        parts.append(s / cnt - cur[:, sl].astype(F32))
    p = jnp.concatenate(parts, axis=1).astype(BF16)
    p = (jnp.dot(p, pw_ref[...], preferred_element_type=F32) * sc_ref[...]).astype(BF16)
    pool_out = jnp.dot(p, wup_ref[...], preferred_element_type=F32)
    pm_ref[...] = (_sigmoid(gp_ref[...].astype(F32)) * pool_out).astype(BF16)


def _pool_bands(t):
    r = jnp.arange(t)[:, None]
    c = jnp.arange(t + 2 * POOL_HALO)[None, :] - POOL_HALO
    return jnp.stack([((c >= r - w // 2) & (c <= r + w // 2 - 1)).astype(BF16) for w in POOL_WINDOWS])


def _pool(u, gp, pool_w_bd, pool_scale, w_pool_up, starts, ends):
    n = u.shape[0]
    t = TILE_POOL
    hb = t // POOL_HALO
    last = n // POOL_HALO - 1
    return pl.pallas_call(
        functools.partial(_pool_kernel, starts=starts, ends=ends),
        grid=(n // t,),
        in_specs=[
            pl.BlockSpec((t, POOL_W), lambda i: (i, 0)),
            pl.BlockSpec((POOL_HALO, POOL_W), lambda i: (jnp.maximum(i * hb - 1, 0), 0)),
            pl.BlockSpec((POOL_HALO, POOL_W), lambda i: (jnp.minimum((i + 1) * hb, last), 0)),
            pl.BlockSpec((t, D_MODEL), lambda i: (i, 0)),
            _const_spec((POOL_GROUPS, t, t + 2 * POOL_HALO)),
            _const_spec((POOL_W, POOL_W)), _const_spec((1, POOL_W)), _const_spec((POOL_W, D_MODEL)),
        ],
        out_specs=pl.BlockSpec((t, D_MODEL), lambda i: (i, 0)),
        out_shape=jax.ShapeDtypeStruct((n, D_MODEL), BF16),
        compiler_params=_params(("parallel",)),
        name="pool",
    )(u, u, u, gp, _pool_bands(t), pool_w_bd, pool_scale, w_pool_up)


def _gla_tile(q, k, v, la, tri, st_ref, *, reverse):
    t = q.shape[0]
    c_sz = GLA_CHUNK
    hi = la.astype(BF16)
    r1 = la - hi.astype(F32)
    mid = r1.astype(BF16)
    lo = (r1 - mid.astype(F32)).astype(BF16)
    scan = lambda rs: (jnp.dot(tri, hi[rs], preferred_element_type=F32)
                       + jnp.dot(tri, mid[rs], preferred_element_type=F32)
                       + jnp.dot(tri, lo[rs], preferred_element_type=F32))
    sb = GLA_SCAN_BLOCK
    b = jnp.concatenate([scan(slice(r0, r0 + sb)) for r0 in range(0, t, sb)], axis=0)
    kf = k.astype(F32)
    qd = (q.astype(F32) * jnp.exp(b) * (GLA_DK ** -0.5)).astype(BF16)
    kd = (kf * jnp.exp(-b)).astype(BF16)
    row = lax.broadcasted_iota(jnp.int32, (c_sz, c_sz), 0)
    col = lax.broadcasted_iota(jnp.int32, (c_sz, c_sz), 1)
    mask = (col >= row) if reverse else (col <= row)
    n_chunks = t // c_sz
    order = list(range(n_chunks - 1, -1, -1) if reverse else range(n_chunks))
    nt = (((1,), (1,)), ((), ()))
    tn = (((0,), (0,)), ((), ()))
    rows = [slice(c * c_sz, (c + 1) * c_sz) for c in range(n_chunks)]
    lanes = [slice(h * GLA_DK, (h + 1) * GLA_DK) for h in range(GLA_HEADS)]
    vcols = [slice(h * GLA_DV, (h + 1) * GLA_DV) for h in range(GLA_HEADS)]

    att, kv, dec = {}, {}, {}
    for c in order:
        b_c = b[rows[c]]
        b_edge = b_c[0:1] if reverse else b_c[c_sz - 1:c_sz]
        k2 = (kf[rows[c]] * jnp.exp(b_edge - b_c)).astype(BF16)
        dec[c] = jnp.exp(b_edge)
        for h in range(GLA_HEADS):
            att[c, h] = lax.dot_general(qd[rows[c], lanes[h]], kd[rows[c], lanes[h]], nt, preferred_element_type=F32)
            kv[c, h] = lax.dot_general(v[rows[c], vcols[h]], k2[:, lanes[h]], tn, preferred_element_type=F32)

    state_before = {}
    for h in range(GLA_HEADS):
        st = st_ref[h]
        for c in order:
            state_before[c, h] = st.astype(BF16)
            st = st * dec[c][:, lanes[h]] + kv[c, h]
        st_ref[h] = st

    outs = []
    for c in range(n_chunks):
        heads = []
        for h in range(GLA_HEADS):
            a = jnp.where(mask, att[c, h], 0.0).astype(BF16)
            o = jnp.dot(a, v[rows[c], vcols[h]], preferred_element_type=F32)
            heads.append(o + lax.dot_general(qd[rows[c], lanes[h]], state_before[c, h], nt,
                                             preferred_element_type=F32))
        outs.append(jnp.concatenate(heads, axis=1))
    return jnp.concatenate(outs, axis=0)


def _log_decay(z, w2_ref, gb_ref):
    return _log_sigmoid(jnp.dot(z, w2_ref[...], preferred_element_type=F32) + gb_ref[...]) * (1.0 / GATE_TAU)


def _gla_bwd_kernel(q_ref, k_ref, v_ref, z_ref, w2_ref, gb_ref, tri_ref, ob_ref, st_ref, *, n_tiles, ends):
    t = q_ref.shape[0]
    j = n_tiles - 1 - pl.program_id(0)

    @pl.when(_tile_hits(j + 1, t, ends))
    def _():
        st_ref[...] = jnp.zeros_like(st_ref)

    la = _log_decay(z_ref[...], w2_ref, gb_ref)
    ob_ref[...] = _gla_tile(q_ref[...], k_ref[...], v_ref[...], la, tri_ref[...], st_ref,
                            reverse=True).astype(ob_ref.dtype)


def _gla_fwd_kernel(q_ref, k_ref, v_ref, z_ref, w2_ref, gb_ref, tri_ref, ob_ref, og_ref, gg_ref, pm_ref, h_ref,
                    ng_ref, wgu_ref, wo_ref, l1g_ref, l1b_ref, h1_ref, h1rc_ref, st_ref, *, starts):
    t = q_ref.shape[0]
    i = pl.program_id(0)

    @pl.when(_tile_hits(i, t, starts))
    def _():
        st_ref[...] = jnp.zeros_like(st_ref)

    la = _log_decay(z_ref[...], w2_ref, gb_ref)
    o = _gla_tile(q_ref[...], k_ref[...], v_ref[...], la, tri_ref[...], st_ref, reverse=False)
    o = o + ob_ref[...].astype(F32)
    heads = []
    for h in range(GLA_HEADS):
        oh = o[:, h * GLA_DV:(h + 1) * GLA_DV]
        heads.append(oh * lax.rsqrt(jnp.mean(oh * oh, axis=-1, keepdims=True) + RMS_EPS))
    o = jnp.concatenate(heads, axis=1) * ng_ref[...]
    o = (o * _silu(og_ref[...].astype(F32))).astype(BF16)
    gla_out = jnp.dot(o, wgu_ref[...], preferred_element_type=F32)
    merged = _sigmoid(gg_ref[...].astype(F32)) * gla_out + pm_ref[...].astype(F32)
    mix = jnp.dot(merged.astype(BF16), wo_ref[...], preferred_element_type=F32)
    h1 = _layer_norm(DN_ALPHA * h_ref[...] + mix, l1g_ref[...], l1b_ref[...])
    h1_ref[...] = h1
    _store_row_contiguous(h1rc_ref, h1)


def _store_row_contiguous(ref, x):
    t = x.shape[0]
    bits = pltpu.bitcast(x.astype(BF16).astype(F32), jnp.uint32)
    words = (bits[:, :ROW_WORDS] >> 16) | (bits[:, ROW_WORDS:] & jnp.uint32(0xFFFF0000))
    words = pltpu.bitcast(words, F32)
    for s in range(ROW_TILES):
        ref[pl.ds(s, t, stride=ROW_TILES), :] = words[:, s * LANES:(s + 1) * LANES]


def _load_row_contiguous(ref, t):
    words = jnp.concatenate([ref[pl.ds(s, t, stride=ROW_TILES), :] for s in range(ROW_TILES)], axis=1)
    bits = pltpu.bitcast(words, jnp.uint32)
    lo = pltpu.bitcast(bits << 16, F32)
    hi = pltpu.bitcast(bits & jnp.uint32(0xFFFF0000), F32)
    return jnp.concatenate([lo, hi], axis=1)


def _chunk_tri(t, reverse):
    r = jnp.arange(t)[:, None]
    c = jnp.arange(t)[None, :]
    same = (r // GLA_CHUNK) == (c // GLA_CHUNK)
    return (same & ((c >= r) if reverse else (c <= r))).astype(BF16)


def _gla_bwd(q, k, v, z, w2b_pad, gate_b, ends):
    n = q.shape[0]
    t = TILE_GLA
    nt = n // t
    rev = lambda i: (nt - 1 - i, 0)
    return pl.pallas_call(
        functools.partial(_gla_bwd_kernel, n_tiles=nt, ends=ends),
        grid=(nt,),
        in_specs=[
            pl.BlockSpec((t, GLA_K_W), rev), pl.BlockSpec((t, GLA_K_W), rev), pl.BlockSpec((t, GLA_V_W), rev),
            pl.BlockSpec((t, Z_PAD), rev),
            _const_spec((Z_PAD, GLA_K_W)), _const_spec((1, GLA_K_W)),
            _const_spec((GLA_SCAN_BLOCK, GLA_SCAN_BLOCK)),
        ],
        out_specs=pl.BlockSpec((t, GLA_V_W), rev),
        out_shape=jax.ShapeDtypeStruct((n, GLA_V_W), BF16),
        scratch_shapes=[pltpu.VMEM((GLA_HEADS, GLA_DV, GLA_DK), F32)],
        compiler_params=_params(("arbitrary",)),
        name="gla_bwd",
    )(q, k, v, z, w2b_pad, gate_b, _chunk_tri(GLA_SCAN_BLOCK, True))


def _gla_fwd_mix(q, k, v, z, w2f_pad, gate_b, ob, og, gg, pm, h, norm_g, w_gla_up, w_out, ln1_g, ln1_b, starts):
    n = q.shape[0]
    t = TILE_GLA
    row = lambda i: (i, 0)
    return pl.pallas_call(
        functools.partial(_gla_fwd_kernel, starts=starts),
        grid=(n // t,),
        in_specs=[
            pl.BlockSpec((t, GLA_K_W), row), pl.BlockSpec((t, GLA_K_W), row), pl.BlockSpec((t, GLA_V_W), row),
            pl.BlockSpec((t, Z_PAD), row),
            _const_spec((Z_PAD, GLA_K_W)), _const_spec((1, GLA_K_W)),
            _const_spec((GLA_SCAN_BLOCK, GLA_SCAN_BLOCK)),
            pl.BlockSpec((t, GLA_V_W), row), pl.BlockSpec((t, GLA_V_W), row), pl.BlockSpec((t, D_MODEL), row),
            pl.BlockSpec((t, D_MODEL), row), pl.BlockSpec((t, D_MODEL), row),
            _const_spec((1, GLA_V_W)), _const_spec((GLA_V_W, D_MODEL)), _const_spec((D_MODEL, D_MODEL)),
            _const_spec((1, D_MODEL)), _const_spec((1, D_MODEL)),
        ],
        out_specs=[pl.BlockSpec((t, D_MODEL), row), pl.BlockSpec((t * ROW_TILES, LANES), row)],
        out_shape=[jax.ShapeDtypeStruct((n, D_MODEL), F32), jax.ShapeDtypeStruct((n * ROW_TILES, LANES), F32)],
        scratch_shapes=[pltpu.VMEM((GLA_HEADS, GLA_DV, GLA_DK), F32)],
        compiler_params=_params(("arbitrary",)),
        name="gla_fwd_mix",
    )(q, k, v, z, w2f_pad, gate_b, _chunk_tri(GLA_SCAN_BLOCK, False), ob, og, gg, pm, h, norm_g, w_gla_up, w_out,
      ln1_g, ln1_b)


def _router_kernel(h_ref, wrh_ref, wrl_ref, rb_ref, tri_ref, idx_ref, wts_ref, rank_ref, cnt_ref, base_ref):
    t = h_ref.shape[0]

    @pl.when(pl.program_id(0) == 0)
    def _():
        base_ref[...] = jnp.zeros_like(base_ref)

    h = h_ref[...]
    hh = h.astype(BF16)
    hl = (h - hh.astype(F32)).astype(BF16)
    nt = (((1,), (1,)), ((), ()))
    logits = (lax.dot_general(wrh_ref[...], hh, nt, preferred_element_type=F32)
              + lax.dot_general(wrl_ref[...], hh, nt, preferred_element_type=F32)
              + lax.dot_general(wrh_ref[...], hl, nt, preferred_element_type=F32))
    scores = _sigmoid(logits)
    sel = scores + rb_ref[:, 0:1]
    neg = jnp.float32(-jnp.inf)

    sub = lax.broadcasted_iota(jnp.int32, (EXPERTS_PER_GROUP, t), 0)
    gscore = []
    for g in range(N_GROUPS):
        sg = sel[g * EXPERTS_PER_GROUP:(g + 1) * EXPERTS_PER_GROUP]
        m1 = jnp.max(sg, axis=0, keepdims=True)
        first = jnp.min(jnp.where(sg == m1, sub, EXPERTS_PER_GROUP), axis=0, keepdims=True)
        m2 = jnp.max(jnp.where(sub == first, neg, sg), axis=0, keepdims=True)
        gscore.append(m1 + m2)
    keep = []
    for g in range(N_GROUPS):
        beaten = jnp.zeros((1, t), jnp.int32)
        for g2 in range(N_GROUPS):
            if g2 == g:
                continue
            wins = (gscore[g2] >= gscore[g]) if g2 < g else (gscore[g2] > gscore[g])
            beaten = beaten + wins.astype(jnp.int32)
        keep.append(beaten < TOPK_GROUPS)
    slot = []
    ahead = jnp.zeros((1, t), jnp.int32)
    for g in range(N_GROUPS):
        slot.append([jnp.logical_and(keep[g], ahead == j) for j in range(TOPK_GROUPS)])
        ahead = ahead + keep[g].astype(jnp.int32)
    groups_of = lambda j: range(j, j + N_GROUPS - TOPK_GROUPS + 1)
    slots_of = lambda g: range(max(0, g - (N_GROUPS - TOPK_GROUPS)), min(TOPK_GROUPS - 1, g) + 1)
    rows_of = lambda x, g: x[g * EXPERTS_PER_GROUP:(g + 1) * EXPERTS_PER_GROUP]

    def compact(x, fill):
        parts = []
        for j in range(TOPK_GROUPS):
            acc = jnp.full((EXPERTS_PER_GROUP, t), fill, x.dtype)
            for g in groups_of(j):
                acc = jnp.where(slot[g][j], rows_of(x, g), acc)
            parts.append(acc)
        return jnp.concatenate(parts, axis=0)

    cand = compact(sel, neg)
    kept_scores = compact(scores, 0.0)
    gid = []
    for j in range(TOPK_GROUPS):
        acc = jnp.zeros((1, t), jnp.int32)
        for g in groups_of(j):
            acc = jnp.where(slot[g][j], g, acc)
        gid.append(acc * EXPERTS_PER_GROUP + sub)
    cid = jnp.concatenate(gid, axis=0)

    picked = []
    wsum = jnp.zeros((1, t), F32)
    for kk in range(TOP_K):
        m = jnp.max(cand, axis=0, keepdims=True)
        e_k = jnp.min(jnp.where(cand == m, cid, N_EXPERTS), axis=0, keepdims=True)
        hit = cid == e_k
        w_k = jnp.sum(jnp.where(hit, kept_scores, 0.0), axis=0, keepdims=True)
        cand = jnp.where(hit, neg, cand)
        wsum = wsum + w_k
        picked.append((e_k, w_k))
        idx_ref[kk:kk + 1, :] = e_k

    chosen = jnp.where(cand == neg, 1.0, 0.0)
    full = []
    for g in range(N_GROUPS):
        acc = jnp.zeros((EXPERTS_PER_GROUP, t), F32)
        for j in slots_of(g):
            acc = jnp.where(slot[g][j], rows_of(chosen, j), acc)
        full.append(acc)
    a16 = jnp.concatenate(full, axis=0).astype(BF16)

    before = jnp.dot(a16, tri_ref[...], preferred_element_type=F32)
    base = base_ref[...]
    before = compact(before + jnp.concatenate([base] * (t // 128), axis=1), 0.0)
    inv = ROUTED_SCALE / wsum
    for kk, (e_k, w_k) in enumerate(picked):
        wts_ref[kk:kk + 1, :] = w_k * inv
        rank_ref[kk:kk + 1, :] = jnp.sum(jnp.where(cid == e_k, before, 0.0), axis=0, keepdims=True).astype(jnp.int32)
    total = base + jnp.dot(a16, jnp.ones((t, 128), BF16), preferred_element_type=F32)
    base_ref[...] = total
    cnt_ref[...] = total


def _router(h1, wr_hi, wr_lo, rbias):
    n = h1.shape[0]
    t = TILE_ROUTER
    r = jnp.arange(t)
    tri = (r[:, None] < r[None, :]).astype(BF16)
    col = lambda i: (0, i)
    idx, wts, rank, cnt = pl.pallas_call(
        _router_kernel,
        grid=(n // t,),
        in_specs=[
            pl.BlockSpec((t, D_MODEL), lambda i: (i, 0)),
            _const_spec((N_EXPERTS, D_MODEL)), _const_spec((N_EXPERTS, D_MODEL)),
            _const_spec((N_EXPERTS, 128)), _const_spec((t, t)),
        ],
        out_specs=[pl.BlockSpec((TOP_K, t), col), pl.BlockSpec((TOP_K, t), col), pl.BlockSpec((TOP_K, t), col),
                   _const_spec((N_EXPERTS, 128))],
        out_shape=[jax.ShapeDtypeStruct((TOP_K, n), jnp.int32), jax.ShapeDtypeStruct((TOP_K, n), F32),
                   jax.ShapeDtypeStruct((TOP_K, n), jnp.int32), jax.ShapeDtypeStruct((N_EXPERTS, 128), F32)],
        scratch_shapes=[pltpu.VMEM((N_EXPERTS, 128), F32)],
        compiler_params=_params(("arbitrary",)),
        name="router",
    )(h1, wr_hi, wr_lo, rbias, tri)
    return idx, wts, rank, cnt[:, 0].astype(jnp.int32)


def _slot_kernel(ps_ref, idx_ref, rank_ref, dest_ref):
    idx = idx_ref[...]

    def add_expert(e, acc):
        return acc + jnp.where(idx == e, ps_ref[e], 0)

    dest_ref[...] = lax.fori_loop(0, N_EXPERTS, add_expert, rank_ref[...], unroll=8)


def _slots(pstart, idx, rank):
    n = idx.shape[1]
    t = TILE_SLOTS
    col = lambda i, ps: (0, i)
    return pl.pallas_call(
        _slot_kernel,
        grid_spec=pltpu.PrefetchScalarGridSpec(
            num_scalar_prefetch=1,
            grid=(n // t,),
            in_specs=[pl.BlockSpec((TOP_K, t), col), pl.BlockSpec((TOP_K, t), col)],
            out_specs=pl.BlockSpec((TOP_K, t), col),
        ),
        out_shape=jax.ShapeDtypeStruct((TOP_K, n), jnp.int32),
        compiler_params=_params(("parallel",)),
        name="slots",
    )(pstart, idx, rank)


def _expert_kernel(first_ref, count_ref, used_ref, wg_ref, wu_ref, wd_ref, x_hbm, y_hbm,
                   wgu_s, wd_s, xbuf, ybuf, xsem, ysem):
    e = pl.program_id(0)
    rows = MOE_BLOCK * ROW_TILES
    used = used_ref[0]

    def x_copy(b, slot):
        return pltpu.make_async_copy(x_hbm.at[pl.ds(b * rows, rows)], xbuf.at[slot], xsem.at[slot])

    def y_copy(b, slot):
        return pltpu.make_async_copy(ybuf.at[slot], y_hbm.at[pl.ds(b * rows, rows)], ysem.at[slot])

    @pl.when(e == 0)
    def _():
        for b in range(X_AHEAD):
            @pl.when(b < used)
            def _():
                x_copy(b, b).start()

    @pl.when(count_ref[e] > 0)
    def _():
        wgu_s[:, :EXPERT_HIDDEN] = wg_ref[0].astype(BF16)
        wgu_s[:, EXPERT_HIDDEN:] = wu_ref[0].astype(BF16)
        wd_s[...] = wd_ref[0].astype(BF16)

    def run_blocks(b0, g):
        for i in range(g):
            ahead = b0 + i + X_AHEAD

            @pl.when(ahead < used)
            def _():
                x_copy(ahead, ahead % X_SLOTS).start()

        for i in range(g):
            done = b0 + i - Y_SLOTS

            @pl.when(done >= 0)
            def _():
                y_copy(done, done % Y_SLOTS).wait()

        x = []
        for i in range(g):
            xs = (b0 + i) % X_SLOTS
            x_copy(b0 + i, xs).wait()
            x.append(_load_row_contiguous(xbuf.at[xs], MOE_BLOCK).astype(BF16))
        gu = [jnp.dot(xi, wgu_s[...], preferred_element_type=F32) for xi in x]
        hid = [(_silu(gi[:, :EXPERT_HIDDEN]) * gi[:, EXPERT_HIDDEN:]).astype(BF16) for gi in gu]
        y = [jnp.dot(hi, wd_s[...], preferred_element_type=F32) for hi in hid]
        for i in range(g):
            ys = (b0 + i) % Y_SLOTS
            _store_row_contiguous(ybuf.at[ys], y[i])
            y_copy(b0 + i, ys).start()

    first = first_ref[e]
    count = count_ref[e]
    n_groups = lax.div(count, EXPERT_GROUP)

    def group(j, carry):
        run_blocks(first + j * EXPERT_GROUP, EXPERT_GROUP)
        return carry

    lax.fori_loop(0, n_groups, group, 0)
    for left in range(1, EXPERT_GROUP):
        @pl.when(count - n_groups * EXPERT_GROUP == left)
        def _():
            run_blocks(first + n_groups * EXPERT_GROUP, left)

    @pl.when(e == pl.num_programs(0) - 1)
    def _():
        for back in range(1, Y_SLOTS + 1):
            @pl.when(used - back >= 0)
            def _():
                y_copy(used - back, (used - back) % Y_SLOTS).wait()


def _experts(first_block, block_count, n_used, xs, w_gate, w_up, w_down):
    rows = MOE_BLOCK * ROW_TILES
    n_exp = w_gate.shape[0]
    any_spec = pl.BlockSpec(memory_space=pl.ANY)
    return pl.pallas_call(
        _expert_kernel,
        grid_spec=pltpu.PrefetchScalarGridSpec(
            num_scalar_prefetch=3,
            grid=(n_exp,),
            in_specs=[
                pl.BlockSpec((1, D_MODEL, EXPERT_HIDDEN), lambda e, *_: (e, 0, 0)),
                pl.BlockSpec((1, D_MODEL, EXPERT_HIDDEN), lambda e, *_: (e, 0, 0)),
                pl.BlockSpec((1, EXPERT_HIDDEN, D_MODEL), lambda e, *_: (e, 0, 0)),
                any_spec,
            ],
            out_specs=any_spec,
            scratch_shapes=[
                pltpu.VMEM((D_MODEL, 2 * EXPERT_HIDDEN), BF16), pltpu.VMEM((EXPERT_HIDDEN, D_MODEL), BF16),
                pltpu.VMEM((X_SLOTS, rows, LANES), F32), pltpu.VMEM((Y_SLOTS, rows, LANES), F32),
                pltpu.SemaphoreType.DMA((X_SLOTS,)), pltpu.SemaphoreType.DMA((Y_SLOTS,)),
            ],
        ),
        out_shape=jax.ShapeDtypeStruct(xs.shape, F32),
        compiler_params=_params(("arbitrary",)),
        name="experts",
    )(first_block, block_count, n_used, w_gate, w_up, w_down, xs)


def _sc_mesh():
    return plsc.VectorSubcoreMesh(core_axis_name="c", subcore_axis_name="s")


def _sc_dispatch(x3, dest_w, n_rows):
    n = x3.shape[0]

    @functools.partial(pl.kernel, out_type=jax.ShapeDtypeStruct((n_rows, ROW_TILES, LANES), F32), mesh=_sc_mesh(),
                       name="sc_dispatch")
    def run(x_hbm, d_hbm, o_hbm):
        def body(x_vmem, d_vmem):
            for kk in range(TOP_K):
                pltpu.sync_copy(x_vmem, o_hbm.at[d_vmem.at[0].at[kk]])

        pltpu.emit_pipeline(
            body,
            grid=(n // SC_WINDOW,),
            in_specs=[pl.BlockSpec((SC_WINDOW, ROW_TILES, LANES), lambda i: (i, 0, 0)),
                      pl.BlockSpec((1, TOP_K, SC_WINDOW), lambda i: (i, 0, 0))],
            out_specs=[],
            core_axis_name=("c", "s"),
            dimension_semantics=(pltpu.PARALLEL,),
        )(x_hbm, d_hbm)

    return run(x3, dest_w)


def _sc_gather(y3, dest_w):
    nw = dest_w.shape[0]
    n = nw * SC_WINDOW

    @functools.partial(pl.kernel, out_type=jax.ShapeDtypeStruct((TOP_K, n, ROW_TILES, LANES), F32), mesh=_sc_mesh(),
                       name="sc_gather")
    def run(y_hbm, d_hbm, o_hbm):
        def body(d_vmem, g_vmem):
            pltpu.sync_copy(y_hbm.at[d_vmem.at[0].at[0]], g_vmem.at[0])

        pltpu.emit_pipeline(
            body,
            grid=(nw, TOP_K),
            in_specs=[pl.BlockSpec((1, 1, SC_WINDOW), lambda i, k: (i, k, 0))],
            out_specs=[pl.BlockSpec((1, SC_WINDOW, ROW_TILES, LANES), lambda i, k: (k, i, 0, 0))],
            core_axis_name=("c", "s"),
            dimension_semantics=(pltpu.PARALLEL, pltpu.PARALLEL),
        )(d_hbm, o_hbm)

    return run(y3, dest_w)


def _final_kernel(h_ref, g_ref, w_ref, wgu_ref, wd_ref, lg_ref, lb_ref, y_ref):
    t = h_ref.shape[0]
    h = h_ref[...]
    gu = jnp.dot(h.astype(BF16), wgu_ref[...], preferred_element_type=F32)
    hid = (_silu(gu[:, :SHARED_HIDDEN]) * gu[:, SHARED_HIDDEN:]).astype(BF16)
    ffn = jnp.dot(hid, wd_ref[...], preferred_element_type=F32)
    w = w_ref[...]
    for kk in range(TOP_K):
        ffn = ffn + _load_row_contiguous(g_ref.at[kk], t) * w[:, kk:kk + 1]
    y_ref[...] = _layer_norm(DN_ALPHA * h + ffn, lg_ref[...], lb_ref[...])


def _final(h1, gathered, wts_t, w_sh_gu, w_sh_down, ln_g, ln_b, first_token):
    m = gathered.shape[1] // ROW_TILES
    t = TILE_FINAL
    off = first_token // t
    part = lambda i: (i + off, 0)
    return pl.pallas_call(
        _final_kernel,
        grid=(m // t,),
        in_specs=[
            pl.BlockSpec((t, D_MODEL), part),
            pl.BlockSpec((TOP_K, t * ROW_TILES, LANES), lambda i: (0, i, 0)),
            pl.BlockSpec((t, TOP_K), part),
            _const_spec((D_MODEL, 2 * SHARED_HIDDEN)), _const_spec((SHARED_HIDDEN, D_MODEL)),
            _const_spec((1, D_MODEL)), _const_spec((1, D_MODEL)),
        ],
        out_specs=pl.BlockSpec((t, D_MODEL), lambda i: (i, 0)),
        out_shape=jax.ShapeDtypeStruct((m, D_MODEL), F32),
        compiler_params=_params(("parallel",)),
        name="final",
    )(h1, gathered, wts_t, w_sh_gu, w_sh_down, ln_g, ln_b)


def _block_diag(w):
    g, a, b = w.shape
    out = jnp.zeros((g * a, g * b), w.dtype)
    for i in range(g):
        out = out.at[i * a:(i + 1) * a, i * b:(i + 1) * b].set(w[i])
    return out


def kernel(x_prompt, x_sample, ln0_g, ln0_b, w_in, pool_w, pool_scale, w_pool_up, gate_w2_fwd, gate_b_fwd,
           gate_w2_bwd, gate_b_bwd, gla_norm_g, w_gla_up, w_out, ln1_g, ln1_b, w_router, router_bias,
           w_exp_gate, w_exp_up, w_exp_down, w_sh_gate, w_sh_up, w_sh_down, ln2_g, ln2_b):
    bp, lp, d = x_prompt.shape
    bs, ls, _ = x_sample.shape
    n_p, n_s = bp * lp, bs * ls
    n = n_p + n_s
    starts = tuple(b * lp for b in range(bp)) + tuple(n_p + b * ls for b in range(bs))
    ends = tuple((b + 1) * lp for b in range(bp)) + tuple(n_p + (b + 1) * ls for b in range(bs))
    vec = lambda a: a.reshape(1, -1).astype(F32)

    w = w_in[0]
    c_u, c_q, c_k, c_v, c_og = 0, 512, 1024, 1536, 2560
    c_zf, c_zb, c_gp, c_gg = 3584, 3600, 3616, 4640
    w_main = jnp.concatenate([w[:, c_u:c_zf], w[:, c_gp:]], axis=1).astype(BF16)
    w_z = jnp.zeros((d, Z_PAD), F32).at[:, :2 * GATE_RANK].set(w[:, c_zf:c_gp]).astype(BF16)
    w2f = jnp.zeros((Z_PAD, GLA_K_W), F32).at[:GATE_RANK].set(gate_w2_fwd[0]).astype(BF16)
    w2b = jnp.zeros((Z_PAD, GLA_K_W), F32).at[GATE_RANK:2 * GATE_RANK].set(gate_w2_bwd[0]).astype(BF16)

    h, u, q, k, v, og, gp, gg, z = _inproj(x_prompt.reshape(n_p, d), x_sample.reshape(n_s, d),
                                           vec(ln0_g), vec(ln0_b), w_main, w_z)
    pm = _pool(u, gp, _block_diag(pool_w[0]).astype(BF16), vec(pool_scale[0]), w_pool_up[0].astype(BF16),
               starts, ends)
    ob = _gla_bwd(q, k, v, z, w2b, vec(gate_b_bwd[0]), ends)
    h1, h1_rc = _gla_fwd_mix(q, k, v, z, w2f, vec(gate_b_fwd[0]), ob, og, gg, pm, h, vec(gla_norm_g[0]),
                             w_gla_up[0].astype(BF16), w_out[0].astype(BF16), vec(ln1_g[0]), vec(ln1_b[0]), starts)

    wr_t = w_router[0].T
    wr_hi = wr_t.astype(BF16)
    wr_lo = (wr_t - wr_hi.astype(F32)).astype(BF16)
    rbias = jnp.broadcast_to(router_bias[0].astype(F32)[:, None], (N_EXPERTS, 128))
    idx, wts, rank, counts = _router(h1, wr_hi, wr_lo, rbias)

    padded = (counts + MOE_BLOCK - 1) // MOE_BLOCK * MOE_BLOCK
    pend = jnp.cumsum(padded)
    pstart = pend - padded
    n_rows = n * TOP_K + N_EXPERTS * MOE_BLOCK
    first_block = (pstart // MOE_BLOCK).astype(jnp.int32)
    block_count = (padded // MOE_BLOCK).astype(jnp.int32)
    n_used = (pend[-1:] // MOE_BLOCK).astype(jnp.int32)
    dest = _slots(pstart.astype(jnp.int32), idx, rank)

    dest_w = dest.reshape(TOP_K, n // SC_WINDOW, SC_WINDOW).transpose(1, 0, 2)
    xs = _sc_dispatch(h1_rc.reshape(n, ROW_TILES, LANES), dest_w, n_rows)
    ys = _experts(first_block, block_count, n_used, xs.reshape(n_rows * ROW_TILES, LANES),
                  w_exp_gate[0], w_exp_up[0], w_exp_down[0])
    ys = ys.reshape(n_rows, ROW_TILES, LANES)
    w_sh_gu = jnp.concatenate([w_sh_gate[0], w_sh_up[0]], axis=1).astype(BF16)
    w_sh_d = w_sh_down[0].astype(BF16)
    wts_t = wts.T
    outs = []
    for first, m in ((0, n_p), (n_p, n_s)):
        windows = dest_w[first // SC_WINDOW:(first + m) // SC_WINDOW]
        gathered = _sc_gather(ys, windows).reshape(TOP_K, m * ROW_TILES, LANES)
        outs.append(_final(h1, gathered, wts_t, w_sh_gu, w_sh_d, vec(ln2_g[0]), vec(ln2_b[0]), first))
    return outs[0].reshape(bp, lp, d), outs[1].reshape(bs, ls, d)
```

```python
import functools

import jax
import jax.numpy as jnp
from jax import lax
from jax.experimental import pallas as pl
from jax.experimental.pallas import tpu as pltpu
from jax.experimental.pallas import tpu_sc as plsc

F32 = jnp.float32
BF16 = jnp.bfloat16

D_MODEL = 1024
POOL_GROUPS = 4
POOL_GROUP_W = 128
POOL_W = POOL_GROUPS * POOL_GROUP_W
POOL_WINDOWS = (2, 4, 8, 16)
POOL_HALO = 16
GLA_HEADS = 4
GLA_K_W = 512
GLA_V_W = 1024
GLA_DK = GLA_K_W // GLA_HEADS
GLA_DV = GLA_V_W // GLA_HEADS
GATE_RANK = 16
GATE_TAU = 16.0
GLA_CHUNK = 64
Z_PAD = 128
N_EXPERTS = 256
TOP_K = 8
N_GROUPS = 8
TOPK_GROUPS = 4
EXPERTS_PER_GROUP = N_EXPERTS // N_GROUPS
EXPERT_HIDDEN = 256
SHARED_HIDDEN = 256
ROUTED_SCALE = 2.5
DEPTH = 1
DN_ALPHA = (2 * DEPTH) ** 0.25
LN_EPS = 1e-5
RMS_EPS = 1e-6

VMEM_LIMIT_BYTES = 56 * 1024 * 1024

TILE_INPROJ = 512
TILE_POOL = 512
TILE_GLA = 512
GLA_SCAN_BLOCK = 256
TILE_ROUTER = 512
TILE_FINAL = 512
TILE_SLOTS = 2048
MOE_BLOCK = 256
EXPERT_GROUP = 4
X_AHEAD = 5
X_SLOTS = X_AHEAD + EXPERT_GROUP
Y_SLOTS = 3 * EXPERT_GROUP
LANES = 128
ROW_WORDS = D_MODEL // 2
ROW_TILES = ROW_WORDS // LANES
SC_WINDOW = 64


def _params(semantics):
    return pltpu.CompilerParams(dimension_semantics=semantics, vmem_limit_bytes=VMEM_LIMIT_BYTES)


def _const_spec(shape):
    nd = len(shape)
    return pl.BlockSpec(shape, lambda *_: (0,) * nd)


def _layer_norm(x, g, b):
    mu = jnp.mean(x, axis=-1, keepdims=True)
    xc = x - mu
    var = jnp.mean(xc * xc, axis=-1, keepdims=True)
    return xc * lax.rsqrt(var + LN_EPS) * g + b


def _sigmoid(x):
    return 1.0 / (1.0 + jnp.exp(-x))


def _log_sigmoid(x):
    return jnp.minimum(x, 0.0) - jnp.log(1.0 + jnp.exp(-jnp.abs(x)))


def _silu(x):
    return x * _sigmoid(x)


def _tile_hits(i, tile, offsets):
    hit = i < 0
    for off in offsets:
        if off % tile == 0:
            hit = jnp.logical_or(hit, i == off // tile)
    return hit


_MAIN_SPLITS = (POOL_W, GLA_K_W, GLA_K_W, GLA_V_W, GLA_V_W, D_MODEL, D_MODEL)


def _inproj_kernel(xp_ref, xs_ref, g_ref, b_ref, w_ref, wz_ref,
                   h_ref, u_ref, q_ref, k_ref, v_ref, og_ref, gp_ref, gg_ref, z_ref, *, n_prompt_tiles):
    i = pl.program_id(0)
    x = jnp.where(i < n_prompt_tiles, xp_ref[...], xs_ref[...])
    h = _layer_norm(x, g_ref[...], b_ref[...])
    h_ref[...] = h
    hb = h.astype(BF16)
    off = 0
    for ref, width in zip((u_ref, q_ref, k_ref, v_ref, og_ref, gp_ref, gg_ref), _MAIN_SPLITS):
        ref[...] = jnp.dot(hb, w_ref[:, off:off + width], preferred_element_type=F32).astype(BF16)
        off += width
    z_ref[...] = jnp.dot(hb, wz_ref[...], preferred_element_type=F32).astype(BF16)


def _inproj(xp, xs, ln_g, ln_b, w_main, w_z):
    n_p, n_s = xp.shape[0], xs.shape[0]
    n = n_p + n_s
    t = TILE_INPROJ
    npt, nst = n_p // t, n_s // t
    row = lambda i: (i, 0)
    widths = _MAIN_SPLITS + (Z_PAD,)
    out_shape = [jax.ShapeDtypeStruct((n, D_MODEL), F32)] + [jax.ShapeDtypeStruct((n, w), BF16) for w in widths]
    out_specs = [pl.BlockSpec((t, D_MODEL), row)] + [pl.BlockSpec((t, w), row) for w in widths]
    return pl.pallas_call(
        functools.partial(_inproj_kernel, n_prompt_tiles=npt),
        grid=(npt + nst,),
        in_specs=[
            pl.BlockSpec((t, D_MODEL), lambda i: (jnp.minimum(i, npt - 1), 0)),
            pl.BlockSpec((t, D_MODEL), lambda i: (jnp.maximum(i - npt, 0), 0)),
            _const_spec((1, D_MODEL)), _const_spec((1, D_MODEL)),
            _const_spec(w_main.shape), _const_spec(w_z.shape),
        ],
        out_specs=out_specs,
        out_shape=out_shape,
        compiler_params=_params(("parallel",)),
        name="inproj",
    )(xp, xs, ln_g, ln_b, w_main, w_z)


def _pool_kernel(u_ref, up_ref, un_ref, gp_ref, band_ref, pw_ref, sc_ref, wup_ref, pm_ref, *, starts, ends):
    t = u_ref.shape[0]
    i = pl.program_id(0)
    is_start = _tile_hits(i, t, starts)
    is_end = _tile_hits(i + 1, t, ends)
    cur = u_ref[...]
    prev = jnp.where(is_start, jnp.zeros_like(up_ref[...]), up_ref[...])
    nxt = jnp.where(is_end, jnp.zeros_like(un_ref[...]), un_ref[...])
    ext = jnp.concatenate([prev, cur, nxt], axis=0)
    r = lax.broadcasted_iota(jnp.int32, (t, POOL_GROUP_W), 0).astype(F32)
    fs = is_start.astype(F32)
    fe = is_end.astype(F32)
    parts = []
    for gi, w in enumerate(POOL_WINDOWS):
        sl = slice(gi * POOL_GROUP_W, (gi + 1) * POOL_GROUP_W)
        s = jnp.dot(band_ref[gi], ext[:, sl], preferred_element_type=F32)
        cnt = w - fs * jnp.maximum(w // 2 - r, 0.0) - fe * jnp.maximum(r + (w // 2 - t), 0.0)
        parts.append(s / cnt - cur[:, sl].astype(F32))
    p = jnp.concatenate(parts, axis=1).astype(BF16)
    p = (jnp.dot(p, pw_ref[...], preferred_element_type=F32) * sc_ref[...]).astype(BF16)
    pool_out = jnp.dot(p, wup_ref[...], preferred_element_type=F32)
    pm_ref[...] = (_sigmoid(gp_ref[...].astype(F32)) * pool_out).astype(BF16)


def _pool_bands(t):
    r = jnp.arange(t)[:, None]
    c = jnp.arange(t + 2 * POOL_HALO)[None, :] - POOL_HALO
    return jnp.stack([((c >= r - w // 2) & (c <= r + w // 2 - 1)).astype(BF16) for w in POOL_WINDOWS])


def _pool(u, gp, pool_w_bd, pool_scale, w_pool_up, starts, ends):
    n = u.shape[0]
    t = TILE_POOL
    hb = t // POOL_HALO
    last = n // POOL_HALO - 1
    return pl.pallas_call(
        functools.partial(_pool_kernel, starts=starts, ends=ends),
        grid=(n // t,),
        in_specs=[
            pl.BlockSpec((t, POOL_W), lambda i: (i, 0)),
            pl.BlockSpec((POOL_HALO, POOL_W), lambda i: (jnp.maximum(i * hb - 1, 0), 0)),
            pl.BlockSpec((POOL_HALO, POOL_W), lambda i: (jnp.minimum((i + 1) * hb, last), 0)),
            pl.BlockSpec((t, D_MODEL), lambda i: (i, 0)),
            _const_spec((POOL_GROUPS, t, t + 2 * POOL_HALO)),
            _const_spec((POOL_W, POOL_W)), _const_spec((1, POOL_W)), _const_spec((POOL_W, D_MODEL)),
        ],
        out_specs=pl.BlockSpec((t, D_MODEL), lambda i: (i, 0)),
        out_shape=jax.ShapeDtypeStruct((n, D_MODEL), BF16),
        compiler_params=_params(("parallel",)),
        name="pool",
    )(u, u, u, gp, _pool_bands(t), pool_w_bd, pool_scale, w_pool_up)


def _gla_tile(q, k, v, la, tri, st_ref, *, reverse):
    t = q.shape[0]
    c_sz = GLA_CHUNK
    hi = la.astype(BF16)
    r1 = la - hi.astype(F32)
    mid = r1.astype(BF16)
    lo = (r1 - mid.astype(F32)).astype(BF16)
    scan = lambda rs: (jnp.dot(tri, hi[rs], preferred_element_type=F32)
                       + jnp.dot(tri, mid[rs], preferred_element_type=F32)
                       + jnp.dot(tri, lo[rs], preferred_element_type=F32))
    sb = GLA_SCAN_BLOCK
    b = jnp.concatenate([scan(slice(r0, r0 + sb)) for r0 in range(0, t, sb)], axis=0)
    kf = k.astype(F32)
    qd = (q.astype(F32) * jnp.exp(b) * (GLA_DK ** -0.5)).astype(BF16)
    kd = (kf * jnp.exp(-b)).astype(BF16)
    row = lax.broadcasted_iota(jnp.int32, (c_sz, c_sz), 0)
    col = lax.broadcasted_iota(jnp.int32, (c_sz, c_sz), 1)
    mask = (col >= row) if reverse else (col <= row)
    n_chunks = t // c_sz
    order = list(range(n_chunks - 1, -1, -1) if reverse else range(n_chunks))
    nt = (((1,), (1,)), ((), ()))
    tn = (((0,), (0,)), ((), ()))
    rows = [slice(c * c_sz, (c + 1) * c_sz) for c in range(n_chunks)]
    lanes = [slice(h * GLA_DK, (h + 1) * GLA_DK) for h in range(GLA_HEADS)]
    vcols = [slice(h * GLA_DV, (h + 1) * GLA_DV) for h in range(GLA_HEADS)]

    att, kv, dec = {}, {}, {}
    for c in order:
        b_c = b[rows[c]]
        b_edge = b_c[0:1] if reverse else b_c[c_sz - 1:c_sz]
        k2 = (kf[rows[c]] * jnp.exp(b_edge - b_c)).astype(BF16)
        dec[c] = jnp.exp(b_edge)
        for h in range(GLA_HEADS):
            att[c, h] = lax.dot_general(qd[rows[c], lanes[h]], kd[rows[c], lanes[h]], nt, preferred_element_type=F32)
            kv[c, h] = lax.dot_general(v[rows[c], vcols[h]], k2[:, lanes[h]], tn, preferred_element_type=F32)

    state_before = {}
    for h in range(GLA_HEADS):
        st = st_ref[h]
        for c in order:
            state_before[c, h] = st.astype(BF16)
            st = st * dec[c][:, lanes[h]] + kv[c, h]
        st_ref[h] = st

    outs = []
    for c in range(n_chunks):
        heads = []
        for h in range(GLA_HEADS):
            a = jnp.where(mask, att[c, h], 0.0).astype(BF16)
            o = jnp.dot(a, v[rows[c], vcols[h]], preferred_element_type=F32)
            heads.append(o + lax.dot_general(qd[rows[c], lanes[h]], state_before[c, h], nt,
                                             preferred_element_type=F32))
        outs.append(jnp.concatenate(heads, axis=1))
    return jnp.concatenate(outs, axis=0)


def _log_decay(z, w2_ref, gb_ref):
    return _log_sigmoid(jnp.dot(z, w2_ref[...], preferred_element_type=F32) + gb_ref[...]) * (1.0 / GATE_TAU)


def _gla_bwd_kernel(q_ref, k_ref, v_ref, z_ref, w2_ref, gb_ref, tri_ref, ob_ref, st_ref, *, n_tiles, ends):
    t = q_ref.shape[0]
    j = n_tiles - 1 - pl.program_id(0)

    @pl.when(_tile_hits(j + 1, t, ends))
    def _():
        st_ref[...] = jnp.zeros_like(st_ref)

    la = _log_decay(z_ref[...], w2_ref, gb_ref)
    ob_ref[...] = _gla_tile(q_ref[...], k_ref[...], v_ref[...], la, tri_ref[...], st_ref, reverse=True)


def _gla_fwd_kernel(q_ref, k_ref, v_ref, z_ref, w2_ref, gb_ref, tri_ref, ob_ref, og_ref, gg_ref, pm_ref, h_ref,
                    ng_ref, wgu_ref, wo_ref, l1g_ref, l1b_ref, h1_ref, h1rc_ref, st_ref, *, starts):
    t = q_ref.shape[0]
    i = pl.program_id(0)

    @pl.when(_tile_hits(i, t, starts))
    def _():
        st_ref[...] = jnp.zeros_like(st_ref)

    la = _log_decay(z_ref[...], w2_ref, gb_ref)
    o = _gla_tile(q_ref[...], k_ref[...], v_ref[...], la, tri_ref[...], st_ref, reverse=False) + ob_ref[...]
    heads = []
    for h in range(GLA_HEADS):
        oh = o[:, h * GLA_DV:(h + 1) * GLA_DV]
        heads.append(oh * lax.rsqrt(jnp.mean(oh * oh, axis=-1, keepdims=True) + RMS_EPS))
    o = jnp.concatenate(heads, axis=1) * ng_ref[...]
    o = (o * _silu(og_ref[...].astype(F32))).astype(BF16)
    gla_out = jnp.dot(o, wgu_ref[...], preferred_element_type=F32)
    merged = _sigmoid(gg_ref[...].astype(F32)) * gla_out + pm_ref[...].astype(F32)
    mix = jnp.dot(merged.astype(BF16), wo_ref[...], preferred_element_type=F32)
    h1 = _layer_norm(DN_ALPHA * h_ref[...] + mix, l1g_ref[...], l1b_ref[...])
    h1_ref[...] = h1
    _store_row_contiguous(h1rc_ref, h1)


def _store_row_contiguous(ref, x):
    t = x.shape[0]
    bits = pltpu.bitcast(x.astype(BF16).astype(F32), jnp.uint32)
    words = (bits[:, :ROW_WORDS] >> 16) | (bits[:, ROW_WORDS:] & jnp.uint32(0xFFFF0000))
    words = pltpu.bitcast(words, F32)
    for s in range(ROW_TILES):
        ref[pl.ds(s, t, stride=ROW_TILES), :] = words[:, s * LANES:(s + 1) * LANES]


def _load_row_contiguous(ref, t):
    words = jnp.concatenate([ref[pl.ds(s, t, stride=ROW_TILES), :] for s in range(ROW_TILES)], axis=1)
    bits = pltpu.bitcast(words, jnp.uint32)
    lo = pltpu.bitcast(bits << 16, F32)
    hi = pltpu.bitcast(bits & jnp.uint32(0xFFFF0000), F32)
    return jnp.concatenate([lo, hi], axis=1)


def _chunk_tri(t, reverse):
    r = jnp.arange(t)[:, None]
    c = jnp.arange(t)[None, :]
    same = (r // GLA_CHUNK) == (c // GLA_CHUNK)
    return (same & ((c >= r) if reverse else (c <= r))).astype(BF16)


def _gla_bwd(q, k, v, z, w2b_pad, gate_b, ends):
    n = q.shape[0]
    t = TILE_GLA
    nt = n // t
    rev = lambda i: (nt - 1 - i, 0)
    return pl.pallas_call(
        functools.partial(_gla_bwd_kernel, n_tiles=nt, ends=ends),
        grid=(nt,),
        in_specs=[
            pl.BlockSpec((t, GLA_K_W), rev), pl.BlockSpec((t, GLA_K_W), rev), pl.BlockSpec((t, GLA_V_W), rev),
            pl.BlockSpec((t, Z_PAD), rev),
            _const_spec((Z_PAD, GLA_K_W)), _const_spec((1, GLA_K_W)),
            _const_spec((GLA_SCAN_BLOCK, GLA_SCAN_BLOCK)),
        ],
        out_specs=pl.BlockSpec((t, GLA_V_W), rev),
        out_shape=jax.ShapeDtypeStruct((n, GLA_V_W), F32),
        scratch_shapes=[pltpu.VMEM((GLA_HEADS, GLA_DV, GLA_DK), F32)],
        compiler_params=_params(("arbitrary",)),
        name="gla_bwd",
    )(q, k, v, z, w2b_pad, gate_b, _chunk_tri(GLA_SCAN_BLOCK, True))


def _gla_fwd_mix(q, k, v, z, w2f_pad, gate_b, ob, og, gg, pm, h, norm_g, w_gla_up, w_out, ln1_g, ln1_b, starts):
    n = q.shape[0]
    t = TILE_GLA
    row = lambda i: (i, 0)
    return pl.pallas_call(
        functools.partial(_gla_fwd_kernel, starts=starts),
        grid=(n // t,),
        in_specs=[
            pl.BlockSpec((t, GLA_K_W), row), pl.BlockSpec((t, GLA_K_W), row), pl.BlockSpec((t, GLA_V_W), row),
            pl.BlockSpec((t, Z_PAD), row),
            _const_spec((Z_PAD, GLA_K_W)), _const_spec((1, GLA_K_W)),
            _const_spec((GLA_SCAN_BLOCK, GLA_SCAN_BLOCK)),
            pl.BlockSpec((t, GLA_V_W), row), pl.BlockSpec((t, GLA_V_W), row), pl.BlockSpec((t, D_MODEL), row),
            pl.BlockSpec((t, D_MODEL), row), pl.BlockSpec((t, D_MODEL), row),
            _const_spec((1, GLA_V_W)), _const_spec((GLA_V_W, D_MODEL)), _const_spec((D_MODEL, D_MODEL)),
            _const_spec((1, D_MODEL)), _const_spec((1, D_MODEL)),
        ],
        out_specs=[pl.BlockSpec((t, D_MODEL), row), pl.BlockSpec((t * ROW_TILES, LANES), row)],
        out_shape=[jax.ShapeDtypeStruct((n, D_MODEL), F32), jax.ShapeDtypeStruct((n * ROW_TILES, LANES), F32)],
        scratch_shapes=[pltpu.VMEM((GLA_HEADS, GLA_DV, GLA_DK), F32)],
        compiler_params=_params(("arbitrary",)),
        name="gla_fwd_mix",
    )(q, k, v, z, w2f_pad, gate_b, _chunk_tri(GLA_SCAN_BLOCK, False), ob, og, gg, pm, h, norm_g, w_gla_up, w_out,
      ln1_g, ln1_b)


def _router_kernel(h_ref, wrh_ref, wrl_ref, rb_ref, tri_ref, idx_ref, wts_ref, rank_ref, cnt_ref, base_ref):
    t = h_ref.shape[0]

    @pl.when(pl.program_id(0) == 0)
    def _():
        base_ref[...] = jnp.zeros_like(base_ref)

    h = h_ref[...]
    hh = h.astype(BF16)
    hl = (h - hh.astype(F32)).astype(BF16)
    nt = (((1,), (1,)), ((), ()))
    logits = (lax.dot_general(wrh_ref[...], hh, nt, preferred_element_type=F32)
              + lax.dot_general(wrl_ref[...], hh, nt, preferred_element_type=F32)
              + lax.dot_general(wrh_ref[...], hl, nt, preferred_element_type=F32))
    scores = _sigmoid(logits)
    sel = scores + rb_ref[:, 0:1]
    neg = jnp.float32(-jnp.inf)

    sub = lax.broadcasted_iota(jnp.int32, (EXPERTS_PER_GROUP, t), 0)
    gscore = []
    for g in range(N_GROUPS):
        sg = sel[g * EXPERTS_PER_GROUP:(g + 1) * EXPERTS_PER_GROUP]
        m1 = jnp.max(sg, axis=0, keepdims=True)
        first = jnp.min(jnp.where(sg == m1, sub, EXPERTS_PER_GROUP), axis=0, keepdims=True)
        m2 = jnp.max(jnp.where(sub == first, neg, sg), axis=0, keepdims=True)
        gscore.append(m1 + m2)
    keep = []
    for g in range(N_GROUPS):
        beaten = jnp.zeros((1, t), jnp.int32)
        for g2 in range(N_GROUPS):
            if g2 == g:
                continue
            wins = (gscore[g2] >= gscore[g]) if g2 < g else (gscore[g2] > gscore[g])
            beaten = beaten + wins.astype(jnp.int32)
        keep.append(beaten < TOPK_GROUPS)
    slot = []
    ahead = jnp.zeros((1, t), jnp.int32)
    for g in range(N_GROUPS):
        slot.append([jnp.logical_and(keep[g], ahead == j) for j in range(TOPK_GROUPS)])
        ahead = ahead + keep[g].astype(jnp.int32)
    groups_of = lambda j: range(j, j + N_GROUPS - TOPK_GROUPS + 1)
    slots_of = lambda g: range(max(0, g - (N_GROUPS - TOPK_GROUPS)), min(TOPK_GROUPS - 1, g) + 1)
    rows_of = lambda x, g: x[g * EXPERTS_PER_GROUP:(g + 1) * EXPERTS_PER_GROUP]

    def compact(x, fill):
        parts = []
        for j in range(TOPK_GROUPS):
            acc = jnp.full((EXPERTS_PER_GROUP, t), fill, x.dtype)
            for g in groups_of(j):
                acc = jnp.where(slot[g][j], rows_of(x, g), acc)
            parts.append(acc)
        return jnp.concatenate(parts, axis=0)

    cand = compact(sel, neg)
    kept_scores = compact(scores, 0.0)
    gid = []
    for j in range(TOPK_GROUPS):
        acc = jnp.zeros((1, t), jnp.int32)
        for g in groups_of(j):
            acc = jnp.where(slot[g][j], g, acc)
        gid.append(acc * EXPERTS_PER_GROUP + sub)
    cid = jnp.concatenate(gid, axis=0)

    picked = []
    wsum = jnp.zeros((1, t), F32)
    for kk in range(TOP_K):
        m = jnp.max(cand, axis=0, keepdims=True)
        e_k = jnp.min(jnp.where(cand == m, cid, N_EXPERTS), axis=0, keepdims=True)
        hit = cid == e_k
        w_k = jnp.sum(jnp.where(hit, kept_scores, 0.0), axis=0, keepdims=True)
        cand = jnp.where(hit, neg, cand)
        wsum = wsum + w_k
        picked.append((e_k, w_k))
        idx_ref[kk:kk + 1, :] = e_k

    chosen = jnp.where(cand == neg, 1.0, 0.0)
    full = []
    for g in range(N_GROUPS):
        acc = jnp.zeros((EXPERTS_PER_GROUP, t), F32)
        for j in slots_of(g):
            acc = jnp.where(slot[g][j], rows_of(chosen, j), acc)
        full.append(acc)
    a16 = jnp.concatenate(full, axis=0).astype(BF16)

    before = jnp.dot(a16, tri_ref[...], preferred_element_type=F32)
    base = base_ref[...]
    before = compact(before + jnp.concatenate([base] * (t // 128), axis=1), 0.0)
    inv = ROUTED_SCALE / wsum
    for kk, (e_k, w_k) in enumerate(picked):
        wts_ref[kk:kk + 1, :] = w_k * inv
        rank_ref[kk:kk + 1, :] = jnp.sum(jnp.where(cid == e_k, before, 0.0), axis=0, keepdims=True).astype(jnp.int32)
    total = base + jnp.dot(a16, jnp.ones((t, 128), BF16), preferred_element_type=F32)
    base_ref[...] = total
    cnt_ref[...] = total


def _router(h1, wr_hi, wr_lo, rbias):
    n = h1.shape[0]
    t = TILE_ROUTER
    r = jnp.arange(t)
    tri = (r[:, None] < r[None, :]).astype(BF16)
    col = lambda i: (0, i)
    idx, wts, rank, cnt = pl.pallas_call(
        _router_kernel,
        grid=(n // t,),
        in_specs=[
            pl.BlockSpec((t, D_MODEL), lambda i: (i, 0)),
            _const_spec((N_EXPERTS, D_MODEL)), _const_spec((N_EXPERTS, D_MODEL)),
            _const_spec((N_EXPERTS, 128)), _const_spec((t, t)),
        ],
        out_specs=[pl.BlockSpec((TOP_K, t), col), pl.BlockSpec((TOP_K, t), col), pl.BlockSpec((TOP_K, t), col),
                   _const_spec((N_EXPERTS, 128))],
        out_shape=[jax.ShapeDtypeStruct((TOP_K, n), jnp.int32), jax.ShapeDtypeStruct((TOP_K, n), F32),
                   jax.ShapeDtypeStruct((TOP_K, n), jnp.int32), jax.ShapeDtypeStruct((N_EXPERTS, 128), F32)],
        scratch_shapes=[pltpu.VMEM((N_EXPERTS, 128), F32)],
        compiler_params=_params(("arbitrary",)),
        name="router",
    )(h1, wr_hi, wr_lo, rbias, tri)
    return idx, wts, rank, cnt[:, 0].astype(jnp.int32)


def _slot_kernel(ps_ref, idx_ref, rank_ref, wts_ref, dest_ref, wt_ref):
    idx = idx_ref[...]

    def add_expert(e, acc):
        return acc + jnp.where(idx == e, ps_ref[e], 0)

    dest = lax.fori_loop(0, N_EXPERTS, add_expert, rank_ref[...], unroll=8)
    for w in range(dest.shape[1] // SC_WINDOW):
        dest_ref[w] = dest[:, w * SC_WINDOW:(w + 1) * SC_WINDOW]
    wt_ref[...] = wts_ref[...].T


def _slots(pstart, idx, rank, wts):
    n = idx.shape[1]
    t = TILE_SLOTS
    col = lambda i, ps: (0, i)
    return pl.pallas_call(
        _slot_kernel,
        grid_spec=pltpu.PrefetchScalarGridSpec(
            num_scalar_prefetch=1,
            grid=(n // t,),
            in_specs=[pl.BlockSpec((TOP_K, t), col), pl.BlockSpec((TOP_K, t), col), pl.BlockSpec((TOP_K, t), col)],
            out_specs=[pl.BlockSpec((t // SC_WINDOW, TOP_K, SC_WINDOW), lambda i, ps: (i, 0, 0)),
                       pl.BlockSpec((t, TOP_K), lambda i, ps: (i, 0))],
        ),
        out_shape=[jax.ShapeDtypeStruct((n // SC_WINDOW, TOP_K, SC_WINDOW), jnp.int32),
                   jax.ShapeDtypeStruct((n, TOP_K), F32)],
        compiler_params=_params(("parallel",)),
        name="slots",
    )(pstart, idx, rank, wts)


def _expert_kernel(first_ref, count_ref, used_ref, wg_ref, wu_ref, wd_ref, x_hbm, y_hbm,
                   wgu_s, wd_s, xbuf, ybuf, xsem, ysem):
    e = pl.program_id(0)
    rows = MOE_BLOCK * ROW_TILES
    used = used_ref[0]

    def x_copy(b, slot):
        return pltpu.make_async_copy(x_hbm.at[pl.ds(b * rows, rows)], xbuf.at[slot], xsem.at[slot])

    def y_copy(b, slot):
        return pltpu.make_async_copy(ybuf.at[slot], y_hbm.at[pl.ds(b * rows, rows)], ysem.at[slot])

    @pl.when(e == 0)
    def _():
        for b in range(min(X_AHEAD, x_hbm.shape[0] // rows)):
            @pl.when(b < used)
            def _():
                x_copy(b, b).start()

    @pl.when(count_ref[e] > 0)
    def _():
        wgu_s[:, :EXPERT_HIDDEN] = wg_ref[0].astype(BF16)
        wgu_s[:, EXPERT_HIDDEN:] = wu_ref[0].astype(BF16)
        wd_s[...] = wd_ref[0].astype(BF16)

    def run_blocks(b0, g):
        for i in range(g):
            ahead = b0 + i + X_AHEAD

            @pl.when(ahead < used)
            def _():
                x_copy(ahead, ahead % X_SLOTS).start()

        for i in range(g):
            done = b0 + i - Y_SLOTS

            @pl.when(done >= 0)
            def _():
                y_copy(done, done % Y_SLOTS).wait()

        x = []
        for i in range(g):
            xs = (b0 + i) % X_SLOTS
            x_copy(b0 + i, xs).wait()
            x.append(_load_row_contiguous(xbuf.at[xs], MOE_BLOCK).astype(BF16))
        gu = [jnp.dot(xi, wgu_s[...], preferred_element_type=F32) for xi in x]
        hid = [(_silu(gi[:, :EXPERT_HIDDEN]) * gi[:, EXPERT_HIDDEN:]).astype(BF16) for gi in gu]
        y = [jnp.dot(hi, wd_s[...], preferred_element_type=F32) for hi in hid]
        for i in range(g):
            ys = (b0 + i) % Y_SLOTS
            _store_row_contiguous(ybuf.at[ys], y[i])
            y_copy(b0 + i, ys).start()

    first = first_ref[e]
    count = count_ref[e]
    n_groups = lax.div(count, EXPERT_GROUP)

    def group(j, carry):
        run_blocks(first + j * EXPERT_GROUP, EXPERT_GROUP)
        return carry

    lax.fori_loop(0, n_groups, group, 0)
    for left in range(1, EXPERT_GROUP):
        @pl.when(count - n_groups * EXPERT_GROUP == left)
        def _():
            run_blocks(first + n_groups * EXPERT_GROUP, left)

    @pl.when(e == pl.num_programs(0) - 1)
    def _():
        for back in range(1, Y_SLOTS + 1):
            @pl.when(used - back >= 0)
            def _():
                y_copy(used - back, (used - back) % Y_SLOTS).wait()


def _experts(first_block, block_count, n_used, xs, w_gate, w_up, w_down):
    rows = MOE_BLOCK * ROW_TILES
    n_exp = w_gate.shape[0]
    any_spec = pl.BlockSpec(memory_space=pl.ANY)
    return pl.pallas_call(
        _expert_kernel,
        grid_spec=pltpu.PrefetchScalarGridSpec(
            num_scalar_prefetch=3,
            grid=(n_exp,),
            in_specs=[
                pl.BlockSpec((1, D_MODEL, EXPERT_HIDDEN), lambda e, *_: (e, 0, 0)),
                pl.BlockSpec((1, D_MODEL, EXPERT_HIDDEN), lambda e, *_: (e, 0, 0)),
                pl.BlockSpec((1, EXPERT_HIDDEN, D_MODEL), lambda e, *_: (e, 0, 0)),
                any_spec,
            ],
            out_specs=any_spec,
            scratch_shapes=[
                pltpu.VMEM((D_MODEL, 2 * EXPERT_HIDDEN), BF16), pltpu.VMEM((EXPERT_HIDDEN, D_MODEL), BF16),
                pltpu.VMEM((X_SLOTS, rows, LANES), F32), pltpu.VMEM((Y_SLOTS, rows, LANES), F32),
                pltpu.SemaphoreType.DMA((X_SLOTS,)), pltpu.SemaphoreType.DMA((Y_SLOTS,)),
            ],
        ),
        out_shape=jax.ShapeDtypeStruct(xs.shape, F32),
        compiler_params=_params(("arbitrary",)),
        name="experts",
    )(first_block, block_count, n_used, w_gate, w_up, w_down, xs)


def _sc_mesh():
    return plsc.VectorSubcoreMesh(core_axis_name="c", subcore_axis_name="s")


def _sc_dispatch(x3, dest_w, n_rows):
    n = x3.shape[0]

    @functools.partial(pl.kernel, out_type=jax.ShapeDtypeStruct((n_rows, ROW_TILES, LANES), F32), mesh=_sc_mesh(),
                       name="sc_dispatch")
    def run(x_hbm, d_hbm, o_hbm):
        def body(x_vmem, d_vmem):
            for kk in range(TOP_K):
                pltpu.sync_copy(x_vmem, o_hbm.at[d_vmem.at[0].at[kk]])

        pltpu.emit_pipeline(
            body,
            grid=(n // SC_WINDOW,),
            in_specs=[pl.BlockSpec((SC_WINDOW, ROW_TILES, LANES), lambda i: (i, 0, 0)),
                      pl.BlockSpec((1, TOP_K, SC_WINDOW), lambda i: (i, 0, 0))],
            out_specs=[],
            core_axis_name=("c", "s"),
            dimension_semantics=(pltpu.PARALLEL,),
        )(x_hbm, d_hbm)

    return run(x3, dest_w)


def _sc_gather(y3, dest_w):
    nw = dest_w.shape[0]
    n = nw * SC_WINDOW

    @functools.partial(pl.kernel, out_type=jax.ShapeDtypeStruct((TOP_K, n, ROW_TILES, LANES), F32), mesh=_sc_mesh(),
                       name="sc_gather")
    def run(y_hbm, d_hbm, o_hbm):
        def body(d_vmem, g_vmem):
            pltpu.sync_copy(y_hbm.at[d_vmem.at[0].at[0]], g_vmem.at[0])

        pltpu.emit_pipeline(
            body,
            grid=(nw, TOP_K),
            in_specs=[pl.BlockSpec((1, 1, SC_WINDOW), lambda i, k: (i, k, 0))],
            out_specs=[pl.BlockSpec((1, SC_WINDOW, ROW_TILES, LANES), lambda i, k: (k, i, 0, 0))],
            core_axis_name=("c", "s"),
            dimension_semantics=(pltpu.PARALLEL, pltpu.PARALLEL),
        )(d_hbm, o_hbm)

    return run(y3, dest_w)


def _final_kernel(h_ref, g_ref, w_ref, wgu_ref, wd_ref, lg_ref, lb_ref, y_ref):
    t = h_ref.shape[0]
    h = h_ref[...]
    gu = jnp.dot(h.astype(BF16), wgu_ref[...], preferred_element_type=F32)
    hid = (_silu(gu[:, :SHARED_HIDDEN]) * gu[:, SHARED_HIDDEN:]).astype(BF16)
    ffn = jnp.dot(hid, wd_ref[...], preferred_element_type=F32)
    w = w_ref[...]
    for kk in range(TOP_K):
        ffn = ffn + _load_row_contiguous(g_ref.at[kk], t) * w[:, kk:kk + 1]
    y_ref[...] = _layer_norm(DN_ALPHA * h + ffn, lg_ref[...], lb_ref[...])


def _final(h1, gathered, wts_t, w_sh_gu, w_sh_down, ln_g, ln_b, first_token):
    m = gathered.shape[1] // ROW_TILES
    t = TILE_FINAL
    off = first_token // t
    part = lambda i: (i + off, 0)
    return pl.pallas_call(
        _final_kernel,
        grid=(m // t,),
        in_specs=[
            pl.BlockSpec((t, D_MODEL), part),
            pl.BlockSpec((TOP_K, t * ROW_TILES, LANES), lambda i: (0, i, 0)),
            pl.BlockSpec((t, TOP_K), part),
            _const_spec((D_MODEL, 2 * SHARED_HIDDEN)), _const_spec((SHARED_HIDDEN, D_MODEL)),
            _const_spec((1, D_MODEL)), _const_spec((1, D_MODEL)),
        ],
        out_specs=pl.BlockSpec((t, D_MODEL), lambda i: (i, 0)),
        out_shape=jax.ShapeDtypeStruct((m, D_MODEL), F32),
        compiler_params=_params(("parallel",)),
        name="final",
    )(h1, gathered, wts_t, w_sh_gu, w_sh_down, ln_g, ln_b)


def _block_diag(w):
    g, a, b = w.shape
    out = jnp.zeros((g * a, g * b), w.dtype)
    for i in range(g):
        out = out.at[i * a:(i + 1) * a, i * b:(i + 1) * b].set(w[i])
    return out


def kernel(x_prompt, x_sample, ln0_g, ln0_b, w_in, pool_w, pool_scale, w_pool_up, gate_w2_fwd, gate_b_fwd,
           gate_w2_bwd, gate_b_bwd, gla_norm_g, w_gla_up, w_out, ln1_g, ln1_b, w_router, router_bias,
           w_exp_gate, w_exp_up, w_exp_down, w_sh_gate, w_sh_up, w_sh_down, ln2_g, ln2_b):
    bp, lp, d = x_prompt.shape
    bs, ls, _ = x_sample.shape
    n_p, n_s = bp * lp, bs * ls
    n = n_p + n_s
    starts = tuple(b * lp for b in range(bp)) + tuple(n_p + b * ls for b in range(bs))
    ends = tuple((b + 1) * lp for b in range(bp)) + tuple(n_p + (b + 1) * ls for b in range(bs))
    vec = lambda a: a.reshape(1, -1).astype(F32)
    assert d == D_MODEL and w_in.shape[0] == DEPTH
    coarsest = max(TILE_INPROJ, TILE_POOL, TILE_GLA, TILE_ROUTER, TILE_FINAL, SC_WINDOW)
    assert lp % coarsest == 0 and ls % coarsest == 0, "sequence boundaries must fall on tile boundaries"
    assert n % TILE_SLOTS == 0

    w = w_in[0]
    c_zf = POOL_W + 2 * GLA_K_W + 2 * GLA_V_W
    c_gp = c_zf + 2 * GATE_RANK
    w_main = jnp.concatenate([w[:, :c_zf], w[:, c_gp:]], axis=1).astype(BF16)
    w_z = jnp.zeros((d, Z_PAD), F32).at[:, :2 * GATE_RANK].set(w[:, c_zf:c_gp]).astype(BF16)
    w2f = jnp.zeros((Z_PAD, GLA_K_W), F32).at[:GATE_RANK].set(gate_w2_fwd[0]).astype(BF16)
    w2b = jnp.zeros((Z_PAD, GLA_K_W), F32).at[GATE_RANK:2 * GATE_RANK].set(gate_w2_bwd[0]).astype(BF16)

    h, u, q, k, v, og, gp, gg, z = _inproj(x_prompt.reshape(n_p, d), x_sample.reshape(n_s, d),
                                           vec(ln0_g), vec(ln0_b), w_main, w_z)
    pm = _pool(u, gp, _block_diag(pool_w[0]).astype(BF16), vec(pool_scale[0]), w_pool_up[0].astype(BF16),
               starts, ends)
    ob = _gla_bwd(q, k, v, z, w2b, vec(gate_b_bwd[0]), ends)
    h1, h1_rc = _gla_fwd_mix(q, k, v, z, w2f, vec(gate_b_fwd[0]), ob, og, gg, pm, h, vec(gla_norm_g[0]),
                             w_gla_up[0].astype(BF16), w_out[0].astype(BF16), vec(ln1_g[0]), vec(ln1_b[0]), starts)

    wr_t = w_router[0].T
    wr_hi = wr_t.astype(BF16)
    wr_lo = (wr_t - wr_hi.astype(F32)).astype(BF16)
    rbias = jnp.broadcast_to(router_bias[0].astype(F32)[:, None], (N_EXPERTS, 128))
    idx, wts, rank, counts = _router(h1, wr_hi, wr_lo, rbias)

    padded = (counts + MOE_BLOCK - 1) // MOE_BLOCK * MOE_BLOCK
    pend = jnp.cumsum(padded)
    pstart = pend - padded
    n_rows = n * TOP_K + N_EXPERTS * MOE_BLOCK
    first_block = (pstart // MOE_BLOCK).astype(jnp.int32)
    block_count = (padded // MOE_BLOCK).astype(jnp.int32)
    n_used = (pend[-1:] // MOE_BLOCK).astype(jnp.int32)
    dest_w, wts_t = _slots(pstart.astype(jnp.int32), idx, rank, wts)

    xs = _sc_dispatch(h1_rc.reshape(n, ROW_TILES, LANES), dest_w, n_rows)
    ys = _experts(first_block, block_count, n_used, xs.reshape(n_rows * ROW_TILES, LANES),
                  w_exp_gate[0], w_exp_up[0], w_exp_down[0])
    ys = ys.reshape(n_rows, ROW_TILES, LANES)
    w_sh_gu = jnp.concatenate([w_sh_gate[0], w_sh_up[0]], axis=1).astype(BF16)
    w_sh_d = w_sh_down[0].astype(BF16)
    outs = []
    for first, m in ((0, n_p), (n_p, n_s)):
        windows = dest_w[first // SC_WINDOW:(first + m) // SC_WINDOW]
        gathered = _sc_gather(ys, windows).reshape(TOP_K, m * ROW_TILES, LANES)
        outs.append(_final(h1, gathered, wts_t, w_sh_gu, w_sh_d, vec(ln2_g[0]), vec(ln2_b[0]), first))
    return outs[0].reshape(bp, lp, d), outs[1].reshape(bs, ls, d)
```

```python
import functools

import jax
import jax.numpy as jnp
from jax import lax
from jax.experimental import pallas as pl
from jax.experimental.pallas import tpu as pltpu
from jax.experimental.pallas import tpu_sc as plsc

F32 = jnp.float32
BF16 = jnp.bfloat16

D_MODEL = 1024
POOL_GROUPS = 4
POOL_GROUP_W = 128
POOL_W = POOL_GROUPS * POOL_GROUP_W
POOL_WINDOWS = (2, 4, 8, 16)
POOL_HALO = 16
GLA_HEADS = 4
GLA_K_W = 512
GLA_V_W = 1024
GLA_DK = GLA_K_W // GLA_HEADS
GLA_DV = GLA_V_W // GLA_HEADS
GATE_RANK = 16
GATE_TAU = 16.0
GLA_CHUNK = 64
Z_PAD = 128
N_EXPERTS = 256
TOP_K = 8
N_GROUPS = 8
TOPK_GROUPS = 4
EXPERTS_PER_GROUP = N_EXPERTS // N_GROUPS
EXPERT_HIDDEN = 256
SHARED_HIDDEN = 256
ROUTED_SCALE = 2.5
DEPTH = 1
DN_ALPHA = (2 * DEPTH) ** 0.25
LN_EPS = 1e-5
RMS_EPS = 1e-6

VMEM_LIMIT_BYTES = 56 * 1024 * 1024

TILE_INPROJ = 512
TILE_POOL = 512
TILE_GLA = 512
GLA_SCAN_BLOCK = 256
TILE_ROUTER = 512
TILE_FINAL = 512
TILE_SLOTS = 2048
MOE_BLOCK = 256
EXPERT_GROUP = 4
X_AHEAD = 5
X_SLOTS = X_AHEAD + EXPERT_GROUP
Y_SLOTS = 3 * EXPERT_GROUP
LANES = 128
ROW_WORDS = D_MODEL // 2
ROW_TILES = ROW_WORDS // LANES
SC_WINDOW = 64


def _params(semantics):
    return pltpu.CompilerParams(dimension_semantics=semantics, vmem_limit_bytes=VMEM_LIMIT_BYTES)


def _const_spec(shape):
    nd = len(shape)
    return pl.BlockSpec(shape, lambda *_: (0,) * nd)


def _layer_norm(x, g, b):
    mu = jnp.mean(x, axis=-1, keepdims=True)
    xc = x - mu
    var = jnp.mean(xc * xc, axis=-1, keepdims=True)
    return xc * lax.rsqrt(var + LN_EPS) * g + b


def _sigmoid(x):
    return 1.0 / (1.0 + jnp.exp(-x))


def _log_sigmoid(x):
    return jnp.minimum(x, 0.0) - jnp.log(1.0 + jnp.exp(-jnp.abs(x)))


def _silu(x):
    return x * _sigmoid(x)


def _tile_hits(i, tile, offsets):
    hit = i < 0
    for off in offsets:
        if off % tile == 0:
            hit = jnp.logical_or(hit, i == off // tile)
    return hit


_MAIN_SPLITS = (POOL_W, GLA_K_W, GLA_K_W, GLA_V_W, GLA_V_W, D_MODEL, D_MODEL)


def _inproj_kernel(xp_ref, xs_ref, g_ref, b_ref, w_ref, wz_ref,
                   h_ref, u_ref, q_ref, k_ref, v_ref, og_ref, gp_ref, gg_ref, z_ref, *, n_prompt_tiles):
    i = pl.program_id(0)
    t = h_ref.shape[0]
    for rows in (slice(0, t // 2), slice(t // 2, t)):
        x = jnp.where(i < n_prompt_tiles, xp_ref[rows, :], xs_ref[rows, :])
        h = _layer_norm(x, g_ref[...], b_ref[...])
        h_ref[rows, :] = h
        hb = h.astype(BF16)
        off = 0
        for ref, width in zip((u_ref, q_ref, k_ref, v_ref, og_ref, gp_ref, gg_ref), _MAIN_SPLITS):
            ref[rows, :] = jnp.dot(hb, w_ref[:, off:off + width], preferred_element_type=F32).astype(BF16)
            off += width
        z_ref[rows, :] = jnp.dot(hb, wz_ref[...], preferred_element_type=F32).astype(BF16)


def _inproj(xp, xs, ln_g, ln_b, w_main, w_z):
    n_p, n_s = xp.shape[0], xs.shape[0]
    n = n_p + n_s
    t = TILE_INPROJ
    npt, nst = n_p // t, n_s // t
    row = lambda i: (i, 0)
    widths = _MAIN_SPLITS + (Z_PAD,)
    out_shape = [jax.ShapeDtypeStruct((n, D_MODEL), F32)] + [jax.ShapeDtypeStruct((n, w), BF16) for w in widths]
    out_specs = [pl.BlockSpec((t, D_MODEL), row)] + [pl.BlockSpec((t, w), row) for w in widths]
    return pl.pallas_call(
        functools.partial(_inproj_kernel, n_prompt_tiles=npt),
        grid=(npt + nst,),
        in_specs=[
            pl.BlockSpec((t, D_MODEL), lambda i: (jnp.minimum(i, npt - 1), 0)),
            pl.BlockSpec((t, D_MODEL), lambda i: (jnp.maximum(i - npt, 0), 0)),
            _const_spec((1, D_MODEL)), _const_spec((1, D_MODEL)),
            _const_spec(w_main.shape), _const_spec(w_z.shape),
        ],
        out_specs=out_specs,
        out_shape=out_shape,
        compiler_params=_params(("parallel",)),
        name="inproj",
    )(xp, xs, ln_g, ln_b, w_main, w_z)


def _pool_kernel(u_ref, up_ref, un_ref, gp_ref, band_ref, pw_ref, sc_ref, wup_ref, pm_ref, *, starts, ends):
    t = u_ref.shape[0]
    i = pl.program_id(0)
    is_start = _tile_hits(i, t, starts)
    is_end = _tile_hits(i + 1, t, ends)
    cur = u_ref[...]
    prev = jnp.where(is_start, jnp.zeros_like(up_ref[...]), up_ref[...])
    nxt = jnp.where(is_end, jnp.zeros_like(un_ref[...]), un_ref[...])
    ext = jnp.concatenate([prev, cur, nxt], axis=0)
    r = lax.broadcasted_iota(jnp.int32, (t, POOL_GROUP_W), 0).astype(F32)
    fs = is_start.astype(F32)
    fe = is_end.astype(F32)
    parts = []
    for gi, w in enumerate(POOL_WINDOWS):
        sl = slice(gi * POOL_GROUP_W, (gi + 1) * POOL_GROUP_W)
        s = jnp.dot(band_ref[gi], ext[:, sl], preferred_element_type=F32)
        cnt = w - fs * jnp.maximum(w // 2 - r, 0.0) - fe * jnp.maximum(r + (w // 2 - t), 0.0)
        parts.append(s / cnt - cur[:, sl].astype(F32))
    p = jnp.concatenate(parts, axis=1).astype(BF16)
    p = (jnp.dot(p, pw_ref[...], preferred_element_type=F32) * sc_ref[...]).astype(BF16)
    pool_out = jnp.dot(p, wup_ref[...], preferred_element_type=F32)
    pm_ref[...] = (_sigmoid(gp_ref[...].astype(F32)) * pool_out).astype(BF16)


def _pool_bands(t):
    r = jnp.arange(t)[:, None]
    c = jnp.arange(t + 2 * POOL_HALO)[None, :] - POOL_HALO
    return jnp.stack([((c >= r - w // 2) & (c <= r + w // 2 - 1)).astype(BF16) for w in POOL_WINDOWS])


def _pool(u, gp, pool_w_bd, pool_scale, w_pool_up, starts, ends):
    n = u.shape[0]
    t = TILE_POOL
    hb = t // POOL_HALO
    last = n // POOL_HALO - 1
    return pl.pallas_call(
        functools.partial(_pool_kernel, starts=starts, ends=ends),
        grid=(n // t,),
        in_specs=[
            pl.BlockSpec((t, POOL_W), lambda i: (i, 0)),
            pl.BlockSpec((POOL_HALO, POOL_W), lambda i: (jnp.maximum(i * hb - 1, 0), 0)),
            pl.BlockSpec((POOL_HALO, POOL_W), lambda i: (jnp.minimum((i + 1) * hb, last), 0)),
            pl.BlockSpec((t, D_MODEL), lambda i: (i, 0)),
            _const_spec((POOL_GROUPS, t, t + 2 * POOL_HALO)),
            _const_spec((POOL_W, POOL_W)), _const_spec((1, POOL_W)), _const_spec((POOL_W, D_MODEL)),
        ],
        out_specs=pl.BlockSpec((t, D_MODEL), lambda i: (i, 0)),
        out_shape=jax.ShapeDtypeStruct((n, D_MODEL), BF16),
        compiler_params=_params(("parallel",)),
        name="pool",
    )(u, u, u, gp, _pool_bands(t), pool_w_bd, pool_scale, w_pool_up)


def _gla_tile(q, k, v, la, tri, st_ref, *, reverse):
    t = q.shape[0]
    c_sz = GLA_CHUNK
    hi = la.astype(BF16)
    r1 = la - hi.astype(F32)
    mid = r1.astype(BF16)
    lo = (r1 - mid.astype(F32)).astype(BF16)
    scan = lambda rs: (jnp.dot(tri, hi[rs], preferred_element_type=F32)
                       + jnp.dot(tri, mid[rs], preferred_element_type=F32)
                       + jnp.dot(tri, lo[rs], preferred_element_type=F32))
    sb = GLA_SCAN_BLOCK
    b = jnp.concatenate([scan(slice(r0, r0 + sb)) for r0 in range(0, t, sb)], axis=0)
    kf = k.astype(F32)
    qd = (q.astype(F32) * jnp.exp(b) * (GLA_DK ** -0.5)).astype(BF16)
    kd = (kf * jnp.exp(-b)).astype(BF16)
    row = lax.broadcasted_iota(jnp.int32, (c_sz, c_sz), 0)
    col = lax.broadcasted_iota(jnp.int32, (c_sz, c_sz), 1)
    mask = (col >= row) if reverse else (col <= row)
    n_chunks = t // c_sz
    order = list(range(n_chunks - 1, -1, -1) if reverse else range(n_chunks))
    nt = (((1,), (1,)), ((), ()))
    tn = (((0,), (0,)), ((), ()))
    rows = [slice(c * c_sz, (c + 1) * c_sz) for c in range(n_chunks)]
    lanes = [slice(h * GLA_DK, (h + 1) * GLA_DK) for h in range(GLA_HEADS)]
    vcols = [slice(h * GLA_DV, (h + 1) * GLA_DV) for h in range(GLA_HEADS)]

    att, kv, dec = {}, {}, {}
    for c in order:
        b_c = b[rows[c]]
        b_edge = b_c[0:1] if reverse else b_c[c_sz - 1:c_sz]
        k2 = (kf[rows[c]] * jnp.exp(b_edge - b_c)).astype(BF16)
        dec[c] = jnp.exp(b_edge)
        for h in range(GLA_HEADS):
            att[c, h] = lax.dot_general(qd[rows[c], lanes[h]], kd[rows[c], lanes[h]], nt, preferred_element_type=F32)
            kv[c, h] = lax.dot_general(v[rows[c], vcols[h]], k2[:, lanes[h]], tn, preferred_element_type=F32)

    state_before = {}
    for h in range(GLA_HEADS):
        st = st_ref[h]
        for c in order:
            state_before[c, h] = st.astype(BF16)
            st = st * dec[c][:, lanes[h]] + kv[c, h]
        st_ref[h] = st

    outs = []
    for c in range(n_chunks):
        heads = []
        for h in range(GLA_HEADS):
            a = jnp.where(mask, att[c, h], 0.0).astype(BF16)
            o = jnp.dot(a, v[rows[c], vcols[h]], preferred_element_type=F32)
            heads.append(o + lax.dot_general(qd[rows[c], lanes[h]], state_before[c, h], nt,
                                             preferred_element_type=F32))
        outs.append(jnp.concatenate(heads, axis=1))
    return jnp.concatenate(outs, axis=0)


def _log_decay(z, w2_ref, gb_ref):
    return _log_sigmoid(jnp.dot(z, w2_ref[...], preferred_element_type=F32) + gb_ref[...]) * (1.0 / GATE_TAU)


def _gla_bwd_kernel(q_ref, k_ref, v_ref, z_ref, w2_ref, gb_ref, tri_ref, ob_ref, st_ref, *, n_tiles, ends):
    t = q_ref.shape[0]
    j = n_tiles - 1 - pl.program_id(0)

    @pl.when(_tile_hits(j + 1, t, ends))
    def _():
        st_ref[...] = jnp.zeros_like(st_ref)

    la = _log_decay(z_ref[...], w2_ref, gb_ref)
    ob_ref[...] = _gla_tile(q_ref[...], k_ref[...], v_ref[...], la, tri_ref[...], st_ref, reverse=True)


def _gla_fwd_kernel(q_ref, k_ref, v_ref, z_ref, w2_ref, gb_ref, tri_ref, ob_ref, og_ref, gg_ref, pm_ref, h_ref,
                    ng_ref, wgu_ref, wo_ref, l1g_ref, l1b_ref, h1_ref, h1rc_ref, st_ref, *, starts):
    t = q_ref.shape[0]
    i = pl.program_id(0)

    @pl.when(_tile_hits(i, t, starts))
    def _():
        st_ref[...] = jnp.zeros_like(st_ref)

    la = _log_decay(z_ref[...], w2_ref, gb_ref)
    o = _gla_tile(q_ref[...], k_ref[...], v_ref[...], la, tri_ref[...], st_ref, reverse=False) + ob_ref[...]
    heads = []
    for h in range(GLA_HEADS):
        oh = o[:, h * GLA_DV:(h + 1) * GLA_DV]
        heads.append(oh * lax.rsqrt(jnp.mean(oh * oh, axis=-1, keepdims=True) + RMS_EPS))
    o = jnp.concatenate(heads, axis=1) * ng_ref[...]
    o = (o * _silu(og_ref[...].astype(F32))).astype(BF16)
    gla_out = jnp.dot(o, wgu_ref[...], preferred_element_type=F32)
    merged = _sigmoid(gg_ref[...].astype(F32)) * gla_out + pm_ref[...].astype(F32)
    mix = jnp.dot(merged.astype(BF16), wo_ref[...], preferred_element_type=F32)
    h1 = _layer_norm(DN_ALPHA * h_ref[...] + mix, l1g_ref[...], l1b_ref[...])
    h1_ref[...] = h1
    _store_row_contiguous(h1rc_ref, h1)


def _store_row_contiguous(ref, x):
    t = x.shape[0]
    bits = pltpu.bitcast(x.astype(BF16).astype(F32), jnp.uint32)
    words = (bits[:, :ROW_WORDS] >> 16) | (bits[:, ROW_WORDS:] & jnp.uint32(0xFFFF0000))
    words = pltpu.bitcast(words, F32)
    for s in range(ROW_TILES):
        ref[pl.ds(s, t, stride=ROW_TILES), :] = words[:, s * LANES:(s + 1) * LANES]


def _load_row_contiguous(ref, t):
    words = jnp.concatenate([ref[pl.ds(s, t, stride=ROW_TILES), :] for s in range(ROW_TILES)], axis=1)
    bits = pltpu.bitcast(words, jnp.uint32)
    lo = pltpu.bitcast(bits << 16, F32)
    hi = pltpu.bitcast(bits & jnp.uint32(0xFFFF0000), F32)
    return jnp.concatenate([lo, hi], axis=1)


def _chunk_tri(t, reverse):
    r = jnp.arange(t)[:, None]
    c = jnp.arange(t)[None, :]
    same = (r // GLA_CHUNK) == (c // GLA_CHUNK)
    return (same & ((c >= r) if reverse else (c <= r))).astype(BF16)


def _gla_bwd(q, k, v, z, w2b_pad, gate_b, ends):
    n = q.shape[0]
    t = TILE_GLA
    nt = n // t
    rev = lambda i: (nt - 1 - i, 0)
    return pl.pallas_call(
        functools.partial(_gla_bwd_kernel, n_tiles=nt, ends=ends),
        grid=(nt,),
        in_specs=[
            pl.BlockSpec((t, GLA_K_W), rev), pl.BlockSpec((t, GLA_K_W), rev), pl.BlockSpec((t, GLA_V_W), rev),
            pl.BlockSpec((t, Z_PAD), rev),
            _const_spec((Z_PAD, GLA_K_W)), _const_spec((1, GLA_K_W)),
            _const_spec((GLA_SCAN_BLOCK, GLA_SCAN_BLOCK)),
        ],
        out_specs=pl.BlockSpec((t, GLA_V_W), rev),
        out_shape=jax.ShapeDtypeStruct((n, GLA_V_W), F32),
        scratch_shapes=[pltpu.VMEM((GLA_HEADS, GLA_DV, GLA_DK), F32)],
        compiler_params=_params(("arbitrary",)),
        name="gla_bwd",
    )(q, k, v, z, w2b_pad, gate_b, _chunk_tri(GLA_SCAN_BLOCK, True))


def _gla_fwd_mix(q, k, v, z, w2f_pad, gate_b, ob, og, gg, pm, h, norm_g, w_gla_up, w_out, ln1_g, ln1_b, starts):
    n = q.shape[0]
    t = TILE_GLA
    row = lambda i: (i, 0)
    return pl.pallas_call(
        functools.partial(_gla_fwd_kernel, starts=starts),
        grid=(n // t,),
        in_specs=[
            pl.BlockSpec((t, GLA_K_W), row), pl.BlockSpec((t, GLA_K_W), row), pl.BlockSpec((t, GLA_V_W), row),
            pl.BlockSpec((t, Z_PAD), row),
            _const_spec((Z_PAD, GLA_K_W)), _const_spec((1, GLA_K_W)),
            _const_spec((GLA_SCAN_BLOCK, GLA_SCAN_BLOCK)),
            pl.BlockSpec((t, GLA_V_W), row), pl.BlockSpec((t, GLA_V_W), row), pl.BlockSpec((t, D_MODEL), row),
            pl.BlockSpec((t, D_MODEL), row), pl.BlockSpec((t, D_MODEL), row),
            _const_spec((1, GLA_V_W)), _const_spec((GLA_V_W, D_MODEL)), _const_spec((D_MODEL, D_MODEL)),
            _const_spec((1, D_MODEL)), _const_spec((1, D_MODEL)),
        ],
        out_specs=[pl.BlockSpec((t, D_MODEL), row), pl.BlockSpec((t * ROW_TILES, LANES), row)],
        out_shape=[jax.ShapeDtypeStruct((n, D_MODEL), F32), jax.ShapeDtypeStruct((n * ROW_TILES, LANES), F32)],
        scratch_shapes=[pltpu.VMEM((GLA_HEADS, GLA_DV, GLA_DK), F32)],
        compiler_params=_params(("arbitrary",)),
        name="gla_fwd_mix",
    )(q, k, v, z, w2f_pad, gate_b, _chunk_tri(GLA_SCAN_BLOCK, False), ob, og, gg, pm, h, norm_g, w_gla_up, w_out,
      ln1_g, ln1_b)


def _router_kernel(h_ref, wrh_ref, wrl_ref, rb_ref, tri_ref, idx_ref, wts_ref, rank_ref, cnt_ref, base_ref):
    t = h_ref.shape[0]

    @pl.when(pl.program_id(0) == 0)
    def _():
        base_ref[...] = jnp.zeros_like(base_ref)

    h = h_ref[...]
    hh = h.astype(BF16)
    hl = (h - hh.astype(F32)).astype(BF16)
    nt = (((1,), (1,)), ((), ()))
    logits = (lax.dot_general(wrh_ref[...], hh, nt, preferred_element_type=F32)
              + lax.dot_general(wrl_ref[...], hh, nt, preferred_element_type=F32)
              + lax.dot_general(wrh_ref[...], hl, nt, preferred_element_type=F32))
    scores = _sigmoid(logits)
    sel = scores + rb_ref[:, 0:1]
    neg = jnp.float32(-jnp.inf)

    sub = lax.broadcasted_iota(jnp.int32, (EXPERTS_PER_GROUP, t), 0)
    gscore = []
    for g in range(N_GROUPS):
        sg = sel[g * EXPERTS_PER_GROUP:(g + 1) * EXPERTS_PER_GROUP]
        m1 = jnp.max(sg, axis=0, keepdims=True)
        first = jnp.min(jnp.where(sg == m1, sub, EXPERTS_PER_GROUP), axis=0, keepdims=True)
        m2 = jnp.max(jnp.where(sub == first, neg, sg), axis=0, keepdims=True)
        gscore.append(m1 + m2)
    keep = []
    for g in range(N_GROUPS):
        beaten = jnp.zeros((1, t), jnp.int32)
        for g2 in range(N_GROUPS):
            if g2 == g:
                continue
            wins = (gscore[g2] >= gscore[g]) if g2 < g else (gscore[g2] > gscore[g])
            beaten = beaten + wins.astype(jnp.int32)
        keep.append(beaten < TOPK_GROUPS)
    slot = []
    ahead = jnp.zeros((1, t), jnp.int32)
    for g in range(N_GROUPS):
        slot.append([jnp.logical_and(keep[g], ahead == j) for j in range(TOPK_GROUPS)])
        ahead = ahead + keep[g].astype(jnp.int32)
    groups_of = lambda j: range(j, j + N_GROUPS - TOPK_GROUPS + 1)
    slots_of = lambda g: range(max(0, g - (N_GROUPS - TOPK_GROUPS)), min(TOPK_GROUPS - 1, g) + 1)
    rows_of = lambda x, g: x[g * EXPERTS_PER_GROUP:(g + 1) * EXPERTS_PER_GROUP]

    def compact(x, fill):
        parts = []
        for j in range(TOPK_GROUPS):
            acc = jnp.full((EXPERTS_PER_GROUP, t), fill, x.dtype)
            for g in groups_of(j):
                acc = jnp.where(slot[g][j], rows_of(x, g), acc)
            parts.append(acc)
        return jnp.concatenate(parts, axis=0)

    cand = compact(sel, neg)
    kept_scores = compact(scores, 0.0)
    gid = []
    for j in range(TOPK_GROUPS):
        acc = jnp.zeros((1, t), jnp.int32)
        for g in groups_of(j):
            acc = jnp.where(slot[g][j], g, acc)
        gid.append(acc * EXPERTS_PER_GROUP + sub)
    cid = jnp.concatenate(gid, axis=0)

    picked = []
    wsum = jnp.zeros((1, t), F32)
    for kk in range(TOP_K):
        m = jnp.max(cand, axis=0, keepdims=True)
        e_k = jnp.min(jnp.where(cand == m, cid, N_EXPERTS), axis=0, keepdims=True)
        hit = cid == e_k
        w_k = jnp.sum(jnp.where(hit, kept_scores, 0.0), axis=0, keepdims=True)
        cand = jnp.where(hit, neg, cand)
        wsum = wsum + w_k
        picked.append((e_k, w_k))
        idx_ref[kk:kk + 1, :] = e_k

    chosen = jnp.where(cand == neg, 1.0, 0.0)
    full = []
    for g in range(N_GROUPS):
        acc = jnp.zeros((EXPERTS_PER_GROUP, t), F32)
        for j in slots_of(g):
            acc = jnp.where(slot[g][j], rows_of(chosen, j), acc)
        full.append(acc)
    a16 = jnp.concatenate(full, axis=0).astype(BF16)

    before = jnp.dot(a16, tri_ref[...], preferred_element_type=F32)
    base = base_ref[...]
    before = compact(before + jnp.concatenate([base] * (t // 128), axis=1), 0.0)
    inv = ROUTED_SCALE / wsum
    for kk, (e_k, w_k) in enumerate(picked):
        wts_ref[kk:kk + 1, :] = w_k * inv
        rank_ref[kk:kk + 1, :] = jnp.sum(jnp.where(cid == e_k, before, 0.0), axis=0, keepdims=True).astype(jnp.int32)
    total = base + jnp.dot(a16, jnp.ones((t, 128), BF16), preferred_element_type=F32)
    base_ref[...] = total
    cnt_ref[...] = total


def _router(h1, wr_hi, wr_lo, rbias):
    n = h1.shape[0]
    t = TILE_ROUTER
    r = jnp.arange(t)
    tri = (r[:, None] < r[None, :]).astype(BF16)
    col = lambda i: (0, i)
    idx, wts, rank, cnt = pl.pallas_call(
        _router_kernel,
        grid=(n // t,),
        in_specs=[
            pl.BlockSpec((t, D_MODEL), lambda i: (i, 0)),
            _const_spec((N_EXPERTS, D_MODEL)), _const_spec((N_EXPERTS, D_MODEL)),
            _const_spec((N_EXPERTS, 128)), _const_spec((t, t)),
        ],
        out_specs=[pl.BlockSpec((TOP_K, t), col), pl.BlockSpec((TOP_K, t), col), pl.BlockSpec((TOP_K, t), col),
                   _const_spec((N_EXPERTS, 128))],
        out_shape=[jax.ShapeDtypeStruct((TOP_K, n), jnp.int32), jax.ShapeDtypeStruct((TOP_K, n), F32),
                   jax.ShapeDtypeStruct((TOP_K, n), jnp.int32), jax.ShapeDtypeStruct((N_EXPERTS, 128), F32)],
        scratch_shapes=[pltpu.VMEM((N_EXPERTS, 128), F32)],
        compiler_params=_params(("arbitrary",)),
        name="router",
    )(h1, wr_hi, wr_lo, rbias, tri)
    return idx, wts, rank, cnt[:, 0].astype(jnp.int32)


def _slot_kernel(ps_ref, idx_ref, rank_ref, wts_ref, dest_ref, wt_ref):
    idx = idx_ref[...]

    def add_expert(e, acc):
        return acc + jnp.where(idx == e, ps_ref[e], 0)

    dest = lax.fori_loop(0, N_EXPERTS, add_expert, rank_ref[...], unroll=8)
    for w in range(dest.shape[1] // SC_WINDOW):
        dest_ref[w] = dest[:, w * SC_WINDOW:(w + 1) * SC_WINDOW]
    wt_ref[...] = wts_ref[...].T


def _slots(pstart, idx, rank, wts):
    n = idx.shape[1]
    t = TILE_SLOTS
    col = lambda i, ps: (0, i)
    return pl.pallas_call(
        _slot_kernel,
        grid_spec=pltpu.PrefetchScalarGridSpec(
            num_scalar_prefetch=1,
            grid=(n // t,),
            in_specs=[pl.BlockSpec((TOP_K, t), col), pl.BlockSpec((TOP_K, t), col), pl.BlockSpec((TOP_K, t), col)],
            out_specs=[pl.BlockSpec((t // SC_WINDOW, TOP_K, SC_WINDOW), lambda i, ps: (i, 0, 0)),
                       pl.BlockSpec((t, TOP_K), lambda i, ps: (i, 0))],
        ),
        out_shape=[jax.ShapeDtypeStruct((n // SC_WINDOW, TOP_K, SC_WINDOW), jnp.int32),
                   jax.ShapeDtypeStruct((n, TOP_K), F32)],
        compiler_params=_params(("parallel",)),
        name="slots",
    )(pstart, idx, rank, wts)


def _expert_kernel(first_ref, count_ref, used_ref, wg_ref, wu_ref, wd_ref, x_hbm, y_hbm,
                   wgu_s, wd_s, xbuf, ybuf, xsem, ysem):
    e = pl.program_id(0)
    rows = MOE_BLOCK * ROW_TILES
    used = used_ref[0]

    def x_copy(b, slot):
        return pltpu.make_async_copy(x_hbm.at[pl.ds(b * rows, rows)], xbuf.at[slot], xsem.at[slot])

    def y_copy(b, slot):
        return pltpu.make_async_copy(ybuf.at[slot], y_hbm.at[pl.ds(b * rows, rows)], ysem.at[slot])

    @pl.when(e == 0)
    def _():
        for b in range(min(X_AHEAD, x_hbm.shape[0] // rows)):
            @pl.when(b < used)
            def _():
                x_copy(b, b).start()

    @pl.when(count_ref[e] > 0)
    def _():
        wgu_s[:, :EXPERT_HIDDEN] = wg_ref[0].astype(BF16)
        wgu_s[:, EXPERT_HIDDEN:] = wu_ref[0].astype(BF16)
        wd_s[...] = wd_ref[0].astype(BF16)

    def run_blocks(b0, g):
        for i in range(g):
            ahead = b0 + i + X_AHEAD

            @pl.when(ahead < used)
            def _():
                x_copy(ahead, ahead % X_SLOTS).start()

        for i in range(g):
            done = b0 + i - Y_SLOTS

            @pl.when(done >= 0)
            def _():
                y_copy(done, done % Y_SLOTS).wait()

        x = []
        for i in range(g):
            xs = (b0 + i) % X_SLOTS
            x_copy(b0 + i, xs).wait()
            x.append(_load_row_contiguous(xbuf.at[xs], MOE_BLOCK).astype(BF16))
        gu = [jnp.dot(xi, wgu_s[...], preferred_element_type=F32) for xi in x]
        hid = [(_silu(gi[:, :EXPERT_HIDDEN]) * gi[:, EXPERT_HIDDEN:]).astype(BF16) for gi in gu]
        y = [jnp.dot(hi, wd_s[...], preferred_element_type=F32) for hi in hid]
        for i in range(g):
            ys = (b0 + i) % Y_SLOTS
            _store_row_contiguous(ybuf.at[ys], y[i])
            y_copy(b0 + i, ys).start()

    first = first_ref[e]
    count = count_ref[e]
    n_groups = lax.div(count, EXPERT_GROUP)

    def group(j, carry):
        run_blocks(first + j * EXPERT_GROUP, EXPERT_GROUP)
        return carry

    lax.fori_loop(0, n_groups, group, 0)
    for left in range(1, EXPERT_GROUP):
        @pl.when(count - n_groups * EXPERT_GROUP == left)
        def _():
            run_blocks(first + n_groups * EXPERT_GROUP, left)

    @pl.when(e == pl.num_programs(0) - 1)
    def _():
        for back in range(1, Y_SLOTS + 1):
            @pl.when(used - back >= 0)
            def _():
                y_copy(used - back, (used - back) % Y_SLOTS).wait()


def _experts(first_block, block_count, n_used, xs, w_gate, w_up, w_down):
    rows = MOE_BLOCK * ROW_TILES
    n_exp = w_gate.shape[0]
    any_spec = pl.BlockSpec(memory_space=pl.ANY)
    return pl.pallas_call(
        _expert_kernel,
        grid_spec=pltpu.PrefetchScalarGridSpec(
            num_scalar_prefetch=3,
            grid=(n_exp,),
            in_specs=[
                pl.BlockSpec((1, D_MODEL, EXPERT_HIDDEN), lambda e, *_: (e, 0, 0)),
                pl.BlockSpec((1, D_MODEL, EXPERT_HIDDEN), lambda e, *_: (e, 0, 0)),
                pl.BlockSpec((1, EXPERT_HIDDEN, D_MODEL), lambda e, *_: (e, 0, 0)),
                any_spec,
            ],
            out_specs=any_spec,
            scratch_shapes=[
                pltpu.VMEM((D_MODEL, 2 * EXPERT_HIDDEN), BF16), pltpu.VMEM((EXPERT_HIDDEN, D_MODEL), BF16),
                pltpu.VMEM((X_SLOTS, rows, LANES), F32), pltpu.VMEM((Y_SLOTS, rows, LANES), F32),
                pltpu.SemaphoreType.DMA((X_SLOTS,)), pltpu.SemaphoreType.DMA((Y_SLOTS,)),
            ],
        ),
        out_shape=jax.ShapeDtypeStruct(xs.shape, F32),
        compiler_params=_params(("arbitrary",)),
        name="experts",
    )(first_block, block_count, n_used, w_gate, w_up, w_down, xs)


def _sc_mesh():
    return plsc.VectorSubcoreMesh(core_axis_name="c", subcore_axis_name="s")


def _sc_dispatch(x3, dest_w, n_rows):
    n = x3.shape[0]

    @functools.partial(pl.kernel, out_type=jax.ShapeDtypeStruct((n_rows, ROW_TILES, LANES), F32), mesh=_sc_mesh(),
                       name="sc_dispatch")
    def run(x_hbm, d_hbm, o_hbm):
        def body(x_vmem, d_vmem):
            for kk in range(TOP_K):
                pltpu.sync_copy(x_vmem, o_hbm.at[d_vmem.at[0].at[kk]])

        pltpu.emit_pipeline(
            body,
            grid=(n // SC_WINDOW,),
            in_specs=[pl.BlockSpec((SC_WINDOW, ROW_TILES, LANES), lambda i: (i, 0, 0)),
                      pl.BlockSpec((1, TOP_K, SC_WINDOW), lambda i: (i, 0, 0))],
            out_specs=[],
            core_axis_name=("c", "s"),
            dimension_semantics=(pltpu.PARALLEL,),
        )(x_hbm, d_hbm)

    return run(x3, dest_w)


def _sc_gather(y3, dest_w):
    nw = dest_w.shape[0]
    n = nw * SC_WINDOW

    @functools.partial(pl.kernel, out_type=jax.ShapeDtypeStruct((TOP_K, n, ROW_TILES, LANES), F32), mesh=_sc_mesh(),
                       name="sc_gather")
    def run(y_hbm, d_hbm, o_hbm):
        def body(d_vmem, g_vmem):
            pltpu.sync_copy(y_hbm.at[d_vmem.at[0].at[0]], g_vmem.at[0])

        pltpu.emit_pipeline(
            body,
            grid=(nw, TOP_K),
            in_specs=[pl.BlockSpec((1, 1, SC_WINDOW), lambda i, k: (i, k, 0))],
            out_specs=[pl.BlockSpec((1, SC_WINDOW, ROW_TILES, LANES), lambda i, k: (k, i, 0, 0))],
            core_axis_name=("c", "s"),
            dimension_semantics=(pltpu.PARALLEL, pltpu.PARALLEL),
        )(d_hbm, o_hbm)

    return run(y3, dest_w)


def _final_kernel(h_ref, g_ref, w_ref, wgu_ref, wd_ref, lg_ref, lb_ref, y_ref):
    t = h_ref.shape[0]
    h = h_ref[...]
    gu = jnp.dot(h.astype(BF16), wgu_ref[...], preferred_element_type=F32)
    hid = (_silu(gu[:, :SHARED_HIDDEN]) * gu[:, SHARED_HIDDEN:]).astype(BF16)
    ffn = jnp.dot(hid, wd_ref[...], preferred_element_type=F32)
    w = w_ref[...]
    for kk in range(TOP_K):
        ffn = ffn + _load_row_contiguous(g_ref.at[kk], t) * w[:, kk:kk + 1]
    y_ref[...] = _layer_norm(DN_ALPHA * h + ffn, lg_ref[...], lb_ref[...])


def _final(h1, gathered, wts_t, w_sh_gu, w_sh_down, ln_g, ln_b, first_token):
    m = gathered.shape[1] // ROW_TILES
    t = TILE_FINAL
    off = first_token // t
    part = lambda i: (i + off, 0)
    return pl.pallas_call(
        _final_kernel,
        grid=(m // t,),
        in_specs=[
            pl.BlockSpec((t, D_MODEL), part),
            pl.BlockSpec((TOP_K, t * ROW_TILES, LANES), lambda i: (0, i, 0)),
            pl.BlockSpec((t, TOP_K), part),
            _const_spec((D_MODEL, 2 * SHARED_HIDDEN)), _const_spec((SHARED_HIDDEN, D_MODEL)),
            _const_spec((1, D_MODEL)), _const_spec((1, D_MODEL)),
        ],
        out_specs=pl.BlockSpec((t, D_MODEL), lambda i: (i, 0)),
        out_shape=jax.ShapeDtypeStruct((m, D_MODEL), F32),
        compiler_params=_params(("parallel",)),
        name="final",
    )(h1, gathered, wts_t, w_sh_gu, w_sh_down, ln_g, ln_b)


def _block_diag(w):
    g, a, b = w.shape
    out = jnp.zeros((g * a, g * b), w.dtype)
    for i in range(g):
        out = out.at[i * a:(i + 1) * a, i * b:(i + 1) * b].set(w[i])
    return out


def kernel(x_prompt, x_sample, ln0_g, ln0_b, w_in, pool_w, pool_scale, w_pool_up, gate_w2_fwd, gate_b_fwd,
           gate_w2_bwd, gate_b_bwd, gla_norm_g, w_gla_up, w_out, ln1_g, ln1_b, w_router, router_bias,
           w_exp_gate, w_exp_up, w_exp_down, w_sh_gate, w_sh_up, w_sh_down, ln2_g, ln2_b):
    bp, lp, d = x_prompt.shape
    bs, ls, _ = x_sample.shape
    n_p, n_s = bp * lp, bs * ls
    n = n_p + n_s
    starts = tuple(b * lp for b in range(bp)) + tuple(n_p + b * ls for b in range(bs))
    ends = tuple((b + 1) * lp for b in range(bp)) + tuple(n_p + (b + 1) * ls for b in range(bs))
    vec = lambda a: a.reshape(1, -1).astype(F32)
    assert d == D_MODEL and w_in.shape[0] == DEPTH
    coarsest = max(TILE_INPROJ, TILE_POOL, TILE_GLA, TILE_ROUTER, TILE_FINAL, SC_WINDOW)
    assert lp % coarsest == 0 and ls % coarsest == 0, "sequence boundaries must fall on tile boundaries"
    assert n % TILE_SLOTS == 0

    w = w_in[0]
    c_zf = POOL_W + 2 * GLA_K_W + 2 * GLA_V_W
    c_gp = c_zf + 2 * GATE_RANK
    w_main = jnp.concatenate([w[:, :c_zf], w[:, c_gp:]], axis=1).astype(BF16)
    w_z = jnp.zeros((d, Z_PAD), F32).at[:, :2 * GATE_RANK].set(w[:, c_zf:c_gp]).astype(BF16)
    w2f = jnp.zeros((Z_PAD, GLA_K_W), F32).at[:GATE_RANK].set(gate_w2_fwd[0]).astype(BF16)
    w2b = jnp.zeros((Z_PAD, GLA_K_W), F32).at[GATE_RANK:2 * GATE_RANK].set(gate_w2_bwd[0]).astype(BF16)

    h, u, q, k, v, og, gp, gg, z = _inproj(x_prompt.reshape(n_p, d), x_sample.reshape(n_s, d),
                                           vec(ln0_g), vec(ln0_b), w_main, w_z)
    pm = _pool(u, gp, _block_diag(pool_w[0]).astype(BF16), vec(pool_scale[0]), w_pool_up[0].astype(BF16),
               starts, ends)
    ob = _gla_bwd(q, k, v, z, w2b, vec(gate_b_bwd[0]), ends)
    h1, h1_rc = _gla_fwd_mix(q, k, v, z, w2f, vec(gate_b_fwd[0]), ob, og, gg, pm, h, vec(gla_norm_g[0]),
                             w_gla_up[0].astype(BF16), w_out[0].astype(BF16), vec(ln1_g[0]), vec(ln1_b[0]), starts)

    wr_t = w_router[0].T
    wr_hi = wr_t.astype(BF16)
    wr_lo = (wr_t - wr_hi.astype(F32)).astype(BF16)
    rbias = jnp.broadcast_to(router_bias[0].astype(F32)[:, None], (N_EXPERTS, 128))
    idx, wts, rank, counts = _router(h1, wr_hi, wr_lo, rbias)

    padded = (counts + MOE_BLOCK - 1) // MOE_BLOCK * MOE_BLOCK
    pend = jnp.cumsum(padded)
    pstart = pend - padded
    n_rows = n * TOP_K + N_EXPERTS * MOE_BLOCK
    first_block = (pstart // MOE_BLOCK).astype(jnp.int32)
    block_count = (padded // MOE_BLOCK).astype(jnp.int32)
    n_used = (pend[-1:] // MOE_BLOCK).astype(jnp.int32)
    dest_w, wts_t = _slots(pstart.astype(jnp.int32), idx, rank, wts)

    xs = _sc_dispatch(h1_rc.reshape(n, ROW_TILES, LANES), dest_w, n_rows)
    ys = _experts(first_block, block_count, n_used, xs.reshape(n_rows * ROW_TILES, LANES),
                  w_exp_gate[0], w_exp_up[0], w_exp_down[0])
    ys = ys.reshape(n_rows, ROW_TILES, LANES)
    w_sh_gu = jnp.concatenate([w_sh_gate[0], w_sh_up[0]], axis=1).astype(BF16)
    w_sh_d = w_sh_down[0].astype(BF16)
    outs = []
    for first, m in ((0, n_p), (n_p, n_s)):
        windows = dest_w[first // SC_WINDOW:(first + m) // SC_WINDOW]
        gathered = _sc_gather(ys, windows).reshape(TOP_K, m * ROW_TILES, LANES)
        outs.append(_final(h1, gathered, wts_t, w_sh_gu, w_sh_d, vec(ln2_g[0]), vec(ln2_b[0]), first))
    return outs[0].reshape(bp, lp, d), outs[1].reshape(bs, ls, d)
```

```python
import functools

import jax
import jax.numpy as jnp
from jax import lax
from jax.experimental import pallas as pl
from jax.experimental.pallas import tpu as pltpu
from jax.experimental.pallas import tpu_sc as plsc

F32 = jnp.float32
BF16 = jnp.bfloat16

D_MODEL = 1024
POOL_GROUPS = 4
POOL_GROUP_W = 128
POOL_W = POOL_GROUPS * POOL_GROUP_W
POOL_WINDOWS = (2, 4, 8, 16)
POOL_HALO = 16
GLA_HEADS = 4
GLA_K_W = 512
GLA_V_W = 1024
GLA_DK = GLA_K_W // GLA_HEADS
GLA_DV = GLA_V_W // GLA_HEADS
GATE_RANK = 16
GATE_TAU = 16.0
GLA_CHUNK = 64
Z_PAD = 128
N_EXPERTS = 256
TOP_K = 8
N_GROUPS = 8
TOPK_GROUPS = 4
EXPERTS_PER_GROUP = N_EXPERTS // N_GROUPS
EXPERT_HIDDEN = 256
SHARED_HIDDEN = 256
ROUTED_SCALE = 2.5
DEPTH = 1
DN_ALPHA = (2 * DEPTH) ** 0.25
LN_EPS = 1e-5
RMS_EPS = 1e-6

VMEM_LIMIT_BYTES = 56 * 1024 * 1024

TILE_INPROJ = 512
TILE_POOL = 512
TILE_GLA = 512
GLA_SCAN_BLOCK = 256
MIX_ROW_PARTS = 2
TILE_ROUTER = 1024
TILE_FINAL = 512
TILE_SLOTS = 2048
MOE_BLOCK = 256
EXPERT_GROUP = 4
X_AHEAD = 5
X_SLOTS = X_AHEAD + EXPERT_GROUP
Y_SLOTS = 3 * EXPERT_GROUP
LANES = 128
ROW_WORDS = D_MODEL // 2
ROW_TILES = ROW_WORDS // LANES
SC_WINDOW = 64


def _params(semantics):
    return pltpu.CompilerParams(dimension_semantics=semantics, vmem_limit_bytes=VMEM_LIMIT_BYTES)


def _const_spec(shape):
    nd = len(shape)
    return pl.BlockSpec(shape, lambda *_: (0,) * nd)


def _layer_norm(x, g, b):
    mu = jnp.mean(x, axis=-1, keepdims=True)
    xc = x - mu
    var = jnp.mean(xc * xc, axis=-1, keepdims=True)
    return xc * lax.rsqrt(var + LN_EPS) * g + b


def _sigmoid(x):
    return 1.0 / (1.0 + jnp.exp(-x))


def _log_sigmoid(x):
    return jnp.minimum(x, 0.0) - jnp.log(1.0 + jnp.exp(-jnp.abs(x)))


def _silu(x):
    return x * _sigmoid(x)


def _tile_hits(i, tile, offsets):
    hit = i < 0
    for off in offsets:
        if off % tile == 0:
            hit = jnp.logical_or(hit, i == off // tile)
    return hit


_MAIN_SPLITS = (POOL_W, GLA_K_W, GLA_K_W, GLA_V_W, GLA_V_W, D_MODEL, D_MODEL)


def _inproj_kernel(xp_ref, xs_ref, g_ref, b_ref, w_ref, wz_ref,
                   h_ref, u_ref, q_ref, k_ref, v_ref, og_ref, gp_ref, gg_ref, z_ref, *, n_prompt_tiles):
    i = pl.program_id(0)
    t = h_ref.shape[0]
    for rows in (slice(0, t // 2), slice(t // 2, t)):
        x = jnp.where(i < n_prompt_tiles, xp_ref[rows, :], xs_ref[rows, :])
        h = _layer_norm(x, g_ref[...], b_ref[...])
        h_ref[rows, :] = h
        hb = h.astype(BF16)
        off = 0
        for ref, width in zip((u_ref, q_ref, k_ref, v_ref, og_ref, gp_ref, gg_ref), _MAIN_SPLITS):
            ref[rows, :] = jnp.dot(hb, w_ref[:, off:off + width], preferred_element_type=F32).astype(BF16)
            off += width
        z_ref[rows, :] = jnp.dot(hb, wz_ref[...], preferred_element_type=F32).astype(BF16)


def _inproj(xp, xs, ln_g, ln_b, w_main, w_z):
    n_p, n_s = xp.shape[0], xs.shape[0]
    n = n_p + n_s
    t = TILE_INPROJ
    npt, nst = n_p // t, n_s // t
    row = lambda i: (i, 0)
    widths = _MAIN_SPLITS + (Z_PAD,)
    out_shape = [jax.ShapeDtypeStruct((n, D_MODEL), F32)] + [jax.ShapeDtypeStruct((n, w), BF16) for w in widths]
    out_specs = [pl.BlockSpec((t, D_MODEL), row)] + [pl.BlockSpec((t, w), row) for w in widths]
    return pl.pallas_call(
        functools.partial(_inproj_kernel, n_prompt_tiles=npt),
        grid=(npt + nst,),
        in_specs=[
            pl.BlockSpec((t, D_MODEL), lambda i: (jnp.minimum(i, npt - 1), 0)),
            pl.BlockSpec((t, D_MODEL), lambda i: (jnp.maximum(i - npt, 0), 0)),
            _const_spec((1, D_MODEL)), _const_spec((1, D_MODEL)),
            _const_spec(w_main.shape), _const_spec(w_z.shape),
        ],
        out_specs=out_specs,
        out_shape=out_shape,
        compiler_params=_params(("parallel",)),
        name="inproj",
    )(xp, xs, ln_g, ln_b, w_main, w_z)


def _pool_kernel(u_ref, up_ref, un_ref, gp_ref, band_ref, pw_ref, sc_ref, wup_ref, pm_ref, *, starts, ends):
    t = u_ref.shape[0]
    i = pl.program_id(0)
    is_start = _tile_hits(i, t, starts)
    is_end = _tile_hits(i + 1, t, ends)
    cur = u_ref[...]
    prev = jnp.where(is_start, jnp.zeros_like(up_ref[...]), up_ref[...])
    nxt = jnp.where(is_end, jnp.zeros_like(un_ref[...]), un_ref[...])
    ext = jnp.concatenate([prev, cur, nxt], axis=0)
    r = lax.broadcasted_iota(jnp.int32, (t, POOL_GROUP_W), 0).astype(F32)
    fs = is_start.astype(F32)
    fe = is_end.astype(F32)
    parts = []
    for gi, w in enumerate(POOL_WINDOWS):
        sl = slice(gi * POOL_GROUP_W, (gi + 1) * POOL_GROUP_W)
        s = jnp.dot(band_ref[gi], ext[:, sl], preferred_element_type=F32)
        cnt = w - fs * jnp.maximum(w // 2 - r, 0.0) - fe * jnp.maximum(r + (w // 2 - t), 0.0)
        parts.append(s / cnt - cur[:, sl].astype(F32))
    p = jnp.concatenate(parts, axis=1).astype(BF16)
    p = (jnp.dot(p, pw_ref[...], preferred_element_type=F32) * sc_ref[...]).astype(BF16)
    pool_out = jnp.dot(p, wup_ref[...], preferred_element_type=F32)
    pm_ref[...] = (_sigmoid(gp_ref[...].astype(F32)) * pool_out).astype(BF16)


def _pool_bands(t):
    r = jnp.arange(t)[:, None]
    c = jnp.arange(t + 2 * POOL_HALO)[None, :] - POOL_HALO
    return jnp.stack([((c >= r - w // 2) & (c <= r + w // 2 - 1)).astype(BF16) for w in POOL_WINDOWS])


def _pool(u, gp, pool_w_bd, pool_scale, w_pool_up, starts, ends):
    n = u.shape[0]
    t = TILE_POOL
    hb = t // POOL_HALO
    last = n // POOL_HALO - 1
    return pl.pallas_call(
        functools.partial(_pool_kernel, starts=starts, ends=ends),
        grid=(n // t,),
        in_specs=[
            pl.BlockSpec((t, POOL_W), lambda i: (i, 0)),
            pl.BlockSpec((POOL_HALO, POOL_W), lambda i: (jnp.maximum(i * hb - 1, 0), 0)),
            pl.BlockSpec((POOL_HALO, POOL_W), lambda i: (jnp.minimum((i + 1) * hb, last), 0)),
            pl.BlockSpec((t, D_MODEL), lambda i: (i, 0)),
            _const_spec((POOL_GROUPS, t, t + 2 * POOL_HALO)),
            _const_spec((POOL_W, POOL_W)), _const_spec((1, POOL_W)), _const_spec((POOL_W, D_MODEL)),
        ],
        out_specs=pl.BlockSpec((t, D_MODEL), lambda i: (i, 0)),
        out_shape=jax.ShapeDtypeStruct((n, D_MODEL), BF16),
        compiler_params=_params(("parallel",)),
        name="pool",
    )(u, u, u, gp, _pool_bands(t), pool_w_bd, pool_scale, w_pool_up)


def _gla_tile(q, k, v, la, tri, st_ref, *, reverse):
    t = q.shape[0]
    c_sz = GLA_CHUNK
    hi = la.astype(BF16)
    r1 = la - hi.astype(F32)
    mid = r1.astype(BF16)
    lo = (r1 - mid.astype(F32)).astype(BF16)
    scan = lambda rs: (jnp.dot(tri, hi[rs], preferred_element_type=F32)
                       + jnp.dot(tri, mid[rs], preferred_element_type=F32)
                       + jnp.dot(tri, lo[rs], preferred_element_type=F32))
    sb = GLA_SCAN_BLOCK
    b = jnp.concatenate([scan(slice(r0, r0 + sb)) for r0 in range(0, t, sb)], axis=0)
    kf = k.astype(F32)
    qd = (q.astype(F32) * jnp.exp(b) * (GLA_DK ** -0.5)).astype(BF16)
    kd = (kf * jnp.exp(-b)).astype(BF16)
    row = lax.broadcasted_iota(jnp.int32, (c_sz, c_sz), 0)
    col = lax.broadcasted_iota(jnp.int32, (c_sz, c_sz), 1)
    mask = (col >= row) if reverse else (col <= row)
    n_chunks = t // c_sz
    order = list(range(n_chunks - 1, -1, -1) if reverse else range(n_chunks))
    nt = (((1,), (1,)), ((), ()))
    tn = (((0,), (0,)), ((), ()))
    rows = [slice(c * c_sz, (c + 1) * c_sz) for c in range(n_chunks)]
    lanes = [slice(h * GLA_DK, (h + 1) * GLA_DK) for h in range(GLA_HEADS)]
    vcols = [slice(h * GLA_DV, (h + 1) * GLA_DV) for h in range(GLA_HEADS)]

    att, kv, dec = {}, {}, {}
    for c in order:
        b_c = b[rows[c]]
        b_edge = b_c[0:1] if reverse else b_c[c_sz - 1:c_sz]
        k2 = (kf[rows[c]] * jnp.exp(b_edge - b_c)).astype(BF16)
        dec[c] = jnp.exp(b_edge)
        for h in range(GLA_HEADS):
            att[c, h] = lax.dot_general(qd[rows[c], lanes[h]], kd[rows[c], lanes[h]], nt, preferred_element_type=F32)
            kv[c, h] = lax.dot_general(v[rows[c], vcols[h]], k2[:, lanes[h]], tn, preferred_element_type=F32)

    state_before = {}
    for h in range(GLA_HEADS):
        st = st_ref[h]
        for c in order:
            state_before[c, h] = st.astype(BF16)
            st = st * dec[c][:, lanes[h]] + kv[c, h]
        st_ref[h] = st

    outs = []
    for c in range(n_chunks):
        heads = []
        for h in range(GLA_HEADS):
            a = jnp.where(mask, att[c, h], 0.0).astype(BF16)
            o = jnp.dot(a, v[rows[c], vcols[h]], preferred_element_type=F32)
            heads.append(o + lax.dot_general(qd[rows[c], lanes[h]], state_before[c, h], nt,
                                             preferred_element_type=F32))
        outs.append(jnp.concatenate(heads, axis=1))
    return jnp.concatenate(outs, axis=0)


def _log_decay(z, w2_ref, gb_ref):
    return _log_sigmoid(jnp.dot(z, w2_ref[...], preferred_element_type=F32) + gb_ref[...]) * (1.0 / GATE_TAU)


def _gla_bwd_kernel(q_ref, k_ref, v_ref, z_ref, w2_ref, gb_ref, tri_ref, ob_ref, st_ref, *, n_tiles, ends):
    t = q_ref.shape[0]
    j = n_tiles - 1 - pl.program_id(0)

    @pl.when(_tile_hits(j + 1, t, ends))
    def _():
        st_ref[...] = jnp.zeros_like(st_ref)

    la = _log_decay(z_ref[...], w2_ref, gb_ref)
    ob_ref[...] = _gla_tile(q_ref[...], k_ref[...], v_ref[...], la, tri_ref[...], st_ref, reverse=True)


def _gla_fwd_kernel(q_ref, k_ref, v_ref, z_ref, w2_ref, gb_ref, tri_ref, ob_ref, og_ref, gg_ref, pm_ref, h_ref,
                    ng_ref, wgu_ref, wo_ref, l1g_ref, l1b_ref, h1_ref, h1rc_ref, st_ref, *, starts):
    t = q_ref.shape[0]
    i = pl.program_id(0)

    @pl.when(_tile_hits(i, t, starts))
    def _():
        st_ref[...] = jnp.zeros_like(st_ref)

    la = _log_decay(z_ref[...], w2_ref, gb_ref)
    o_all = _gla_tile(q_ref[...], k_ref[...], v_ref[...], la, tri_ref[...], st_ref, reverse=False)
    tp = t // MIX_ROW_PARTS
    for p in range(MIX_ROW_PARTS):
        rows = slice(p * tp, (p + 1) * tp)
        o = o_all[rows] + ob_ref[rows, :]
        heads = []
        for h in range(GLA_HEADS):
            oh = o[:, h * GLA_DV:(h + 1) * GLA_DV]
            heads.append(oh * lax.rsqrt(jnp.mean(oh * oh, axis=-1, keepdims=True) + RMS_EPS))
        o = jnp.concatenate(heads, axis=1) * ng_ref[...]
        o = (o * _silu(og_ref[rows, :].astype(F32))).astype(BF16)
        gla_out = jnp.dot(o, wgu_ref[...], preferred_element_type=F32)
        merged = _sigmoid(gg_ref[rows, :].astype(F32)) * gla_out + pm_ref[rows, :].astype(F32)
        mix = jnp.dot(merged.astype(BF16), wo_ref[...], preferred_element_type=F32)
        h1 = _layer_norm(DN_ALPHA * h_ref[rows, :] + mix, l1g_ref[...], l1b_ref[...])
        h1_ref[rows, :] = h1
        _store_row_contiguous(h1rc_ref.at[pl.ds(p * tp * ROW_TILES, tp * ROW_TILES)], h1)


def _store_row_contiguous(ref, x):
    t = x.shape[0]
    bits = pltpu.bitcast(x.astype(BF16).astype(F32), jnp.uint32)
    words = (bits[:, :ROW_WORDS] >> 16) | (bits[:, ROW_WORDS:] & jnp.uint32(0xFFFF0000))
    words = pltpu.bitcast(words, F32)
    for s in range(ROW_TILES):
        ref[pl.ds(s, t, stride=ROW_TILES), :] = words[:, s * LANES:(s + 1) * LANES]


def _load_row_contiguous(ref, t):
    words = jnp.concatenate([ref[pl.ds(s, t, stride=ROW_TILES), :] for s in range(ROW_TILES)], axis=1)
    bits = pltpu.bitcast(words, jnp.uint32)
    lo = pltpu.bitcast(bits << 16, F32)
    hi = pltpu.bitcast(bits & jnp.uint32(0xFFFF0000), F32)
    return jnp.concatenate([lo, hi], axis=1)


def _chunk_tri(t, reverse):
    r = jnp.arange(t)[:, None]
    c = jnp.arange(t)[None, :]
    same = (r // GLA_CHUNK) == (c // GLA_CHUNK)
    return (same & ((c >= r) if reverse else (c <= r))).astype(BF16)


def _gla_bwd(q, k, v, z, w2b_pad, gate_b, ends):
    n = q.shape[0]
    t = TILE_GLA
    nt = n // t
    rev = lambda i: (nt - 1 - i, 0)
    return pl.pallas_call(
        functools.partial(_gla_bwd_kernel, n_tiles=nt, ends=ends),
        grid=(nt,),
        in_specs=[
            pl.BlockSpec((t, GLA_K_W), rev), pl.BlockSpec((t, GLA_K_W), rev), pl.BlockSpec((t, GLA_V_W), rev),
            pl.BlockSpec((t, Z_PAD), rev),
            _const_spec((Z_PAD, GLA_K_W)), _const_spec((1, GLA_K_W)),
            _const_spec((GLA_SCAN_BLOCK, GLA_SCAN_BLOCK)),
        ],
        out_specs=pl.BlockSpec((t, GLA_V_W), rev),
        out_shape=jax.ShapeDtypeStruct((n, GLA_V_W), F32),
        scratch_shapes=[pltpu.VMEM((GLA_HEADS, GLA_DV, GLA_DK), F32)],
        compiler_params=_params(("arbitrary",)),
        name="gla_bwd",
    )(q, k, v, z, w2b_pad, gate_b, _chunk_tri(GLA_SCAN_BLOCK, True))


def _gla_fwd_mix(q, k, v, z, w2f_pad, gate_b, ob, og, gg, pm, h, norm_g, w_gla_up, w_out, ln1_g, ln1_b, starts):
    n = q.shape[0]
    t = TILE_GLA
    row = lambda i: (i, 0)
    return pl.pallas_call(
        functools.partial(_gla_fwd_kernel, starts=starts),
        grid=(n // t,),
        in_specs=[
            pl.BlockSpec((t, GLA_K_W), row), pl.BlockSpec((t, GLA_K_W), row), pl.BlockSpec((t, GLA_V_W), row),
            pl.BlockSpec((t, Z_PAD), row),
            _const_spec((Z_PAD, GLA_K_W)), _const_spec((1, GLA_K_W)),
            _const_spec((GLA_SCAN_BLOCK, GLA_SCAN_BLOCK)),
            pl.BlockSpec((t, GLA_V_W), row), pl.BlockSpec((t, GLA_V_W), row), pl.BlockSpec((t, D_MODEL), row),
            pl.BlockSpec((t, D_MODEL), row), pl.BlockSpec((t, D_MODEL), row),
            _const_spec((1, GLA_V_W)), _const_spec((GLA_V_W, D_MODEL)), _const_spec((D_MODEL, D_MODEL)),
            _const_spec((1, D_MODEL)), _const_spec((1, D_MODEL)),
        ],
        out_specs=[pl.BlockSpec((t, D_MODEL), row), pl.BlockSpec((t * ROW_TILES, LANES), row)],
        out_shape=[jax.ShapeDtypeStruct((n, D_MODEL), F32), jax.ShapeDtypeStruct((n * ROW_TILES, LANES), F32)],
        scratch_shapes=[pltpu.VMEM((GLA_HEADS, GLA_DV, GLA_DK), F32)],
        compiler_params=_params(("arbitrary",)),
        name="gla_fwd_mix",
    )(q, k, v, z, w2f_pad, gate_b, _chunk_tri(GLA_SCAN_BLOCK, False), ob, og, gg, pm, h, norm_g, w_gla_up, w_out,
      ln1_g, ln1_b)


def _router_kernel(h_ref, wrh_ref, wrl_ref, rb_ref, tri_ref, idx_ref, wts_ref, rank_ref, cnt_ref, base_ref):
    t = h_ref.shape[0]

    @pl.when(pl.program_id(0) == 0)
    def _():
        base_ref[...] = jnp.zeros_like(base_ref)

    h = h_ref[...]
    hh = h.astype(BF16)
    hl = (h - hh.astype(F32)).astype(BF16)
    nt = (((1,), (1,)), ((), ()))
    logits = (lax.dot_general(wrh_ref[...], hh, nt, preferred_element_type=F32)
              + lax.dot_general(wrl_ref[...], hh, nt, preferred_element_type=F32)
              + lax.dot_general(wrh_ref[...], hl, nt, preferred_element_type=F32))
    scores = _sigmoid(logits)
    sel = scores + rb_ref[:, 0:1]
    neg = jnp.float32(-jnp.inf)

    sub = lax.broadcasted_iota(jnp.int32, (EXPERTS_PER_GROUP, t), 0)
    gscore = []
    for g in range(N_GROUPS):
        sg = sel[g * EXPERTS_PER_GROUP:(g + 1) * EXPERTS_PER_GROUP]
        m1 = jnp.max(sg, axis=0, keepdims=True)
        first = jnp.min(jnp.where(sg == m1, sub, EXPERTS_PER_GROUP), axis=0, keepdims=True)
        m2 = jnp.max(jnp.where(sub == first, neg, sg), axis=0, keepdims=True)
        gscore.append(m1 + m2)
    keep = []
    for g in range(N_GROUPS):
        beaten = jnp.zeros((1, t), jnp.int32)
        for g2 in range(N_GROUPS):
            if g2 == g:
                continue
            wins = (gscore[g2] >= gscore[g]) if g2 < g else (gscore[g2] > gscore[g])
            beaten = beaten + wins.astype(jnp.int32)
        keep.append(beaten < TOPK_GROUPS)
    slot = []
    ahead = jnp.zeros((1, t), jnp.int32)
    for g in range(N_GROUPS):
        slot.append([jnp.logical_and(keep[g], ahead == j) for j in range(TOPK_GROUPS)])
        ahead = ahead + keep[g].astype(jnp.int32)
    groups_of = lambda j: range(j, j + N_GROUPS - TOPK_GROUPS + 1)
    slots_of = lambda g: range(max(0, g - (N_GROUPS - TOPK_GROUPS)), min(TOPK_GROUPS - 1, g) + 1)
    rows_of = lambda x, g: x[g * EXPERTS_PER_GROUP:(g + 1) * EXPERTS_PER_GROUP]

    def compact(x, fill):
        parts = []
        for j in range(TOPK_GROUPS):
            acc = jnp.full((EXPERTS_PER_GROUP, t), fill, x.dtype)
            for g in groups_of(j):
                acc = jnp.where(slot[g][j], rows_of(x, g), acc)
            parts.append(acc)
        return jnp.concatenate(parts, axis=0)

    cand = compact(sel, neg)
    kept_scores = compact(scores, 0.0)
    gid = []
    for j in range(TOPK_GROUPS):
        acc = jnp.zeros((1, t), jnp.int32)
        for g in groups_of(j):
            acc = jnp.where(slot[g][j], g, acc)
        gid.append(acc * EXPERTS_PER_GROUP + sub)
    cid = jnp.concatenate(gid, axis=0)

    picked = []
    wsum = jnp.zeros((1, t), F32)
    for kk in range(TOP_K):
        m = jnp.max(cand, axis=0, keepdims=True)
        e_k = jnp.min(jnp.where(cand == m, cid, N_EXPERTS), axis=0, keepdims=True)
        hit = cid == e_k
        w_k = jnp.sum(jnp.where(hit, kept_scores, 0.0), axis=0, keepdims=True)
        cand = jnp.where(hit, neg, cand)
        wsum = wsum + w_k
        picked.append((e_k, w_k))
        idx_ref[kk:kk + 1, :] = e_k

    chosen = jnp.where(cand == neg, 1.0, 0.0)
    full = []
    for g in range(N_GROUPS):
        acc = jnp.zeros((EXPERTS_PER_GROUP, t), F32)
        for j in slots_of(g):
            acc = jnp.where(slot[g][j], rows_of(chosen, j), acc)
        full.append(acc)
    a16 = jnp.concatenate(full, axis=0).astype(BF16)

    before = jnp.dot(a16, tri_ref[...], preferred_element_type=F32)
    base = base_ref[...]
    before = compact(before + jnp.concatenate([base] * (t // 128), axis=1), 0.0)
    inv = ROUTED_SCALE / wsum
    for kk, (e_k, w_k) in enumerate(picked):
        wts_ref[kk:kk + 1, :] = w_k * inv
        rank_ref[kk:kk + 1, :] = jnp.sum(jnp.where(cid == e_k, before, 0.0), axis=0, keepdims=True).astype(jnp.int32)
    total = base + jnp.dot(a16, jnp.ones((t, 128), BF16), preferred_element_type=F32)
    base_ref[...] = total
    cnt_ref[...] = total


def _router(h1, wr_hi, wr_lo, rbias):
    n = h1.shape[0]
    t = TILE_ROUTER
    r = jnp.arange(t)
    tri = (r[:, None] < r[None, :]).astype(BF16)
    col = lambda i: (0, i)
    idx, wts, rank, cnt = pl.pallas_call(
        _router_kernel,
        grid=(n // t,),
        in_specs=[
            pl.BlockSpec((t, D_MODEL), lambda i: (i, 0)),
            _const_spec((N_EXPERTS, D_MODEL)), _const_spec((N_EXPERTS, D_MODEL)),
            _const_spec((N_EXPERTS, 128)), _const_spec((t, t)),
        ],
        out_specs=[pl.BlockSpec((TOP_K, t), col), pl.BlockSpec((TOP_K, t), col), pl.BlockSpec((TOP_K, t), col),
                   _const_spec((N_EXPERTS, 128))],
        out_shape=[jax.ShapeDtypeStruct((TOP_K, n), jnp.int32), jax.ShapeDtypeStruct((TOP_K, n), F32),
                   jax.ShapeDtypeStruct((TOP_K, n), jnp.int32), jax.ShapeDtypeStruct((N_EXPERTS, 128), F32)],
        scratch_shapes=[pltpu.VMEM((N_EXPERTS, 128), F32)],
        compiler_params=_params(("arbitrary",)),
        name="router",
    )(h1, wr_hi, wr_lo, rbias, tri)
    return idx, wts, rank, cnt[:, 0].astype(jnp.int32)


def _slot_kernel(ps_ref, idx_ref, rank_ref, wts_ref, dest_ref, wt_ref):
    idx = idx_ref[...]

    def add_expert(e, acc):
        return acc + jnp.where(idx == e, ps_ref[e], 0)

    dest = lax.fori_loop(0, N_EXPERTS, add_expert, rank_ref[...], unroll=8)
    for w in range(dest.shape[1] // SC_WINDOW):
        dest_ref[w] = dest[:, w * SC_WINDOW:(w + 1) * SC_WINDOW]
    wt_ref[...] = wts_ref[...].T


def _slots(pstart, idx, rank, wts):
    n = idx.shape[1]
    t = TILE_SLOTS
    col = lambda i, ps: (0, i)
    return pl.pallas_call(
        _slot_kernel,
        grid_spec=pltpu.PrefetchScalarGridSpec(
            num_scalar_prefetch=1,
            grid=(n // t,),
            in_specs=[pl.BlockSpec((TOP_K, t), col), pl.BlockSpec((TOP_K, t), col), pl.BlockSpec((TOP_K, t), col)],
            out_specs=[pl.BlockSpec((t // SC_WINDOW, TOP_K, SC_WINDOW), lambda i, ps: (i, 0, 0)),
                       pl.BlockSpec((t, TOP_K), lambda i, ps: (i, 0))],
        ),
        out_shape=[jax.ShapeDtypeStruct((n // SC_WINDOW, TOP_K, SC_WINDOW), jnp.int32),
                   jax.ShapeDtypeStruct((n, TOP_K), F32)],
        compiler_params=_params(("parallel",)),
        name="slots",
    )(pstart, idx, rank, wts)


def _expert_kernel(first_ref, count_ref, used_ref, wg_ref, wu_ref, wd_ref, x_hbm, y_hbm,
                   wgu_s, wd_s, xbuf, ybuf, xsem, ysem):
    e = pl.program_id(0)
    rows = MOE_BLOCK * ROW_TILES
    used = used_ref[0]

    def x_copy(b, slot):
        return pltpu.make_async_copy(x_hbm.at[pl.ds(b * rows, rows)], xbuf.at[slot], xsem.at[slot])

    def y_copy(b, slot):
        return pltpu.make_async_copy(ybuf.at[slot], y_hbm.at[pl.ds(b * rows, rows)], ysem.at[slot])

    @pl.when(e == 0)
    def _():
        for b in range(min(X_AHEAD, x_hbm.shape[0] // rows)):
            @pl.when(b < used)
            def _():
                x_copy(b, b).start()

    @pl.when(count_ref[e] > 0)
    def _():
        wgu_s[:, :EXPERT_HIDDEN] = wg_ref[0].astype(BF16)
        wgu_s[:, EXPERT_HIDDEN:] = wu_ref[0].astype(BF16)
        wd_s[...] = wd_ref[0].astype(BF16)

    def run_blocks(b0, g):
        for i in range(g):
            ahead = b0 + i + X_AHEAD

            @pl.when(ahead < used)
            def _():
                x_copy(ahead, ahead % X_SLOTS).start()

        for i in range(g):
            done = b0 + i - Y_SLOTS

            @pl.when(done >= 0)
            def _():
                y_copy(done, done % Y_SLOTS).wait()

        x = []
        for i in range(g):
            xs = (b0 + i) % X_SLOTS
            x_copy(b0 + i, xs).wait()
            x.append(_load_row_contiguous(xbuf.at[xs], MOE_BLOCK).astype(BF16))
        gu = [jnp.dot(xi, wgu_s[...], preferred_element_type=F32) for xi in x]
        hid = [(_silu(gi[:, :EXPERT_HIDDEN]) * gi[:, EXPERT_HIDDEN:]).astype(BF16) for gi in gu]
        y = [jnp.dot(hi, wd_s[...], preferred_element_type=F32) for hi in hid]
        for i in range(g):
            ys = (b0 + i) % Y_SLOTS
            _store_row_contiguous(ybuf.at[ys], y[i])
            y_copy(b0 + i, ys).start()

    first = first_ref[e]
    count = count_ref[e]
    n_groups = lax.div(count, EXPERT_GROUP)

    def group(j, carry):
        run_blocks(first + j * EXPERT_GROUP, EXPERT_GROUP)
        return carry

    lax.fori_loop(0, n_groups, group, 0)
    for left in range(1, EXPERT_GROUP):
        @pl.when(count - n_groups * EXPERT_GROUP == left)
        def _():
            run_blocks(first + n_groups * EXPERT_GROUP, left)

    @pl.when(e == pl.num_programs(0) - 1)
    def _():
        for back in range(1, Y_SLOTS + 1):
            @pl.when(used - back >= 0)
            def _():
                y_copy(used - back, (used - back) % Y_SLOTS).wait()


def _experts(first_block, block_count, n_used, xs, w_gate, w_up, w_down):
    rows = MOE_BLOCK * ROW_TILES
    n_exp = w_gate.shape[0]
    any_spec = pl.BlockSpec(memory_space=pl.ANY)
    return pl.pallas_call(
        _expert_kernel,
        grid_spec=pltpu.PrefetchScalarGridSpec(
            num_scalar_prefetch=3,
            grid=(n_exp,),
            in_specs=[
                pl.BlockSpec((1, D_MODEL, EXPERT_HIDDEN), lambda e, *_: (e, 0, 0)),
                pl.BlockSpec((1, D_MODEL, EXPERT_HIDDEN), lambda e, *_: (e, 0, 0)),
                pl.BlockSpec((1, EXPERT_HIDDEN, D_MODEL), lambda e, *_: (e, 0, 0)),
                any_spec,
            ],
            out_specs=any_spec,
            scratch_shapes=[
                pltpu.VMEM((D_MODEL, 2 * EXPERT_HIDDEN), BF16), pltpu.VMEM((EXPERT_HIDDEN, D_MODEL), BF16),
                pltpu.VMEM((X_SLOTS, rows, LANES), F32), pltpu.VMEM((Y_SLOTS, rows, LANES), F32),
                pltpu.SemaphoreType.DMA((X_SLOTS,)), pltpu.SemaphoreType.DMA((Y_SLOTS,)),
            ],
        ),
        out_shape=jax.ShapeDtypeStruct(xs.shape, F32),
        compiler_params=_params(("arbitrary",)),
        name="experts",
    )(first_block, block_count, n_used, w_gate, w_up, w_down, xs)


def _sc_mesh():
    return plsc.VectorSubcoreMesh(core_axis_name="c", subcore_axis_name="s")


def _sc_dispatch(x3, dest_w, n_rows):
    n = x3.shape[0]

    @functools.partial(pl.kernel, out_type=jax.ShapeDtypeStruct((n_rows, ROW_TILES, LANES), F32), mesh=_sc_mesh(),
                       name="sc_dispatch")
    def run(x_hbm, d_hbm, o_hbm):
        def body(x_vmem, d_vmem):
            for kk in range(TOP_K):
                pltpu.sync_copy(x_vmem, o_hbm.at[d_vmem.at[0].at[kk]])

        pltpu.emit_pipeline(
            body,
            grid=(n // SC_WINDOW,),
            in_specs=[pl.BlockSpec((SC_WINDOW, ROW_TILES, LANES), lambda i: (i, 0, 0)),
                      pl.BlockSpec((1, TOP_K, SC_WINDOW), lambda i: (i, 0, 0))],
            out_specs=[],
            core_axis_name=("c", "s"),
            dimension_semantics=(pltpu.PARALLEL,),
        )(x_hbm, d_hbm)

    return run(x3, dest_w)


def _sc_gather(y3, dest_w):
    nw = dest_w.shape[0]
    n = nw * SC_WINDOW

    @functools.partial(pl.kernel, out_type=jax.ShapeDtypeStruct((TOP_K, n, ROW_TILES, LANES), F32), mesh=_sc_mesh(),
                       name="sc_gather")
    def run(y_hbm, d_hbm, o_hbm):
        def body(d_vmem, g_vmem):
            pltpu.sync_copy(y_hbm.at[d_vmem.at[0].at[0]], g_vmem.at[0])

        pltpu.emit_pipeline(
            body,
            grid=(nw, TOP_K),
            in_specs=[pl.BlockSpec((1, 1, SC_WINDOW), lambda i, k: (i, k, 0))],
            out_specs=[pl.BlockSpec((1, SC_WINDOW, ROW_TILES, LANES), lambda i, k: (k, i, 0, 0))],
            core_axis_name=("c", "s"),
            dimension_semantics=(pltpu.PARALLEL, pltpu.PARALLEL),
        )(d_hbm, o_hbm)

    return run(y3, dest_w)


def _final_kernel(h_ref, g_ref, w_ref, wgu_ref, wd_ref, lg_ref, lb_ref, y_ref):
    t = h_ref.shape[0]
    h = h_ref[...]
    gu = jnp.dot(h.astype(BF16), wgu_ref[...], preferred_element_type=F32)
    hid = (_silu(gu[:, :SHARED_HIDDEN]) * gu[:, SHARED_HIDDEN:]).astype(BF16)
    ffn = jnp.dot(hid, wd_ref[...], preferred_element_type=F32)
    w = w_ref[...]
    for kk in range(TOP_K):
        ffn = ffn + _load_row_contiguous(g_ref.at[kk], t) * w[:, kk:kk + 1]
    y_ref[...] = _layer_norm(DN_ALPHA * h + ffn, lg_ref[...], lb_ref[...])


def _final(h1, gathered, wts_t, w_sh_gu, w_sh_down, ln_g, ln_b, first_token):
    m = gathered.shape[1] // ROW_TILES
    t = TILE_FINAL
    off = first_token // t
    part = lambda i: (i + off, 0)
    return pl.pallas_call(
        _final_kernel,
        grid=(m // t,),
        in_specs=[
            pl.BlockSpec((t, D_MODEL), part),
            pl.BlockSpec((TOP_K, t * ROW_TILES, LANES), lambda i: (0, i, 0)),
            pl.BlockSpec((t, TOP_K), part),
            _const_spec((D_MODEL, 2 * SHARED_HIDDEN)), _const_spec((SHARED_HIDDEN, D_MODEL)),
            _const_spec((1, D_MODEL)), _const_spec((1, D_MODEL)),
        ],
        out_specs=pl.BlockSpec((t, D_MODEL), lambda i: (i, 0)),
        out_shape=jax.ShapeDtypeStruct((m, D_MODEL), F32),
        compiler_params=_params(("parallel",)),
        name="final",
    )(h1, gathered, wts_t, w_sh_gu, w_sh_down, ln_g, ln_b)


def _block_diag(w):
    g, a, b = w.shape
    out = jnp.zeros((g * a, g * b), w.dtype)
    for i in range(g):
        out = out.at[i * a:(i + 1) * a, i * b:(i + 1) * b].set(w[i])
    return out


def kernel(x_prompt, x_sample, ln0_g, ln0_b, w_in, pool_w, pool_scale, w_pool_up, gate_w2_fwd, gate_b_fwd,
           gate_w2_bwd, gate_b_bwd, gla_norm_g, w_gla_up, w_out, ln1_g, ln1_b, w_router, router_bias,
           w_exp_gate, w_exp_up, w_exp_down, w_sh_gate, w_sh_up, w_sh_down, ln2_g, ln2_b):
    bp, lp, d = x_prompt.shape
    bs, ls, _ = x_sample.shape
    n_p, n_s = bp * lp, bs * ls
    n = n_p + n_s
    starts = tuple(b * lp for b in range(bp)) + tuple(n_p + b * ls for b in range(bs))
    ends = tuple((b + 1) * lp for b in range(bp)) + tuple(n_p + (b + 1) * ls for b in range(bs))
    vec = lambda a: a.reshape(1, -1).astype(F32)
    assert d == D_MODEL and w_in.shape[0] == DEPTH
    coarsest = max(TILE_INPROJ, TILE_POOL, TILE_GLA, TILE_FINAL, SC_WINDOW)
    assert lp % coarsest == 0 and ls % coarsest == 0, "sequence boundaries must fall on tile boundaries"
    assert n % TILE_SLOTS == 0 and n % TILE_ROUTER == 0

    w = w_in[0]
    c_zf = POOL_W + 2 * GLA_K_W + 2 * GLA_V_W
    c_gp = c_zf + 2 * GATE_RANK
    w_main = jnp.concatenate([w[:, :c_zf], w[:, c_gp:]], axis=1).astype(BF16)
    w_z = jnp.zeros((d, Z_PAD), F32).at[:, :2 * GATE_RANK].set(w[:, c_zf:c_gp]).astype(BF16)
    w2f = jnp.zeros((Z_PAD, GLA_K_W), F32).at[:GATE_RANK].set(gate_w2_fwd[0]).astype(BF16)
    w2b = jnp.zeros((Z_PAD, GLA_K_W), F32).at[GATE_RANK:2 * GATE_RANK].set(gate_w2_bwd[0]).astype(BF16)

    h, u, q, k, v, og, gp, gg, z = _inproj(x_prompt.reshape(n_p, d), x_sample.reshape(n_s, d),
                                           vec(ln0_g), vec(ln0_b), w_main, w_z)
    pm = _pool(u, gp, _block_diag(pool_w[0]).astype(BF16), vec(pool_scale[0]), w_pool_up[0].astype(BF16),
               starts, ends)
    ob = _gla_bwd(q, k, v, z, w2b, vec(gate_b_bwd[0]), ends)
    h1, h1_rc = _gla_fwd_mix(q, k, v, z, w2f, vec(gate_b_fwd[0]), ob, og, gg, pm, h, vec(gla_norm_g[0]),
                             w_gla_up[0].astype(BF16), w_out[0].astype(BF16), vec(ln1_g[0]), vec(ln1_b[0]), starts)

    wr_t = w_router[0].T
    wr_hi = wr_t.astype(BF16)
    wr_lo = (wr_t - wr_hi.astype(F32)).astype(BF16)
    rbias = jnp.broadcast_to(router_bias[0].astype(F32)[:, None], (N_EXPERTS, 128))
    idx, wts, rank, counts = _router(h1, wr_hi, wr_lo, rbias)

    padded = (counts + MOE_BLOCK - 1) // MOE_BLOCK * MOE_BLOCK
    pend = jnp.cumsum(padded)
    pstart = pend - padded
    n_rows = n * TOP_K + N_EXPERTS * MOE_BLOCK
    first_block = (pstart // MOE_BLOCK).astype(jnp.int32)
    block_count = (padded // MOE_BLOCK).astype(jnp.int32)
    n_used = (pend[-1:] // MOE_BLOCK).astype(jnp.int32)
    dest_w, wts_t = _slots(pstart.astype(jnp.int32), idx, rank, wts)

    xs = _sc_dispatch(h1_rc.reshape(n, ROW_TILES, LANES), dest_w, n_rows)
    ys = _experts(first_block, block_count, n_used, xs.reshape(n_rows * ROW_TILES, LANES),
                  w_exp_gate[0], w_exp_up[0], w_exp_down[0])
    ys = ys.reshape(n_rows, ROW_TILES, LANES)
    w_sh_gu = jnp.concatenate([w_sh_gate[0], w_sh_up[0]], axis=1).astype(BF16)
    w_sh_d = w_sh_down[0].astype(BF16)
    outs = []
    for first, m in ((0, n_p), (n_p, n_s)):
        windows = dest_w[first // SC_WINDOW:(first + m) // SC_WINDOW]
        gathered = _sc_gather(ys, windows).reshape(TOP_K, m * ROW_TILES, LANES)
        outs.append(_final(h1, gathered, wts_t, w_sh_gu, w_sh_d, vec(ln2_g[0]), vec(ln2_b[0]), first))
    return outs[0].reshape(bp, lp, d), outs[1].reshape(bs, ls, d)
```

```python
import functools

import jax
import jax.numpy as jnp
from jax import lax
from jax.experimental import pallas as pl
from jax.experimental.pallas import tpu as pltpu
from jax.experimental.pallas import tpu_sc as plsc

F32 = jnp.float32
BF16 = jnp.bfloat16

D_MODEL = 1024
POOL_GROUPS = 4
POOL_GROUP_W = 128
POOL_W = POOL_GROUPS * POOL_GROUP_W
POOL_WINDOWS = (2, 4, 8, 16)
POOL_HALO = 16
GLA_HEADS = 4
GLA_K_W = 512
GLA_V_W = 1024
GLA_DK = GLA_K_W // GLA_HEADS
GLA_DV = GLA_V_W // GLA_HEADS
GATE_RANK = 16
GATE_TAU = 16.0
GLA_CHUNK = 64
Z_PAD = 128
N_EXPERTS = 256
TOP_K = 8
N_GROUPS = 8
TOPK_GROUPS = 4
EXPERTS_PER_GROUP = N_EXPERTS // N_GROUPS
EXPERT_HIDDEN = 256
SHARED_HIDDEN = 256
ROUTED_SCALE = 2.5
DEPTH = 1
DN_ALPHA = (2 * DEPTH) ** 0.25
LN_EPS = 1e-5
RMS_EPS = 1e-6

VMEM_LIMIT_BYTES = 56 * 1024 * 1024

TILE_INPROJ = 512
TILE_POOL = 512
TILE_GLA = 512
GLA_SCAN_BLOCK = 256
MIX_ROW_PARTS = 2
TILE_ROUTER = 1024
TILE_FINAL = 512
TILE_SLOTS = 2048
MOE_BLOCK = 256
EXPERT_GROUP = 4
X_AHEAD = 5
X_SLOTS = X_AHEAD + EXPERT_GROUP
Y_SLOTS = 3 * EXPERT_GROUP
Y_DMA_PRIORITY = 1
LANES = 128
ROW_WORDS = D_MODEL // 2
ROW_TILES = ROW_WORDS // LANES
SC_WINDOW = 64


def _params(semantics):
    return pltpu.CompilerParams(dimension_semantics=semantics, vmem_limit_bytes=VMEM_LIMIT_BYTES)


def _const_spec(shape):
    nd = len(shape)
    return pl.BlockSpec(shape, lambda *_: (0,) * nd)


def _layer_norm(x, g, b):
    mu = jnp.mean(x, axis=-1, keepdims=True)
    xc = x - mu
    var = jnp.mean(xc * xc, axis=-1, keepdims=True)
    return xc * lax.rsqrt(var + LN_EPS) * g + b


def _sigmoid(x):
    return 1.0 / (1.0 + jnp.exp(-x))


def _log_sigmoid(x):
    return jnp.minimum(x, 0.0) - jnp.log(1.0 + jnp.exp(-jnp.abs(x)))


def _silu(x):
    return x * _sigmoid(x)


def _tile_hits(i, tile, offsets):
    hit = i < 0
    for off in offsets:
        if off % tile == 0:
            hit = jnp.logical_or(hit, i == off // tile)
    return hit


_MAIN_SPLITS = (POOL_W, GLA_K_W, GLA_K_W, GLA_V_W, GLA_V_W, D_MODEL, D_MODEL)


def _inproj_kernel(xp_ref, xs_ref, g_ref, b_ref, w_ref, wz_ref,
                   h_ref, u_ref, q_ref, k_ref, v_ref, og_ref, gp_ref, gg_ref, z_ref, *, n_prompt_tiles):
    i = pl.program_id(0)
    t = h_ref.shape[0]
    for rows in (slice(0, t // 2), slice(t // 2, t)):
        x = jnp.where(i < n_prompt_tiles, xp_ref[rows, :], xs_ref[rows, :])
        h = _layer_norm(x, g_ref[...], b_ref[...])
        h_ref[rows, :] = h
        hb = h.astype(BF16)
        off = 0
        for ref, width in zip((u_ref, q_ref, k_ref, v_ref, og_ref, gp_ref, gg_ref), _MAIN_SPLITS):
            ref[rows, :] = jnp.dot(hb, w_ref[:, off:off + width], preferred_element_type=F32).astype(BF16)
            off += width
        z_ref[rows, :] = jnp.dot(hb, wz_ref[...], preferred_element_type=F32).astype(BF16)


def _inproj(xp, xs, ln_g, ln_b, w_main, w_z):
    n_p, n_s = xp.shape[0], xs.shape[0]
    n = n_p + n_s
    t = TILE_INPROJ
    npt, nst = n_p // t, n_s // t
    row = lambda i: (i, 0)
    widths = _MAIN_SPLITS + (Z_PAD,)
    out_shape = [jax.ShapeDtypeStruct((n, D_MODEL), F32)] + [jax.ShapeDtypeStruct((n, w), BF16) for w in widths]
    out_specs = [pl.BlockSpec((t, D_MODEL), row)] + [pl.BlockSpec((t, w), row) for w in widths]
    return pl.pallas_call(
        functools.partial(_inproj_kernel, n_prompt_tiles=npt),
        grid=(npt + nst,),
        in_specs=[
            pl.BlockSpec((t, D_MODEL), lambda i: (jnp.minimum(i, npt - 1), 0)),
            pl.BlockSpec((t, D_MODEL), lambda i: (jnp.maximum(i - npt, 0), 0)),
            _const_spec((1, D_MODEL)), _const_spec((1, D_MODEL)),
            _const_spec(w_main.shape), _const_spec(w_z.shape),
        ],
        out_specs=out_specs,
        out_shape=out_shape,
        compiler_params=_params(("parallel",)),
        name="inproj",
    )(xp, xs, ln_g, ln_b, w_main, w_z)


def _pool_kernel(u_ref, up_ref, un_ref, gp_ref, band_ref, pw_ref, sc_ref, wup_ref, pm_ref, *, starts, ends):
    t = u_ref.shape[0]
    i = pl.program_id(0)
    is_start = _tile_hits(i, t, starts)
    is_end = _tile_hits(i + 1, t, ends)
    cur = u_ref[...]
    prev = jnp.where(is_start, jnp.zeros_like(up_ref[...]), up_ref[...])
    nxt = jnp.where(is_end, jnp.zeros_like(un_ref[...]), un_ref[...])
    ext = jnp.concatenate([prev, cur, nxt], axis=0)
    r = lax.broadcasted_iota(jnp.int32, (t, POOL_GROUP_W), 0).astype(F32)
    fs = is_start.astype(F32)
    fe = is_end.astype(F32)
    parts = []
    for gi, w in enumerate(POOL_WINDOWS):
        sl = slice(gi * POOL_GROUP_W, (gi + 1) * POOL_GROUP_W)
        s = jnp.dot(band_ref[gi], ext[:, sl], preferred_element_type=F32)
        cnt = w - fs * jnp.maximum(w // 2 - r, 0.0) - fe * jnp.maximum(r + (w // 2 - t), 0.0)
        parts.append(s / cnt - cur[:, sl].astype(F32))
    p = jnp.concatenate(parts, axis=1).astype(BF16)
    p = (jnp.dot(p, pw_ref[...], preferred_element_type=F32) * sc_ref[...]).astype(BF16)
    pool_out = jnp.dot(p, wup_ref[...], preferred_element_type=F32)
    pm_ref[...] = (_sigmoid(gp_ref[...].astype(F32)) * pool_out).astype(BF16)


def _pool_bands(t):
    r = jnp.arange(t)[:, None]
    c = jnp.arange(t + 2 * POOL_HALO)[None, :] - POOL_HALO
    return jnp.stack([((c >= r - w // 2) & (c <= r + w // 2 - 1)).astype(BF16) for w in POOL_WINDOWS])


def _pool(u, gp, pool_w_bd, pool_scale, w_pool_up, starts, ends):
    n = u.shape[0]
    t = TILE_POOL
    hb = t // POOL_HALO
    last = n // POOL_HALO - 1
    return pl.pallas_call(
        functools.partial(_pool_kernel, starts=starts, ends=ends),
        grid=(n // t,),
        in_specs=[
            pl.BlockSpec((t, POOL_W), lambda i: (i, 0)),
            pl.BlockSpec((POOL_HALO, POOL_W), lambda i: (jnp.maximum(i * hb - 1, 0), 0)),
            pl.BlockSpec((POOL_HALO, POOL_W), lambda i: (jnp.minimum((i + 1) * hb, last), 0)),
            pl.BlockSpec((t, D_MODEL), lambda i: (i, 0)),
            _const_spec((POOL_GROUPS, t, t + 2 * POOL_HALO)),
            _const_spec((POOL_W, POOL_W)), _const_spec((1, POOL_W)), _const_spec((POOL_W, D_MODEL)),
        ],
        out_specs=pl.BlockSpec((t, D_MODEL), lambda i: (i, 0)),
        out_shape=jax.ShapeDtypeStruct((n, D_MODEL), BF16),
        compiler_params=_params(("parallel",)),
        name="pool",
    )(u, u, u, gp, _pool_bands(t), pool_w_bd, pool_scale, w_pool_up)


def _gla_tile(q, k, v, la, tri, st_ref, *, reverse):
    t = q.shape[0]
    c_sz = GLA_CHUNK
    hi = la.astype(BF16)
    r1 = la - hi.astype(F32)
    mid = r1.astype(BF16)
    lo = (r1 - mid.astype(F32)).astype(BF16)
    scan = lambda rs: (jnp.dot(tri, hi[rs], preferred_element_type=F32)
                       + jnp.dot(tri, mid[rs], preferred_element_type=F32)
                       + jnp.dot(tri, lo[rs], preferred_element_type=F32))
    sb = GLA_SCAN_BLOCK
    b = jnp.concatenate([scan(slice(r0, r0 + sb)) for r0 in range(0, t, sb)], axis=0)
    kf = k.astype(F32)
    qd = (q.astype(F32) * jnp.exp(b) * (GLA_DK ** -0.5)).astype(BF16)
    kd = (kf * jnp.exp(-b)).astype(BF16)
    row = lax.broadcasted_iota(jnp.int32, (c_sz, c_sz), 0)
    col = lax.broadcasted_iota(jnp.int32, (c_sz, c_sz), 1)
    mask = (col >= row) if reverse else (col <= row)
    n_chunks = t // c_sz
    order = list(range(n_chunks - 1, -1, -1) if reverse else range(n_chunks))
    nt = (((1,), (1,)), ((), ()))
    tn = (((0,), (0,)), ((), ()))
    rows = [slice(c * c_sz, (c + 1) * c_sz) for c in range(n_chunks)]
    lanes = [slice(h * GLA_DK, (h + 1) * GLA_DK) for h in range(GLA_HEADS)]
    vcols = [slice(h * GLA_DV, (h + 1) * GLA_DV) for h in range(GLA_HEADS)]

    att, kv, dec = {}, {}, {}
    for c in order:
        b_c = b[rows[c]]
        b_edge = b_c[0:1] if reverse else b_c[c_sz - 1:c_sz]
        k2 = (kf[rows[c]] * jnp.exp(b_edge - b_c)).astype(BF16)
        dec[c] = jnp.exp(b_edge)
        for h in range(GLA_HEADS):
            att[c, h] = lax.dot_general(qd[rows[c], lanes[h]], kd[rows[c], lanes[h]], nt, preferred_element_type=F32)
            kv[c, h] = lax.dot_general(v[rows[c], vcols[h]], k2[:, lanes[h]], tn, preferred_element_type=F32)

    state_before = {}
    for h in range(GLA_HEADS):
        st = st_ref[h]
        for c in order:
            state_before[c, h] = st.astype(BF16)
            st = st * dec[c][:, lanes[h]] + kv[c, h]
        st_ref[h] = st

    outs = []
    for c in range(n_chunks):
        heads = []
        for h in range(GLA_HEADS):
            a = jnp.where(mask, att[c, h], 0.0).astype(BF16)
            o = jnp.dot(a, v[rows[c], vcols[h]], preferred_element_type=F32)
            heads.append(o + lax.dot_general(qd[rows[c], lanes[h]], state_before[c, h], nt,
                                             preferred_element_type=F32))
        outs.append(jnp.concatenate(heads, axis=1))
    return jnp.concatenate(outs, axis=0)


def _log_decay(z, w2_ref, gb_ref):
    return _log_sigmoid(jnp.dot(z, w2_ref[...], preferred_element_type=F32) + gb_ref[...]) * (1.0 / GATE_TAU)


def _gla_bwd_kernel(q_ref, k_ref, v_ref, z_ref, w2_ref, gb_ref, tri_ref, ob_ref, st_ref, *, n_tiles, ends):
    t = q_ref.shape[0]
    j = n_tiles - 1 - pl.program_id(0)

    @pl.when(_tile_hits(j + 1, t, ends))
    def _():
        st_ref[...] = jnp.zeros_like(st_ref)

    la = _log_decay(z_ref[...], w2_ref, gb_ref)
    ob_ref[...] = _gla_tile(q_ref[...], k_ref[...], v_ref[...], la, tri_ref[...], st_ref, reverse=True)


def _gla_fwd_kernel(q_ref, k_ref, v_ref, z_ref, w2_ref, gb_ref, tri_ref, ob_ref, og_ref, gg_ref, pm_ref, h_ref,
                    ng_ref, wgu_ref, wo_ref, l1g_ref, l1b_ref, h1_ref, h1rc_ref, st_ref, *, starts):
    t = q_ref.shape[0]
    i = pl.program_id(0)

    @pl.when(_tile_hits(i, t, starts))
    def _():
        st_ref[...] = jnp.zeros_like(st_ref)

    la = _log_decay(z_ref[...], w2_ref, gb_ref)
    o_all = _gla_tile(q_ref[...], k_ref[...], v_ref[...], la, tri_ref[...], st_ref, reverse=False)
    tp = t // MIX_ROW_PARTS
    for p in range(MIX_ROW_PARTS):
        rows = slice(p * tp, (p + 1) * tp)
        o = o_all[rows] + ob_ref[rows, :]
        heads = []
        for h in range(GLA_HEADS):
            oh = o[:, h * GLA_DV:(h + 1) * GLA_DV]
            heads.append(oh * lax.rsqrt(jnp.mean(oh * oh, axis=-1, keepdims=True) + RMS_EPS))
        o = jnp.concatenate(heads, axis=1) * ng_ref[...]
        o = (o * _silu(og_ref[rows, :].astype(F32))).astype(BF16)
        gla_out = jnp.dot(o, wgu_ref[...], preferred_element_type=F32)
        merged = _sigmoid(gg_ref[rows, :].astype(F32)) * gla_out + pm_ref[rows, :].astype(F32)
        mix = jnp.dot(merged.astype(BF16), wo_ref[...], preferred_element_type=F32)
        h1 = _layer_norm(DN_ALPHA * h_ref[rows, :] + mix, l1g_ref[...], l1b_ref[...])
        h1_ref[rows, :] = h1
        _store_row_contiguous(h1rc_ref.at[pl.ds(p * tp * ROW_TILES, tp * ROW_TILES)], h1)


def _store_row_contiguous(ref, x):
    t = x.shape[0]
    bits = pltpu.bitcast(x.astype(BF16).astype(F32), jnp.uint32)
    words = (bits[:, :ROW_WORDS] >> 16) | (bits[:, ROW_WORDS:] & jnp.uint32(0xFFFF0000))
    words = pltpu.bitcast(words, F32)
    for s in range(ROW_TILES):
        ref[pl.ds(s, t, stride=ROW_TILES), :] = words[:, s * LANES:(s + 1) * LANES]


def _load_row_contiguous(ref, t):
    words = jnp.concatenate([ref[pl.ds(s, t, stride=ROW_TILES), :] for s in range(ROW_TILES)], axis=1)
    bits = pltpu.bitcast(words, jnp.uint32)
    lo = pltpu.bitcast(bits << 16, F32)
    hi = pltpu.bitcast(bits & jnp.uint32(0xFFFF0000), F32)
    return jnp.concatenate([lo, hi], axis=1)


def _chunk_tri(t, reverse):
    r = jnp.arange(t)[:, None]
    c = jnp.arange(t)[None, :]
    same = (r // GLA_CHUNK) == (c // GLA_CHUNK)
    return (same & ((c >= r) if reverse else (c <= r))).astype(BF16)


def _gla_bwd(q, k, v, z, w2b_pad, gate_b, ends):
    n = q.shape[0]
    t = TILE_GLA
    nt = n // t
    rev = lambda i: (nt - 1 - i, 0)
    return pl.pallas_call(
        functools.partial(_gla_bwd_kernel, n_tiles=nt, ends=ends),
        grid=(nt,),
        in_specs=[
            pl.BlockSpec((t, GLA_K_W), rev), pl.BlockSpec((t, GLA_K_W), rev), pl.BlockSpec((t, GLA_V_W), rev),
            pl.BlockSpec((t, Z_PAD), rev),
            _const_spec((Z_PAD, GLA_K_W)), _const_spec((1, GLA_K_W)),
            _const_spec((GLA_SCAN_BLOCK, GLA_SCAN_BLOCK)),
        ],
        out_specs=pl.BlockSpec((t, GLA_V_W), rev),
        out_shape=jax.ShapeDtypeStruct((n, GLA_V_W), F32),
        scratch_shapes=[pltpu.VMEM((GLA_HEADS, GLA_DV, GLA_DK), F32)],
        compiler_params=_params(("arbitrary",)),
        name="gla_bwd",
    )(q, k, v, z, w2b_pad, gate_b, _chunk_tri(GLA_SCAN_BLOCK, True))


def _gla_fwd_mix(q, k, v, z, w2f_pad, gate_b, ob, og, gg, pm, h, norm_g, w_gla_up, w_out, ln1_g, ln1_b, starts):
    n = q.shape[0]
    t = TILE_GLA
    row = lambda i: (i, 0)
    return pl.pallas_call(
        functools.partial(_gla_fwd_kernel, starts=starts),
        grid=(n // t,),
        in_specs=[
            pl.BlockSpec((t, GLA_K_W), row), pl.BlockSpec((t, GLA_K_W), row), pl.BlockSpec((t, GLA_V_W), row),
            pl.BlockSpec((t, Z_PAD), row),
            _const_spec((Z_PAD, GLA_K_W)), _const_spec((1, GLA_K_W)),
            _const_spec((GLA_SCAN_BLOCK, GLA_SCAN_BLOCK)),
            pl.BlockSpec((t, GLA_V_W), row), pl.BlockSpec((t, GLA_V_W), row), pl.BlockSpec((t, D_MODEL), row),
            pl.BlockSpec((t, D_MODEL), row), pl.BlockSpec((t, D_MODEL), row),
            _const_spec((1, GLA_V_W)), _const_spec((GLA_V_W, D_MODEL)), _const_spec((D_MODEL, D_MODEL)),
            _const_spec((1, D_MODEL)), _const_spec((1, D_MODEL)),
        ],
        out_specs=[pl.BlockSpec((t, D_MODEL), row), pl.BlockSpec((t * ROW_TILES, LANES), row)],
        out_shape=[jax.ShapeDtypeStruct((n, D_MODEL), F32), jax.ShapeDtypeStruct((n * ROW_TILES, LANES), F32)],
        scratch_shapes=[pltpu.VMEM((GLA_HEADS, GLA_DV, GLA_DK), F32)],
        compiler_params=_params(("arbitrary",)),
        name="gla_fwd_mix",
    )(q, k, v, z, w2f_pad, gate_b, _chunk_tri(GLA_SCAN_BLOCK, False), ob, og, gg, pm, h, norm_g, w_gla_up, w_out,
      ln1_g, ln1_b)


def _router_kernel(h_ref, wrh_ref, wrl_ref, rb_ref, tri_ref, idx_ref, wts_ref, rank_ref, cnt_ref, base_ref):
    t = h_ref.shape[0]

    @pl.when(pl.program_id(0) == 0)
    def _():
        base_ref[...] = jnp.zeros_like(base_ref)

    h = h_ref[...]
    hh = h.astype(BF16)
    hl = (h - hh.astype(F32)).astype(BF16)
    nt = (((1,), (1,)), ((), ()))
    logits = (lax.dot_general(wrh_ref[...], hh, nt, preferred_element_type=F32)
              + lax.dot_general(wrl_ref[...], hh, nt, preferred_element_type=F32)
              + lax.dot_general(wrh_ref[...], hl, nt, preferred_element_type=F32))
    scores = _sigmoid(logits)
    sel = scores + rb_ref[:, 0:1]
    neg = jnp.float32(-jnp.inf)

    sub = lax.broadcasted_iota(jnp.int32, (EXPERTS_PER_GROUP, t), 0)
    gscore = []
    for g in range(N_GROUPS):
        sg = sel[g * EXPERTS_PER_GROUP:(g + 1) * EXPERTS_PER_GROUP]
        m1 = jnp.max(sg, axis=0, keepdims=True)
        first = jnp.min(jnp.where(sg == m1, sub, EXPERTS_PER_GROUP), axis=0, keepdims=True)
        m2 = jnp.max(jnp.where(sub == first, neg, sg), axis=0, keepdims=True)
        gscore.append(m1 + m2)
    keep = []
    for g in range(N_GROUPS):
        beaten = jnp.zeros((1, t), jnp.int32)
        for g2 in range(N_GROUPS):
            if g2 == g:
                continue
            wins = (gscore[g2] >= gscore[g]) if g2 < g else (gscore[g2] > gscore[g])
            beaten = beaten + wins.astype(jnp.int32)
        keep.append(beaten < TOPK_GROUPS)
    slot = []
    ahead = jnp.zeros((1, t), jnp.int32)
    for g in range(N_GROUPS):
        slot.append([jnp.logical_and(keep[g], ahead == j) for j in range(TOPK_GROUPS)])
        ahead = ahead + keep[g].astype(jnp.int32)
    groups_of = lambda j: range(j, j + N_GROUPS - TOPK_GROUPS + 1)
    slots_of = lambda g: range(max(0, g - (N_GROUPS - TOPK_GROUPS)), min(TOPK_GROUPS - 1, g) + 1)
    rows_of = lambda x, g: x[g * EXPERTS_PER_GROUP:(g + 1) * EXPERTS_PER_GROUP]

    def compact(x, fill):
        parts = []
        for j in range(TOPK_GROUPS):
            acc = jnp.full((EXPERTS_PER_GROUP, t), fill, x.dtype)
            for g in groups_of(j):
                acc = jnp.where(slot[g][j], rows_of(x, g), acc)
            parts.append(acc)
        return jnp.concatenate(parts, axis=0)

    cand = compact(sel, neg)
    kept_scores = compact(scores, 0.0)
    gid = []
    for j in range(TOPK_GROUPS):
        acc = jnp.zeros((1, t), jnp.int32)
        for g in groups_of(j):
            acc = jnp.where(slot[g][j], g, acc)
        gid.append(acc * EXPERTS_PER_GROUP + sub)
    cid = jnp.concatenate(gid, axis=0)

    picked = []
    wsum = jnp.zeros((1, t), F32)
    for kk in range(TOP_K):
        m = jnp.max(cand, axis=0, keepdims=True)
        e_k = jnp.min(jnp.where(cand == m, cid, N_EXPERTS), axis=0, keepdims=True)
        hit = cid == e_k
        w_k = jnp.sum(jnp.where(hit, kept_scores, 0.0), axis=0, keepdims=True)
        cand = jnp.where(hit, neg, cand)
        wsum = wsum + w_k
        picked.append((e_k, w_k))
        idx_ref[kk:kk + 1, :] = e_k

    chosen = jnp.where(cand == neg, 1.0, 0.0)
    full = []
    for g in range(N_GROUPS):
        acc = jnp.zeros((EXPERTS_PER_GROUP, t), F32)
        for j in slots_of(g):
            acc = jnp.where(slot[g][j], rows_of(chosen, j), acc)
        full.append(acc)
    a16 = jnp.concatenate(full, axis=0).astype(BF16)

    before = jnp.dot(a16, tri_ref[...], preferred_element_type=F32)
    base = base_ref[...]
    before = compact(before + jnp.concatenate([base] * (t // 128), axis=1), 0.0)
    inv = ROUTED_SCALE / wsum
    for kk, (e_k, w_k) in enumerate(picked):
        wts_ref[kk:kk + 1, :] = w_k * inv
        rank_ref[kk:kk + 1, :] = jnp.sum(jnp.where(cid == e_k, before, 0.0), axis=0, keepdims=True).astype(jnp.int32)
    total = base + jnp.dot(a16, jnp.ones((t, 128), BF16), preferred_element_type=F32)
    base_ref[...] = total
    cnt_ref[...] = total


def _router(h1, wr_hi, wr_lo, rbias):
    n = h1.shape[0]
    t = TILE_ROUTER
    r = jnp.arange(t)
    tri = (r[:, None] < r[None, :]).astype(BF16)
    col = lambda i: (0, i)
    idx, wts, rank, cnt = pl.pallas_call(
        _router_kernel,
        grid=(n // t,),
        in_specs=[
            pl.BlockSpec((t, D_MODEL), lambda i: (i, 0)),
            _const_spec((N_EXPERTS, D_MODEL)), _const_spec((N_EXPERTS, D_MODEL)),
            _const_spec((N_EXPERTS, 128)), _const_spec((t, t)),
        ],
        out_specs=[pl.BlockSpec((TOP_K, t), col), pl.BlockSpec((TOP_K, t), col), pl.BlockSpec((TOP_K, t), col),
                   _const_spec((N_EXPERTS, 128))],
        out_shape=[jax.ShapeDtypeStruct((TOP_K, n), jnp.int32), jax.ShapeDtypeStruct((TOP_K, n), F32),
                   jax.ShapeDtypeStruct((TOP_K, n), jnp.int32), jax.ShapeDtypeStruct((N_EXPERTS, 128), F32)],
        scratch_shapes=[pltpu.VMEM((N_EXPERTS, 128), F32)],
        compiler_params=_params(("arbitrary",)),
        name="router",
    )(h1, wr_hi, wr_lo, rbias, tri)
    return idx, wts, rank, cnt[:, 0].astype(jnp.int32)


def _slot_kernel(ps_ref, idx_ref, rank_ref, wts_ref, dest_ref, wt_ref):
    idx = idx_ref[...]

    def add_expert(e, acc):
        return acc + jnp.where(idx == e, ps_ref[e], 0)

    dest = lax.fori_loop(0, N_EXPERTS, add_expert, rank_ref[...], unroll=8)
    for w in range(dest.shape[1] // SC_WINDOW):
        dest_ref[w] = dest[:, w * SC_WINDOW:(w + 1) * SC_WINDOW]
    wt_ref[...] = wts_ref[...].T


def _slots(pstart, idx, rank, wts):
    n = idx.shape[1]
    t = TILE_SLOTS
    col = lambda i, ps: (0, i)
    return pl.pallas_call(
        _slot_kernel,
        grid_spec=pltpu.PrefetchScalarGridSpec(
            num_scalar_prefetch=1,
            grid=(n // t,),
            in_specs=[pl.BlockSpec((TOP_K, t), col), pl.BlockSpec((TOP_K, t), col), pl.BlockSpec((TOP_K, t), col)],
            out_specs=[pl.BlockSpec((t // SC_WINDOW, TOP_K, SC_WINDOW), lambda i, ps: (i, 0, 0)),
                       pl.BlockSpec((t, TOP_K), lambda i, ps: (i, 0))],
        ),
        out_shape=[jax.ShapeDtypeStruct((n // SC_WINDOW, TOP_K, SC_WINDOW), jnp.int32),
                   jax.ShapeDtypeStruct((n, TOP_K), F32)],
        compiler_params=_params(("parallel",)),
        name="slots",
    )(pstart, idx, rank, wts)


def _expert_kernel(first_ref, count_ref, used_ref, wg_ref, wu_ref, wd_ref, x_hbm, y_hbm,
                   wgu_s, wd_s, xbuf, ybuf, xsem, ysem):
    e = pl.program_id(0)
    rows = MOE_BLOCK * ROW_TILES
    used = used_ref[0]

    def x_copy(b, slot):
        return pltpu.make_async_copy(x_hbm.at[pl.ds(b * rows, rows)], xbuf.at[slot], xsem.at[slot])

    def y_copy(b, slot):
        return pltpu.make_async_copy(ybuf.at[slot], y_hbm.at[pl.ds(b * rows, rows)], ysem.at[slot])

    @pl.when(e == 0)
    def _():
        for b in range(min(X_AHEAD, x_hbm.shape[0] // rows)):
            @pl.when(b < used)
            def _():
                x_copy(b, b).start()

    @pl.when(count_ref[e] > 0)
    def _():
        wgu_s[:, :EXPERT_HIDDEN] = wg_ref[0].astype(BF16)
        wgu_s[:, EXPERT_HIDDEN:] = wu_ref[0].astype(BF16)
        wd_s[...] = wd_ref[0].astype(BF16)

    def run_blocks(b0, g):
        for i in range(g):
            ahead = b0 + i + X_AHEAD

            @pl.when(ahead < used)
            def _():
                x_copy(ahead, ahead % X_SLOTS).start()

        for i in range(g):
            done = b0 + i - Y_SLOTS

            @pl.when(done >= 0)
            def _():
                y_copy(done, done % Y_SLOTS).wait()

        x = []
        for i in range(g):
            xs = (b0 + i) % X_SLOTS
            x_copy(b0 + i, xs).wait()
            x.append(_load_row_contiguous(xbuf.at[xs], MOE_BLOCK).astype(BF16))
        gu = [jnp.dot(xi, wgu_s[...], preferred_element_type=F32) for xi in x]
        hid = [(_silu(gi[:, :EXPERT_HIDDEN]) * gi[:, EXPERT_HIDDEN:]).astype(BF16) for gi in gu]
        y = [jnp.dot(hi, wd_s[...], preferred_element_type=F32) for hi in hid]
        for i in range(g):
            ys = (b0 + i) % Y_SLOTS
            _store_row_contiguous(ybuf.at[ys], y[i])
            y_copy(b0 + i, ys).start(priority=Y_DMA_PRIORITY)

    first = first_ref[e]
    count = count_ref[e]
    n_groups = lax.div(count, EXPERT_GROUP)

    def group(j, carry):
        run_blocks(first + j * EXPERT_GROUP, EXPERT_GROUP)
        return carry

    lax.fori_loop(0, n_groups, group, 0)
    for left in range(1, EXPERT_GROUP):
        @pl.when(count - n_groups * EXPERT_GROUP == left)
        def _():
            run_blocks(first + n_groups * EXPERT_GROUP, left)

    @pl.when(e == pl.num_programs(0) - 1)
    def _():
        for back in range(1, Y_SLOTS + 1):
            @pl.when(used - back >= 0)
            def _():
                y_copy(used - back, (used - back) % Y_SLOTS).wait()


def _experts(first_block, block_count, n_used, xs, w_gate, w_up, w_down):
    rows = MOE_BLOCK * ROW_TILES
    n_exp = w_gate.shape[0]
    any_spec = pl.BlockSpec(memory_space=pl.ANY)
    return pl.pallas_call(
        _expert_kernel,
        grid_spec=pltpu.PrefetchScalarGridSpec(
            num_scalar_prefetch=3,
            grid=(n_exp,),
            in_specs=[
                pl.BlockSpec((1, D_MODEL, EXPERT_HIDDEN), lambda e, *_: (e, 0, 0)),
                pl.BlockSpec((1, D_MODEL, EXPERT_HIDDEN), lambda e, *_: (e, 0, 0)),
                pl.BlockSpec((1, EXPERT_HIDDEN, D_MODEL), lambda e, *_: (e, 0, 0)),
                any_spec,
            ],
            out_specs=any_spec,
            scratch_shapes=[
                pltpu.VMEM((D_MODEL, 2 * EXPERT_HIDDEN), BF16), pltpu.VMEM((EXPERT_HIDDEN, D_MODEL), BF16),
                pltpu.VMEM((X_SLOTS, rows, LANES), F32), pltpu.VMEM((Y_SLOTS, rows, LANES), F32),
                pltpu.SemaphoreType.DMA((X_SLOTS,)), pltpu.SemaphoreType.DMA((Y_SLOTS,)),
            ],
        ),
        out_shape=jax.ShapeDtypeStruct(xs.shape, F32),
        compiler_params=_params(("arbitrary",)),
        name="experts",
    )(first_block, block_count, n_used, w_gate, w_up, w_down, xs)


def _sc_mesh():
    return plsc.VectorSubcoreMesh(core_axis_name="c", subcore_axis_name="s")


def _sc_dispatch(x3, dest_w, n_rows):
    n = x3.shape[0]

    @functools.partial(pl.kernel, out_type=jax.ShapeDtypeStruct((n_rows, ROW_TILES, LANES), F32), mesh=_sc_mesh(),
                       name="sc_dispatch")
    def run(x_hbm, d_hbm, o_hbm):
        def body(x_vmem, d_vmem):
            for kk in range(TOP_K):
                pltpu.sync_copy(x_vmem, o_hbm.at[d_vmem.at[0].at[kk]])

        pltpu.emit_pipeline(
            body,
            grid=(n // SC_WINDOW,),
            in_specs=[pl.BlockSpec((SC_WINDOW, ROW_TILES, LANES), lambda i: (i, 0, 0)),
                      pl.BlockSpec((1, TOP_K, SC_WINDOW), lambda i: (i, 0, 0))],
            out_specs=[],
            core_axis_name=("c", "s"),
            dimension_semantics=(pltpu.PARALLEL,),
        )(x_hbm, d_hbm)

    return run(x3, dest_w)


def _sc_gather(y3, dest_w):
    nw = dest_w.shape[0]
    n = nw * SC_WINDOW

    @functools.partial(pl.kernel, out_type=jax.ShapeDtypeStruct((TOP_K, n, ROW_TILES, LANES), F32), mesh=_sc_mesh(),
                       name="sc_gather")
    def run(y_hbm, d_hbm, o_hbm):
        def body(d_vmem, g_vmem):
            pltpu.sync_copy(y_hbm.at[d_vmem.at[0].at[0]], g_vmem.at[0])

        pltpu.emit_pipeline(
            body,
            grid=(nw, TOP_K),
            in_specs=[pl.BlockSpec((1, 1, SC_WINDOW), lambda i, k: (i, k, 0))],
            out_specs=[pl.BlockSpec((1, SC_WINDOW, ROW_TILES, LANES), lambda i, k: (k, i, 0, 0))],
            core_axis_name=("c", "s"),
            dimension_semantics=(pltpu.PARALLEL, pltpu.PARALLEL),
        )(d_hbm, o_hbm)

    return run(y3, dest_w)


def _final_kernel(h_ref, g_ref, w_ref, wgu_ref, wd_ref, lg_ref, lb_ref, y_ref):
    t = h_ref.shape[0]
    h = h_ref[...]
    gu = jnp.dot(h.astype(BF16), wgu_ref[...], preferred_element_type=F32)
    hid = (_silu(gu[:, :SHARED_HIDDEN]) * gu[:, SHARED_HIDDEN:]).astype(BF16)
    ffn = jnp.dot(hid, wd_ref[...], preferred_element_type=F32)
    w = w_ref[...]
    for kk in range(TOP_K):
        ffn = ffn + _load_row_contiguous(g_ref.at[kk], t) * w[:, kk:kk + 1]
    y_ref[...] = _layer_norm(DN_ALPHA * h + ffn, lg_ref[...], lb_ref[...])


def _final(h1, gathered, wts_t, w_sh_gu, w_sh_down, ln_g, ln_b, first_token):
    m = gathered.shape[1] // ROW_TILES
    t = TILE_FINAL
    off = first_token // t
    part = lambda i: (i + off, 0)
    return pl.pallas_call(
        _final_kernel,
        grid=(m // t,),
        in_specs=[
            pl.BlockSpec((t, D_MODEL), part),
            pl.BlockSpec((TOP_K, t * ROW_TILES, LANES), lambda i: (0, i, 0)),
            pl.BlockSpec((t, TOP_K), part),
            _const_spec((D_MODEL, 2 * SHARED_HIDDEN)), _const_spec((SHARED_HIDDEN, D_MODEL)),
            _const_spec((1, D_MODEL)), _const_spec((1, D_MODEL)),
        ],
        out_specs=pl.BlockSpec((t, D_MODEL), lambda i: (i, 0)),
        out_shape=jax.ShapeDtypeStruct((m, D_MODEL), F32),
        compiler_params=_params(("parallel",)),
        name="final",
    )(h1, gathered, wts_t, w_sh_gu, w_sh_down, ln_g, ln_b)


def _block_diag(w):
    g, a, b = w.shape
    out = jnp.zeros((g * a, g * b), w.dtype)
    for i in range(g):
        out = out.at[i * a:(i + 1) * a, i * b:(i + 1) * b].set(w[i])
    return out


def kernel(x_prompt, x_sample, ln0_g, ln0_b, w_in, pool_w, pool_scale, w_pool_up, gate_w2_fwd, gate_b_fwd,
           gate_w2_bwd, gate_b_bwd, gla_norm_g, w_gla_up, w_out, ln1_g, ln1_b, w_router, router_bias,
           w_exp_gate, w_exp_up, w_exp_down, w_sh_gate, w_sh_up, w_sh_down, ln2_g, ln2_b):
    bp, lp, d = x_prompt.shape
    bs, ls, _ = x_sample.shape
    n_p, n_s = bp * lp, bs * ls
    n = n_p + n_s
    starts = tuple(b * lp for b in range(bp)) + tuple(n_p + b * ls for b in range(bs))
    ends = tuple((b + 1) * lp for b in range(bp)) + tuple(n_p + (b + 1) * ls for b in range(bs))
    vec = lambda a: a.reshape(1, -1).astype(F32)
    assert d == D_MODEL and w_in.shape[0] == DEPTH
    coarsest = max(TILE_INPROJ, TILE_POOL, TILE_GLA, TILE_FINAL, SC_WINDOW)
    assert lp % coarsest == 0 and ls % coarsest == 0, "sequence boundaries must fall on tile boundaries"
    assert n % TILE_SLOTS == 0 and n % TILE_ROUTER == 0

    w = w_in[0]
    c_zf = POOL_W + 2 * GLA_K_W + 2 * GLA_V_W
    c_gp = c_zf + 2 * GATE_RANK
    w_main = jnp.concatenate([w[:, :c_zf], w[:, c_gp:]], axis=1).astype(BF16)
    w_z = jnp.zeros((d, Z_PAD), F32).at[:, :2 * GATE_RANK].set(w[:, c_zf:c_gp]).astype(BF16)
    w2f = jnp.zeros((Z_PAD, GLA_K_W), F32).at[:GATE_RANK].set(gate_w2_fwd[0]).astype(BF16)
    w2b = jnp.zeros((Z_PAD, GLA_K_W), F32).at[GATE_RANK:2 * GATE_RANK].set(gate_w2_bwd[0]).astype(BF16)

    h, u, q, k, v, og, gp, gg, z = _inproj(x_prompt.reshape(n_p, d), x_sample.reshape(n_s, d),
                                           vec(ln0_g), vec(ln0_b), w_main, w_z)
    pm = _pool(u, gp, _block_diag(pool_w[0]).astype(BF16), vec(pool_scale[0]), w_pool_up[0].astype(BF16),
               starts, ends)
    ob = _gla_bwd(q, k, v, z, w2b, vec(gate_b_bwd[0]), ends)
    h1, h1_rc = _gla_fwd_mix(q, k, v, z, w2f, vec(gate_b_fwd[0]), ob, og, gg, pm, h, vec(gla_norm_g[0]),
                             w_gla_up[0].astype(BF16), w_out[0].astype(BF16), vec(ln1_g[0]), vec(ln1_b[0]), starts)

    wr_t = w_router[0].T
    wr_hi = wr_t.astype(BF16)
    wr_lo = (wr_t - wr_hi.astype(F32)).astype(BF16)
    rbias = jnp.broadcast_to(router_bias[0].astype(F32)[:, None], (N_EXPERTS, 128))
    idx, wts, rank, counts = _router(h1, wr_hi, wr_lo, rbias)

    padded = (counts + MOE_BLOCK - 1) // MOE_BLOCK * MOE_BLOCK
    pend = jnp.cumsum(padded)
    pstart = pend - padded
    n_rows = n * TOP_K + N_EXPERTS * MOE_BLOCK
    first_block = (pstart // MOE_BLOCK).astype(jnp.int32)
    block_count = (padded // MOE_BLOCK).astype(jnp.int32)
    n_used = (pend[-1:] // MOE_BLOCK).astype(jnp.int32)
    dest_w, wts_t = _slots(pstart.astype(jnp.int32), idx, rank, wts)

    xs = _sc_dispatch(h1_rc.reshape(n, ROW_TILES, LANES), dest_w, n_rows)
    ys = _experts(first_block, block_count, n_used, xs.reshape(n_rows * ROW_TILES, LANES),
                  w_exp_gate[0], w_exp_up[0], w_exp_down[0])
    ys = ys.reshape(n_rows, ROW_TILES, LANES)
    w_sh_gu = jnp.concatenate([w_sh_gate[0], w_sh_up[0]], axis=1).astype(BF16)
    w_sh_d = w_sh_down[0].astype(BF16)
    outs = []
    for first, m in ((0, n_p), (n_p, n_s)):
        windows = dest_w[first // SC_WINDOW:(first + m) // SC_WINDOW]
        gathered = _sc_gather(ys, windows).reshape(TOP_K, m * ROW_TILES, LANES)
        outs.append(_final(h1, gathered, wts_t, w_sh_gu, w_sh_d, vec(ln2_g[0]), vec(ln2_b[0]), first))
    return outs[0].reshape(bp, lp, d), outs[1].reshape(bs, ls, d)
```
